```python
import math
import jax, jax.numpy as jnp
from jax import lax
import numpy as np

D_MODEL = 2048
BATCH = 4
SEQ = 4096
DEPTH = 2

CHUNK = 64
QBLK = 128
N_MIXERS = 2
N_LAYERS_A = (DEPTH + 1) // 2
N_LAYERS_B = DEPTH // 2
A_HEADS = 16
A_HEAD_DIM = D_MODEL // A_HEADS
B_HEADS = 16
B_HEAD_DIM = 128
B_V_DIM = 128
Q_LATENT = D_MODEL // 4
KV_LATENT = D_MODEL // 8
IDX_HEADS = 16
IDX_DIM = 64
IDX_TOPK = 256
REL_BUCKETS = 32
REL_MAX_DIST = 128
N_EXPERTS = 64
TOP_K = 8
D_EXPERT = 512
D_SHARED = 512
ROUTED_SCALE = 2.5
EXPERT_BLOCK = 128
DEEPNORM_ALPHA = (2 * DEPTH) ** 0.25
DEEPNORM_BETA = (8 * DEPTH) ** -0.25
LN_EPS = 1e-5
RMS_EPS = 1e-6

kernel_name = "fox_dsa_moe_deepnorm_hybrid"


def layer_norm(x, g, b):
    xf = x.astype(jnp.float32)
    mu = jnp.mean(xf, axis=-1, keepdims=True)
    var = jnp.mean(jnp.square(xf - mu), axis=-1, keepdims=True)
    return ((xf - mu) * lax.rsqrt(var + LN_EPS) * g.astype(jnp.float32) + b.astype(jnp.float32)).astype(x.dtype)


def rms_norm(x, g):
    xf = x.astype(jnp.float32)
    ms = jnp.mean(jnp.square(xf), axis=-1, keepdims=True)
    return (xf * lax.rsqrt(ms + RMS_EPS) * g.astype(jnp.float32)).astype(x.dtype)


def t5_bucket(rel):
    half = REL_BUCKETS // 2
    max_exact = half // 2
    n = jnp.abs(rel)
    large = max_exact + (jnp.log(jnp.maximum(n, 1).astype(jnp.float32) / max_exact)
                         / math.log(REL_MAX_DIST / max_exact) * (half - max_exact)).astype(jnp.int32)
    large = jnp.minimum(large, half - 1)
    return jnp.where(rel > 0, half, 0) + jnp.where(n < max_exact, n, large)


def forgetting_attention(h, w_in, b_f, w_out):
    Bn, S, _ = h.shape
    dA = A_HEADS * A_HEAD_DIM
    proj = h @ w_in
    q = proj[..., :dA].reshape(Bn, S, A_HEADS, A_HEAD_DIM)
    k = proj[..., dA:2 * dA].reshape(Bn, S, A_HEADS, A_HEAD_DIM)
    v = proj[..., 2 * dA:3 * dA].reshape(Bn, S, A_HEADS, A_HEAD_DIM)
    logf = jax.nn.log_sigmoid((proj[..., 3 * dA:] + b_f).astype(jnp.float32))
    F = jnp.cumsum(logf, axis=1)
    F_k = jnp.transpose(F, (0, 2, 1))[:, :, None, :]
    nblk = S // QBLK
    q_blocks = jnp.swapaxes(q.reshape(Bn, nblk, QBLK, A_HEADS, A_HEAD_DIM), 0, 1)
    F_blocks = jnp.swapaxes(F.reshape(Bn, nblk, QBLK, A_HEADS), 0, 1)
    kpos = jnp.arange(S, dtype=jnp.int32)
    scale = A_HEAD_DIM ** -0.5

    def block(args):
        q_blk, F_blk, i = args
        qpos = i * QBLK + jnp.arange(QBLK, dtype=jnp.int32)
        logits = jnp.einsum('bqhd,bshd->bhqs', q_blk, k).astype(jnp.float32) * scale
        logits = logits + jnp.transpose(F_blk, (0, 2, 1))[..., None] - F_k
        logits = jnp.where((kpos[None, :] <= qpos[:, None])[None, None], logits, -jnp.inf)
        p = jax.nn.softmax(logits, axis=-1).astype(v.dtype)
        return jnp.einsum('bhqs,bshd->bqhd', p, v).reshape(Bn, QBLK, dA)

    o = lax.map(block, (q_blocks, F_blocks, jnp.arange(nblk, dtype=jnp.int32)))
    o = jnp.swapaxes(o, 0, 1).reshape(Bn, S, dA)
    return o @ w_out


def indexed_sparse_attention(h, w_in, q_norm, kv_norm, w_uq, w_iq, w_uk, w_uv, w_out, rel_bias):
    Bn, S, _ = h.shape
    topk = min(IDX_TOPK, S // 4)
    proj = h @ w_in
    o1 = Q_LATENT
    o2 = o1 + KV_LATENT
    o3 = o2 + IDX_DIM
    c_q = rms_norm(proj[..., :o1], q_norm)
    c_kv = rms_norm(proj[..., o1:o2], kv_norm)
    k_idx = proj[..., o2:o3]
    w_idx = proj[..., o3:] * (IDX_HEADS ** -0.5)
    q = (c_q @ w_uq).reshape(Bn, S, B_HEADS, B_HEAD_DIM)
    q_lat = jnp.einsum('bthd,hcd->bthc', q, w_uk)
    q_idx = (c_q @ w_iq).reshape(Bn, S, IDX_HEADS, IDX_DIM)
    nblk = S // QBLK
    qi_blocks = jnp.swapaxes(q_idx.reshape(Bn, nblk, QBLK, IDX_HEADS, IDX_DIM), 0, 1)
    wi_blocks = jnp.swapaxes(w_idx.reshape(Bn, nblk, QBLK, IDX_HEADS), 0, 1)
    ql_blocks = jnp.swapaxes(q_lat.reshape(Bn, nblk, QBLK, B_HEADS, KV_LATENT), 0, 1)
    kchunk = jnp.arange(S, dtype=jnp.int32) // CHUNK
    scale = B_HEAD_DIM ** -0.5
    gather_rows = jax.vmap(lambda c, idx: c[idx])

    def block(args):
        qi, wi, ql, i = args
        qpos = i * QBLK + jnp.arange(QBLK, dtype=jnp.int32)
        qchunk = qpos // CHUNK
        sc = jnp.einsum('bqhd,bsd->bqhs', qi, k_idx).astype(jnp.float32)
        I = jnp.einsum('bqhs,bqh->bqs', jax.nn.relu(sc), wi.astype(jnp.float32)) * (IDX_DIM ** -0.5)
        I = jnp.where((kchunk[None, :] <= qchunk[:, None])[None], I, -jnp.inf)
        _, sel = lax.top_k(I, topk)
        cg = gather_rows(c_kv, sel)
        valid = (sel // CHUNK) <= qchunk[None, :, None]
        bias = rel_bias[t5_bucket(sel - qpos[None, :, None])]
        logits = jnp.einsum('bqhc,bqkc->bqhk', ql, cg).astype(jnp.float32) * scale
        logits = logits + jnp.moveaxis(bias, -1, 2).astype(jnp.float32)
        logits = jnp.where(valid[:, :, None, :], logits, -jnp.inf)
        p = jax.nn.softmax(logits, axis=-1).astype(cg.dtype)
        o_lat = jnp.einsum('bqhk,bqkc->bqhc', p, cg)
        return jnp.einsum('bqhc,hcd->bqhd', o_lat, w_uv).reshape(Bn, QBLK, B_HEADS * B_V_DIM)

    o = lax.map(block, (qi_blocks, wi_blocks, ql_blocks, jnp.arange(nblk, dtype=jnp.int32)))
    o = jnp.swapaxes(o, 0, 1).reshape(Bn, S, B_HEADS * B_V_DIM)
    return o @ w_out


def moe_ffn(h, router_w, router_b, w_gate, w_up, w_down, sh_gate, sh_up, sh_down):
    Bn, S, D = h.shape
    x2 = h.reshape(-1, D)
    N = x2.shape[0]
    scores = jax.nn.sigmoid(x2.astype(jnp.float32) @ router_w.astype(jnp.float32))
    _, sel = lax.top_k(scores + router_b.astype(jnp.float32), TOP_K)
    gate = jnp.take_along_axis(scores, sel, axis=-1)
    gate = gate / jnp.sum(gate, axis=-1, keepdims=True) * ROUTED_SCALE
    flat_e = sel.reshape(-1)
    order = jnp.argsort(flat_e)
    e_sorted = flat_e[order]
    tok_sorted = (order // TOP_K).astype(jnp.int32)
    g_sorted = gate.reshape(-1)[order]
    counts = jnp.bincount(flat_e, length=N_EXPERTS)
    padded = (counts + EXPERT_BLOCK - 1) // EXPERT_BLOCK * EXPERT_BLOCK
    pend = jnp.cumsum(padded)
    pstart = pend - padded
    cstart = jnp.cumsum(counts) - counts
    dest = pstart[e_sorted] + jnp.arange(N * TOP_K) - cstart[e_sorted]
    P = N * TOP_K + N_EXPERTS * EXPERT_BLOCK
    nblk = P // EXPERT_BLOCK
    row_tok = jnp.full((P,), N, jnp.int32).at[dest].set(tok_sorted)
    row_g = jnp.zeros((P,), jnp.float32).at[dest].set(g_sorted)
    blk_e = jnp.minimum(jnp.searchsorted(pend, jnp.arange(nblk) * EXPERT_BLOCK, side='right'),
                        N_EXPERTS - 1).astype(jnp.int32)
    xpad = jnp.concatenate([x2, jnp.zeros((1, D), x2.dtype)], axis=0)

    def step(y, inp):
        rows, g, e = inp
        xb = xpad[rows]
        hb = jax.nn.silu(xb @ w_gate[e]) * (xb @ w_up[e])
        yb = (hb @ w_down[e]) * g[:, None].astype(xb.dtype)
        return y.at[rows].add(yb), None

    y0 = jnp.zeros((N + 1, D), x2.dtype)
    y, _ = lax.scan(step, y0, (row_tok.reshape(nblk, EXPERT_BLOCK),
                               row_g.reshape(nblk, EXPERT_BLOCK), blk_e))
    shared = (jax.nn.silu(x2 @ sh_gate) * (x2 @ sh_up)) @ sh_down
    return (y[:N] + shared).reshape(Bn, S, D)


def setup_inputs(seed: int = 0) -> dict:
    key = jax.random.key(seed)
    ks = jax.random.split(key, 32)
    f32 = jnp.float32
    D = D_MODEL
    beta = DEEPNORM_BETA
    dA = A_HEADS * A_HEAD_DIM
    b_width = Q_LATENT + KV_LATENT + IDX_DIM + IDX_HEADS

    def nrm(k, shape, scale):
        return jax.random.normal(k, shape, f32) * scale

    a_w_in = nrm(ks[1], (N_LAYERS_A, D, 3 * dA + A_HEADS), D ** -0.5)
    a_w_in = a_w_in.at[:, :, 2 * dA:3 * dA].multiply(beta)
    return {
        "x": nrm(ks[0], (BATCH, SEQ, D), 1.0),
        "a_w_in": a_w_in,
        "a_b_f": 2.0 + nrm(ks[2], (N_LAYERS_A, A_HEADS), 0.5),
        "a_w_out": nrm(ks[3], (N_LAYERS_A, dA, D), dA ** -0.5 * beta),
        "b_w_in": nrm(ks[4], (N_LAYERS_B, D, b_width), D ** -0.5),
        "b_q_norm": 1.0 + nrm(ks[5], (N_LAYERS_B, Q_LATENT), 0.02),
        "b_kv_norm": 1.0 + nrm(ks[6], (N_LAYERS_B, KV_LATENT), 0.02),
        "b_w_uq": nrm(ks[7], (N_LAYERS_B, Q_LATENT, B_HEADS * B_HEAD_DIM), Q_LATENT ** -0.5),
        "b_w_iq": nrm(ks[8], (N_LAYERS_B, Q_LATENT, IDX_HEADS * IDX_DIM), Q_LATENT ** -0.5),
        "b_w_uk": nrm(ks[9], (N_LAYERS_B, B_HEADS, KV_LATENT, B_HEAD_DIM), KV_LATENT ** -0.5),
        "b_w_uv": nrm(ks[10], (N_LAYERS_B, B_HEADS, KV_LATENT, B_V_DIM), KV_LATENT ** -0.5 * beta),
        "b_w_out": nrm(ks[11], (N_LAYERS_B, B_HEADS * B_V_DIM, D), (B_HEADS * B_V_DIM) ** -0.5 * beta),
        "rel_bias": nrm(ks[12], (REL_BUCKETS, B_HEADS), 0.5),
        "ln1_g": 1.0 + nrm(ks[13], (DEPTH, D), 0.02),
        "ln1_b": nrm(ks[14], (DEPTH, D), 0.02),
        "ln2_g": 1.0 + nrm(ks[15], (DEPTH, D), 0.02),
        "ln2_b": nrm(ks[16], (DEPTH, D), 0.02),
        "router_w": nrm(ks[17], (DEPTH, D, N_EXPERTS), D ** -0.5),
        "router_b": nrm(ks[18], (DEPTH, N_EXPERTS), 0.01),
        "w_gate": nrm(ks[19], (DEPTH, N_EXPERTS, D, D_EXPERT), D ** -0.5),
        "w_up": nrm(ks[20], (DEPTH, N_EXPERTS, D, D_EXPERT), D ** -0.5 * beta),
        "w_down": nrm(ks[21], (DEPTH, N_EXPERTS, D_EXPERT, D), D_EXPERT ** -0.5 * beta),
        "sh_gate": nrm(ks[22], (DEPTH, D, D_SHARED), D ** -0.5),
        "sh_up": nrm(ks[23], (DEPTH, D, D_SHARED), D ** -0.5 * beta),
        "sh_down": nrm(ks[24], (DEPTH, D_SHARED, D), D_SHARED ** -0.5 * beta),
    }


def reference(x, a_w_in, a_b_f, a_w_out, b_w_in, b_q_norm, b_kv_norm, b_w_uq, b_w_iq, b_w_uk,
              b_w_uv, b_w_out, rel_bias, ln1_g, ln1_b, ln2_g, ln2_b, router_w, router_b,
              w_gate, w_up, w_down, sh_gate, sh_up, sh_down):
    for i in range(DEPTH):
        j = i // N_MIXERS
        if i % N_MIXERS == 0:
            m = forgetting_attention(x, a_w_in[j], a_b_f[j], a_w_out[j])
        else:
            m = indexed_sparse_attention(x, b_w_in[j], b_q_norm[j], b_kv_norm[j], b_w_uq[j],
                                         b_w_iq[j], b_w_uk[j], b_w_uv[j], b_w_out[j], rel_bias)
        x = layer_norm(DEEPNORM_ALPHA * x + m, ln1_g[i], ln1_b[i])
        f = moe_ffn(x, router_w[i], router_b[i], w_gate[i], w_up[i], w_down[i],
                    sh_gate[i], sh_up[i], sh_down[i])
        x = layer_norm(DEEPNORM_ALPHA * x + f, ln2_g[i], ln2_b[i])
    return x
```

```python
import functools
import math

import numpy as np
import jax
import jax.numpy as jnp
from jax import lax
from jax.experimental import pallas as pl
from jax.experimental.pallas import tpu as pltpu

BF16 = jnp.bfloat16
F32 = jnp.float32
I32 = jnp.int32

A_HEADS = 16
A_HEAD_DIM = 128
B_HEADS = 16
B_HEAD_DIM = 128
B_V_DIM = 128
IDX_HEADS = 16
IDX_DIM = 64
IDX_TOPK = 256
CHUNK = 64
REL_BUCKETS = 32
REL_MAX_DIST = 128
N_EXPERTS = 64
TOP_K = 8
ROUTED_SCALE = 2.5
DEPTH = 2
DEEPNORM_ALPHA = (2 * DEPTH) ** 0.25
LN_EPS = 1e-5
RMS_EPS = 1e-6

LANES = 128
VMEM_LIMIT = 56 * 1024 * 1024
NEG_BIG = -1e30

EXPERT_ROWS = 256
FOX_TQ = 512
FOX_TK = 512
DSA_T = 256
IDX_TQ = 128
IDX_TKC = 512


def _cparams(sem, vmem=VMEM_LIMIT):
    return pltpu.CompilerParams(dimension_semantics=sem, vmem_limit_bytes=vmem)


def _dot(a, b):
    return jnp.dot(a, b, preferred_element_type=F32)


def _dot_nt(a, b):
    return lax.dot_general(a, b, (((1,), (1,)), ((), ())), preferred_element_type=F32)


def _split2(a):
    hi = a.astype(BF16)
    lo = (a - hi.astype(F32)).astype(BF16)
    return hi, lo


def _split3(a):
    hi = a.astype(BF16)
    r = a - hi.astype(F32)
    mid = r.astype(BF16)
    lo = (r - mid.astype(F32)).astype(BF16)
    return hi, mid, lo


def _dot_x3(a, b):
    ah, al = _split2(a)
    bh, bl = _split2(b)
    return _dot(ah, bh) + (_dot(ah, bl) + _dot(al, bh))


def _layer_norm_rows(z, g, b):
    mu = jnp.mean(z, axis=-1, keepdims=True)
    d = z - mu
    var = jnp.mean(d * d, axis=-1, keepdims=True)
    return d * lax.rsqrt(var + LN_EPS) * g + b


def _rms_norm_rows(z, g):
    ms = jnp.mean(z * z, axis=-1, keepdims=True)
    return z * lax.rsqrt(ms + RMS_EPS) * g


def _proj0_kernel(x_ref, w_ref, wf_ref, qkv_ref, fl_ref, xb_sc, *, n_q_blocks, q_scale):
    j = pl.program_id(1)

    @pl.when(j == 0)
    def _():
        x = x_ref[...]
        xb_sc[...] = x.astype(BF16)
        fl_ref[...] = _dot_x3(x, wf_ref[...])

    acc = _dot(xb_sc[...], w_ref[...])
    scale = jnp.where(j < n_q_blocks, q_scale, 1.0).astype(F32)
    qkv_ref[...] = (acc * scale).astype(BF16)


def _proj0(x2, w_qkv_b, w_f):
    n, d = x2.shape
    nout = w_qkv_b.shape[1]
    tm, tn = 1024, 512
    dq = A_HEADS * A_HEAD_DIM
    kern = functools.partial(_proj0_kernel, n_q_blocks=dq // tn, q_scale=A_HEAD_DIM ** -0.5)
    return pl.pallas_call(
        kern,
        out_shape=(jax.ShapeDtypeStruct((n, nout), BF16),
                   jax.ShapeDtypeStruct((n, A_HEADS), F32)),
        grid=(n // tm, nout // tn),
        in_specs=[pl.BlockSpec((tm, d), lambda i, j: (i, 0)),
                  pl.BlockSpec((d, tn), lambda i, j: (0, j)),
                  pl.BlockSpec((d, A_HEADS), lambda i, j: (0, 0))],
        out_specs=(pl.BlockSpec((tm, tn), lambda i, j: (i, j)),
                   pl.BlockSpec((tm, A_HEADS), lambda i, j: (i, 0))),
        scratch_shapes=[pltpu.VMEM((tm, d), BF16)],
        compiler_params=_cparams(("arbitrary", "arbitrary")),
        name="fox_proj",
    )(x2, w_qkv_b, w_f)


def _forget_cumsum_kernel(fl_ref, bf_ref, f_ref, carry_sc, *, t):
    @pl.when(pl.program_id(1) == 0)
    def _():
        carry_sc[...] = jnp.zeros_like(carry_sc)

    z = fl_ref[...] + bf_ref[...]
    logf = jnp.minimum(z, 0.0) - jnp.log1p(jnp.exp(-jnp.abs(z)))
    row = lax.broadcasted_iota(I32, (t, t), 0)
    col = lax.broadcasted_iota(I32, (t, t), 1)
    tri = (col <= row).astype(BF16)
    hi, mid, lo = _split3(logf)
    cs = _dot(tri, hi) + (_dot(tri, mid) + _dot(tri, lo)) + carry_sc[...]
    f_ref[...] = cs
    carry_sc[...] = cs[t - 1:t, :]


def _forget_cumsum(fl, b_f, batch, seq):
    t = 256
    nb = seq // t
    return pl.pallas_call(
        functools.partial(_forget_cumsum_kernel, t=t),
        out_shape=jax.ShapeDtypeStruct(fl.shape, F32),
        grid=(batch, nb),
        in_specs=[pl.BlockSpec((t, A_HEADS), lambda b, i: (b * nb + i, 0)),
                  pl.BlockSpec((1, A_HEADS), lambda b, i: (0, 0))],
        out_specs=pl.BlockSpec((t, A_HEADS), lambda b, i: (b * nb + i, 0)),
        scratch_shapes=[pltpu.VMEM((1, A_HEADS), F32)],
        compiler_params=_cparams(("arbitrary", "arbitrary")),
        name="fox_forget_cumsum",
    )(fl, b_f.reshape(1, A_HEADS))


def _fox_attn_kernel(qi_ref, kj_ref, q_ref, k_ref, v_ref, fk_ref, o_ref, m_sc, l_sc, acc_sc, *, tq, tk):
    p = pl.program_id(2)
    i = qi_ref[p]
    j = kj_ref[p]

    @pl.when(j == 0)
    def _():
        m_sc[...] = jnp.full_like(m_sc, -jnp.inf)
        l_sc[...] = jnp.zeros_like(l_sc)
        acc_sc[...] = jnp.zeros_like(acc_sc)

    def step(diag):
        s = _dot_nt(q_ref[...], k_ref[...]) - fk_ref[0]
        if diag:
            row = lax.broadcasted_iota(I32, (tq, tk), 0)
            col = lax.broadcasted_iota(I32, (tq, tk), 1)
            s = jnp.where(col <= row, s, -jnp.inf)
        m_prev = m_sc[...]
        m_new = jnp.maximum(m_prev, jnp.max(s, axis=1, keepdims=True))
        alpha = jnp.exp(m_prev - m_new)
        pexp = jnp.exp(s - m_new[:, :1])
        l_sc[...] = alpha * l_sc[...] + jnp.sum(pexp, axis=1, keepdims=True)
        acc_sc[...] = alpha * acc_sc[...] + _dot(pexp.astype(BF16), v_ref[...])
        m_sc[...] = m_new

    @pl.when(j < i)
    def _():
        step(False)

    @pl.when(j == i)
    def _():
        step(True)
        o_ref[...] = (acc_sc[...] / l_sc[...]).astype(o_ref.dtype)


def _fox_attention(qkv, f_rows, batch, seq):
    t = min(FOX_TQ, seq)
    nq = seq // t
    qi = np.concatenate([np.full(i + 1, i) for i in range(nq)]).astype(np.int32)
    kj = np.concatenate([np.arange(i + 1) for i in range(nq)]).astype(np.int32)
    h_ = A_HEADS
    dh = A_HEAD_DIM
    kern = functools.partial(_fox_attn_kernel, tq=t, tk=t)
    grid_spec = pltpu.PrefetchScalarGridSpec(
        num_scalar_prefetch=2,
        grid=(batch, h_, len(qi)),
        in_specs=[
            pl.BlockSpec((t, dh), lambda b, h, p, qi, kj: (b * nq + qi[p], h)),
            pl.BlockSpec((t, dh), lambda b, h, p, qi, kj: (b * nq + kj[p], h_ + h)),
            pl.BlockSpec((t, dh), lambda b, h, p, qi, kj: (b * nq + kj[p], 2 * h_ + h)),
            pl.BlockSpec((1, 1, t), lambda b, h, p, qi, kj: (b * h_ + h, 0, kj[p])),
        ],
        out_specs=pl.BlockSpec((t, dh), lambda b, h, p, qi, kj: (b * nq + qi[p], h)),
        scratch_shapes=[pltpu.VMEM((t, dh), F32), pltpu.VMEM((t, dh), F32), pltpu.VMEM((t, dh), F32)],
    )
    return pl.pallas_call(
        kern,
        out_shape=jax.ShapeDtypeStruct((batch * seq, h_ * dh), BF16),
        grid_spec=grid_spec,
        compiler_params=_cparams(("arbitrary", "arbitrary", "arbitrary")),
        name="fox_attention",
    )(jnp.asarray(qi), jnp.asarray(kj), qkv, qkv, qkv, f_rows)


def _outproj_ln_kernel(o_ref, w_ref, x_ref, g_ref, b_ref, y_ref):
    z = DEEPNORM_ALPHA * x_ref[...] + _dot(o_ref[...], w_ref[...])
    y_ref[...] = _layer_norm_rows(z, g_ref[...], b_ref[...])


def _outproj_ln(o, w_b, x2, g, b):
    n, d = x2.shape
    k = o.shape[1]
    tm = 512
    return pl.pallas_call(
        _outproj_ln_kernel,
        out_shape=jax.ShapeDtypeStruct((n, d), F32),
        grid=(n // tm,),
        in_specs=[pl.BlockSpec((tm, k), lambda i: (i, 0)),
                  pl.BlockSpec((k, d), lambda i: (0, 0)),
                  pl.BlockSpec((tm, d), lambda i: (i, 0)),
                  pl.BlockSpec((1, d), lambda i: (0, 0)),
                  pl.BlockSpec((1, d), lambda i: (0, 0))],
        out_specs=pl.BlockSpec((tm, d), lambda i: (i, 0)),
        compiler_params=_cparams(("arbitrary",)),
        name="outproj_deepnorm",
    )(o, w_b, x2, g.reshape(1, d), b.reshape(1, d))


def _fox_layer(x2, w_in, b_f, w_out, ln_g, ln_b, batch, seq):
    dq = A_HEADS * A_HEAD_DIM
    qkv, fl = _proj0(x2, w_in[:, :3 * dq].astype(BF16), w_in[:, 3 * dq:])
    f = _forget_cumsum(fl, b_f, batch, seq)
    f_rows = f.reshape(batch, seq, A_HEADS).transpose(0, 2, 1).reshape(batch * A_HEADS, 1, seq)
    o = _fox_attention(qkv, f_rows, batch, seq)
    return _outproj_ln(o, w_out.astype(BF16), x2, ln_g, ln_b)


def _router_kernel(x_ref, rw_ref, rb_ref, sel_ref, gate_ref, rank_ref, cnt_ref, carry_sc, *, tm):
    @pl.when(pl.program_id(0) == 0)
    def _():
        carry_sc[...] = jnp.zeros_like(carry_sc)

    e = N_EXPERTS
    scores = jax.nn.sigmoid(_dot_x3(x_ref[...], rw_ref[...]))
    lane = lax.broadcasted_iota(I32, (tm, e), 1)
    slot = lax.broadcasted_iota(I32, (tm, TOP_K), 1)
    work = scores + rb_ref[...]
    chosen = jnp.zeros((tm, e), F32)
    sel = jnp.zeros((tm, TOP_K), I32)
    gate = jnp.zeros((tm, TOP_K), F32)
    idxs = []
    for k in range(TOP_K):
        mx = jnp.max(work, axis=1, keepdims=True)
        idx = jnp.min(jnp.where(work == mx, lane, e), axis=1, keepdims=True)
        hit = lane == idx
        gk = jnp.sum(jnp.where(hit, scores, 0.0), axis=1, keepdims=True)
        work = jnp.where(hit, -jnp.inf, work)
        chosen = jnp.where(hit, 1.0, chosen)
        sel = jnp.where(slot == k, idx, sel)
        gate = jnp.where(slot == k, gk, gate)
        idxs.append(idx)
    gate = gate / jnp.sum(gate, axis=1, keepdims=True) * ROUTED_SCALE

    row = lax.broadcasted_iota(I32, (tm, tm), 0)
    col = lax.broadcasted_iota(I32, (tm, tm), 1)
    before = _dot((col < row).astype(BF16), chosen.astype(BF16)) + carry_sc[...]
    rank = jnp.zeros((tm, TOP_K), F32)
    for k in range(TOP_K):
        rk = jnp.sum(jnp.where(lane == idxs[k], before, 0.0), axis=1, keepdims=True)
        rank = jnp.where(slot == k, rk, rank)
    total = carry_sc[...] + jnp.sum(chosen, axis=0, keepdims=True)
    carry_sc[...] = total
    sel_ref[...] = sel
    gate_ref[...] = gate
    rank_ref[...] = rank.astype(I32)
    cnt_ref[...] = total.astype(I32)


def _router(x2, rw, rb):
    n, d = x2.shape
    tm = 512
    e = N_EXPERTS
    return pl.pallas_call(
        functools.partial(_router_kernel, tm=tm),
        out_shape=(jax.ShapeDtypeStruct((n, TOP_K), I32),
                   jax.ShapeDtypeStruct((n, TOP_K), F32),
                   jax.ShapeDtypeStruct((n, TOP_K), I32),
                   jax.ShapeDtypeStruct((1, e), I32)),
        grid=(n // tm,),
        in_specs=[pl.BlockSpec((tm, d), lambda i: (i, 0)),
                  pl.BlockSpec((d, e), lambda i: (0, 0)),
                  pl.BlockSpec((1, e), lambda i: (0, 0))],
        out_specs=(pl.BlockSpec((tm, TOP_K), lambda i: (i, 0)),
                   pl.BlockSpec((tm, TOP_K), lambda i: (i, 0)),
                   pl.BlockSpec((tm, TOP_K), lambda i: (i, 0)),
                   pl.BlockSpec((1, e), lambda i: (0, 0))),
        scratch_shapes=[pltpu.VMEM((1, e), F32)],
        compiler_params=_cparams(("arbitrary",)),
        name="moe_router",
    )(x2, rw, rb.reshape(1, e))


def _dispatch_kernel(pad_ref, dest_hbm, x_hbm, xs_hbm, dest_sm, zero_sc, sem_idx, sem_zero, sem_row, *, tt):
    i = pl.program_id(0)

    def zero_copy(e):
        return pltpu.make_async_copy(zero_sc, xs_hbm.at[pl.ds(pad_ref[e], EXPERT_ROWS)], sem_zero)

    @pl.when(i == 0)
    def _():
        zero_sc[...] = jnp.zeros_like(zero_sc)

        def start(e, c):
            @pl.when(pad_ref[N_EXPERTS + e] > 0)
            def _():
                zero_copy(e).start()
            return c

        def wait(e, c):
            @pl.when(pad_ref[N_EXPERTS + e] > 0)
            def _():
                zero_copy(e).wait()
            return c

        lax.fori_loop(0, N_EXPERTS, start, 0)
        lax.fori_loop(0, N_EXPERTS, wait, 0)

    idx_copy = pltpu.make_async_copy(dest_hbm.at[i], dest_sm, sem_idx)
    idx_copy.start()
    idx_copy.wait()

    def row_copy(r):
        t = i * tt + lax.shift_right_logical(r, 3)
        return pltpu.make_async_copy(x_hbm.at[t], xs_hbm.at[dest_sm[r]], sem_row)

    def start(r, c):
        row_copy(r).start()
        return c

    def wait(r, c):
        row_copy(r).wait()
        return c

    lax.fori_loop(0, tt * TOP_K, start, 0, unroll=8)
    lax.fori_loop(0, tt * TOP_K, wait, 0, unroll=8)


def _dispatch(pad_info, dest, x3, p_rows):
    n, s, l = x3.shape
    tt = 512
    assert TOP_K == 8
    dest2 = dest.reshape(n // tt, tt * TOP_K)
    grid_spec = pltpu.PrefetchScalarGridSpec(
        num_scalar_prefetch=1,
        grid=(n // tt,),
        in_specs=[pl.BlockSpec(memory_space=pl.ANY), pl.BlockSpec(memory_space=pl.ANY)],
        out_specs=pl.BlockSpec(memory_space=pl.ANY),
        scratch_shapes=[pltpu.SMEM((tt * TOP_K,), I32),
                        pltpu.VMEM((EXPERT_ROWS, s, l), x3.dtype),
                        pltpu.SemaphoreType.DMA, pltpu.SemaphoreType.DMA, pltpu.SemaphoreType.DMA],
    )
    return pl.pallas_call(
        functools.partial(_dispatch_kernel, tt=tt),
        out_shape=jax.ShapeDtypeStruct((p_rows, s, l), x3.dtype),
        grid_spec=grid_spec,
        compiler_params=_cparams(("arbitrary",)),
        name="moe_dispatch",
    )(pad_info, dest2, x3)


def _slab_to_rows(ref3):
    return jnp.concatenate([ref3[:, c, :] for c in range(ref3.shape[1])], axis=1)


def _rows_to_slab(ref3, val):
    for c in range(ref3.shape[1]):
        ref3[:, c, :] = val[:, c * LANES:(c + 1) * LANES]


def _expert_mlp_kernel(be_ref, nu_ref, xs_ref, wg_ref, wu_ref, wd_ref, ys_ref, wg_sc, wu_sc, wd_sc):
    b = pl.program_id(0)
    e = be_ref[b]
    first = jnp.logical_or(b == 0, e != be_ref[jnp.maximum(b - 1, 0)])
    active = b < nu_ref[0]

    @pl.when(jnp.logical_and(active, first))
    def _():
        wg_sc[...] = wg_ref[...].astype(BF16)
        wu_sc[...] = wu_ref[...].astype(BF16)
        wd_sc[...] = wd_ref[...].astype(BF16)

    @pl.when(active)
    def _():
        x = _slab_to_rows(xs_ref).astype(BF16)
        g = _dot(x, wg_sc[...])
        u = _dot(x, wu_sc[...])
        h = (g * jax.nn.sigmoid(g) * u).astype(BF16)
        _rows_to_slab(ys_ref, _dot(h, wd_sc[...]))


def _expert_mlp(blk_e, n_used, xs3, w_gate, w_up, w_down):
    p_rows, s, l = xs3.shape
    d = s * l
    de = w_gate.shape[2]
    r = EXPERT_ROWS
    nblk = p_rows // r

    def row_map(b, be, nu):
        return (jnp.minimum(b, nu[0] - 1), 0, 0)

    grid_spec = pltpu.PrefetchScalarGridSpec(
        num_scalar_prefetch=2,
        grid=(nblk,),
        in_specs=[pl.BlockSpec((r, s, l), row_map),
                  pl.BlockSpec((None, d, de), lambda b, be, nu: (be[b], 0, 0)),
                  pl.BlockSpec((None, d, de), lambda b, be, nu: (be[b], 0, 0)),
                  pl.BlockSpec((None, de, d), lambda b, be, nu: (be[b], 0, 0))],
        out_specs=pl.BlockSpec((r, s, l), row_map),
        scratch_shapes=[pltpu.VMEM((d, de), BF16), pltpu.VMEM((d, de), BF16), pltpu.VMEM((de, d), BF16)],
    )
    return pl.pallas_call(
        _expert_mlp_kernel,
        out_shape=jax.ShapeDtypeStruct((p_rows, s, l), F32),
        grid_spec=grid_spec,
        compiler_params=_cparams(("arbitrary",)),
        name="moe_experts",
    )(blk_e, n_used, xs3, w_gate, w_up, w_down)


def _combine_kernel(dest_hbm, ys_hbm, x_ref, gate_ref, sg_ref, su_ref, sd_ref, g_ref, b_ref,
                    y_ref, dest_sm, rows_sc, sem_idx, sem_row, *, tt):
    i = pl.program_id(0)
    idx_copy = pltpu.make_async_copy(dest_hbm.at[i], dest_sm, sem_idx)
    idx_copy.start()
    idx_copy.wait()

    def row_copy(r):
        t = lax.shift_right_logical(r, 3)
        k = jnp.bitwise_and(r, TOP_K - 1)
        return pltpu.make_async_copy(ys_hbm.at[dest_sm[r]], rows_sc.at[k, t], sem_row)

    def start(r, c):
        row_copy(r).start()
        return c

    def wait(r, c):
        row_copy(r).wait()
        return c

    lax.fori_loop(0, tt * TOP_K, start, 0, unroll=8)

    x = x_ref[...]
    xb = x.astype(BF16)
    g = _dot(xb, sg_ref[...])
    u = _dot(xb, su_ref[...])
    h = (g * jax.nn.sigmoid(g) * u).astype(BF16)
    z = DEEPNORM_ALPHA * x + _dot(h, sd_ref[...])

    lax.fori_loop(0, tt * TOP_K, wait, 0, unroll=8)
    gate = gate_ref[...]
    pieces = []
    for c in range(rows_sc.shape[2]):
        acc = gate[:, 0:1] * rows_sc[0, :, c, :]
        for k in range(1, TOP_K):
            acc = acc + gate[:, k:k + 1] * rows_sc[k, :, c, :]
        pieces.append(acc)
    routed = jnp.concatenate(pieces, axis=1)
    y_ref[...] = _layer_norm_rows(z + routed, g_ref[...], b_ref[...])


def _combine(dest, ys3, x2, gate, sg_b, su_b, sd_b, ln_g, ln_b):
    n, d = x2.shape
    ds_ = sg_b.shape[1]
    _, s, l = ys3.shape
    tt = 256
    assert TOP_K == 8
    dest2 = dest.reshape(n // tt, tt * TOP_K)
    return pl.pallas_call(
        functools.partial(_combine_kernel, tt=tt),
        out_shape=jax.ShapeDtypeStruct((n, d), F32),
        grid=(n // tt,),
        in_specs=[pl.BlockSpec(memory_space=pl.ANY),
                  pl.BlockSpec(memory_space=pl.ANY),
                  pl.BlockSpec((tt, d), lambda i: (i, 0)),
                  pl.BlockSpec((tt, TOP_K), lambda i: (i, 0)),
                  pl.BlockSpec((d, ds_), lambda i: (0, 0)),
                  pl.BlockSpec((d, ds_), lambda i: (0, 0)),
                  pl.BlockSpec((ds_, d), lambda i: (0, 0)),
                  pl.BlockSpec((1, d), lambda i: (0, 0)),
                  pl.BlockSpec((1, d), lambda i: (0, 0))],
        out_specs=pl.BlockSpec((tt, d), lambda i: (i, 0)),
        scratch_shapes=[pltpu.SMEM((tt * TOP_K,), I32),
                        pltpu.VMEM((TOP_K, tt, s, l), F32),
                        pltpu.SemaphoreType.DMA, pltpu.SemaphoreType.DMA],
        compiler_params=_cparams(("arbitrary",)),
        name="moe_combine",
    )(dest2, ys3, x2, gate, sg_b, su_b, sd_b, ln_g.reshape(1, d), ln_b.reshape(1, d))


def _moe_layer(x2, rw, rb, w_gate, w_up, w_down, sh_gate, sh_up, sh_down, ln_g, ln_b):
    n, d = x2.shape
    e = N_EXPERTS
    r = EXPERT_ROWS
    sel, gate, rank, counts = _router(x2, rw, rb)
    counts = counts.reshape(e)
    padded = (counts + r - 1) // r * r
    pend = jnp.cumsum(padded)
    pstart = pend - padded
    dest = rank + jnp.sum(jnp.where(sel[..., None] == jnp.arange(e, dtype=I32), pstart.astype(I32), 0), axis=-1)
    p_rows = n * TOP_K + e * r
    nblk = p_rows // r
    blk_e = jnp.minimum(jnp.searchsorted(pend, jnp.arange(nblk, dtype=I32) * r, side='right'), e - 1).astype(I32)
    n_used = (pend[-1] // r).astype(I32).reshape(1)
    pad_info = jnp.concatenate([jnp.maximum(pend - r, 0), padded]).astype(I32)
    xs = _dispatch(pad_info, dest.astype(I32), x2.reshape(n, d // LANES, LANES), p_rows)
    ys = _expert_mlp(blk_e, n_used, xs, w_gate, w_up, w_down)
    return _combine(dest.astype(I32), ys, x2, gate, sh_gate.astype(BF16), sh_up.astype(BF16),
                    sh_down.astype(BF16), ln_g, ln_b)


FAR_BUCKET = REL_BUCKETS // 2 - 1


def _t5_bucket(rel):
    half = REL_BUCKETS // 2
    max_exact = half // 2
    n = jnp.abs(rel)
    large = max_exact + (jnp.log(jnp.maximum(n, 1).astype(F32) / max_exact)
                         / math.log(REL_MAX_DIST / max_exact) * (half - max_exact)).astype(I32)
    large = jnp.minimum(large, half - 1)
    return jnp.where(rel > 0, half, 0) + jnp.where(n < max_exact, n, large)


def _bias_tiles_kernel(rb_ref, out_ref, *, t):
    d = pl.program_id(0) - 1
    row = lax.broadcasted_iota(I32, (t, t), 0)
    col = lax.broadcasted_iota(I32, (t, t), 1)
    bucket = _t5_bucket(d * t + col - row)
    for h in range(B_HEADS):
        far = rb_ref[FAR_BUCKET * B_HEADS + h]
        acc = jnp.zeros((t, t), F32)
        for b in range(REL_BUCKETS):
            acc = jnp.where(bucket == b, rb_ref[b * B_HEADS + h] - far, acc)
        out_ref[0, h] = acc


def _bias_tiles(rel_bias, t):
    assert t > REL_MAX_DIST
    grid_spec = pltpu.PrefetchScalarGridSpec(
        num_scalar_prefetch=1,
        grid=(2,),
        in_specs=[],
        out_specs=pl.BlockSpec((1, B_HEADS, t, t), lambda i, rb: (i, 0, 0, 0)),
    )
    return pl.pallas_call(
        functools.partial(_bias_tiles_kernel, t=t),
        out_shape=jax.ShapeDtypeStruct((2, B_HEADS, t, t), F32),
        grid_spec=grid_spec,
        compiler_params=_cparams(("arbitrary",)),
        name="dsa_bias_tiles",
    )(rel_bias.reshape(-1))


def _proj1_kernel(x_ref, w_ref, qn_ref, kvn_ref, cq_ref, ckv_ref, ki_ref, wi_ref, *, o1, o2, o3, wscale):
    proj = _dot(x_ref[...].astype(BF16), w_ref[...])
    cq_ref[...] = _rms_norm_rows(proj[:, :o1], qn_ref[...]).astype(BF16)
    ckv_ref[...] = _rms_norm_rows(proj[:, o1:o2], kvn_ref[...]).astype(BF16)
    ki_ref[...] = proj[:, o2:o3].astype(BF16)
    wi_ref[...] = proj[:, o3:] * wscale


def _proj1(x2, w_b, q_norm, kv_norm, ql, kvl):
    n, d = x2.shape
    wout = w_b.shape[1]
    o1, o2, o3 = ql, ql + kvl, ql + kvl + IDX_DIM
    tm = 512
    kern = functools.partial(_proj1_kernel, o1=o1, o2=o2, o3=o3,
                             wscale=(IDX_HEADS ** -0.5) * (IDX_DIM ** -0.5))
    return pl.pallas_call(
        kern,
        out_shape=(jax.ShapeDtypeStruct((n, ql), BF16), jax.ShapeDtypeStruct((n, kvl), BF16),
                   jax.ShapeDtypeStruct((n, IDX_DIM), BF16), jax.ShapeDtypeStruct((n, IDX_HEADS), F32)),
        grid=(n // tm,),
        in_specs=[pl.BlockSpec((tm, d), lambda i: (i, 0)),
                  pl.BlockSpec((d, wout), lambda i: (0, 0)),
                  pl.BlockSpec((1, ql), lambda i: (0, 0)),
                  pl.BlockSpec((1, kvl), lambda i: (0, 0))],
        out_specs=(pl.BlockSpec((tm, ql), lambda i: (i, 0)), pl.BlockSpec((tm, kvl), lambda i: (i, 0)),
                   pl.BlockSpec((tm, IDX_DIM), lambda i: (i, 0)), pl.BlockSpec((tm, IDX_HEADS), lambda i: (i, 0))),
        compiler_params=_cparams(("arbitrary",)),
        name="dsa_proj",
    )(x2, w_b, q_norm.reshape(1, ql), kv_norm.reshape(1, kvl))


def _qside_kernel(cq_ref, wuq_ref, wuk_ref, wiq_ref, ql_ref, qi_ref, *, kvl, scale):
    cq = cq_ref[...]
    q = _dot(cq, wuq_ref[...]).astype(BF16)
    for h in range(B_HEADS):
        qh = q[:, h * B_HEAD_DIM:(h + 1) * B_HEAD_DIM]
        ql_ref[:, h * kvl:(h + 1) * kvl] = (_dot_nt(qh, wuk_ref[h]) * scale).astype(BF16)
    qi_ref[...] = _dot(cq, wiq_ref[...]).astype(BF16)


def _qside(cq, w_uq_b, w_uk_b, w_iq_b):
    n, ql = cq.shape
    kvl = w_uk_b.shape[1]
    tm = 512
    kern = functools.partial(_qside_kernel, kvl=kvl, scale=B_HEAD_DIM ** -0.5)
    return pl.pallas_call(
        kern,
        out_shape=(jax.ShapeDtypeStruct((n, B_HEADS * kvl), BF16),
                   jax.ShapeDtypeStruct((n, IDX_HEADS * IDX_DIM), BF16)),
        grid=(n // tm,),
        in_specs=[pl.BlockSpec((tm, ql), lambda i: (i, 0)),
                  pl.BlockSpec(w_uq_b.shape, lambda i: (0, 0)),
                  pl.BlockSpec(w_uk_b.shape, lambda i: (0, 0, 0)),
                  pl.BlockSpec(w_iq_b.shape, lambda i: (0, 0))],
        out_specs=(pl.BlockSpec((tm, B_HEADS * kvl), lambda i: (i, 0)),
                   pl.BlockSpec((tm, IDX_HEADS * IDX_DIM), lambda i: (i, 0))),
        compiler_params=_cparams(("arbitrary",)),
        name="dsa_qside",
    )(cq, w_uq_b, w_uk_b, w_iq_b)


INT_MIN = -2 ** 31
KEY_NEG_INF = (0xFF800000 ^ 0x7FFFFFFF) - 2 ** 32


def _ordered_key(v):
    bits = lax.bitcast_convert_type(v, I32)
    return bits ^ (lax.shift_right_arithmetic(bits, 31) & 0x7FFFFFFF)


def _indexer_kernel(qi_ref, wi_ref, ki_ref, mask_ref, key_sc, *, tq, tkc, seq, topk):
    i = pl.program_id(1)
    n_chunks = (i * tq + tq + tkc - 1) // tkc
    row = lax.broadcasted_iota(I32, (tq, 1), 0)
    limit = (lax.shift_right_logical(i * tq + row, 6) + 1) * CHUNK
    w = wi_ref[...]
    key_sc[...] = jnp.full(key_sc.shape, KEY_NEG_INF, I32)

    def score_chunk(c, carry):
        start = pl.multiple_of(c * tkc, tkc)
        k = ki_ref[pl.ds(start, tkc), :]
        acc = jnp.zeros((tq, tkc), F32)
        for h in range(IDX_HEADS):
            sc = _dot_nt(qi_ref[:, h * IDX_DIM:(h + 1) * IDX_DIM], k)
            acc = acc + jnp.maximum(sc, 0.0) * w[:, h:h + 1]
        kpos = start + lax.broadcasted_iota(I32, (tq, tkc), 1)
        key_sc[:, pl.ds(start, tkc)] = _ordered_key(jnp.where(kpos < limit, acc, -jnp.inf))
        return carry

    lax.fori_loop(0, n_chunks, score_chunk, 0)

    def count(pred_fn):
        def body(c, acc):
            start = pl.multiple_of(c * tkc, tkc)
            hit = pred_fn(key_sc[:, pl.ds(start, tkc)]).astype(I32)
            part = hit[:, 0:LANES]
            for q in range(1, tkc // LANES):
                part = part + hit[:, q * LANES:(q + 1) * LANES]
            return acc + part
        acc = lax.fori_loop(0, n_chunks, body, jnp.zeros((tq, LANES), I32))
        return jnp.sum(acc, axis=1, keepdims=True)

    def bisect(it, prefix):
        cand = prefix + lax.shift_left(jnp.int32(1), 31 - it)
        cnt = count(lambda kk: kk >= cand)
        return jnp.where(cnt >= topk, cand, prefix)

    thr = lax.fori_loop(0, 32, bisect, jnp.full((tq, 1), INT_MIN, I32))
    n_gt = count(lambda kk: kk > thr)
    n_eq = count(lambda kk: kk == thr)
    need = topk - n_gt
    tie_break = jnp.max(jnp.where(jnp.logical_and(thr > KEY_NEG_INF, n_eq > need), 1, 0)) > 0

    mask_ref[...] = jnp.full(mask_ref.shape, NEG_BIG, F32)

    @pl.when(jnp.logical_not(tie_break))
    def _():
        def write(c, carry):
            start = pl.multiple_of(c * tkc, tkc)
            kk = key_sc[:, pl.ds(start, tkc)]
            sel = jnp.logical_and(kk >= thr, kk > KEY_NEG_INF)
            mask_ref[:, pl.ds(start, tkc)] = jnp.where(sel, 0.0, NEG_BIG)
            return carry
        lax.fori_loop(0, n_chunks, write, 0)

    @pl.when(tie_break)
    def _():
        r_ = lax.broadcasted_iota(I32, (LANES, LANES), 0)
        c_ = lax.broadcasted_iota(I32, (LANES, LANES), 1)
        upper = (r_ < c_).astype(BF16)

        def write(c, seen):
            start = pl.multiple_of(c * LANES, LANES)
            kk = key_sc[:, pl.ds(start, LANES)]
            eq = kk == thr
            before = seen + _dot(eq.astype(BF16), upper)
            sel = jnp.logical_or(kk > thr, jnp.logical_and(eq, before < need.astype(F32)))
            sel = jnp.logical_and(sel, kk > KEY_NEG_INF)
            mask_ref[:, pl.ds(start, LANES)] = jnp.where(sel, 0.0, NEG_BIG)
            return seen + jnp.sum(eq.astype(F32), axis=1, keepdims=True)
        lax.fori_loop(0, n_chunks * (tkc // LANES), write, jnp.zeros((tq, 1), F32))


def _indexer(qidx, widx, kidx, batch, seq, topk):
    tq = IDX_TQ
    tkc = min(IDX_TKC, seq)
    nq = seq // tq
    kern = functools.partial(_indexer_kernel, tq=tq, tkc=tkc, seq=seq, topk=topk)
    return pl.pallas_call(
        kern,
        out_shape=jax.ShapeDtypeStruct((batch * seq, seq), F32),
        grid=(batch, nq),
        in_specs=[pl.BlockSpec((tq, IDX_HEADS * IDX_DIM), lambda b, i: (b * nq + i, 0)),
                  pl.BlockSpec((tq, IDX_HEADS), lambda b, i: (b * nq + i, 0)),
                  pl.BlockSpec((seq, IDX_DIM), lambda b, i: (b, 0))],
        out_specs=pl.BlockSpec((tq, seq), lambda b, i: (b * nq + i, 0)),
        scratch_shapes=[pltpu.VMEM((tq, seq), I32)],
        compiler_params=_cparams(("arbitrary", "arbitrary")),
        name="dsa_indexer",
    )(qidx, widx, kidx)


def _dsa_attn_kernel(qi_ref, kj_ref, ql_ref, ckv_ref, mask_ref, bias_ref, wuv_ref, o_ref,
                     m_sc, l_sc, acc_sc, *, t, kvl):
    p = pl.program_id(1)
    i = qi_ref[p]
    j = kj_ref[p]

    @pl.when(j == 0)
    def _():
        m_sc[...] = jnp.full_like(m_sc, -jnp.inf)
        l_sc[...] = jnp.zeros_like(l_sc)
        acc_sc[...] = jnp.zeros_like(acc_sc)

    def heads(near):
        kv = ckv_ref[...]
        msk = mask_ref[...]

        def body(h, carry):
            q = ql_ref[:, pl.ds(pl.multiple_of(h * kvl, kvl), kvl)]
            s = _dot_nt(q, kv) + msk
            if near:
                s = s + bias_ref[j - i + 1, h]
            m_prev = m_sc[h]
            m_new = jnp.maximum(m_prev, jnp.max(s, axis=1, keepdims=True))
            alpha = jnp.exp(m_prev - m_new)
            pexp = jnp.exp(s - m_new[:, :1])
            l_sc[h] = alpha * l_sc[h] + jnp.sum(pexp, axis=1, keepdims=True)
            acc_sc[h] = alpha[:, :1] * acc_sc[h] + _dot(pexp.astype(BF16), kv)
            m_sc[h] = m_new
            return carry

        lax.fori_loop(0, B_HEADS, body, 0)

    @pl.when(j < i - 1)
    def _():
        heads(False)

    @pl.when(j >= i - 1)
    def _():
        heads(True)

    @pl.when(j == i)
    def _():
        def out(h, carry):
            o_lat = (acc_sc[h] / l_sc[h][:, :1]).astype(BF16)
            o_ref[:, pl.ds(pl.multiple_of(h * B_V_DIM, B_V_DIM), B_V_DIM)] = (
                _dot(o_lat, wuv_ref[h]).astype(o_ref.dtype))
            return carry
        lax.fori_loop(0, B_HEADS, out, 0)


def _dsa_attention(ql, ckv, mask, bias_tiles, w_uv_b, batch, seq):
    t = DSA_T
    kvl = ckv.shape[1]
    nq = seq // t
    qi = np.concatenate([np.full(i + 1, i) for i in range(nq)]).astype(np.int32)
    kj = np.concatenate([np.arange(i + 1) for i in range(nq)]).astype(np.int32)
    kern = functools.partial(_dsa_attn_kernel, t=t, kvl=kvl)
    grid_spec = pltpu.PrefetchScalarGridSpec(
        num_scalar_prefetch=2,
        grid=(batch, len(qi)),
        in_specs=[
            pl.BlockSpec((t, B_HEADS * kvl), lambda b, p, qi, kj: (b * nq + qi[p], 0)),
            pl.BlockSpec((t, kvl), lambda b, p, qi, kj: (b * nq + kj[p], 0)),
            pl.BlockSpec((t, t), lambda b, p, qi, kj: (b * nq + qi[p], kj[p])),
            pl.BlockSpec(bias_tiles.shape, lambda b, p, qi, kj: (0, 0, 0, 0)),
            pl.BlockSpec(w_uv_b.shape, lambda b, p, qi, kj: (0, 0, 0)),
        ],
        out_specs=pl.BlockSpec((t, B_HEADS * B_V_DIM), lambda b, p, qi, kj: (b * nq + qi[p], 0)),
        scratch_shapes=[pltpu.VMEM((B_HEADS, t, LANES), F32), pltpu.VMEM((B_HEADS, t, LANES), F32),
                        pltpu.VMEM((B_HEADS, t, kvl), F32)],
    )
    return pl.pallas_call(
        kern,
        out_shape=jax.ShapeDtypeStruct((batch * seq, B_HEADS * B_V_DIM), BF16),
        grid_spec=grid_spec,
        compiler_params=_cparams(("arbitrary", "arbitrary")),
        name="dsa_attention",
    )(jnp.asarray(qi), jnp.asarray(kj), ql, ckv, mask, bias_tiles, w_uv_b)


def _dsa_layer(x2, w_in, q_norm, kv_norm, w_uq, w_iq, w_uk, w_uv, w_out, rel_bias, ln_g, ln_b, batch, seq):
    ql_dim = q_norm.shape[0]
    kvl = kv_norm.shape[0]
    topk = min(IDX_TOPK, seq // 4)
    cq, ckv, kidx, widx = _proj1(x2, w_in.astype(BF16), q_norm, kv_norm, ql_dim, kvl)
    ql, qidx = _qside(cq, w_uq.astype(BF16), w_uk.astype(BF16), w_iq.astype(BF16))
    mask = _indexer(qidx, widx, kidx, batch, seq, topk)
    bias_tiles = _bias_tiles(rel_bias, DSA_T)
    o = _dsa_attention(ql, ckv, mask, bias_tiles, w_uv.astype(BF16), batch, seq)
    return _outproj_ln(o, w_out.astype(BF16), x2, ln_g, ln_b)


def kernel(x, a_w_in, a_b_f, a_w_out, b_w_in, b_q_norm, b_kv_norm, b_w_uq, b_w_iq, b_w_uk, b_w_uv, b_w_out,
           rel_bias, ln1_g, ln1_b, ln2_g, ln2_b, router_w, router_b, w_gate, w_up, w_down, sh_gate, sh_up,
           sh_down):
    batch, seq, d = x.shape
    x2 = x.reshape(batch * seq, d)
    x2 = _fox_layer(x2, a_w_in[0], a_b_f[0], a_w_out[0], ln1_g[0], ln1_b[0], batch, seq)
    x2 = _moe_layer(x2, router_w[0], router_b[0], w_gate[0], w_up[0], w_down[0], sh_gate[0], sh_up[0],
                    sh_down[0], ln2_g[0], ln2_b[0])
    x2 = _dsa_layer(x2, b_w_in[0], b_q_norm[0], b_kv_norm[0], b_w_uq[0], b_w_iq[0], b_w_uk[0], b_w_uv[0],
                    b_w_out[0], rel_bias, ln1_g[1], ln1_b[1], batch, seq)
    x2 = _moe_layer(x2, router_w[1], router_b[1], w_gate[1], w_up[1], w_down[1], sh_gate[1], sh_up[1],
                    sh_down[1], ln2_g[1], ln2_b[1])
    return x2.reshape(batch, seq, d)
```

```python
import functools
import math

import numpy as np
import jax
import jax.numpy as jnp
from jax import lax
from jax.experimental import pallas as pl
from jax.experimental.pallas import tpu as pltpu

BF16 = jnp.bfloat16
F32 = jnp.float32
I32 = jnp.int32

A_HEADS = 16
A_HEAD_DIM = 128
B_HEADS = 16
B_HEAD_DIM = 128
B_V_DIM = 128
IDX_HEADS = 16
IDX_DIM = 64
IDX_TOPK = 256
CHUNK = 64
REL_BUCKETS = 32
REL_MAX_DIST = 128
N_EXPERTS = 64
TOP_K = 8
ROUTED_SCALE = 2.5
DEPTH = 2
DEEPNORM_ALPHA = (2 * DEPTH) ** 0.25
LN_EPS = 1e-5
RMS_EPS = 1e-6

LANES = 128
VMEM_LIMIT = 56 * 1024 * 1024
NEG_BIG = -1e30

EXPERT_ROWS = 256
FOX_TQ = 512
FOX_TK = 512
FOX_HEADS_PER_STEP = 2
DSA_TQ = 128
DSA_TK = 256
IDX_TQ = 128
IDX_TKC = 512


def _cparams(sem, vmem=VMEM_LIMIT):
    return pltpu.CompilerParams(dimension_semantics=sem, vmem_limit_bytes=vmem)


def _dot(a, b):
    return jnp.dot(a, b, preferred_element_type=F32)


def _dot_nt(a, b):
    return lax.dot_general(a, b, (((1,), (1,)), ((), ())), preferred_element_type=F32)


def _lane_tile(a, width):
    return a if width == LANES else jnp.concatenate([a] * (width // LANES), axis=1)


def _split2(a):
    hi = a.astype(BF16)
    lo = (a - hi.astype(F32)).astype(BF16)
    return hi, lo


def _split3(a):
    hi = a.astype(BF16)
    r = a - hi.astype(F32)
    mid = r.astype(BF16)
    lo = (r - mid.astype(F32)).astype(BF16)
    return hi, mid, lo


def _dot_x3(a, b):
    ah, al = _split2(a)
    bh, bl = _split2(b)
    return _dot(ah, bh) + (_dot(ah, bl) + _dot(al, bh))


def _layer_norm_rows(z, g, b):
    mu = jnp.mean(z, axis=-1, keepdims=True)
    d = z - mu
    var = jnp.mean(d * d, axis=-1, keepdims=True)
    return d * lax.rsqrt(var + LN_EPS) * g + b


def _rms_norm_rows(z, g):
    ms = jnp.mean(z * z, axis=-1, keepdims=True)
    return z * lax.rsqrt(ms + RMS_EPS) * g


def _proj0_kernel(x_ref, w_ref, wf_ref, qkv_ref, fl_ref, xb_sc, *, n_q_blocks, q_scale):
    j = pl.program_id(1)

    @pl.when(j == 0)
    def _():
        x = x_ref[...]
        xb_sc[...] = x.astype(BF16)
        fl_ref[...] = _dot_x3(x, wf_ref[...])

    acc = _dot(xb_sc[...], w_ref[...])
    scale = jnp.where(j < n_q_blocks, q_scale, 1.0).astype(F32)
    qkv_ref[...] = (acc * scale).astype(BF16)


def _proj0(x2, w_qkv_b, w_f):
    n, d = x2.shape
    nout = w_qkv_b.shape[1]
    tm, tn = 1024, 512
    dq = A_HEADS * A_HEAD_DIM
    kern = functools.partial(_proj0_kernel, n_q_blocks=dq // tn, q_scale=A_HEAD_DIM ** -0.5)
    return pl.pallas_call(
        kern,
        out_shape=(jax.ShapeDtypeStruct((n, nout), BF16),
                   jax.ShapeDtypeStruct((n, A_HEADS), F32)),
        grid=(n // tm, nout // tn),
        in_specs=[pl.BlockSpec((tm, d), lambda i, j: (i, 0)),
                  pl.BlockSpec((d, tn), lambda i, j: (0, j)),
                  pl.BlockSpec((d, A_HEADS), lambda i, j: (0, 0))],
        out_specs=(pl.BlockSpec((tm, tn), lambda i, j: (i, j)),
                   pl.BlockSpec((tm, A_HEADS), lambda i, j: (i, 0))),
        scratch_shapes=[pltpu.VMEM((tm, d), BF16)],
        compiler_params=_cparams(("arbitrary", "arbitrary")),
        name="fox_proj",
    )(x2, w_qkv_b, w_f)


def _forget_cumsum_kernel(fl_ref, bf_ref, f_ref, carry_sc, *, t):
    @pl.when(pl.program_id(1) == 0)
    def _():
        carry_sc[...] = jnp.zeros_like(carry_sc)

    z = fl_ref[...] + bf_ref[...]
    logf = jnp.minimum(z, 0.0) - jnp.log1p(jnp.exp(-jnp.abs(z)))
    row = lax.broadcasted_iota(I32, (t, t), 0)
    col = lax.broadcasted_iota(I32, (t, t), 1)
    tri = (col <= row).astype(BF16)
    hi, mid, lo = _split3(logf)
    cs = _dot(tri, hi) + (_dot(tri, mid) + _dot(tri, lo)) + carry_sc[...]
    f_ref[...] = cs
    carry_sc[...] = cs[t - 1:t, :]


def _forget_cumsum(fl, b_f, batch, seq):
    t = 256
    nb = seq // t
    return pl.pallas_call(
        functools.partial(_forget_cumsum_kernel, t=t),
        out_shape=jax.ShapeDtypeStruct(fl.shape, F32),
        grid=(batch, nb),
        in_specs=[pl.BlockSpec((t, A_HEADS), lambda b, i: (b * nb + i, 0)),
                  pl.BlockSpec((1, A_HEADS), lambda b, i: (0, 0))],
        out_specs=pl.BlockSpec((t, A_HEADS), lambda b, i: (b * nb + i, 0)),
        scratch_shapes=[pltpu.VMEM((1, A_HEADS), F32)],
        compiler_params=_cparams(("arbitrary", "arbitrary")),
        name="fox_forget_cumsum",
    )(fl, b_f.reshape(1, A_HEADS))


def _fox_attn_kernel(qi_ref, kj_ref, q_ref, k_ref, v_ref, fk_ref, o_ref, m_sc, l_sc, acc_sc, *, tq, tk, hp):
    p = pl.program_id(2)
    i = qi_ref[p]
    j = kj_ref[p]
    dh = A_HEAD_DIM

    @pl.when(j == 0)
    def _():
        m_sc[...] = jnp.full_like(m_sc, -jnp.inf)
        l_sc[...] = jnp.zeros_like(l_sc)
        acc_sc[...] = jnp.zeros_like(acc_sc)

    def step(diag):
        for hh in range(hp):
            cols = slice(hh * dh, (hh + 1) * dh)
            s = _dot_nt(q_ref[:, cols], k_ref[:, cols]) - fk_ref[hh]
            if diag:
                row = lax.broadcasted_iota(I32, (tq, tk), 0)
                col = lax.broadcasted_iota(I32, (tq, tk), 1)
                s = jnp.where(col <= row, s, -jnp.inf)
            m_prev = m_sc[hh]
            m_new = jnp.maximum(m_prev, jnp.max(s, axis=1, keepdims=True))
            alpha = jnp.exp(m_prev - m_new)
            pexp = jnp.exp(s - _lane_tile(m_new, tk))
            l_sc[hh] = alpha * l_sc[hh] + jnp.sum(pexp, axis=1, keepdims=True)
            acc_sc[hh] = alpha * acc_sc[hh] + _dot(pexp.astype(BF16), v_ref[:, cols])
            m_sc[hh] = m_new

    @pl.when(j < i)
    def _():
        step(False)

    @pl.when(j == i)
    def _():
        step(True)
        for hh in range(hp):
            o_ref[:, hh * dh:(hh + 1) * dh] = (acc_sc[hh] / l_sc[hh]).astype(o_ref.dtype)


def _fox_attention(qkv, f_rows, batch, seq):
    t = min(FOX_TQ, seq)
    nq = seq // t
    qi = np.concatenate([np.full(i + 1, i) for i in range(nq)]).astype(np.int32)
    kj = np.concatenate([np.arange(i + 1) for i in range(nq)]).astype(np.int32)
    hp = FOX_HEADS_PER_STEP
    hg = A_HEADS // hp
    dh = A_HEAD_DIM
    w = hp * dh
    kern = functools.partial(_fox_attn_kernel, tq=t, tk=t, hp=hp)
    grid_spec = pltpu.PrefetchScalarGridSpec(
        num_scalar_prefetch=2,
        grid=(batch, hg, len(qi)),
        in_specs=[
            pl.BlockSpec((t, w), lambda b, h, p, qi, kj: (b * nq + qi[p], h)),
            pl.BlockSpec((t, w), lambda b, h, p, qi, kj: (b * nq + kj[p], hg + h)),
            pl.BlockSpec((t, w), lambda b, h, p, qi, kj: (b * nq + kj[p], 2 * hg + h)),
            pl.BlockSpec((hp, 1, t), lambda b, h, p, qi, kj: (b * hg + h, 0, kj[p])),
        ],
        out_specs=pl.BlockSpec((t, w), lambda b, h, p, qi, kj: (b * nq + qi[p], h)),
        scratch_shapes=[pltpu.VMEM((hp, t, dh), F32), pltpu.VMEM((hp, t, dh), F32),
                        pltpu.VMEM((hp, t, dh), F32)],
    )
    return pl.pallas_call(
        kern,
        out_shape=jax.ShapeDtypeStruct((batch * seq, A_HEADS * dh), BF16),
        grid_spec=grid_spec,
        compiler_params=_cparams(("arbitrary", "arbitrary", "arbitrary")),
        name="fox_attention",
    )(jnp.asarray(qi), jnp.asarray(kj), qkv, qkv, qkv, f_rows)


def _outproj_ln_kernel(o_ref, w_ref, x_ref, g_ref, b_ref, y_ref):
    z = DEEPNORM_ALPHA * x_ref[...] + _dot(o_ref[...], w_ref[...])
    y_ref[...] = _layer_norm_rows(z, g_ref[...], b_ref[...])


def _outproj_ln(o, w_b, x2, g, b):
    n, d = x2.shape
    k = o.shape[1]
    tm = 512
    return pl.pallas_call(
        _outproj_ln_kernel,
        out_shape=jax.ShapeDtypeStruct((n, d), F32),
        grid=(n // tm,),
        in_specs=[pl.BlockSpec((tm, k), lambda i: (i, 0)),
                  pl.BlockSpec((k, d), lambda i: (0, 0)),
                  pl.BlockSpec((tm, d), lambda i: (i, 0)),
                  pl.BlockSpec((1, d), lambda i: (0, 0)),
                  pl.BlockSpec((1, d), lambda i: (0, 0))],
        out_specs=pl.BlockSpec((tm, d), lambda i: (i, 0)),
        compiler_params=_cparams(("arbitrary",)),
        name="outproj_deepnorm",
    )(o, w_b, x2, g.reshape(1, d), b.reshape(1, d))


def _fox_layer(x2, w_in, b_f, w_out, ln_g, ln_b, batch, seq):
    dq = A_HEADS * A_HEAD_DIM
    qkv, fl = _proj0(x2, w_in[:, :3 * dq].astype(BF16), w_in[:, 3 * dq:])
    f = _forget_cumsum(fl, b_f, batch, seq)
    f_rows = f.reshape(batch, seq, A_HEADS).transpose(0, 2, 1).reshape(batch * A_HEADS, 1, seq)
    o = _fox_attention(qkv, f_rows, batch, seq)
    return _outproj_ln(o, w_out.astype(BF16), x2, ln_g, ln_b)


def _router_kernel(x_ref, rw_ref, rb_ref, sel_ref, gate_ref, rank_ref, cnt_ref, carry_sc, *, tm):
    @pl.when(pl.program_id(0) == 0)
    def _():
        carry_sc[...] = jnp.zeros_like(carry_sc)

    e = N_EXPERTS
    scores = jax.nn.sigmoid(_dot_x3(x_ref[...], rw_ref[...]))
    lane = lax.broadcasted_iota(I32, (tm, e), 1)
    slot = lax.broadcasted_iota(I32, (tm, TOP_K), 1)
    work = scores + rb_ref[...]
    chosen = jnp.zeros((tm, e), F32)
    sel = jnp.zeros((tm, TOP_K), I32)
    gate = jnp.zeros((tm, TOP_K), F32)
    idxs = []
    for k in range(TOP_K):
        mx = jnp.max(work, axis=1, keepdims=True)
        idx = jnp.min(jnp.where(work == mx, lane, e), axis=1, keepdims=True)
        hit = lane == idx
        gk = jnp.sum(jnp.where(hit, scores, 0.0), axis=1, keepdims=True)
        work = jnp.where(hit, -jnp.inf, work)
        chosen = jnp.where(hit, 1.0, chosen)
        sel = jnp.where(slot == k, idx, sel)
        gate = jnp.where(slot == k, gk, gate)
        idxs.append(idx)
    gate = gate / jnp.sum(gate, axis=1, keepdims=True) * ROUTED_SCALE

    row = lax.broadcasted_iota(I32, (tm, tm), 0)
    col = lax.broadcasted_iota(I32, (tm, tm), 1)
    before = _dot((col < row).astype(BF16), chosen.astype(BF16)) + carry_sc[...]
    rank = jnp.zeros((tm, TOP_K), F32)
    for k in range(TOP_K):
        rk = jnp.sum(jnp.where(lane == idxs[k], before, 0.0), axis=1, keepdims=True)
        rank = jnp.where(slot == k, rk, rank)
    total = carry_sc[...] + jnp.sum(chosen, axis=0, keepdims=True)
    carry_sc[...] = total
    sel_ref[...] = sel
    gate_ref[...] = gate
    rank_ref[...] = rank.astype(I32)
    cnt_ref[...] = total.astype(I32)


def _router(x2, rw, rb):
    n, d = x2.shape
    tm = 512
    e = N_EXPERTS
    return pl.pallas_call(
        functools.partial(_router_kernel, tm=tm),
        out_shape=(jax.ShapeDtypeStruct((n, TOP_K), I32),
                   jax.ShapeDtypeStruct((n, TOP_K), F32),
                   jax.ShapeDtypeStruct((n, TOP_K), I32),
                   jax.ShapeDtypeStruct((1, e), I32)),
        grid=(n // tm,),
        in_specs=[pl.BlockSpec((tm, d), lambda i: (i, 0)),
                  pl.BlockSpec((d, e), lambda i: (0, 0)),
                  pl.BlockSpec((1, e), lambda i: (0, 0))],
        out_specs=(pl.BlockSpec((tm, TOP_K), lambda i: (i, 0)),
                   pl.BlockSpec((tm, TOP_K), lambda i: (i, 0)),
                   pl.BlockSpec((tm, TOP_K), lambda i: (i, 0)),
                   pl.BlockSpec((1, e), lambda i: (0, 0))),
        scratch_shapes=[pltpu.VMEM((1, e), F32)],
        compiler_params=_cparams(("arbitrary",)),
        name="moe_router",
    )(x2, rw, rb.reshape(1, e))


def _dispatch_kernel(pad_ref, dest_hbm, x_ref, xs_hbm, dest_sm, zero_sc, sem_idx, sem_zero, sem_row, *, tt):
    i = pl.program_id(0)

    def zero_copy(e):
        return pltpu.make_async_copy(zero_sc, xs_hbm.at[pl.ds(pad_ref[e], EXPERT_ROWS)], sem_zero)

    @pl.when(i == 0)
    def _():
        zero_sc[...] = jnp.zeros_like(zero_sc)

        def start(e, c):
            @pl.when(pad_ref[N_EXPERTS + e] > 0)
            def _():
                zero_copy(e).start()
            return c

        def wait(e, c):
            @pl.when(pad_ref[N_EXPERTS + e] > 0)
            def _():
                zero_copy(e).wait()
            return c

        lax.fori_loop(0, N_EXPERTS, start, 0)
        lax.fori_loop(0, N_EXPERTS, wait, 0)

    idx_copy = pltpu.make_async_copy(dest_hbm.at[i], dest_sm, sem_idx)
    idx_copy.start()
    idx_copy.wait()

    def row_copy(r):
        return pltpu.make_async_copy(x_ref.at[lax.shift_right_logical(r, 3)], xs_hbm.at[dest_sm[r]], sem_row)

    def start(r, c):
        row_copy(r).start()
        return c

    def wait(r, c):
        row_copy(r).wait()
        return c

    lax.fori_loop(0, tt * TOP_K, start, 0, unroll=8)
    lax.fori_loop(0, tt * TOP_K, wait, 0, unroll=8)


def _dispatch(pad_info, dest, x3, p_rows):
    n, s, l = x3.shape
    tt = 512
    assert TOP_K == 8
    dest2 = dest.reshape(n // tt, tt * TOP_K)
    grid_spec = pltpu.PrefetchScalarGridSpec(
        num_scalar_prefetch=1,
        grid=(n // tt,),
        in_specs=[pl.BlockSpec(memory_space=pl.ANY),
                  pl.BlockSpec((tt, s, l), lambda i, pad: (i, 0, 0))],
        out_specs=pl.BlockSpec(memory_space=pl.ANY),
        scratch_shapes=[pltpu.SMEM((tt * TOP_K,), I32),
                        pltpu.VMEM((EXPERT_ROWS, s, l), x3.dtype),
                        pltpu.SemaphoreType.DMA, pltpu.SemaphoreType.DMA, pltpu.SemaphoreType.DMA],
    )
    return pl.pallas_call(
        functools.partial(_dispatch_kernel, tt=tt),
        out_shape=jax.ShapeDtypeStruct((p_rows, s, l), x3.dtype),
        grid_spec=grid_spec,
        compiler_params=_cparams(("arbitrary",)),
        name="moe_dispatch",
    )(pad_info, dest2, x3)


def _slab_to_rows(ref3):
    return jnp.concatenate([ref3[:, c, :] for c in range(ref3.shape[1])], axis=1)


def _rows_to_slab(ref3, val):
    for c in range(ref3.shape[1]):
        ref3[:, c, :] = val[:, c * LANES:(c + 1) * LANES]


def _expert_mlp_kernel(be_ref, nu_ref, xs_ref, wg_ref, wu_ref, wd_ref, ys_ref, wg_sc, wu_sc, wd_sc):
    b = pl.program_id(0)
    e = be_ref[b]
    first = jnp.logical_or(b == 0, e != be_ref[jnp.maximum(b - 1, 0)])
    active = b < nu_ref[0]

    @pl.when(jnp.logical_and(active, first))
    def _():
        wg_sc[...] = wg_ref[...].astype(BF16)
        wu_sc[...] = wu_ref[...].astype(BF16)
        wd_sc[...] = wd_ref[...].astype(BF16)

    @pl.when(active)
    def _():
        x = _slab_to_rows(xs_ref).astype(BF16)
        g = _dot(x, wg_sc[...])
        u = _dot(x, wu_sc[...])
        h = (g * jax.nn.sigmoid(g) * u).astype(BF16)
        _rows_to_slab(ys_ref, _dot(h, wd_sc[...]))


def _expert_mlp(blk_e, n_used, xs3, w_gate, w_up, w_down):
    p_rows, s, l = xs3.shape
    d = s * l
    de = w_gate.shape[2]
    r = EXPERT_ROWS
    nblk = p_rows // r

    def row_map(b, be, nu):
        return (jnp.minimum(b, nu[0] - 1), 0, 0)

    grid_spec = pltpu.PrefetchScalarGridSpec(
        num_scalar_prefetch=2,
        grid=(nblk,),
        in_specs=[pl.BlockSpec((r, s, l), row_map),
                  pl.BlockSpec((None, d, de), lambda b, be, nu: (be[b], 0, 0)),
                  pl.BlockSpec((None, d, de), lambda b, be, nu: (be[b], 0, 0)),
                  pl.BlockSpec((None, de, d), lambda b, be, nu: (be[b], 0, 0))],
        out_specs=pl.BlockSpec((r, s, l), row_map),
        scratch_shapes=[pltpu.VMEM((d, de), BF16), pltpu.VMEM((d, de), BF16), pltpu.VMEM((de, d), BF16)],
    )
    return pl.pallas_call(
        _expert_mlp_kernel,
        out_shape=jax.ShapeDtypeStruct((p_rows, s, l), F32),
        grid_spec=grid_spec,
        compiler_params=_cparams(("arbitrary",)),
        name="moe_experts",
    )(blk_e, n_used, xs3, w_gate, w_up, w_down)


def _combine_kernel(dest_hbm, ys_hbm, x_ref, gate_ref, sg_ref, su_ref, sd_ref, g_ref, b_ref,
                    y_ref, dest_sm, rows_sc, sem_idx, sem_row, *, tt):
    i = pl.program_id(0)
    idx_copy = pltpu.make_async_copy(dest_hbm.at[i], dest_sm, sem_idx)
    idx_copy.start()
    idx_copy.wait()

    def row_copy(r):
        t = lax.shift_right_logical(r, 3)
        k = jnp.bitwise_and(r, TOP_K - 1)
        return pltpu.make_async_copy(ys_hbm.at[dest_sm[r]], rows_sc.at[k, t], sem_row)

    def start(r, c):
        row_copy(r).start()
        return c

    def wait(r, c):
        row_copy(r).wait()
        return c

    lax.fori_loop(0, tt * TOP_K, start, 0, unroll=8)

    x = x_ref[...]
    xb = x.astype(BF16)
    g = _dot(xb, sg_ref[...])
    u = _dot(xb, su_ref[...])
    h = (g * jax.nn.sigmoid(g) * u).astype(BF16)
    z = DEEPNORM_ALPHA * x + _dot(h, sd_ref[...])

    lax.fori_loop(0, tt * TOP_K, wait, 0, unroll=8)
    gate = gate_ref[...]
    pieces = []
    for c in range(rows_sc.shape[2]):
        acc = gate[:, 0:1] * rows_sc[0, :, c, :]
        for k in range(1, TOP_K):
            acc = acc + gate[:, k:k + 1] * rows_sc[k, :, c, :]
        pieces.append(acc)
    routed = jnp.concatenate(pieces, axis=1)
    y_ref[...] = _layer_norm_rows(z + routed, g_ref[...], b_ref[...])


def _combine(dest, ys3, x2, gate, sg_b, su_b, sd_b, ln_g, ln_b):
    n, d = x2.shape
    ds_ = sg_b.shape[1]
    _, s, l = ys3.shape
    tt = 256
    assert TOP_K == 8
    dest2 = dest.reshape(n // tt, tt * TOP_K)
    return pl.pallas_call(
        functools.partial(_combine_kernel, tt=tt),
        out_shape=jax.ShapeDtypeStruct((n, d), F32),
        grid=(n // tt,),
        in_specs=[pl.BlockSpec(memory_space=pl.ANY),
                  pl.BlockSpec(memory_space=pl.ANY),
                  pl.BlockSpec((tt, d), lambda i: (i, 0)),
                  pl.BlockSpec((tt, TOP_K), lambda i: (i, 0)),
                  pl.BlockSpec((d, ds_), lambda i: (0, 0)),
                  pl.BlockSpec((d, ds_), lambda i: (0, 0)),
                  pl.BlockSpec((ds_, d), lambda i: (0, 0)),
                  pl.BlockSpec((1, d), lambda i: (0, 0)),
                  pl.BlockSpec((1, d), lambda i: (0, 0))],
        out_specs=pl.BlockSpec((tt, d), lambda i: (i, 0)),
        scratch_shapes=[pltpu.SMEM((tt * TOP_K,), I32),
                        pltpu.VMEM((TOP_K, tt, s, l), F32),
                        pltpu.SemaphoreType.DMA, pltpu.SemaphoreType.DMA],
        compiler_params=_cparams(("arbitrary",)),
        name="moe_combine",
    )(dest2, ys3, x2, gate, sg_b, su_b, sd_b, ln_g.reshape(1, d), ln_b.reshape(1, d))


def _moe_layer(x2, rw, rb, w_gate, w_up, w_down, sh_gate, sh_up, sh_down, ln_g, ln_b):
    n, d = x2.shape
    e = N_EXPERTS
    r = EXPERT_ROWS
    sel, gate, rank, counts = _router(x2, rw, rb)
    counts = counts.reshape(e)
    padded = (counts + r - 1) // r * r
    pend = jnp.cumsum(padded)
    pstart = pend - padded
    dest = rank + jnp.sum(jnp.where(sel[..., None] == jnp.arange(e, dtype=I32), pstart.astype(I32), 0), axis=-1)
    p_rows = n * TOP_K + e * r
    nblk = p_rows // r
    blk_start = jnp.arange(nblk, dtype=I32) * r
    blk_e = jnp.minimum(jnp.sum((pend[None, :] <= blk_start[:, None]).astype(I32), axis=1), e - 1)
    n_used = (pend[-1] // r).astype(I32).reshape(1)
    pad_info = jnp.concatenate([jnp.maximum(pend - r, 0), padded]).astype(I32)
    xs = _dispatch(pad_info, dest.astype(I32), x2.reshape(n, d // LANES, LANES), p_rows)
    ys = _expert_mlp(blk_e, n_used, xs, w_gate, w_up, w_down)
    return _combine(dest.astype(I32), ys, x2, gate, sh_gate.astype(BF16), sh_up.astype(BF16),
                    sh_down.astype(BF16), ln_g, ln_b)


FAR_BUCKET = REL_BUCKETS // 2 - 1


def _n_near_offsets(tq, tk):
    return (tk + REL_MAX_DIST - 1 + tq - 1) // tq


def _t5_bucket(rel):
    half = REL_BUCKETS // 2
    max_exact = half // 2
    n = jnp.abs(rel)
    large = max_exact + (jnp.log(jnp.maximum(n, 1).astype(F32) / max_exact)
                         / math.log(REL_MAX_DIST / max_exact) * (half - max_exact)).astype(I32)
    large = jnp.minimum(large, half - 1)
    return jnp.where(rel > 0, half, 0) + jnp.where(n < max_exact, n, large)


def _bias_tiles_kernel(rb_ref, out_ref, *, tq, tk):
    offset = -tq * pl.program_id(0)
    row = lax.broadcasted_iota(I32, (tq, tk), 0)
    col = lax.broadcasted_iota(I32, (tq, tk), 1)
    bucket = _t5_bucket(offset + col - row)
    for h in range(B_HEADS):
        far = rb_ref[FAR_BUCKET * B_HEADS + h]
        acc = jnp.zeros((tq, tk), F32)
        for b in range(REL_BUCKETS):
            acc = jnp.where(bucket == b, rb_ref[b * B_HEADS + h] - far, acc)
        out_ref[0, h] = acc


def _bias_tiles(rel_bias, tq, tk):
    n_off = _n_near_offsets(tq, tk)
    grid_spec = pltpu.PrefetchScalarGridSpec(
        num_scalar_prefetch=1,
        grid=(n_off,),
        in_specs=[],
        out_specs=pl.BlockSpec((1, B_HEADS, tq, tk), lambda i, rb: (i, 0, 0, 0)),
    )
    return pl.pallas_call(
        functools.partial(_bias_tiles_kernel, tq=tq, tk=tk),
        out_shape=jax.ShapeDtypeStruct((n_off, B_HEADS, tq, tk), F32),
        grid_spec=grid_spec,
        compiler_params=_cparams(("arbitrary",)),
        name="dsa_bias_tiles",
    )(rel_bias.reshape(-1))


def _proj1_kernel(x_ref, w_ref, qn_ref, kvn_ref, cq_ref, ckv_ref, ckvt_ref, ki_ref, wi_ref,
                  *, o1, o2, o3, wscale):
    proj = _dot(x_ref[...].astype(BF16), w_ref[...])
    cq_ref[...] = _rms_norm_rows(proj[:, :o1], qn_ref[...]).astype(BF16)
    ckv = _rms_norm_rows(proj[:, o1:o2], kvn_ref[...])
    ckv_ref[...] = ckv.astype(BF16)
    ckvt_ref[...] = ckv.T.astype(BF16)
    ki_ref[...] = proj[:, o2:o3].astype(BF16)
    wi_ref[...] = proj[:, o3:] * wscale


def _proj1(x2, w_b, q_norm, kv_norm, ql, kvl, batch, seq):
    n, d = x2.shape
    wout = w_b.shape[1]
    o1, o2, o3 = ql, ql + kvl, ql + kvl + IDX_DIM
    tm = 512
    nt = seq // tm
    kern = functools.partial(_proj1_kernel, o1=o1, o2=o2, o3=o3,
                             wscale=(IDX_HEADS ** -0.5) * (IDX_DIM ** -0.5))
    return pl.pallas_call(
        kern,
        out_shape=(jax.ShapeDtypeStruct((n, ql), BF16), jax.ShapeDtypeStruct((n, kvl), BF16),
                   jax.ShapeDtypeStruct((batch * kvl, seq), BF16),
                   jax.ShapeDtypeStruct((n, IDX_DIM), BF16), jax.ShapeDtypeStruct((n, IDX_HEADS), F32)),
        grid=(n // tm,),
        in_specs=[pl.BlockSpec((tm, d), lambda i: (i, 0)),
                  pl.BlockSpec((d, wout), lambda i: (0, 0)),
                  pl.BlockSpec((1, ql), lambda i: (0, 0)),
                  pl.BlockSpec((1, kvl), lambda i: (0, 0))],
        out_specs=(pl.BlockSpec((tm, ql), lambda i: (i, 0)), pl.BlockSpec((tm, kvl), lambda i: (i, 0)),
                   pl.BlockSpec((kvl, tm), lambda i: (i // nt, i % nt)),
                   pl.BlockSpec((tm, IDX_DIM), lambda i: (i, 0)), pl.BlockSpec((tm, IDX_HEADS), lambda i: (i, 0))),
        compiler_params=_cparams(("arbitrary",)),
        name="dsa_proj",
    )(x2, w_b, q_norm.reshape(1, ql), kv_norm.reshape(1, kvl))


def _qside_kernel(cq_ref, wuq_ref, wuk_ref, wiq_ref, ql_ref, qi_ref, *, kvl, scale):
    cq = cq_ref[...]
    q = _dot(cq, wuq_ref[...]).astype(BF16)
    for h in range(B_HEADS):
        qh = q[:, h * B_HEAD_DIM:(h + 1) * B_HEAD_DIM]
        ql_ref[h] = (_dot_nt(qh, wuk_ref[h]) * scale).astype(BF16)
    qi_ref[...] = _dot(cq, wiq_ref[...]).astype(BF16)


def _qside(cq, w_uq_b, w_uk_b, w_iq_b):
    n, ql = cq.shape
    kvl = w_uk_b.shape[1]
    tm = 512
    kern = functools.partial(_qside_kernel, kvl=kvl, scale=B_HEAD_DIM ** -0.5)
    return pl.pallas_call(
        kern,
        out_shape=(jax.ShapeDtypeStruct((B_HEADS, n, kvl), BF16),
                   jax.ShapeDtypeStruct((n, IDX_HEADS * IDX_DIM), BF16)),
        grid=(n // tm,),
        in_specs=[pl.BlockSpec((tm, ql), lambda i: (i, 0)),
                  pl.BlockSpec(w_uq_b.shape, lambda i: (0, 0)),
                  pl.BlockSpec(w_uk_b.shape, lambda i: (0, 0, 0)),
                  pl.BlockSpec(w_iq_b.shape, lambda i: (0, 0))],
        out_specs=(pl.BlockSpec((B_HEADS, tm, kvl), lambda i: (0, i, 0)),
                   pl.BlockSpec((tm, IDX_HEADS * IDX_DIM), lambda i: (i, 0))),
        compiler_params=_cparams(("arbitrary",)),
        name="dsa_qside",
    )(cq, w_uq_b, w_uk_b, w_iq_b)


INT_MIN = -2 ** 31
KEY_NEG_INF = (0xFF800000 ^ 0x7FFFFFFF) - 2 ** 32


def _ordered_key(v):
    bits = lax.bitcast_convert_type(v, I32)
    return bits ^ (lax.shift_right_arithmetic(bits, 31) & 0x7FFFFFFF)


def _indexer_kernel(qi_ref, wi_ref, ki_ref, mask_ref, key_sc, *, tq, tkc, seq, topk):
    i = pl.program_id(1)
    n_chunks = (i * tq + tq + tkc - 1) // tkc
    row = lax.broadcasted_iota(I32, (tq, 1), 0)
    limit = (lax.shift_right_logical(i * tq + row, 6) + 1) * CHUNK
    w = wi_ref[...]
    key_sc[...] = jnp.full(key_sc.shape, KEY_NEG_INF, I32)

    def score_chunk(c, carry):
        start = pl.multiple_of(c * tkc, tkc)
        k = ki_ref[pl.ds(start, tkc), :]
        acc = jnp.zeros((tq, tkc), F32)
        for h in range(IDX_HEADS):
            sc = _dot_nt(qi_ref[:, h * IDX_DIM:(h + 1) * IDX_DIM], k)
            acc = acc + jnp.maximum(sc, 0.0) * w[:, h:h + 1]
        kpos = start + lax.broadcasted_iota(I32, (tq, tkc), 1)
        key_sc[:, pl.ds(start, tkc)] = _ordered_key(jnp.where(kpos < limit, acc, -jnp.inf))
        return carry

    lax.fori_loop(0, n_chunks, score_chunk, 0)

    def count(pred_fn):
        def body(c, acc):
            start = pl.multiple_of(c * tkc, tkc)
            hit = pred_fn(key_sc[:, pl.ds(start, tkc)]).astype(I32)
            part = hit[:, 0:LANES]
            for q in range(1, tkc // LANES):
                part = part + hit[:, q * LANES:(q + 1) * LANES]
            return acc + part
        acc = lax.fori_loop(0, n_chunks, body, jnp.zeros((tq, LANES), I32))
        return jnp.sum(acc, axis=1, keepdims=True)

    def bisect(it, prefix):
        cand = prefix + lax.shift_left(jnp.int32(1), 31 - it)
        cnt = count(lambda kk: kk >= cand)
        return jnp.where(cnt >= topk, cand, prefix)

    thr = lax.fori_loop(0, 32, bisect, jnp.full((tq, 1), INT_MIN, I32))
    n_gt = count(lambda kk: kk > thr)
    n_eq = count(lambda kk: kk == thr)
    need = topk - n_gt
    tie_break = jnp.max(jnp.where(jnp.logical_and(thr > KEY_NEG_INF, n_eq > need), 1, 0)) > 0

    mask_ref[...] = jnp.full(mask_ref.shape, NEG_BIG, F32)

    @pl.when(jnp.logical_not(tie_break))
    def _():
        def write(c, carry):
            start = pl.multiple_of(c * tkc, tkc)
            kk = key_sc[:, pl.ds(start, tkc)]
            sel = jnp.logical_and(kk >= thr, kk > KEY_NEG_INF)
            mask_ref[:, pl.ds(start, tkc)] = jnp.where(sel, 0.0, NEG_BIG)
            return carry
        lax.fori_loop(0, n_chunks, write, 0)

    @pl.when(tie_break)
    def _():
        r_ = lax.broadcasted_iota(I32, (LANES, LANES), 0)
        c_ = lax.broadcasted_iota(I32, (LANES, LANES), 1)
        upper = (r_ < c_).astype(BF16)

        def write(c, seen):
            start = pl.multiple_of(c * LANES, LANES)
            kk = key_sc[:, pl.ds(start, LANES)]
            eq = kk == thr
            before = seen + _dot(eq.astype(BF16), upper)
            sel = jnp.logical_or(kk > thr, jnp.logical_and(eq, before < need.astype(F32)))
            sel = jnp.logical_and(sel, kk > KEY_NEG_INF)
            mask_ref[:, pl.ds(start, LANES)] = jnp.where(sel, 0.0, NEG_BIG)
            return seen + jnp.sum(eq.astype(F32), axis=1, keepdims=True)
        lax.fori_loop(0, n_chunks * (tkc // LANES), write, jnp.zeros((tq, 1), F32))


def _indexer(qidx, widx, kidx, batch, seq, topk):
    tq = IDX_TQ
    tkc = min(IDX_TKC, seq)
    nq = seq // tq
    kern = functools.partial(_indexer_kernel, tq=tq, tkc=tkc, seq=seq, topk=topk)
    return pl.pallas_call(
        kern,
        out_shape=jax.ShapeDtypeStruct((batch * seq, seq), F32),
        grid=(batch, nq),
        in_specs=[pl.BlockSpec((tq, IDX_HEADS * IDX_DIM), lambda b, i: (b * nq + i, 0)),
                  pl.BlockSpec((tq, IDX_HEADS), lambda b, i: (b * nq + i, 0)),
                  pl.BlockSpec((seq, IDX_DIM), lambda b, i: (b, 0))],
        out_specs=pl.BlockSpec((tq, seq), lambda b, i: (b * nq + i, 0)),
        scratch_shapes=[pltpu.VMEM((tq, seq), I32)],
        compiler_params=_cparams(("arbitrary", "arbitrary")),
        name="dsa_indexer",
    )(qidx, widx, kidx)


def _dsa_attn_kernel(qi_ref, kj_ref, ql_ref, kt_ref, kv_ref, mask_ref, bias_ref, wuv_ref, o_ref,
                     m_sc, l_sc, acc_sc, *, tq, tk, kvl, n_near):
    p = pl.program_id(1)
    i = qi_ref[p]
    j = kj_ref[p]
    nh = B_HEADS
    behind = (i * tq - j * tk) // tq

    @pl.when(j == 0)
    def _():
        m_sc[...] = jnp.full_like(m_sc, -jnp.inf)
        l_sc[...] = jnp.zeros_like(l_sc)
        acc_sc[...] = jnp.zeros_like(acc_sc)

    def step(near):
        kt = kt_ref[...]
        kv = kv_ref[...]
        msk = mask_ref[...]
        for h in range(nh):
            s = _dot(ql_ref[h], kt) + msk
            if near:
                s = s + bias_ref[behind, h]
            m_prev = m_sc[h]
            m_new = jnp.maximum(m_prev, jnp.max(s, axis=1, keepdims=True))
            alpha = jnp.exp(m_prev - m_new)
            pexp = jnp.exp(s - _lane_tile(m_new, tk))
            l_sc[h] = alpha * l_sc[h] + jnp.sum(pexp, axis=1, keepdims=True)
            acc_sc[h] = _lane_tile(alpha, kvl) * acc_sc[h] + _dot(pexp.astype(BF16), kv)
            m_sc[h] = m_new

    @pl.when(behind >= n_near)
    def _():
        step(False)

    @pl.when(behind < n_near)
    def _():
        step(True)

    @pl.when(j == (i * tq + tq - 1) // tk)
    def _():
        for h in range(nh):
            o_lat = (acc_sc[h] / _lane_tile(l_sc[h], kvl)).astype(BF16)
            o_ref[:, h * B_V_DIM:(h + 1) * B_V_DIM] = _dot(o_lat, wuv_ref[h]).astype(o_ref.dtype)


def _dsa_attention(ql, ckvt, ckv, mask, bias_tiles, w_uv_b, batch, seq):
    tq, tk = DSA_TQ, DSA_TK
    kvl = ckv.shape[1]
    nq = seq // tq
    nk = seq // tk
    last = [(i * tq + tq - 1) // tk for i in range(nq)]
    qi = np.concatenate([np.full(last[i] + 1, i) for i in range(nq)]).astype(np.int32)
    kj = np.concatenate([np.arange(last[i] + 1) for i in range(nq)]).astype(np.int32)
    kern = functools.partial(_dsa_attn_kernel, tq=tq, tk=tk, kvl=kvl, n_near=bias_tiles.shape[0])
    grid_spec = pltpu.PrefetchScalarGridSpec(
        num_scalar_prefetch=2,
        grid=(batch, len(qi)),
        in_specs=[
            pl.BlockSpec((B_HEADS, tq, kvl), lambda b, p, qi, kj: (0, b * nq + qi[p], 0)),
            pl.BlockSpec((kvl, tk), lambda b, p, qi, kj: (b, kj[p])),
            pl.BlockSpec((tk, kvl), lambda b, p, qi, kj: (b * nk + kj[p], 0)),
            pl.BlockSpec((tq, tk), lambda b, p, qi, kj: (b * nq + qi[p], kj[p])),
            pl.BlockSpec(bias_tiles.shape, lambda b, p, qi, kj: (0, 0, 0, 0)),
            pl.BlockSpec(w_uv_b.shape, lambda b, p, qi, kj: (0, 0, 0)),
        ],
        out_specs=pl.BlockSpec((tq, B_HEADS * B_V_DIM), lambda b, p, qi, kj: (b * nq + qi[p], 0)),
        scratch_shapes=[pltpu.VMEM((B_HEADS, tq, LANES), F32), pltpu.VMEM((B_HEADS, tq, LANES), F32),
                        pltpu.VMEM((B_HEADS, tq, kvl), F32)],
    )
    return pl.pallas_call(
        kern,
        out_shape=jax.ShapeDtypeStruct((batch * seq, B_HEADS * B_V_DIM), BF16),
        grid_spec=grid_spec,
        compiler_params=_cparams(("arbitrary", "arbitrary")),
        name="dsa_attention",
    )(jnp.asarray(qi), jnp.asarray(kj), ql, ckvt, ckv, mask, bias_tiles, w_uv_b)


def _dsa_layer(x2, w_in, q_norm, kv_norm, w_uq, w_iq, w_uk, w_uv, w_out, rel_bias, ln_g, ln_b, batch, seq):
    ql_dim = q_norm.shape[0]
    kvl = kv_norm.shape[0]
    topk = min(IDX_TOPK, seq // 4)
    cq, ckv, ckvt, kidx, widx = _proj1(x2, w_in.astype(BF16), q_norm, kv_norm, ql_dim, kvl, batch, seq)
    ql, qidx = _qside(cq, w_uq.astype(BF16), w_uk.astype(BF16), w_iq.astype(BF16))
    mask = _indexer(qidx, widx, kidx, batch, seq, topk)
    bias_tiles = _bias_tiles(rel_bias, DSA_TQ, DSA_TK)
    o = _dsa_attention(ql, ckvt, ckv, mask, bias_tiles, w_uv.astype(BF16), batch, seq)
    return _outproj_ln(o, w_out.astype(BF16), x2, ln_g, ln_b)


def kernel(x, a_w_in, a_b_f, a_w_out, b_w_in, b_q_norm, b_kv_norm, b_w_uq, b_w_iq, b_w_uk, b_w_uv, b_w_out,
           rel_bias, ln1_g, ln1_b, ln2_g, ln2_b, router_w, router_b, w_gate, w_up, w_down, sh_gate, sh_up,
           sh_down):
    batch, seq, d = x.shape
    x2 = x.reshape(batch * seq, d)
    x2 = _fox_layer(x2, a_w_in[0], a_b_f[0], a_w_out[0], ln1_g[0], ln1_b[0], batch, seq)
    x2 = _moe_layer(x2, router_w[0], router_b[0], w_gate[0], w_up[0], w_down[0], sh_gate[0], sh_up[0],
                    sh_down[0], ln2_g[0], ln2_b[0])
    x2 = _dsa_layer(x2, b_w_in[0], b_q_norm[0], b_kv_norm[0], b_w_uq[0], b_w_iq[0], b_w_uk[0], b_w_uv[0],
                    b_w_out[0], rel_bias, ln1_g[1], ln1_b[1], batch, seq)
    x2 = _moe_layer(x2, router_w[1], router_b[1], w_gate[1], w_up[1], w_down[1], sh_gate[1], sh_up[1],
                    sh_down[1], ln2_g[1], ln2_b[1])
    return x2.reshape(batch, seq, d)
```

```python
import functools
import math

import numpy as np
import jax
import jax.numpy as jnp
from jax import lax
from jax.experimental import pallas as pl
from jax.experimental.pallas import tpu as pltpu

BF16 = jnp.bfloat16
F32 = jnp.float32
I32 = jnp.int32
U32 = jnp.uint32

A_HEADS = 16
A_HEAD_DIM = 128
B_HEADS = 16
B_HEAD_DIM = 128
B_V_DIM = 128
IDX_HEADS = 16
IDX_DIM = 64
IDX_TOPK = 256
CHUNK = 64
REL_BUCKETS = 32
REL_MAX_DIST = 128
N_EXPERTS = 64
TOP_K = 8
ROUTED_SCALE = 2.5
DEPTH = 2
DEEPNORM_ALPHA = (2 * DEPTH) ** 0.25
LN_EPS = 1e-5
RMS_EPS = 1e-6

LANES = 128
SLAB = 8
VMEM_LIMIT = 56 * 1024 * 1024
NEG_BIG = -1e30

EXPERT_ROWS = 256
FOX_TQ = 512
FOX_TK = 512
FOX_HEADS_PER_STEP = 2
DSA_TQ = 128
DSA_TK = 256
IDX_TQ = 128
IDX_TKC = 512


def _cparams(sem, vmem=VMEM_LIMIT):
    return pltpu.CompilerParams(dimension_semantics=sem, vmem_limit_bytes=vmem)


def _dot(a, b):
    return jnp.dot(a, b, preferred_element_type=F32)


def _dot_nt(a, b):
    return lax.dot_general(a, b, (((1,), (1,)), ((), ())), preferred_element_type=F32)


def _lane_tile(a, width):
    return a if width == LANES else jnp.concatenate([a] * (width // LANES), axis=1)


def _split2(a):
    hi = a.astype(BF16)
    lo = (a - hi.astype(F32)).astype(BF16)
    return hi, lo


def _split3(a):
    hi = a.astype(BF16)
    r = a - hi.astype(F32)
    mid = r.astype(BF16)
    lo = (r - mid.astype(F32)).astype(BF16)
    return hi, mid, lo


def _dot_x3(a, b):
    ah, al = _split2(a)
    bh, bl = _split2(b)
    return _dot(ah, bh) + (_dot(ah, bl) + _dot(al, bh))


def _layer_norm_rows(z, g, b):
    mu = jnp.mean(z, axis=-1, keepdims=True)
    d = z - mu
    var = jnp.mean(d * d, axis=-1, keepdims=True)
    return d * lax.rsqrt(var + LN_EPS) * g + b


def _rms_norm_rows(z, g):
    ms = jnp.mean(z * z, axis=-1, keepdims=True)
    return z * lax.rsqrt(ms + RMS_EPS) * g


def _proj0_kernel(x_ref, w_ref, wf_ref, qkv_ref, fl_ref, xb_sc, *, n_q_blocks, q_scale):
    j = pl.program_id(1)

    @pl.when(j == 0)
    def _():
        x = x_ref[...]
        xb_sc[...] = x.astype(BF16)
        fl_ref[...] = _dot_x3(x, wf_ref[...])

    acc = _dot(xb_sc[...], w_ref[...])
    scale = jnp.where(j < n_q_blocks, q_scale, 1.0).astype(F32)
    qkv_ref[...] = (acc * scale).astype(BF16)


def _proj0(x2, w_qkv_b, w_f):
    n, d = x2.shape
    nout = w_qkv_b.shape[1]
    tm, tn = 1024, 512
    dq = A_HEADS * A_HEAD_DIM
    kern = functools.partial(_proj0_kernel, n_q_blocks=dq // tn, q_scale=A_HEAD_DIM ** -0.5)
    return pl.pallas_call(
        kern,
        out_shape=(jax.ShapeDtypeStruct((n, nout), BF16),
                   jax.ShapeDtypeStruct((n, A_HEADS), F32)),
        grid=(n // tm, nout // tn),
        in_specs=[pl.BlockSpec((tm, d), lambda i, j: (i, 0)),
                  pl.BlockSpec((d, tn), lambda i, j: (0, j)),
                  pl.BlockSpec((d, A_HEADS), lambda i, j: (0, 0))],
        out_specs=(pl.BlockSpec((tm, tn), lambda i, j: (i, j)),
                   pl.BlockSpec((tm, A_HEADS), lambda i, j: (i, 0))),
        scratch_shapes=[pltpu.VMEM((tm, d), BF16)],
        compiler_params=_cparams(("arbitrary", "arbitrary")),
        name="fox_proj",
    )(x2, w_qkv_b, w_f)


def _forget_cumsum_kernel(fl_ref, bf_ref, f_ref, carry_sc, *, t):
    @pl.when(pl.program_id(1) == 0)
    def _():
        carry_sc[...] = jnp.zeros_like(carry_sc)

    z = fl_ref[...] + bf_ref[...]
    logf = jnp.minimum(z, 0.0) - jnp.log1p(jnp.exp(-jnp.abs(z)))
    row = lax.broadcasted_iota(I32, (t, t), 0)
    col = lax.broadcasted_iota(I32, (t, t), 1)
    tri = (col <= row).astype(BF16)
    hi, mid, lo = _split3(logf)
    cs = _dot(tri, hi) + (_dot(tri, mid) + _dot(tri, lo)) + carry_sc[...]
    f_ref[...] = cs
    carry_sc[...] = cs[t - 1:t, :]


def _forget_cumsum(fl, b_f, batch, seq):
    t = 256
    nb = seq // t
    return pl.pallas_call(
        functools.partial(_forget_cumsum_kernel, t=t),
        out_shape=jax.ShapeDtypeStruct(fl.shape, F32),
        grid=(batch, nb),
        in_specs=[pl.BlockSpec((t, A_HEADS), lambda b, i: (b * nb + i, 0)),
                  pl.BlockSpec((1, A_HEADS), lambda b, i: (0, 0))],
        out_specs=pl.BlockSpec((t, A_HEADS), lambda b, i: (b * nb + i, 0)),
        scratch_shapes=[pltpu.VMEM((1, A_HEADS), F32)],
        compiler_params=_cparams(("arbitrary", "arbitrary")),
        name="fox_forget_cumsum",
    )(fl, b_f.reshape(1, A_HEADS))


def _fox_attn_kernel(qi_ref, kj_ref, q_ref, k_ref, v_ref, fk_ref, o_ref, m_sc, l_sc, acc_sc, *, tq, tk, hp):
    p = pl.program_id(2)
    i = qi_ref[p]
    j = kj_ref[p]
    dh = A_HEAD_DIM

    @pl.when(j == 0)
    def _():
        m_sc[...] = jnp.full_like(m_sc, -jnp.inf)
        l_sc[...] = jnp.zeros_like(l_sc)
        acc_sc[...] = jnp.zeros_like(acc_sc)

    def step(diag):
        for hh in range(hp):
            cols = slice(hh * dh, (hh + 1) * dh)
            s = _dot_nt(q_ref[:, cols], k_ref[:, cols]) - fk_ref[hh]
            if diag:
                row = lax.broadcasted_iota(I32, (tq, tk), 0)
                col = lax.broadcasted_iota(I32, (tq, tk), 1)
                s = jnp.where(col <= row, s, -jnp.inf)
            m_prev = m_sc[hh]
            m_new = jnp.maximum(m_prev, jnp.max(s, axis=1, keepdims=True))
            alpha = jnp.exp(m_prev - m_new)
            pexp = jnp.exp(s - _lane_tile(m_new, tk))
            l_sc[hh] = alpha * l_sc[hh] + jnp.sum(pexp, axis=1, keepdims=True)
            acc_sc[hh] = alpha * acc_sc[hh] + _dot(pexp.astype(BF16), v_ref[:, cols])
            m_sc[hh] = m_new

    @pl.when(j < i)
    def _():
        step(False)

    @pl.when(j == i)
    def _():
        step(True)
        for hh in range(hp):
            o_ref[:, hh * dh:(hh + 1) * dh] = (acc_sc[hh] / l_sc[hh]).astype(o_ref.dtype)


def _fox_attention(qkv, f_rows, batch, seq):
    t = min(FOX_TQ, seq)
    nq = seq // t
    qi = np.concatenate([np.full(i + 1, i) for i in range(nq)]).astype(np.int32)
    kj = np.concatenate([np.arange(i + 1) for i in range(nq)]).astype(np.int32)
    hp = FOX_HEADS_PER_STEP
    hg = A_HEADS // hp
    dh = A_HEAD_DIM
    w = hp * dh
    kern = functools.partial(_fox_attn_kernel, tq=t, tk=t, hp=hp)
    grid_spec = pltpu.PrefetchScalarGridSpec(
        num_scalar_prefetch=2,
        grid=(batch, hg, len(qi)),
        in_specs=[
            pl.BlockSpec((t, w), lambda b, h, p, qi, kj: (b * nq + qi[p], h)),
            pl.BlockSpec((t, w), lambda b, h, p, qi, kj: (b * nq + kj[p], hg + h)),
            pl.BlockSpec((t, w), lambda b, h, p, qi, kj: (b * nq + kj[p], 2 * hg + h)),
            pl.BlockSpec((hp, 1, t), lambda b, h, p, qi, kj: (b * hg + h, 0, kj[p])),
        ],
        out_specs=pl.BlockSpec((t, w), lambda b, h, p, qi, kj: (b * nq + qi[p], h)),
        scratch_shapes=[pltpu.VMEM((hp, t, dh), F32), pltpu.VMEM((hp, t, dh), F32),
                        pltpu.VMEM((hp, t, dh), F32)],
    )
    return pl.pallas_call(
        kern,
        out_shape=jax.ShapeDtypeStruct((batch * seq, A_HEADS * dh), BF16),
        grid_spec=grid_spec,
        compiler_params=_cparams(("arbitrary", "arbitrary", "arbitrary")),
        name="fox_attention",
    )(jnp.asarray(qi), jnp.asarray(kj), qkv, qkv, qkv, f_rows)


def _outproj_ln_kernel(o_ref, w_ref, x_ref, g_ref, b_ref, y_ref):
    z = DEEPNORM_ALPHA * x_ref[...] + _dot(o_ref[...], w_ref[...])
    y_ref[...] = _layer_norm_rows(z, g_ref[...], b_ref[...])


def _outproj_ln(o, w_b, x2, g, b):
    n, d = x2.shape
    k = o.shape[1]
    tm = 512
    return pl.pallas_call(
        _outproj_ln_kernel,
        out_shape=jax.ShapeDtypeStruct((n, d), F32),
        grid=(n // tm,),
        in_specs=[pl.BlockSpec((tm, k), lambda i: (i, 0)),
                  pl.BlockSpec((k, d), lambda i: (0, 0)),
                  pl.BlockSpec((tm, d), lambda i: (i, 0)),
                  pl.BlockSpec((1, d), lambda i: (0, 0)),
                  pl.BlockSpec((1, d), lambda i: (0, 0))],
        out_specs=pl.BlockSpec((tm, d), lambda i: (i, 0)),
        compiler_params=_cparams(("arbitrary",)),
        name="outproj_deepnorm",
    )(o, w_b, x2, g.reshape(1, d), b.reshape(1, d))


def _fox_layer(x2, w_in, b_f, w_out, ln_g, ln_b, batch, seq):
    dq = A_HEADS * A_HEAD_DIM
    qkv, fl = _proj0(x2, w_in[:, :3 * dq].astype(BF16), w_in[:, 3 * dq:])
    f = _forget_cumsum(fl, b_f, batch, seq)
    f_rows = f.reshape(batch, seq, A_HEADS).transpose(0, 2, 1).reshape(batch * A_HEADS, 1, seq)
    o = _fox_attention(qkv, f_rows, batch, seq)
    return _outproj_ln(o, w_out.astype(BF16), x2, ln_g, ln_b)


def _router_kernel(x_ref, rw_ref, rb_ref, sel_ref, gate_ref, rank_ref, cnt_ref, carry_sc, *, tm):
    @pl.when(pl.program_id(0) == 0)
    def _():
        carry_sc[...] = jnp.zeros_like(carry_sc)

    e = N_EXPERTS
    scores = jax.nn.sigmoid(_dot_x3(x_ref[...], rw_ref[...]))
    lane = lax.broadcasted_iota(I32, (tm, e), 1)
    slot = lax.broadcasted_iota(I32, (tm, TOP_K), 1)
    work = scores + rb_ref[...]
    chosen = jnp.zeros((tm, e), F32)
    sel = jnp.zeros((tm, TOP_K), I32)
    gate = jnp.zeros((tm, TOP_K), F32)
    idxs = []
    for k in range(TOP_K):
        mx = jnp.max(work, axis=1, keepdims=True)
        idx = jnp.min(jnp.where(work == mx, lane, e), axis=1, keepdims=True)
        hit = lane == idx
        gk = jnp.sum(jnp.where(hit, scores, 0.0), axis=1, keepdims=True)
        work = jnp.where(hit, -jnp.inf, work)
        chosen = jnp.where(hit, 1.0, chosen)
        sel = jnp.where(slot == k, idx, sel)
        gate = jnp.where(slot == k, gk, gate)
        idxs.append(idx)
    gate = gate / jnp.sum(gate, axis=1, keepdims=True) * ROUTED_SCALE

    row = lax.broadcasted_iota(I32, (tm, tm), 0)
    col = lax.broadcasted_iota(I32, (tm, tm), 1)
    before = _dot((col < row).astype(BF16), chosen.astype(BF16)) + carry_sc[...]
    rank = jnp.zeros((tm, TOP_K), F32)
    for k in range(TOP_K):
        rk = jnp.sum(jnp.where(lane == idxs[k], before, 0.0), axis=1, keepdims=True)
        rank = jnp.where(slot == k, rk, rank)
    total = carry_sc[...] + jnp.sum(chosen, axis=0, keepdims=True)
    carry_sc[...] = total
    sel_ref[...] = sel
    gate_ref[...] = gate
    rank_ref[...] = rank.astype(I32)
    cnt_ref[...] = total.astype(I32)


def _router(x2, rw, rb):
    n, d = x2.shape
    tm = 512
    e = N_EXPERTS
    return pl.pallas_call(
        functools.partial(_router_kernel, tm=tm),
        out_shape=(jax.ShapeDtypeStruct((n, TOP_K), I32),
                   jax.ShapeDtypeStruct((n, TOP_K), F32),
                   jax.ShapeDtypeStruct((n, TOP_K), I32),
                   jax.ShapeDtypeStruct((1, e), I32)),
        grid=(n // tm,),
        in_specs=[pl.BlockSpec((tm, d), lambda i: (i, 0)),
                  pl.BlockSpec((d, e), lambda i: (0, 0)),
                  pl.BlockSpec((1, e), lambda i: (0, 0))],
        out_specs=(pl.BlockSpec((tm, TOP_K), lambda i: (i, 0)),
                   pl.BlockSpec((tm, TOP_K), lambda i: (i, 0)),
                   pl.BlockSpec((tm, TOP_K), lambda i: (i, 0)),
                   pl.BlockSpec((1, e), lambda i: (0, 0))),
        scratch_shapes=[pltpu.VMEM((1, e), F32)],
        compiler_params=_cparams(("arbitrary",)),
        name="moe_router",
    )(x2, rw, rb.reshape(1, e))


def _pack_bf16_pairs(y):
    w = y.shape[1] // 2
    lo = lax.bitcast_convert_type(y[:, :w].astype(BF16).astype(F32), U32)
    hi = lax.bitcast_convert_type(y[:, w:].astype(BF16).astype(F32), U32)
    return lax.shift_right_logical(lo, jnp.uint32(16)) | hi


def _unpack_bf16_pairs(words):
    lo = lax.bitcast_convert_type(lax.shift_left(words, jnp.uint32(16)), F32).astype(BF16)
    hi = lax.bitcast_convert_type(words & jnp.uint32(0xFFFF0000), F32).astype(BF16)
    return lo, hi


def _slab_load(ref, rows):
    return jnp.concatenate([ref[pl.ds(c, rows, stride=SLAB), :] for c in range(SLAB)], axis=1)


def _slab_store(ref, rows, val):
    for c in range(SLAB):
        ref[pl.ds(c, rows, stride=SLAB), :] = val[:, c * LANES:(c + 1) * LANES]


def _dispatch_kernel(pad_ref, dest_hbm, x_ref, xs_hbm, dest_sm, slab_sc, zero_sc, sem_idx, sem_zero, sem_row,
                     *, tt):
    i = pl.program_id(0)

    def zero_copy(e):
        start = pl.multiple_of(pad_ref[e] * SLAB, SLAB)
        return pltpu.make_async_copy(zero_sc, xs_hbm.at[pl.ds(start, EXPERT_ROWS * SLAB)], sem_zero)

    @pl.when(i == 0)
    def _():
        zero_sc[...] = jnp.zeros_like(zero_sc)

        def start(e, c):
            @pl.when(pad_ref[N_EXPERTS + e] > 0)
            def _():
                zero_copy(e).start()
            return c

        def wait(e, c):
            @pl.when(pad_ref[N_EXPERTS + e] > 0)
            def _():
                zero_copy(e).wait()
            return c

        lax.fori_loop(0, N_EXPERTS, start, 0)
        lax.fori_loop(0, N_EXPERTS, wait, 0)

    idx_copy = pltpu.make_async_copy(dest_hbm.at[i], dest_sm, sem_idx)
    idx_copy.start()
    _slab_store(slab_sc, tt, _pack_bf16_pairs(x_ref[...]))
    idx_copy.wait()

    def row_copy(r):
        src = pl.multiple_of(lax.shift_right_logical(r, 3) * SLAB, SLAB)
        dst = pl.multiple_of(dest_sm[r] * SLAB, SLAB)
        return pltpu.make_async_copy(slab_sc.at[pl.ds(src, SLAB)], xs_hbm.at[pl.ds(dst, SLAB)], sem_row)

    def start(r, c):
        row_copy(r).start()
        return c

    def wait(r, c):
        row_copy(r).wait()
        return c

    lax.fori_loop(0, tt * TOP_K, start, 0, unroll=8)
    lax.fori_loop(0, tt * TOP_K, wait, 0, unroll=8)


def _dispatch(pad_info, dest, x2, p_rows):
    n, d = x2.shape
    tt = 512
    assert TOP_K == 8
    assert d == 2 * SLAB * LANES
    dest2 = dest.reshape(n // tt, tt * TOP_K)
    grid_spec = pltpu.PrefetchScalarGridSpec(
        num_scalar_prefetch=1,
        grid=(n // tt,),
        in_specs=[pl.BlockSpec(memory_space=pl.ANY),
                  pl.BlockSpec((tt, d), lambda i, pad: (i, 0))],
        out_specs=pl.BlockSpec(memory_space=pl.ANY),
        scratch_shapes=[pltpu.SMEM((tt * TOP_K,), I32),
                        pltpu.VMEM((tt * SLAB, LANES), U32),
                        pltpu.VMEM((EXPERT_ROWS * SLAB, LANES), U32),
                        pltpu.SemaphoreType.DMA, pltpu.SemaphoreType.DMA, pltpu.SemaphoreType.DMA],
    )
    return pl.pallas_call(
        functools.partial(_dispatch_kernel, tt=tt),
        out_shape=jax.ShapeDtypeStruct((p_rows * SLAB, LANES), U32),
        grid_spec=grid_spec,
        compiler_params=_cparams(("arbitrary",)),
        name="moe_dispatch",
    )(pad_info, dest2, x2)


def _expert_mlp_kernel(be_ref, nu_ref, xs_ref, wg_ref, wu_ref, wd_ref, ys_ref, wg_sc, wu_sc, wd_sc):
    b = pl.program_id(0)
    e = be_ref[b]
    first = jnp.logical_or(b == 0, e != be_ref[jnp.maximum(b - 1, 0)])
    active = b < nu_ref[0]

    @pl.when(jnp.logical_and(active, first))
    def _():
        wg_sc[...] = wg_ref[...].astype(BF16)
        wu_sc[...] = wu_ref[...].astype(BF16)
        wd_sc[...] = wd_ref[...].astype(BF16)

    @pl.when(active)
    def _():
        r = EXPERT_ROWS
        half = wg_sc.shape[0] // 2
        x_lo, x_hi = _unpack_bf16_pairs(_slab_load(xs_ref, r))
        g = _dot(x_lo, wg_sc[:half]) + _dot(x_hi, wg_sc[half:])
        u = _dot(x_lo, wu_sc[:half]) + _dot(x_hi, wu_sc[half:])
        h = (g * jax.nn.sigmoid(g) * u).astype(BF16)
        _slab_store(ys_ref, r, _pack_bf16_pairs(_dot(h, wd_sc[...])))


def _expert_mlp(blk_e, n_used, xs, w_gate, w_up, w_down, layer):
    r = EXPERT_ROWS
    nblk = xs.shape[0] // (r * SLAB)
    d, de = w_gate.shape[2], w_gate.shape[3]

    def row_map(b, be, nu):
        return (jnp.minimum(b, nu[0] - 1), 0)

    def w_map(b, be, nu):
        return (layer, be[b], 0, 0)

    grid_spec = pltpu.PrefetchScalarGridSpec(
        num_scalar_prefetch=2,
        grid=(nblk,),
        in_specs=[pl.BlockSpec((r * SLAB, LANES), row_map),
                  pl.BlockSpec((None, None, d, de), w_map),
                  pl.BlockSpec((None, None, d, de), w_map),
                  pl.BlockSpec((None, None, de, d), w_map)],
        out_specs=pl.BlockSpec((r * SLAB, LANES), row_map),
        scratch_shapes=[pltpu.VMEM((d, de), BF16), pltpu.VMEM((d, de), BF16), pltpu.VMEM((de, d), BF16)],
    )
    return pl.pallas_call(
        _expert_mlp_kernel,
        out_shape=jax.ShapeDtypeStruct(xs.shape, U32),
        grid_spec=grid_spec,
        compiler_params=_cparams(("arbitrary",)),
        name="moe_experts",
    )(blk_e, n_used, xs, w_gate, w_up, w_down)


def _combine_kernel(dest_hbm, ys_hbm, x_ref, gate_ref, sg_ref, su_ref, sd_ref, g_ref, b_ref,
                    y_ref, dest_sm, rows_sc, sem_idx, sem_row, *, tt, n_tiles):
    i = pl.program_id(0)
    slot = lax.rem(i, 2)
    n_rows = tt * TOP_K

    def idx_copy(tile, s):
        return pltpu.make_async_copy(dest_hbm.at[tile], dest_sm.at[pl.ds(pl.multiple_of(s * n_rows, n_rows), n_rows)],
                                     sem_idx.at[s])

    def row_copy(s, r, src_row):
        t = lax.shift_right_logical(r, 3)
        k = jnp.bitwise_and(r, TOP_K - 1)
        return pltpu.make_async_copy(ys_hbm.at[pl.ds(pl.multiple_of(src_row * SLAB, SLAB), SLAB)],
                                     rows_sc.at[s, k, pl.ds(pl.multiple_of(t * SLAB, SLAB), SLAB)], sem_row.at[s])

    def start_rows(s):
        def start(r, c):
            row_copy(s, r, dest_sm[s * n_rows + r]).start()
            return c
        lax.fori_loop(0, n_rows, start, 0, unroll=8)

    @pl.when(i == 0)
    def _():
        idx_copy(0, 0).start()
        idx_copy(0, 0).wait()
        start_rows(0)
        if n_tiles > 1:
            idx_copy(1, 1).start()

    @pl.when(i + 1 < n_tiles)
    def _():
        idx_copy(i + 1, 1 - slot).wait()
        start_rows(1 - slot)

    @pl.when(i + 2 < n_tiles)
    def _():
        idx_copy(i + 2, slot).start()

    x = x_ref[...]
    xb = x.astype(BF16)
    g = _dot(xb, sg_ref[...])
    u = _dot(xb, su_ref[...])
    h = (g * jax.nn.sigmoid(g) * u).astype(BF16)
    z = DEEPNORM_ALPHA * x + _dot(h, sd_ref[...])

    def wait(r, c):
        row_copy(slot, r, 0).wait()
        return c
    lax.fori_loop(0, n_rows, wait, 0, unroll=8)

    gate = gate_ref[...]
    lo_pieces, hi_pieces = [], []
    for c in range(SLAB):
        acc_lo = jnp.zeros((tt, LANES), F32)
        acc_hi = jnp.zeros((tt, LANES), F32)
        for k in range(TOP_K):
            words = rows_sc[slot, k, pl.ds(c, tt, stride=SLAB), :]
            gk = gate[:, k:k + 1]
            acc_lo = acc_lo + gk * lax.bitcast_convert_type(lax.shift_left(words, jnp.uint32(16)), F32)
            acc_hi = acc_hi + gk * lax.bitcast_convert_type(words & jnp.uint32(0xFFFF0000), F32)
        lo_pieces.append(acc_lo)
        hi_pieces.append(acc_hi)
    routed = jnp.concatenate(lo_pieces + hi_pieces, axis=1)
    y_ref[...] = _layer_norm_rows(z + routed, g_ref[...], b_ref[...])


def _combine(dest, ys, x2, gate, sg_b, su_b, sd_b, ln_g, ln_b):
    n, d = x2.shape
    ds_ = sg_b.shape[1]
    tt = 256
    assert TOP_K == 8
    n_tiles = n // tt
    dest2 = dest.reshape(n_tiles, tt * TOP_K)
    return pl.pallas_call(
        functools.partial(_combine_kernel, tt=tt, n_tiles=n_tiles),
        out_shape=jax.ShapeDtypeStruct((n, d), F32),
        grid=(n_tiles,),
        in_specs=[pl.BlockSpec(memory_space=pl.ANY),
                  pl.BlockSpec(memory_space=pl.ANY),
                  pl.BlockSpec((tt, d), lambda i: (i, 0)),
                  pl.BlockSpec((tt, TOP_K), lambda i: (i, 0)),
                  pl.BlockSpec((d, ds_), lambda i: (0, 0)),
                  pl.BlockSpec((d, ds_), lambda i: (0, 0)),
                  pl.BlockSpec((ds_, d), lambda i: (0, 0)),
                  pl.BlockSpec((1, d), lambda i: (0, 0)),
                  pl.BlockSpec((1, d), lambda i: (0, 0))],
        out_specs=pl.BlockSpec((tt, d), lambda i: (i, 0)),
        scratch_shapes=[pltpu.SMEM((2 * tt * TOP_K,), I32),
                        pltpu.VMEM((2, TOP_K, tt * SLAB, LANES), U32),
                        pltpu.SemaphoreType.DMA((2,)), pltpu.SemaphoreType.DMA((2,))],
        compiler_params=_cparams(("arbitrary",)),
        name="moe_combine",
    )(dest2, ys, x2, gate, sg_b, su_b, sd_b, ln_g.reshape(1, d), ln_b.reshape(1, d))


def _moe_layer(x2, rw, rb, w_gate, w_up, w_down, layer, sh_gate, sh_up, sh_down, ln_g, ln_b):
    n, d = x2.shape
    e = N_EXPERTS
    r = EXPERT_ROWS
    sel, gate, rank, counts = _router(x2, rw, rb)
    counts = counts.reshape(e)
    padded = (counts + r - 1) // r * r
    pend = jnp.cumsum(padded)
    pstart = pend - padded
    dest = rank + jnp.sum(jnp.where(sel[..., None] == jnp.arange(e, dtype=I32), pstart.astype(I32), 0), axis=-1)
    p_rows = n * TOP_K + e * r
    nblk = p_rows // r
    blk_start = jnp.arange(nblk, dtype=I32) * r
    blk_e = jnp.minimum(jnp.sum((pend[None, :] <= blk_start[:, None]).astype(I32), axis=1), e - 1)
    n_used = (pend[-1] // r).astype(I32).reshape(1)
    pad_info = jnp.concatenate([jnp.maximum(pend - r, 0), padded]).astype(I32)
    xs = _dispatch(pad_info, dest.astype(I32), x2, p_rows)
    ys = _expert_mlp(blk_e, n_used, xs, w_gate, w_up, w_down, layer)
    return _combine(dest.astype(I32), ys, x2, gate, sh_gate.astype(BF16), sh_up.astype(BF16),
                    sh_down.astype(BF16), ln_g, ln_b)


FAR_BUCKET = REL_BUCKETS // 2 - 1


def _n_near_offsets(tq, tk):
    return (tk + REL_MAX_DIST - 1 + tq - 1) // tq


def _t5_bucket(rel):
    half = REL_BUCKETS // 2
    max_exact = half // 2
    n = jnp.abs(rel)
    large = max_exact + (jnp.log(jnp.maximum(n, 1).astype(F32) / max_exact)
                         / math.log(REL_MAX_DIST / max_exact) * (half - max_exact)).astype(I32)
    large = jnp.minimum(large, half - 1)
    return jnp.where(rel > 0, half, 0) + jnp.where(n < max_exact, n, large)


def _bias_tiles_kernel(rb_ref, out_ref, *, tq, tk):
    offset = -tq * pl.program_id(0)
    row = lax.broadcasted_iota(I32, (tq, tk), 0)
    col = lax.broadcasted_iota(I32, (tq, tk), 1)
    bucket = _t5_bucket(offset + col - row)
    for h in range(B_HEADS):
        far = rb_ref[FAR_BUCKET * B_HEADS + h]
        acc = jnp.zeros((tq, tk), F32)
        for b in range(REL_BUCKETS):
            acc = jnp.where(bucket == b, rb_ref[b * B_HEADS + h] - far, acc)
        out_ref[0, h] = acc


def _bias_tiles(rel_bias, tq, tk):
    n_off = _n_near_offsets(tq, tk)
    grid_spec = pltpu.PrefetchScalarGridSpec(
        num_scalar_prefetch=1,
        grid=(n_off,),
        in_specs=[],
        out_specs=pl.BlockSpec((1, B_HEADS, tq, tk), lambda i, rb: (i, 0, 0, 0)),
    )
    return pl.pallas_call(
        functools.partial(_bias_tiles_kernel, tq=tq, tk=tk),
        out_shape=jax.ShapeDtypeStruct((n_off, B_HEADS, tq, tk), F32),
        grid_spec=grid_spec,
        compiler_params=_cparams(("arbitrary",)),
        name="dsa_bias_tiles",
    )(rel_bias.reshape(-1))


def _proj1_kernel(x_ref, w_ref, qn_ref, kvn_ref, cq_ref, ckv_ref, ckvt_ref, ki_ref, wi_ref,
                  *, o1, o2, o3, wscale):
    proj = _dot(x_ref[...].astype(BF16), w_ref[...])
    cq_ref[...] = _rms_norm_rows(proj[:, :o1], qn_ref[...]).astype(BF16)
    ckv = _rms_norm_rows(proj[:, o1:o2], kvn_ref[...])
    ckv_ref[...] = ckv.astype(BF16)
    ckvt_ref[...] = ckv.T.astype(BF16)
    ki_ref[...] = proj[:, o2:o3].astype(BF16)
    wi_ref[...] = proj[:, o3:] * wscale


def _proj1(x2, w_b, q_norm, kv_norm, ql, kvl, batch, seq):
    n, d = x2.shape
    wout = w_b.shape[1]
    o1, o2, o3 = ql, ql + kvl, ql + kvl + IDX_DIM
    tm = 512
    nt = seq // tm
    kern = functools.partial(_proj1_kernel, o1=o1, o2=o2, o3=o3,
                             wscale=(IDX_HEADS ** -0.5) * (IDX_DIM ** -0.5))
    return pl.pallas_call(
        kern,
        out_shape=(jax.ShapeDtypeStruct((n, ql), BF16), jax.ShapeDtypeStruct((n, kvl), BF16),
                   jax.ShapeDtypeStruct((batch * kvl, seq), BF16),
                   jax.ShapeDtypeStruct((n, IDX_DIM), BF16), jax.ShapeDtypeStruct((n, IDX_HEADS), F32)),
        grid=(n // tm,),
        in_specs=[pl.BlockSpec((tm, d), lambda i: (i, 0)),
                  pl.BlockSpec((d, wout), lambda i: (0, 0)),
                  pl.BlockSpec((1, ql), lambda i: (0, 0)),
                  pl.BlockSpec((1, kvl), lambda i: (0, 0))],
        out_specs=(pl.BlockSpec((tm, ql), lambda i: (i, 0)), pl.BlockSpec((tm, kvl), lambda i: (i, 0)),
                   pl.BlockSpec((kvl, tm), lambda i: (i // nt, i % nt)),
                   pl.BlockSpec((tm, IDX_DIM), lambda i: (i, 0)), pl.BlockSpec((tm, IDX_HEADS), lambda i: (i, 0))),
        compiler_params=_cparams(("arbitrary",)),
        name="dsa_proj",
    )(x2, w_b, q_norm.reshape(1, ql), kv_norm.reshape(1, kvl))


def _qside_kernel(cq_ref, wuq_ref, wuk_ref, wiq_ref, ql_ref, qi_ref, *, kvl, scale):
    cq = cq_ref[...]
    q = _dot(cq, wuq_ref[...]).astype(BF16)
    for h in range(B_HEADS):
        qh = q[:, h * B_HEAD_DIM:(h + 1) * B_HEAD_DIM]
        ql_ref[h] = (_dot_nt(qh, wuk_ref[h]) * scale).astype(BF16)
    qi_ref[...] = _dot(cq, wiq_ref[...]).astype(BF16)


def _qside(cq, w_uq_b, w_uk_b, w_iq_b):
    n, ql = cq.shape
    kvl = w_uk_b.shape[1]
    tm = 512
    kern = functools.partial(_qside_kernel, kvl=kvl, scale=B_HEAD_DIM ** -0.5)
    return pl.pallas_call(
        kern,
        out_shape=(jax.ShapeDtypeStruct((B_HEADS, n, kvl), BF16),
                   jax.ShapeDtypeStruct((n, IDX_HEADS * IDX_DIM), BF16)),
        grid=(n // tm,),
        in_specs=[pl.BlockSpec((tm, ql), lambda i: (i, 0)),
                  pl.BlockSpec(w_uq_b.shape, lambda i: (0, 0)),
                  pl.BlockSpec(w_uk_b.shape, lambda i: (0, 0, 0)),
                  pl.BlockSpec(w_iq_b.shape, lambda i: (0, 0))],
        out_specs=(pl.BlockSpec((B_HEADS, tm, kvl), lambda i: (0, i, 0)),
                   pl.BlockSpec((tm, IDX_HEADS * IDX_DIM), lambda i: (i, 0))),
        compiler_params=_cparams(("arbitrary",)),
        name="dsa_qside",
    )(cq, w_uq_b, w_uk_b, w_iq_b)


INT_MIN = -2 ** 31
KEY_NEG_INF = (0xFF800000 ^ 0x7FFFFFFF) - 2 ** 32


def _ordered_key(v):
    bits = lax.bitcast_convert_type(v, I32)
    return bits ^ (lax.shift_right_arithmetic(bits, 31) & 0x7FFFFFFF)


def _indexer_kernel(qi_ref, wi_ref, ki_ref, mask_ref, key_sc, *, tq, tkc, seq, topk):
    i = pl.program_id(1)
    n_chunks = (i * tq + tq + tkc - 1) // tkc
    row = lax.broadcasted_iota(I32, (tq, 1), 0)
    limit = (lax.shift_right_logical(i * tq + row, 6) + 1) * CHUNK
    w = wi_ref[...]
    key_sc[...] = jnp.full(key_sc.shape, KEY_NEG_INF, I32)

    def score_chunk(c, carry):
        start = pl.multiple_of(c * tkc, tkc)
        k = ki_ref[pl.ds(start, tkc), :]
        acc = jnp.zeros((tq, tkc), F32)
        for h in range(IDX_HEADS):
            sc = _dot_nt(qi_ref[:, h * IDX_DIM:(h + 1) * IDX_DIM], k)
            acc = acc + jnp.maximum(sc, 0.0) * w[:, h:h + 1]
        kpos = start + lax.broadcasted_iota(I32, (tq, tkc), 1)
        key_sc[:, pl.ds(start, tkc)] = _ordered_key(jnp.where(kpos < limit, acc, -jnp.inf))
        return carry

    lax.fori_loop(0, n_chunks, score_chunk, 0)

    def count(pred_fn):
        def body(c, acc):
            start = pl.multiple_of(c * tkc, tkc)
            hit = pred_fn(key_sc[:, pl.ds(start, tkc)]).astype(I32)
            part = hit[:, 0:LANES]
            for q in range(1, tkc // LANES):
                part = part + hit[:, q * LANES:(q + 1) * LANES]
            return acc + part
        acc = lax.fori_loop(0, n_chunks, body, jnp.zeros((tq, LANES), I32))
        return jnp.sum(acc, axis=1, keepdims=True)

    def bisect(it, prefix):
        cand = prefix + lax.shift_left(jnp.int32(1), 31 - it)
        cnt = count(lambda kk: kk >= cand)
        return jnp.where(cnt >= topk, cand, prefix)

    thr = lax.fori_loop(0, 32, bisect, jnp.full((tq, 1), INT_MIN, I32))
    n_gt = count(lambda kk: kk > thr)
    n_eq = count(lambda kk: kk == thr)
    need = topk - n_gt
    tie_break = jnp.max(jnp.where(jnp.logical_and(thr > KEY_NEG_INF, n_eq > need), 1, 0)) > 0

    mask_ref[...] = jnp.full(mask_ref.shape, NEG_BIG, F32)

    @pl.when(jnp.logical_not(tie_break))
    def _():
        def write(c, carry):
            start = pl.multiple_of(c * tkc, tkc)
            kk = key_sc[:, pl.ds(start, tkc)]
            sel = jnp.logical_and(kk >= thr, kk > KEY_NEG_INF)
            mask_ref[:, pl.ds(start, tkc)] = jnp.where(sel, 0.0, NEG_BIG)
            return carry
        lax.fori_loop(0, n_chunks, write, 0)

    @pl.when(tie_break)
    def _():
        r_ = lax.broadcasted_iota(I32, (LANES, LANES), 0)
        c_ = lax.broadcasted_iota(I32, (LANES, LANES), 1)
        upper = (r_ < c_).astype(BF16)

        def write(c, seen):
            start = pl.multiple_of(c * LANES, LANES)
            kk = key_sc[:, pl.ds(start, LANES)]
            eq = kk == thr
            before = seen + _dot(eq.astype(BF16), upper)
            sel = jnp.logical_or(kk > thr, jnp.logical_and(eq, before < need.astype(F32)))
            sel = jnp.logical_and(sel, kk > KEY_NEG_INF)
            mask_ref[:, pl.ds(start, LANES)] = jnp.where(sel, 0.0, NEG_BIG)
            return seen + jnp.sum(eq.astype(F32), axis=1, keepdims=True)
        lax.fori_loop(0, n_chunks * (tkc // LANES), write, jnp.zeros((tq, 1), F32))


def _indexer(qidx, widx, kidx, batch, seq, topk):
    tq = IDX_TQ
    tkc = min(IDX_TKC, seq)
    nq = seq // tq
    kern = functools.partial(_indexer_kernel, tq=tq, tkc=tkc, seq=seq, topk=topk)
    return pl.pallas_call(
        kern,
        out_shape=jax.ShapeDtypeStruct((batch * seq, seq), F32),
        grid=(batch, nq),
        in_specs=[pl.BlockSpec((tq, IDX_HEADS * IDX_DIM), lambda b, i: (b * nq + i, 0)),
                  pl.BlockSpec((tq, IDX_HEADS), lambda b, i: (b * nq + i, 0)),
                  pl.BlockSpec((seq, IDX_DIM), lambda b, i: (b, 0))],
        out_specs=pl.BlockSpec((tq, seq), lambda b, i: (b * nq + i, 0)),
        scratch_shapes=[pltpu.VMEM((tq, seq), I32)],
        compiler_params=_cparams(("arbitrary", "arbitrary")),
        name="dsa_indexer",
    )(qidx, widx, kidx)


def _dsa_attn_kernel(qi_ref, kj_ref, ql_ref, kt_ref, kv_ref, mask_ref, bias_ref, wuv_ref, o_ref,
                     m_sc, l_sc, acc_sc, *, tq, tk, kvl, n_near):
    p = pl.program_id(1)
    i = qi_ref[p]
    j = kj_ref[p]
    nh = B_HEADS
    behind = (i * tq - j * tk) // tq

    @pl.when(j == 0)
    def _():
        m_sc[...] = jnp.full_like(m_sc, -jnp.inf)
        l_sc[...] = jnp.zeros_like(l_sc)
        acc_sc[...] = jnp.zeros_like(acc_sc)

    def step(near):
        kt = kt_ref[...]
        kv = kv_ref[...]
        msk = mask_ref[...]
        for h in range(nh):
            s = _dot(ql_ref[h], kt) + msk
            if near:
                s = s + bias_ref[behind, h]
            m_prev = m_sc[h]
            m_new = jnp.maximum(m_prev, jnp.max(s, axis=1, keepdims=True))
            alpha = jnp.exp(m_prev - m_new)
            pexp = jnp.exp(s - _lane_tile(m_new, tk))
            l_sc[h] = alpha * l_sc[h] + jnp.sum(pexp, axis=1, keepdims=True)
            acc_sc[h] = _lane_tile(alpha, kvl) * acc_sc[h] + _dot(pexp.astype(BF16), kv)
            m_sc[h] = m_new

    @pl.when(behind >= n_near)
    def _():
        step(False)

    @pl.when(behind < n_near)
    def _():
        step(True)

    @pl.when(j == (i * tq + tq - 1) // tk)
    def _():
        for h in range(nh):
            o_lat = (acc_sc[h] / _lane_tile(l_sc[h], kvl)).astype(BF16)
            o_ref[:, h * B_V_DIM:(h + 1) * B_V_DIM] = _dot(o_lat, wuv_ref[h]).astype(o_ref.dtype)


def _dsa_attention(ql, ckvt, ckv, mask, bias_tiles, w_uv_b, batch, seq):
    tq, tk = DSA_TQ, DSA_TK
    kvl = ckv.shape[1]
    nq = seq // tq
    nk = seq // tk
    last = [(i * tq + tq - 1) // tk for i in range(nq)]
    qi = np.concatenate([np.full(last[i] + 1, i) for i in range(nq)]).astype(np.int32)
    kj = np.concatenate([np.arange(last[i] + 1) for i in range(nq)]).astype(np.int32)
    kern = functools.partial(_dsa_attn_kernel, tq=tq, tk=tk, kvl=kvl, n_near=bias_tiles.shape[0])
    grid_spec = pltpu.PrefetchScalarGridSpec(
        num_scalar_prefetch=2,
        grid=(batch, len(qi)),
        in_specs=[
            pl.BlockSpec((B_HEADS, tq, kvl), lambda b, p, qi, kj: (0, b * nq + qi[p], 0)),
            pl.BlockSpec((kvl, tk), lambda b, p, qi, kj: (b, kj[p])),
            pl.BlockSpec((tk, kvl), lambda b, p, qi, kj: (b * nk + kj[p], 0)),
            pl.BlockSpec((tq, tk), lambda b, p, qi, kj: (b * nq + qi[p], kj[p])),
            pl.BlockSpec(bias_tiles.shape, lambda b, p, qi, kj: (0, 0, 0, 0)),
            pl.BlockSpec(w_uv_b.shape, lambda b, p, qi, kj: (0, 0, 0)),
        ],
        out_specs=pl.BlockSpec((tq, B_HEADS * B_V_DIM), lambda b, p, qi, kj: (b * nq + qi[p], 0)),
        scratch_shapes=[pltpu.VMEM((B_HEADS, tq, LANES), F32), pltpu.VMEM((B_HEADS, tq, LANES), F32),
                        pltpu.VMEM((B_HEADS, tq, kvl), F32)],
    )
    return pl.pallas_call(
        kern,
        out_shape=jax.ShapeDtypeStruct((batch * seq, B_HEADS * B_V_DIM), BF16),
        grid_spec=grid_spec,
        compiler_params=_cparams(("arbitrary", "arbitrary")),
        name="dsa_attention",
    )(jnp.asarray(qi), jnp.asarray(kj), ql, ckvt, ckv, mask, bias_tiles, w_uv_b)


def _dsa_layer(x2, w_in, q_norm, kv_norm, w_uq, w_iq, w_uk, w_uv, w_out, rel_bias, ln_g, ln_b, batch, seq):
    ql_dim = q_norm.shape[0]
    kvl = kv_norm.shape[0]
    topk = min(IDX_TOPK, seq // 4)
    cq, ckv, ckvt, kidx, widx = _proj1(x2, w_in.astype(BF16), q_norm, kv_norm, ql_dim, kvl, batch, seq)
    ql, qidx = _qside(cq, w_uq.astype(BF16), w_uk.astype(BF16), w_iq.astype(BF16))
    mask = _indexer(qidx, widx, kidx, batch, seq, topk)
    bias_tiles = _bias_tiles(rel_bias, DSA_TQ, DSA_TK)
    o = _dsa_attention(ql, ckvt, ckv, mask, bias_tiles, w_uv.astype(BF16), batch, seq)
    return _outproj_ln(o, w_out.astype(BF16), x2, ln_g, ln_b)


def kernel(x, a_w_in, a_b_f, a_w_out, b_w_in, b_q_norm, b_kv_norm, b_w_uq, b_w_iq, b_w_uk, b_w_uv, b_w_out,
           rel_bias, ln1_g, ln1_b, ln2_g, ln2_b, router_w, router_b, w_gate, w_up, w_down, sh_gate, sh_up,
           sh_down):
    batch, seq, d = x.shape
    x2 = x.reshape(batch * seq, d)
    x2 = _fox_layer(x2, a_w_in[0], a_b_f[0], a_w_out[0], ln1_g[0], ln1_b[0], batch, seq)
    x2 = _moe_layer(x2, router_w[0], router_b[0], w_gate, w_up, w_down, 0, sh_gate[0], sh_up[0],
                    sh_down[0], ln2_g[0], ln2_b[0])
    x2 = _dsa_layer(x2, b_w_in[0], b_q_norm[0], b_kv_norm[0], b_w_uq[0], b_w_iq[0], b_w_uk[0], b_w_uv[0],
                    b_w_out[0], rel_bias, ln1_g[1], ln1_b[1], batch, seq)
    x2 = _moe_layer(x2, router_w[1], router_b[1], w_gate, w_up, w_down, 1, sh_gate[1], sh_up[1],
                    sh_down[1], ln2_g[1], ln2_b[1])
    return x2.reshape(batch, seq, d)
```

```python
import functools
import math

import numpy as np
import jax
import jax.numpy as jnp
from jax import lax
from jax.experimental import pallas as pl
from jax.experimental.pallas import tpu as pltpu

BF16 = jnp.bfloat16
F32 = jnp.float32
I32 = jnp.int32
U32 = jnp.uint32

A_HEADS = 16
A_HEAD_DIM = 128
B_HEADS = 16
B_HEAD_DIM = 128
B_V_DIM = 128
IDX_HEADS = 16
IDX_DIM = 64
IDX_TOPK = 256
CHUNK = 64
REL_BUCKETS = 32
REL_MAX_DIST = 128
N_EXPERTS = 64
TOP_K = 8
ROUTED_SCALE = 2.5
DEPTH = 2
DEEPNORM_ALPHA = (2 * DEPTH) ** 0.25
LN_EPS = 1e-5
RMS_EPS = 1e-6

LANES = 128
SLAB = 8
VMEM_LIMIT = 56 * 1024 * 1024
LOG2E = math.log2(math.e)
NEG_BIG = -1e30

EXPERT_ROWS = 256
FOX_TQ = 512
FOX_TK = 512
FOX_HEADS_PER_STEP = 4
DSA_TQ = 128
DSA_TK = 256
IDX_TQ = 128
IDX_TKC = 512


def _cparams(sem, vmem=VMEM_LIMIT):
    return pltpu.CompilerParams(dimension_semantics=sem, vmem_limit_bytes=vmem)


def _dot(a, b):
    return jnp.dot(a, b, preferred_element_type=F32)


def _dot_nt(a, b):
    return lax.dot_general(a, b, (((1,), (1,)), ((), ())), preferred_element_type=F32)


def _lane_tile(a, width):
    return a if width == LANES else jnp.concatenate([a] * (width // LANES), axis=1)


def _split2(a):
    hi = a.astype(BF16)
    lo = (a - hi.astype(F32)).astype(BF16)
    return hi, lo


def _split3(a):
    hi = a.astype(BF16)
    r = a - hi.astype(F32)
    mid = r.astype(BF16)
    lo = (r - mid.astype(F32)).astype(BF16)
    return hi, mid, lo


def _dot_x3(a, b):
    ah, al = _split2(a)
    bh, bl = _split2(b)
    return _dot(ah, bh) + (_dot(ah, bl) + _dot(al, bh))


def _layer_norm_rows(z, g, b):
    mu = jnp.mean(z, axis=-1, keepdims=True)
    d = z - mu
    var = jnp.mean(d * d, axis=-1, keepdims=True)
    return d * lax.rsqrt(var + LN_EPS) * g + b


def _rms_norm_rows(z, g):
    ms = jnp.mean(z * z, axis=-1, keepdims=True)
    return z * lax.rsqrt(ms + RMS_EPS) * g


def _proj0_kernel(x_ref, w_ref, wf_ref, qkv_ref, fl_ref, xb_sc, *, n_q_blocks, q_scale):
    j = pl.program_id(1)

    @pl.when(j == 0)
    def _():
        x = x_ref[...]
        xb_sc[...] = x.astype(BF16)
        fl_ref[...] = _dot_x3(x, wf_ref[...])

    acc = _dot(xb_sc[...], w_ref[...])
    scale = jnp.where(j < n_q_blocks, q_scale, 1.0).astype(F32)
    qkv_ref[...] = (acc * scale).astype(BF16)


def _proj0(x2, w_qkv_b, w_f):
    n, d = x2.shape
    nout = w_qkv_b.shape[1]
    tm, tn = 1024, 512
    dq = A_HEADS * A_HEAD_DIM
    kern = functools.partial(_proj0_kernel, n_q_blocks=dq // tn, q_scale=A_HEAD_DIM ** -0.5 * LOG2E)
    return pl.pallas_call(
        kern,
        out_shape=(jax.ShapeDtypeStruct((n, nout), BF16),
                   jax.ShapeDtypeStruct((n, A_HEADS), F32)),
        grid=(n // tm, nout // tn),
        in_specs=[pl.BlockSpec((tm, d), lambda i, j: (i, 0)),
                  pl.BlockSpec((d, tn), lambda i, j: (0, j)),
                  pl.BlockSpec((d, A_HEADS), lambda i, j: (0, 0))],
        out_specs=(pl.BlockSpec((tm, tn), lambda i, j: (i, j)),
                   pl.BlockSpec((tm, A_HEADS), lambda i, j: (i, 0))),
        scratch_shapes=[pltpu.VMEM((tm, d), BF16)],
        compiler_params=_cparams(("arbitrary", "arbitrary")),
        name="fox_proj",
    )(x2, w_qkv_b, w_f)


def _forget_cumsum_kernel(fl_ref, bf_ref, f_ref, carry_sc, *, t):
    @pl.when(pl.program_id(1) == 0)
    def _():
        carry_sc[...] = jnp.zeros_like(carry_sc)

    z = fl_ref[...] + bf_ref[...]
    logf = jnp.minimum(z, 0.0) - jnp.log1p(jnp.exp(-jnp.abs(z)))
    row = lax.broadcasted_iota(I32, (t, t), 0)
    col = lax.broadcasted_iota(I32, (t, t), 1)
    tri = (col <= row).astype(BF16)
    hi, mid, lo = _split3(logf)
    cs = _dot(tri, hi) + (_dot(tri, mid) + _dot(tri, lo)) + carry_sc[...]
    f_ref[...] = cs * LOG2E
    carry_sc[...] = cs[t - 1:t, :]


def _forget_cumsum(fl, b_f, batch, seq):
    t = 256
    nb = seq // t
    return pl.pallas_call(
        functools.partial(_forget_cumsum_kernel, t=t),
        out_shape=jax.ShapeDtypeStruct(fl.shape, F32),
        grid=(batch, nb),
        in_specs=[pl.BlockSpec((t, A_HEADS), lambda b, i: (b * nb + i, 0)),
                  pl.BlockSpec((1, A_HEADS), lambda b, i: (0, 0))],
        out_specs=pl.BlockSpec((t, A_HEADS), lambda b, i: (b * nb + i, 0)),
        scratch_shapes=[pltpu.VMEM((1, A_HEADS), F32)],
        compiler_params=_cparams(("arbitrary", "arbitrary")),
        name="fox_forget_cumsum",
    )(fl, b_f.reshape(1, A_HEADS))


def _fox_attn_kernel(qi_ref, kj_ref, q_ref, k_ref, v_ref, fk_ref, o_ref, m_sc, l_sc, acc_sc, *, tq, tk, hp):
    p = pl.program_id(2)
    i = qi_ref[p]
    j = kj_ref[p]
    dh = A_HEAD_DIM

    @pl.when(j == 0)
    def _():
        m_sc[...] = jnp.full_like(m_sc, -jnp.inf)
        l_sc[...] = jnp.zeros_like(l_sc)
        acc_sc[...] = jnp.zeros_like(acc_sc)

    def step(diag):
        for hh in range(hp):
            cols = slice(hh * dh, (hh + 1) * dh)
            s = _dot_nt(q_ref[:, cols], k_ref[:, cols]) - fk_ref[hh]
            if diag:
                row = lax.broadcasted_iota(I32, (tq, tk), 0)
                col = lax.broadcasted_iota(I32, (tq, tk), 1)
                s = jnp.where(col <= row, s, -jnp.inf)
            m_prev = m_sc[hh]
            m_new = jnp.maximum(m_prev, jnp.max(s, axis=1, keepdims=True))
            alpha = jnp.exp2(m_prev - m_new)
            pexp = jnp.exp2(s - _lane_tile(m_new, tk))
            l_sc[hh] = alpha * l_sc[hh] + jnp.sum(pexp, axis=1, keepdims=True)
            acc_sc[hh] = alpha * acc_sc[hh] + _dot(pexp.astype(BF16), v_ref[:, cols])
            m_sc[hh] = m_new

    @pl.when(j < i)
    def _():
        step(False)

    @pl.when(j == i)
    def _():
        step(True)
        for hh in range(hp):
            o_ref[:, hh * dh:(hh + 1) * dh] = (acc_sc[hh] / l_sc[hh]).astype(o_ref.dtype)


def _fox_attention(qkv, f_rows, batch, seq):
    t = min(FOX_TQ, seq)
    nq = seq // t
    qi = np.concatenate([np.full(i + 1, i) for i in range(nq)]).astype(np.int32)
    kj = np.concatenate([np.arange(i + 1) for i in range(nq)]).astype(np.int32)
    hp = FOX_HEADS_PER_STEP
    hg = A_HEADS // hp
    dh = A_HEAD_DIM
    w = hp * dh
    kern = functools.partial(_fox_attn_kernel, tq=t, tk=t, hp=hp)
    grid_spec = pltpu.PrefetchScalarGridSpec(
        num_scalar_prefetch=2,
        grid=(batch, hg, len(qi)),
        in_specs=[
            pl.BlockSpec((t, w), lambda b, h, p, qi, kj: (b * nq + qi[p], h)),
            pl.BlockSpec((t, w), lambda b, h, p, qi, kj: (b * nq + kj[p], hg + h)),
            pl.BlockSpec((t, w), lambda b, h, p, qi, kj: (b * nq + kj[p], 2 * hg + h)),
            pl.BlockSpec((hp, 1, t), lambda b, h, p, qi, kj: (b * hg + h, 0, kj[p])),
        ],
        out_specs=pl.BlockSpec((t, w), lambda b, h, p, qi, kj: (b * nq + qi[p], h)),
        scratch_shapes=[pltpu.VMEM((hp, t, dh), F32), pltpu.VMEM((hp, t, dh), F32),
                        pltpu.VMEM((hp, t, dh), F32)],
    )
    return pl.pallas_call(
        kern,
        out_shape=jax.ShapeDtypeStruct((batch * seq, A_HEADS * dh), BF16),
        grid_spec=grid_spec,
        compiler_params=_cparams(("arbitrary", "arbitrary", "arbitrary")),
        name="fox_attention",
    )(jnp.asarray(qi), jnp.asarray(kj), qkv, qkv, qkv, f_rows)


def _outproj_ln_kernel(o_ref, w_ref, x_ref, g_ref, b_ref, y_ref):
    z = DEEPNORM_ALPHA * x_ref[...] + _dot(o_ref[...], w_ref[...])
    y_ref[...] = _layer_norm_rows(z, g_ref[...], b_ref[...])


def _outproj_ln(o, w_b, x2, g, b):
    n, d = x2.shape
    k = o.shape[1]
    tm = 512
    return pl.pallas_call(
        _outproj_ln_kernel,
        out_shape=jax.ShapeDtypeStruct((n, d), F32),
        grid=(n // tm,),
        in_specs=[pl.BlockSpec((tm, k), lambda i: (i, 0)),
                  pl.BlockSpec((k, d), lambda i: (0, 0)),
                  pl.BlockSpec((tm, d), lambda i: (i, 0)),
                  pl.BlockSpec((1, d), lambda i: (0, 0)),
                  pl.BlockSpec((1, d), lambda i: (0, 0))],
        out_specs=pl.BlockSpec((tm, d), lambda i: (i, 0)),
        compiler_params=_cparams(("arbitrary",)),
        name="outproj_deepnorm",
    )(o, w_b, x2, g.reshape(1, d), b.reshape(1, d))


def _fox_layer(x2, w_in, b_f, w_out, ln_g, ln_b, batch, seq):
    dq = A_HEADS * A_HEAD_DIM
    qkv, fl = _proj0(x2, w_in[:, :3 * dq].astype(BF16), w_in[:, 3 * dq:])
    f = _forget_cumsum(fl, b_f, batch, seq)
    f_rows = f.reshape(batch, seq, A_HEADS).transpose(0, 2, 1).reshape(batch * A_HEADS, 1, seq)
    o = _fox_attention(qkv, f_rows, batch, seq)
    return _outproj_ln(o, w_out.astype(BF16), x2, ln_g, ln_b)


def _router_kernel(x_ref, rw_ref, rb_ref, sel_ref, gate_ref, rank_ref, cnt_ref, carry_sc, *, tm):
    @pl.when(pl.program_id(0) == 0)
    def _():
        carry_sc[...] = jnp.zeros_like(carry_sc)

    e = N_EXPERTS
    scores = jax.nn.sigmoid(_dot_x3(x_ref[...], rw_ref[...]))
    lane = lax.broadcasted_iota(I32, (tm, e), 1)
    slot = lax.broadcasted_iota(I32, (tm, TOP_K), 1)
    work = scores + rb_ref[...]
    chosen = jnp.zeros((tm, e), F32)
    sel = jnp.zeros((tm, TOP_K), I32)
    gate = jnp.zeros((tm, TOP_K), F32)
    idxs = []
    for k in range(TOP_K):
        mx = jnp.max(work, axis=1, keepdims=True)
        idx = jnp.min(jnp.where(work == mx, lane, e), axis=1, keepdims=True)
        hit = lane == idx
        gk = jnp.sum(jnp.where(hit, scores, 0.0), axis=1, keepdims=True)
        work = jnp.where(hit, -jnp.inf, work)
        chosen = jnp.where(hit, 1.0, chosen)
        sel = jnp.where(slot == k, idx, sel)
        gate = jnp.where(slot == k, gk, gate)
        idxs.append(idx)
    gate = gate / jnp.sum(gate, axis=1, keepdims=True) * ROUTED_SCALE

    row = lax.broadcasted_iota(I32, (tm, tm), 0)
    col = lax.broadcasted_iota(I32, (tm, tm), 1)
    before = _dot((col < row).astype(BF16), chosen.astype(BF16)) + carry_sc[...]
    rank = jnp.zeros((tm, TOP_K), F32)
    for k in range(TOP_K):
        rk = jnp.sum(jnp.where(lane == idxs[k], before, 0.0), axis=1, keepdims=True)
        rank = jnp.where(slot == k, rk, rank)
    total = carry_sc[...] + jnp.sum(chosen, axis=0, keepdims=True)
    carry_sc[...] = total
    sel_ref[...] = sel
    gate_ref[...] = gate
    rank_ref[...] = rank.astype(I32)
    cnt_ref[...] = total.astype(I32)


def _router(x2, rw, rb):
    n, d = x2.shape
    tm = 512
    e = N_EXPERTS
    return pl.pallas_call(
        functools.partial(_router_kernel, tm=tm),
        out_shape=(jax.ShapeDtypeStruct((n, TOP_K), I32),
                   jax.ShapeDtypeStruct((n, TOP_K), F32),
                   jax.ShapeDtypeStruct((n, TOP_K), I32),
                   jax.ShapeDtypeStruct((1, e), I32)),
        grid=(n // tm,),
        in_specs=[pl.BlockSpec((tm, d), lambda i: (i, 0)),
                  pl.BlockSpec((d, e), lambda i: (0, 0)),
                  pl.BlockSpec((1, e), lambda i: (0, 0))],
        out_specs=(pl.BlockSpec((tm, TOP_K), lambda i: (i, 0)),
                   pl.BlockSpec((tm, TOP_K), lambda i: (i, 0)),
                   pl.BlockSpec((tm, TOP_K), lambda i: (i, 0)),
                   pl.BlockSpec((1, e), lambda i: (0, 0))),
        scratch_shapes=[pltpu.VMEM((1, e), F32)],
        compiler_params=_cparams(("arbitrary",)),
        name="moe_router",
    )(x2, rw, rb.reshape(1, e))


def _pack_bf16_pairs(y):
    w = y.shape[1] // 2
    lo = lax.bitcast_convert_type(y[:, :w].astype(BF16).astype(F32), U32)
    hi = lax.bitcast_convert_type(y[:, w:].astype(BF16).astype(F32), U32)
    return lax.shift_right_logical(lo, jnp.uint32(16)) | hi


def _unpack_bf16_pairs(words):
    lo = lax.bitcast_convert_type(lax.shift_left(words, jnp.uint32(16)), F32).astype(BF16)
    hi = lax.bitcast_convert_type(words & jnp.uint32(0xFFFF0000), F32).astype(BF16)
    return lo, hi


def _slab_load(ref, rows):
    return jnp.concatenate([ref[pl.ds(c, rows, stride=SLAB), :] for c in range(SLAB)], axis=1)


def _slab_store(ref, rows, val):
    for c in range(SLAB):
        ref[pl.ds(c, rows, stride=SLAB), :] = val[:, c * LANES:(c + 1) * LANES]


def _dispatch_kernel(pad_ref, dest_hbm, x_ref, xs_hbm, dest_sm, slab_sc, zero_sc, sem_idx, sem_zero, sem_row,
                     *, tt):
    i = pl.program_id(0)

    def zero_copy(e):
        start = pl.multiple_of(pad_ref[e] * SLAB, SLAB)
        return pltpu.make_async_copy(zero_sc, xs_hbm.at[pl.ds(start, EXPERT_ROWS * SLAB)], sem_zero)

    @pl.when(i == 0)
    def _():
        zero_sc[...] = jnp.zeros_like(zero_sc)

        def start(e, c):
            @pl.when(pad_ref[N_EXPERTS + e] > 0)
            def _():
                zero_copy(e).start()
            return c

        def wait(e, c):
            @pl.when(pad_ref[N_EXPERTS + e] > 0)
            def _():
                zero_copy(e).wait()
            return c

        lax.fori_loop(0, N_EXPERTS, start, 0)
        lax.fori_loop(0, N_EXPERTS, wait, 0)

    idx_copy = pltpu.make_async_copy(dest_hbm.at[i], dest_sm, sem_idx)
    idx_copy.start()
    _slab_store(slab_sc, tt, _pack_bf16_pairs(x_ref[...]))
    idx_copy.wait()

    def row_copy(t, dst_row):
        src = pl.multiple_of(t * SLAB, SLAB)
        dst = pl.multiple_of(dst_row * SLAB, SLAB)
        return pltpu.make_async_copy(slab_sc.at[pl.ds(src, SLAB)], xs_hbm.at[pl.ds(dst, SLAB)], sem_row)

    def start(t, c):
        for k in range(TOP_K):
            row_copy(t, dest_sm[t * TOP_K + k]).start(priority=k % 2)
        return c

    def wait(t, c):
        for k in range(TOP_K):
            row_copy(0, 0).wait()
        return c

    lax.fori_loop(0, tt, start, 0)
    lax.fori_loop(0, tt, wait, 0)


def _dispatch(pad_info, dest, x2, p_rows):
    n, d = x2.shape
    tt = 512
    assert TOP_K == 8
    assert d == 2 * SLAB * LANES
    dest2 = dest.reshape(n // tt, tt * TOP_K)
    grid_spec = pltpu.PrefetchScalarGridSpec(
        num_scalar_prefetch=1,
        grid=(n // tt,),
        in_specs=[pl.BlockSpec(memory_space=pl.ANY),
                  pl.BlockSpec((tt, d), lambda i, pad: (i, 0))],
        out_specs=pl.BlockSpec(memory_space=pl.ANY),
        scratch_shapes=[pltpu.SMEM((tt * TOP_K,), I32),
                        pltpu.VMEM((tt * SLAB, LANES), U32),
                        pltpu.VMEM((EXPERT_ROWS * SLAB, LANES), U32),
                        pltpu.SemaphoreType.DMA, pltpu.SemaphoreType.DMA, pltpu.SemaphoreType.DMA],
    )
    return pl.pallas_call(
        functools.partial(_dispatch_kernel, tt=tt),
        out_shape=jax.ShapeDtypeStruct((p_rows * SLAB, LANES), U32),
        grid_spec=grid_spec,
        compiler_params=_cparams(("arbitrary",)),
        name="moe_dispatch",
    )(pad_info, dest2, x2)


def _expert_mlp_kernel(be_ref, nu_ref, xs_ref, wg_ref, wu_ref, wd_ref, ys_ref, wg_sc, wu_sc, wd_sc):
    b = pl.program_id(0)
    e = be_ref[b]
    first = jnp.logical_or(b == 0, e != be_ref[jnp.maximum(b - 1, 0)])
    active = b < nu_ref[0]

    @pl.when(jnp.logical_and(active, first))
    def _():
        wg_sc[...] = wg_ref[...].astype(BF16)
        wu_sc[...] = wu_ref[...].astype(BF16)
        wd_sc[...] = wd_ref[...].astype(BF16)

    @pl.when(active)
    def _():
        r = EXPERT_ROWS
        half = wg_sc.shape[0] // 2
        x_lo, x_hi = _unpack_bf16_pairs(_slab_load(xs_ref, r))
        g = _dot(x_lo, wg_sc[:half]) + _dot(x_hi, wg_sc[half:])
        u = _dot(x_lo, wu_sc[:half]) + _dot(x_hi, wu_sc[half:])
        h = (g * jax.nn.sigmoid(g) * u).astype(BF16)
        _slab_store(ys_ref, r, _pack_bf16_pairs(_dot(h, wd_sc[...])))


def _expert_mlp(blk_e, n_used, xs, w_gate, w_up, w_down, layer):
    r = EXPERT_ROWS
    nblk = xs.shape[0] // (r * SLAB)
    d, de = w_gate.shape[2], w_gate.shape[3]

    def row_map(b, be, nu):
        return (jnp.minimum(b, nu[0] - 1), 0)

    def w_map(b, be, nu):
        return (layer, be[b], 0, 0)

    grid_spec = pltpu.PrefetchScalarGridSpec(
        num_scalar_prefetch=2,
        grid=(nblk,),
        in_specs=[pl.BlockSpec((r * SLAB, LANES), row_map),
                  pl.BlockSpec((None, None, d, de), w_map),
                  pl.BlockSpec((None, None, d, de), w_map),
                  pl.BlockSpec((None, None, de, d), w_map)],
        out_specs=pl.BlockSpec((r * SLAB, LANES), row_map),
        scratch_shapes=[pltpu.VMEM((d, de), BF16), pltpu.VMEM((d, de), BF16), pltpu.VMEM((de, d), BF16)],
    )
    return pl.pallas_call(
        _expert_mlp_kernel,
        out_shape=jax.ShapeDtypeStruct(xs.shape, U32),
        grid_spec=grid_spec,
        compiler_params=_cparams(("arbitrary",)),
        name="moe_experts",
    )(blk_e, n_used, xs, w_gate, w_up, w_down)


def _combine_kernel(dest_hbm, ys_hbm, x_ref, gate_ref, sg_ref, su_ref, sd_ref, g_ref, b_ref,
                    y_ref, dest_sm, rows_sc, sem_idx, sem_row, *, tt, n_tiles):
    i = pl.program_id(0)
    slot = lax.rem(i, 2)
    n_rows = tt * TOP_K

    def idx_copy(tile, s):
        return pltpu.make_async_copy(dest_hbm.at[tile], dest_sm.at[pl.ds(pl.multiple_of(s * n_rows, n_rows), n_rows)],
                                     sem_idx.at[s])

    def row_copy(s, t, k, src_row):
        return pltpu.make_async_copy(ys_hbm.at[pl.ds(pl.multiple_of(src_row * SLAB, SLAB), SLAB)],
                                     rows_sc.at[s, k, pl.ds(pl.multiple_of(t * SLAB, SLAB), SLAB)], sem_row.at[s])

    def start_rows(s):
        def start(t, c):
            for k in range(TOP_K):
                row_copy(s, t, k, dest_sm[s * n_rows + t * TOP_K + k]).start(priority=k % 2)
            return c
        lax.fori_loop(0, tt, start, 0)

    @pl.when(i == 0)
    def _():
        idx_copy(0, 0).start()
        idx_copy(0, 0).wait()
        start_rows(0)
        if n_tiles > 1:
            idx_copy(1, 1).start()

    @pl.when(i + 1 < n_tiles)
    def _():
        idx_copy(i + 1, 1 - slot).wait()
        start_rows(1 - slot)

    @pl.when(i + 2 < n_tiles)
    def _():
        idx_copy(i + 2, slot).start()

    x = x_ref[...]
    xb = x.astype(BF16)
    g = _dot(xb, sg_ref[...])
    u = _dot(xb, su_ref[...])
    h = (g * jax.nn.sigmoid(g) * u).astype(BF16)
    z = DEEPNORM_ALPHA * x + _dot(h, sd_ref[...])

    def wait(t, c):
        for k in range(TOP_K):
            row_copy(slot, 0, 0, 0).wait()
        return c
    lax.fori_loop(0, tt, wait, 0)

    gate = gate_ref[...]
    lo_pieces, hi_pieces = [], []
    for c in range(SLAB):
        acc_lo = jnp.zeros((tt, LANES), F32)
        acc_hi = jnp.zeros((tt, LANES), F32)
        for k in range(TOP_K):
            words = rows_sc[slot, k, pl.ds(c, tt, stride=SLAB), :]
            gk = gate[:, k:k + 1]
            acc_lo = acc_lo + gk * lax.bitcast_convert_type(lax.shift_left(words, jnp.uint32(16)), F32)
            acc_hi = acc_hi + gk * lax.bitcast_convert_type(words & jnp.uint32(0xFFFF0000), F32)
        lo_pieces.append(acc_lo)
        hi_pieces.append(acc_hi)
    routed = jnp.concatenate(lo_pieces + hi_pieces, axis=1)
    y_ref[...] = _layer_norm_rows(z + routed, g_ref[...], b_ref[...])


def _combine(dest, ys, x2, gate, sg_b, su_b, sd_b, ln_g, ln_b):
    n, d = x2.shape
    ds_ = sg_b.shape[1]
    tt = 256
    assert TOP_K == 8
    n_tiles = n // tt
    dest2 = dest.reshape(n_tiles, tt * TOP_K)
    return pl.pallas_call(
        functools.partial(_combine_kernel, tt=tt, n_tiles=n_tiles),
        out_shape=jax.ShapeDtypeStruct((n, d), F32),
        grid=(n_tiles,),
        in_specs=[pl.BlockSpec(memory_space=pl.ANY),
                  pl.BlockSpec(memory_space=pl.ANY),
                  pl.BlockSpec((tt, d), lambda i: (i, 0)),
                  pl.BlockSpec((tt, TOP_K), lambda i: (i, 0)),
                  pl.BlockSpec((d, ds_), lambda i: (0, 0)),
                  pl.BlockSpec((d, ds_), lambda i: (0, 0)),
                  pl.BlockSpec((ds_, d), lambda i: (0, 0)),
                  pl.BlockSpec((1, d), lambda i: (0, 0)),
                  pl.BlockSpec((1, d), lambda i: (0, 0))],
        out_specs=pl.BlockSpec((tt, d), lambda i: (i, 0)),
        scratch_shapes=[pltpu.SMEM((2 * tt * TOP_K,), I32),
                        pltpu.VMEM((2, TOP_K, tt * SLAB, LANES), U32),
                        pltpu.SemaphoreType.DMA((2,)), pltpu.SemaphoreType.DMA((2,))],
        compiler_params=_cparams(("arbitrary",)),
        name="moe_combine",
    )(dest2, ys, x2, gate, sg_b, su_b, sd_b, ln_g.reshape(1, d), ln_b.reshape(1, d))


def _moe_layer(x2, rw, rb, w_gate, w_up, w_down, layer, sh_gate, sh_up, sh_down, ln_g, ln_b):
    n, d = x2.shape
    e = N_EXPERTS
    r = EXPERT_ROWS
    sel, gate, rank, counts = _router(x2, rw, rb)
    counts = counts.reshape(e)
    padded = (counts + r - 1) // r * r
    pend = jnp.cumsum(padded)
    pstart = pend - padded
    dest = rank + jnp.sum(jnp.where(sel[..., None] == jnp.arange(e, dtype=I32), pstart.astype(I32), 0), axis=-1)
    p_rows = n * TOP_K + e * r
    nblk = p_rows // r
    blk_start = jnp.arange(nblk, dtype=I32) * r
    blk_e = jnp.minimum(jnp.sum((pend[None, :] <= blk_start[:, None]).astype(I32), axis=1), e - 1)
    n_used = (pend[-1] // r).astype(I32).reshape(1)
    pad_info = jnp.concatenate([jnp.maximum(pend - r, 0), padded]).astype(I32)
    xs = _dispatch(pad_info, dest.astype(I32), x2, p_rows)
    ys = _expert_mlp(blk_e, n_used, xs, w_gate, w_up, w_down, layer)
    return _combine(dest.astype(I32), ys, x2, gate, sh_gate.astype(BF16), sh_up.astype(BF16),
                    sh_down.astype(BF16), ln_g, ln_b)


FAR_BUCKET = REL_BUCKETS // 2 - 1


def _n_near_offsets(tq, tk):
    return (tk + REL_MAX_DIST - 1 + tq - 1) // tq


def _t5_bucket(rel):
    half = REL_BUCKETS // 2
    max_exact = half // 2
    n = jnp.abs(rel)
    large = max_exact + (jnp.log(jnp.maximum(n, 1).astype(F32) / max_exact)
                         / math.log(REL_MAX_DIST / max_exact) * (half - max_exact)).astype(I32)
    large = jnp.minimum(large, half - 1)
    return jnp.where(rel > 0, half, 0) + jnp.where(n < max_exact, n, large)


def _bias_tiles_kernel(rb_ref, out_ref, *, tq, tk):
    offset = -tq * pl.program_id(0)
    row = lax.broadcasted_iota(I32, (tq, tk), 0)
    col = lax.broadcasted_iota(I32, (tq, tk), 1)
    bucket = _t5_bucket(offset + col - row)
    for h in range(B_HEADS):
        far = rb_ref[FAR_BUCKET * B_HEADS + h]
        acc = jnp.zeros((tq, tk), F32)
        for b in range(REL_BUCKETS):
            acc = jnp.where(bucket == b, (rb_ref[b * B_HEADS + h] - far) * LOG2E, acc)
        out_ref[0, h] = acc


def _bias_tiles(rel_bias, tq, tk):
    n_off = _n_near_offsets(tq, tk)
    grid_spec = pltpu.PrefetchScalarGridSpec(
        num_scalar_prefetch=1,
        grid=(n_off,),
        in_specs=[],
        out_specs=pl.BlockSpec((1, B_HEADS, tq, tk), lambda i, rb: (i, 0, 0, 0)),
    )
    return pl.pallas_call(
        functools.partial(_bias_tiles_kernel, tq=tq, tk=tk),
        out_shape=jax.ShapeDtypeStruct((n_off, B_HEADS, tq, tk), F32),
        grid_spec=grid_spec,
        compiler_params=_cparams(("arbitrary",)),
        name="dsa_bias_tiles",
    )(rel_bias.reshape(-1))


def _proj1_kernel(x_ref, w_ref, qn_ref, kvn_ref, cq_ref, ckv_ref, ckvt_ref, ki_ref, wi_ref,
                  *, o1, o2, o3, wscale):
    proj = _dot(x_ref[...].astype(BF16), w_ref[...])
    cq_ref[...] = _rms_norm_rows(proj[:, :o1], qn_ref[...]).astype(BF16)
    ckv = _rms_norm_rows(proj[:, o1:o2], kvn_ref[...])
    ckv_ref[...] = ckv.astype(BF16)
    ckvt_ref[...] = ckv.T.astype(BF16)
    ki_ref[...] = proj[:, o2:o3].astype(BF16)
    wi_ref[...] = proj[:, o3:] * wscale


def _proj1(x2, w_b, q_norm, kv_norm, ql, kvl, batch, seq):
    n, d = x2.shape
    wout = w_b.shape[1]
    o1, o2, o3 = ql, ql + kvl, ql + kvl + IDX_DIM
    tm = 512
    nt = seq // tm
    kern = functools.partial(_proj1_kernel, o1=o1, o2=o2, o3=o3,
                             wscale=(IDX_HEADS ** -0.5) * (IDX_DIM ** -0.5))
    return pl.pallas_call(
        kern,
        out_shape=(jax.ShapeDtypeStruct((n, ql), BF16), jax.ShapeDtypeStruct((n, kvl), BF16),
                   jax.ShapeDtypeStruct((batch * kvl, seq), BF16),
                   jax.ShapeDtypeStruct((n, IDX_DIM), BF16), jax.ShapeDtypeStruct((n, IDX_HEADS), F32)),
        grid=(n // tm,),
        in_specs=[pl.BlockSpec((tm, d), lambda i: (i, 0)),
                  pl.BlockSpec((d, wout), lambda i: (0, 0)),
                  pl.BlockSpec((1, ql), lambda i: (0, 0)),
                  pl.BlockSpec((1, kvl), lambda i: (0, 0))],
        out_specs=(pl.BlockSpec((tm, ql), lambda i: (i, 0)), pl.BlockSpec((tm, kvl), lambda i: (i, 0)),
                   pl.BlockSpec((kvl, tm), lambda i: (i // nt, i % nt)),
                   pl.BlockSpec((tm, IDX_DIM), lambda i: (i, 0)), pl.BlockSpec((tm, IDX_HEADS), lambda i: (i, 0))),
        compiler_params=_cparams(("arbitrary",)),
        name="dsa_proj",
    )(x2, w_b, q_norm.reshape(1, ql), kv_norm.reshape(1, kvl))


def _qside_kernel(cq_ref, wuq_ref, wuk_ref, wiq_ref, ql_ref, qi_ref, *, kvl, scale):
    cq = cq_ref[...]
    q = _dot(cq, wuq_ref[...]).astype(BF16)
    for h in range(B_HEADS):
        qh = q[:, h * B_HEAD_DIM:(h + 1) * B_HEAD_DIM]
        ql_ref[h] = (_dot_nt(qh, wuk_ref[h]) * scale).astype(BF16)
    qi_ref[...] = _dot(cq, wiq_ref[...]).astype(BF16)


def _qside(cq, w_uq_b, w_uk_b, w_iq_b):
    n, ql = cq.shape
    kvl = w_uk_b.shape[1]
    tm = 512
    kern = functools.partial(_qside_kernel, kvl=kvl, scale=B_HEAD_DIM ** -0.5 * LOG2E)
    return pl.pallas_call(
        kern,
        out_shape=(jax.ShapeDtypeStruct((B_HEADS, n, kvl), BF16),
                   jax.ShapeDtypeStruct((n, IDX_HEADS * IDX_DIM), BF16)),
        grid=(n // tm,),
        in_specs=[pl.BlockSpec((tm, ql), lambda i: (i, 0)),
                  pl.BlockSpec(w_uq_b.shape, lambda i: (0, 0)),
                  pl.BlockSpec(w_uk_b.shape, lambda i: (0, 0, 0)),
                  pl.BlockSpec(w_iq_b.shape, lambda i: (0, 0))],
        out_specs=(pl.BlockSpec((B_HEADS, tm, kvl), lambda i: (0, i, 0)),
                   pl.BlockSpec((tm, IDX_HEADS * IDX_DIM), lambda i: (i, 0))),
        compiler_params=_cparams(("arbitrary",)),
        name="dsa_qside",
    )(cq, w_uq_b, w_uk_b, w_iq_b)


INT_MIN = -2 ** 31
KEY_NEG_INF = (0xFF800000 ^ 0x7FFFFFFF) - 2 ** 32


def _ordered_key(v):
    bits = lax.bitcast_convert_type(v, I32)
    return bits ^ (lax.shift_right_arithmetic(bits, 31) & 0x7FFFFFFF)


def _indexer_kernel(qi_ref, wi_ref, ki_ref, mask_ref, key_sc, *, tq, tkc, seq, topk):
    i = pl.program_id(1)
    n_chunks = (i * tq + tq + tkc - 1) // tkc
    row = lax.broadcasted_iota(I32, (tq, 1), 0)
    limit = (lax.shift_right_logical(i * tq + row, 6) + 1) * CHUNK
    w = wi_ref[...]
    key_sc[...] = jnp.full(key_sc.shape, KEY_NEG_INF, I32)

    def score_chunk(c, carry):
        start = pl.multiple_of(c * tkc, tkc)
        k = ki_ref[pl.ds(start, tkc), :]
        acc = jnp.zeros((tq, tkc), F32)
        for h in range(IDX_HEADS):
            sc = _dot_nt(qi_ref[:, h * IDX_DIM:(h + 1) * IDX_DIM], k)
            acc = acc + jnp.maximum(sc, 0.0) * w[:, h:h + 1]
        kpos = start + lax.broadcasted_iota(I32, (tq, tkc), 1)
        key_sc[:, pl.ds(start, tkc)] = _ordered_key(jnp.where(kpos < limit, acc, -jnp.inf))
        return carry

    lax.fori_loop(0, n_chunks, score_chunk, 0)

    def count(pred_fn):
        def body(c, acc):
            start = pl.multiple_of(c * tkc, tkc)
            hit = pred_fn(key_sc[:, pl.ds(start, tkc)]).astype(I32)
            part = hit[:, 0:LANES]
            for q in range(1, tkc // LANES):
                part = part + hit[:, q * LANES:(q + 1) * LANES]
            return acc + part
        acc = lax.fori_loop(0, n_chunks, body, jnp.zeros((tq, LANES), I32))
        return jnp.sum(acc, axis=1, keepdims=True)

    def bisect(it, prefix):
        cand = prefix + lax.shift_left(jnp.int32(1), 31 - it)
        cnt = count(lambda kk: kk >= cand)
        return jnp.where(cnt >= topk, cand, prefix)

    thr = lax.fori_loop(0, 32, bisect, jnp.full((tq, 1), INT_MIN, I32))
    n_gt = count(lambda kk: kk > thr)
    n_eq = count(lambda kk: kk == thr)
    need = topk - n_gt
    tie_break = jnp.max(jnp.where(jnp.logical_and(thr > KEY_NEG_INF, n_eq > need), 1, 0)) > 0

    mask_ref[...] = jnp.full(mask_ref.shape, NEG_BIG, F32)

    @pl.when(jnp.logical_not(tie_break))
    def _():
        def write(c, carry):
            start = pl.multiple_of(c * tkc, tkc)
            kk = key_sc[:, pl.ds(start, tkc)]
            sel = jnp.logical_and(kk >= thr, kk > KEY_NEG_INF)
            mask_ref[:, pl.ds(start, tkc)] = jnp.where(sel, 0.0, NEG_BIG)
            return carry
        lax.fori_loop(0, n_chunks, write, 0)

    @pl.when(tie_break)
    def _():
        r_ = lax.broadcasted_iota(I32, (LANES, LANES), 0)
        c_ = lax.broadcasted_iota(I32, (LANES, LANES), 1)
        upper = (r_ < c_).astype(BF16)

        def write(c, seen):
            start = pl.multiple_of(c * LANES, LANES)
            kk = key_sc[:, pl.ds(start, LANES)]
            eq = kk == thr
            before = seen + _dot(eq.astype(BF16), upper)
            sel = jnp.logical_or(kk > thr, jnp.logical_and(eq, before < need.astype(F32)))
            sel = jnp.logical_and(sel, kk > KEY_NEG_INF)
            mask_ref[:, pl.ds(start, LANES)] = jnp.where(sel, 0.0, NEG_BIG)
            return seen + jnp.sum(eq.astype(F32), axis=1, keepdims=True)
        lax.fori_loop(0, n_chunks * (tkc // LANES), write, jnp.zeros((tq, 1), F32))


def _indexer(qidx, widx, kidx, batch, seq, topk):
    tq = IDX_TQ
    tkc = min(IDX_TKC, seq)
    nq = seq // tq
    kern = functools.partial(_indexer_kernel, tq=tq, tkc=tkc, seq=seq, topk=topk)
    return pl.pallas_call(
        kern,
        out_shape=jax.ShapeDtypeStruct((batch * seq, seq), F32),
        grid=(batch, nq),
        in_specs=[pl.BlockSpec((tq, IDX_HEADS * IDX_DIM), lambda b, i: (b * nq + i, 0)),
                  pl.BlockSpec((tq, IDX_HEADS), lambda b, i: (b * nq + i, 0)),
                  pl.BlockSpec((seq, IDX_DIM), lambda b, i: (b, 0))],
        out_specs=pl.BlockSpec((tq, seq), lambda b, i: (b * nq + i, 0)),
        scratch_shapes=[pltpu.VMEM((tq, seq), I32)],
        compiler_params=_cparams(("arbitrary", "arbitrary")),
        name="dsa_indexer",
    )(qidx, widx, kidx)


def _dsa_attn_kernel(qi_ref, kj_ref, ql_ref, kt_ref, kv_ref, mask_ref, bias_ref, wuv_ref, o_ref,
                     m_sc, l_sc, acc_sc, *, tq, tk, kvl, n_near):
    p = pl.program_id(1)
    i = qi_ref[p]
    j = kj_ref[p]
    nh = B_HEADS
    behind = (i * tq - j * tk) // tq

    @pl.when(j == 0)
    def _():
        m_sc[...] = jnp.full_like(m_sc, -jnp.inf)
        l_sc[...] = jnp.zeros_like(l_sc)
        acc_sc[...] = jnp.zeros_like(acc_sc)

    def step(near):
        kt = kt_ref[...]
        kv = kv_ref[...]
        msk = mask_ref[...]
        for h in range(nh):
            s = _dot(ql_ref[h], kt) + msk
            if near:
                s = s + bias_ref[behind, h]
            m_prev = m_sc[h]
            m_new = jnp.maximum(m_prev, jnp.max(s, axis=1, keepdims=True))
            alpha = jnp.exp2(m_prev - m_new)
            pexp = jnp.exp2(s - _lane_tile(m_new, tk))
            l_sc[h] = alpha * l_sc[h] + jnp.sum(pexp, axis=1, keepdims=True)
            acc_sc[h] = _lane_tile(alpha, kvl) * acc_sc[h] + _dot(pexp.astype(BF16), kv)
            m_sc[h] = m_new

    @pl.when(behind >= n_near)
    def _():
        step(False)

    @pl.when(behind < n_near)
    def _():
        step(True)

    @pl.when(j == (i * tq + tq - 1) // tk)
    def _():
        for h in range(nh):
            o_lat = (acc_sc[h] / _lane_tile(l_sc[h], kvl)).astype(BF16)
            o_ref[:, h * B_V_DIM:(h + 1) * B_V_DIM] = _dot(o_lat, wuv_ref[h]).astype(o_ref.dtype)


def _dsa_attention(ql, ckvt, ckv, mask, bias_tiles, w_uv_b, batch, seq):
    tq, tk = DSA_TQ, DSA_TK
    kvl = ckv.shape[1]
    nq = seq // tq
    nk = seq // tk
    last = [(i * tq + tq - 1) // tk for i in range(nq)]
    qi = np.concatenate([np.full(last[i] + 1, i) for i in range(nq)]).astype(np.int32)
    kj = np.concatenate([np.arange(last[i] + 1) for i in range(nq)]).astype(np.int32)
    kern = functools.partial(_dsa_attn_kernel, tq=tq, tk=tk, kvl=kvl, n_near=bias_tiles.shape[0])
    grid_spec = pltpu.PrefetchScalarGridSpec(
        num_scalar_prefetch=2,
        grid=(batch, len(qi)),
        in_specs=[
            pl.BlockSpec((B_HEADS, tq, kvl), lambda b, p, qi, kj: (0, b * nq + qi[p], 0)),
            pl.BlockSpec((kvl, tk), lambda b, p, qi, kj: (b, kj[p])),
            pl.BlockSpec((tk, kvl), lambda b, p, qi, kj: (b * nk + kj[p], 0)),
            pl.BlockSpec((tq, tk), lambda b, p, qi, kj: (b * nq + qi[p], kj[p])),
            pl.BlockSpec(bias_tiles.shape, lambda b, p, qi, kj: (0, 0, 0, 0)),
            pl.BlockSpec(w_uv_b.shape, lambda b, p, qi, kj: (0, 0, 0)),
        ],
        out_specs=pl.BlockSpec((tq, B_HEADS * B_V_DIM), lambda b, p, qi, kj: (b * nq + qi[p], 0)),
        scratch_shapes=[pltpu.VMEM((B_HEADS, tq, LANES), F32), pltpu.VMEM((B_HEADS, tq, LANES), F32),
                        pltpu.VMEM((B_HEADS, tq, kvl), F32)],
    )
    return pl.pallas_call(
        kern,
        out_shape=jax.ShapeDtypeStruct((batch * seq, B_HEADS * B_V_DIM), BF16),
        grid_spec=grid_spec,
        compiler_params=_cparams(("arbitrary", "arbitrary")),
        name="dsa_attention",
    )(jnp.asarray(qi), jnp.asarray(kj), ql, ckvt, ckv, mask, bias_tiles, w_uv_b)


def _dsa_layer(x2, w_in, q_norm, kv_norm, w_uq, w_iq, w_uk, w_uv, w_out, rel_bias, ln_g, ln_b, batch, seq):
    ql_dim = q_norm.shape[0]
    kvl = kv_norm.shape[0]
    topk = min(IDX_TOPK, seq // 4)
    cq, ckv, ckvt, kidx, widx = _proj1(x2, w_in.astype(BF16), q_norm, kv_norm, ql_dim, kvl, batch, seq)
    ql, qidx = _qside(cq, w_uq.astype(BF16), w_uk.astype(BF16), w_iq.astype(BF16))
    mask = _indexer(qidx, widx, kidx, batch, seq, topk)
    bias_tiles = _bias_tiles(rel_bias, DSA_TQ, DSA_TK)
    o = _dsa_attention(ql, ckvt, ckv, mask, bias_tiles, w_uv.astype(BF16), batch, seq)
    return _outproj_ln(o, w_out.astype(BF16), x2, ln_g, ln_b)


def kernel(x, a_w_in, a_b_f, a_w_out, b_w_in, b_q_norm, b_kv_norm, b_w_uq, b_w_iq, b_w_uk, b_w_uv, b_w_out,
           rel_bias, ln1_g, ln1_b, ln2_g, ln2_b, router_w, router_b, w_gate, w_up, w_down, sh_gate, sh_up,
           sh_down):
    batch, seq, d = x.shape
    x2 = x.reshape(batch * seq, d)
    x2 = _fox_layer(x2, a_w_in[0], a_b_f[0], a_w_out[0], ln1_g[0], ln1_b[0], batch, seq)
    x2 = _moe_layer(x2, router_w[0], router_b[0], w_gate, w_up, w_down, 0, sh_gate[0], sh_up[0],
                    sh_down[0], ln2_g[0], ln2_b[0])
    x2 = _dsa_layer(x2, b_w_in[0], b_q_norm[0], b_kv_norm[0], b_w_uq[0], b_w_iq[0], b_w_uk[0], b_w_uv[0],
                    b_w_out[0], rel_bias, ln1_g[1], ln1_b[1], batch, seq)
    x2 = _moe_layer(x2, router_w[1], router_b[1], w_gate, w_up, w_down, 1, sh_gate[1], sh_up[1],
                    sh_down[1], ln2_g[1], ln2_b[1])
    return x2.reshape(batch, seq, d)
```

```python
import functools
import math

import numpy as np
import jax
import jax.numpy as jnp
from jax import lax
from jax.experimental import pallas as pl
from jax.experimental.pallas import tpu as pltpu

BF16 = jnp.bfloat16
F32 = jnp.float32
I32 = jnp.int32
U32 = jnp.uint32

A_HEADS = 16
A_HEAD_DIM = 128
B_HEADS = 16
B_HEAD_DIM = 128
B_V_DIM = 128
IDX_HEADS = 16
IDX_DIM = 64
IDX_TOPK = 256
CHUNK = 64
REL_BUCKETS = 32
REL_MAX_DIST = 128
N_EXPERTS = 64
TOP_K = 8
ROUTED_SCALE = 2.5
DEPTH = 2
DEEPNORM_ALPHA = (2 * DEPTH) ** 0.25
LN_EPS = 1e-5
RMS_EPS = 1e-6

LANES = 128
SLAB = 8
VMEM_LIMIT = 56 * 1024 * 1024
LOG2E = math.log2(math.e)
NEG_BIG = -1e30

EXPERT_ROWS = 512
FOX_TQ = 512
FOX_TK = 512
FOX_HEADS_PER_STEP = 4
DSA_TQ = 128
DSA_TK = 256
IDX_TQ = 128
IDX_TKC = 512


def _cparams(sem, vmem=VMEM_LIMIT):
    return pltpu.CompilerParams(dimension_semantics=sem, vmem_limit_bytes=vmem)


def _dot(a, b):
    return jnp.dot(a, b, preferred_element_type=F32)


def _dot_nt(a, b):
    return lax.dot_general(a, b, (((1,), (1,)), ((), ())), preferred_element_type=F32)


def _lane_tile(a, width):
    return a if width == LANES else jnp.concatenate([a] * (width // LANES), axis=1)


def _split2(a):
    hi = a.astype(BF16)
    lo = (a - hi.astype(F32)).astype(BF16)
    return hi, lo


def _split3(a):
    hi = a.astype(BF16)
    r = a - hi.astype(F32)
    mid = r.astype(BF16)
    lo = (r - mid.astype(F32)).astype(BF16)
    return hi, mid, lo


def _dot_x3(a, b):
    ah, al = _split2(a)
    bh, bl = _split2(b)
    return _dot(ah, bh) + (_dot(ah, bl) + _dot(al, bh))


def _layer_norm_rows(z, g, b):
    mu = jnp.mean(z, axis=-1, keepdims=True)
    d = z - mu
    var = jnp.mean(d * d, axis=-1, keepdims=True)
    return d * lax.rsqrt(var + LN_EPS) * g + b


def _rms_norm_rows(z, g):
    ms = jnp.mean(z * z, axis=-1, keepdims=True)
    return z * lax.rsqrt(ms + RMS_EPS) * g


def _proj0_kernel(x_ref, w_ref, wf_ref, qkv_ref, fl_ref, xb_sc, *, n_q_blocks, q_scale):
    j = pl.program_id(1)

    @pl.when(j == 0)
    def _():
        x = x_ref[...]
        xb_sc[...] = x.astype(BF16)
        fl_ref[...] = _dot_x3(x, wf_ref[...])

    acc = _dot(xb_sc[...], w_ref[...])
    scale = jnp.where(j < n_q_blocks, q_scale, 1.0).astype(F32)
    qkv_ref[...] = (acc * scale).astype(BF16)


def _proj0(x2, w_qkv_b, w_f):
    n, d = x2.shape
    nout = w_qkv_b.shape[1]
    tm, tn = 1024, 512
    dq = A_HEADS * A_HEAD_DIM
    kern = functools.partial(_proj0_kernel, n_q_blocks=dq // tn, q_scale=A_HEAD_DIM ** -0.5 * LOG2E)
    return pl.pallas_call(
        kern,
        out_shape=(jax.ShapeDtypeStruct((n, nout), BF16),
                   jax.ShapeDtypeStruct((n, A_HEADS), F32)),
        grid=(n // tm, nout // tn),
        in_specs=[pl.BlockSpec((tm, d), lambda i, j: (i, 0)),
                  pl.BlockSpec((d, tn), lambda i, j: (0, j)),
                  pl.BlockSpec((d, A_HEADS), lambda i, j: (0, 0))],
        out_specs=(pl.BlockSpec((tm, tn), lambda i, j: (i, j)),
                   pl.BlockSpec((tm, A_HEADS), lambda i, j: (i, 0))),
        scratch_shapes=[pltpu.VMEM((tm, d), BF16)],
        compiler_params=_cparams(("arbitrary", "arbitrary")),
        name="fox_proj",
    )(x2, w_qkv_b, w_f)


def _forget_cumsum_kernel(fl_ref, bf_ref, f_ref, carry_sc, *, t):
    @pl.when(pl.program_id(1) == 0)
    def _():
        carry_sc[...] = jnp.zeros_like(carry_sc)

    z = fl_ref[...] + bf_ref[...]
    logf = jnp.minimum(z, 0.0) - jnp.log1p(jnp.exp(-jnp.abs(z)))
    row = lax.broadcasted_iota(I32, (t, t), 0)
    col = lax.broadcasted_iota(I32, (t, t), 1)
    tri = (col <= row).astype(BF16)
    hi, mid, lo = _split3(logf)
    cs = _dot(tri, hi) + (_dot(tri, mid) + _dot(tri, lo)) + carry_sc[...]
    f_ref[...] = cs * LOG2E
    carry_sc[...] = cs[t - 1:t, :]


def _forget_cumsum(fl, b_f, batch, seq):
    t = 256
    nb = seq // t
    return pl.pallas_call(
        functools.partial(_forget_cumsum_kernel, t=t),
        out_shape=jax.ShapeDtypeStruct(fl.shape, F32),
        grid=(batch, nb),
        in_specs=[pl.BlockSpec((t, A_HEADS), lambda b, i: (b * nb + i, 0)),
                  pl.BlockSpec((1, A_HEADS), lambda b, i: (0, 0))],
        out_specs=pl.BlockSpec((t, A_HEADS), lambda b, i: (b * nb + i, 0)),
        scratch_shapes=[pltpu.VMEM((1, A_HEADS), F32)],
        compiler_params=_cparams(("arbitrary", "arbitrary")),
        name="fox_forget_cumsum",
    )(fl, b_f.reshape(1, A_HEADS))


def _fox_attn_kernel(qi_ref, kj_ref, q_ref, k_ref, v_ref, fk_ref, o_ref, m_sc, l_sc, acc_sc, *, tq, tk, hp):
    p = pl.program_id(2)
    i = qi_ref[p]
    j = kj_ref[p]
    dh = A_HEAD_DIM

    @pl.when(j == 0)
    def _():
        m_sc[...] = jnp.full_like(m_sc, -jnp.inf)
        l_sc[...] = jnp.zeros_like(l_sc)
        acc_sc[...] = jnp.zeros_like(acc_sc)

    def step(diag):
        for hh in range(hp):
            cols = slice(hh * dh, (hh + 1) * dh)
            s = _dot_nt(q_ref[:, cols], k_ref[:, cols]) - fk_ref[hh]
            if diag:
                row = lax.broadcasted_iota(I32, (tq, tk), 0)
                col = lax.broadcasted_iota(I32, (tq, tk), 1)
                s = jnp.where(col <= row, s, -jnp.inf)
            m_prev = m_sc[hh]
            m_new = jnp.maximum(m_prev, jnp.max(s, axis=1, keepdims=True))
            alpha = jnp.exp2(m_prev - m_new)
            pexp = jnp.exp2(s - _lane_tile(m_new, tk))
            l_sc[hh] = alpha * l_sc[hh] + jnp.sum(pexp, axis=1, keepdims=True)
            acc_sc[hh] = alpha * acc_sc[hh] + _dot(pexp.astype(BF16), v_ref[:, cols])
            m_sc[hh] = m_new

    @pl.when(j < i)
    def _():
        step(False)

    @pl.when(j == i)
    def _():
        step(True)
        for hh in range(hp):
            o_ref[:, hh * dh:(hh + 1) * dh] = (acc_sc[hh] / l_sc[hh]).astype(o_ref.dtype)


def _fox_attention(qkv, f_rows, batch, seq):
    t = min(FOX_TQ, seq)
    nq = seq // t
    qi = np.concatenate([np.full(i + 1, i) for i in range(nq)]).astype(np.int32)
    kj = np.concatenate([np.arange(i + 1) for i in range(nq)]).astype(np.int32)
    hp = FOX_HEADS_PER_STEP
    hg = A_HEADS // hp
    dh = A_HEAD_DIM
    w = hp * dh
    kern = functools.partial(_fox_attn_kernel, tq=t, tk=t, hp=hp)
    grid_spec = pltpu.PrefetchScalarGridSpec(
        num_scalar_prefetch=2,
        grid=(batch, hg, len(qi)),
        in_specs=[
            pl.BlockSpec((t, w), lambda b, h, p, qi, kj: (b * nq + qi[p], h)),
            pl.BlockSpec((t, w), lambda b, h, p, qi, kj: (b * nq + kj[p], hg + h)),
            pl.BlockSpec((t, w), lambda b, h, p, qi, kj: (b * nq + kj[p], 2 * hg + h)),
            pl.BlockSpec((hp, 1, t), lambda b, h, p, qi, kj: (b * hg + h, 0, kj[p])),
        ],
        out_specs=pl.BlockSpec((t, w), lambda b, h, p, qi, kj: (b * nq + qi[p], h)),
        scratch_shapes=[pltpu.VMEM((hp, t, dh), F32), pltpu.VMEM((hp, t, dh), F32),
                        pltpu.VMEM((hp, t, dh), F32)],
    )
    return pl.pallas_call(
        kern,
        out_shape=jax.ShapeDtypeStruct((batch * seq, A_HEADS * dh), BF16),
        grid_spec=grid_spec,
        compiler_params=_cparams(("arbitrary", "arbitrary", "arbitrary")),
        name="fox_attention",
    )(jnp.asarray(qi), jnp.asarray(kj), qkv, qkv, qkv, f_rows)


def _outproj_ln_kernel(o_ref, w_ref, x_ref, g_ref, b_ref, y_ref):
    z = DEEPNORM_ALPHA * x_ref[...] + _dot(o_ref[...], w_ref[...])
    y_ref[...] = _layer_norm_rows(z, g_ref[...], b_ref[...])


def _outproj_ln(o, w_b, x2, g, b):
    n, d = x2.shape
    k = o.shape[1]
    tm = 512
    return pl.pallas_call(
        _outproj_ln_kernel,
        out_shape=jax.ShapeDtypeStruct((n, d), F32),
        grid=(n // tm,),
        in_specs=[pl.BlockSpec((tm, k), lambda i: (i, 0)),
                  pl.BlockSpec((k, d), lambda i: (0, 0)),
                  pl.BlockSpec((tm, d), lambda i: (i, 0)),
                  pl.BlockSpec((1, d), lambda i: (0, 0)),
                  pl.BlockSpec((1, d), lambda i: (0, 0))],
        out_specs=pl.BlockSpec((tm, d), lambda i: (i, 0)),
        compiler_params=_cparams(("arbitrary",)),
        name="outproj_deepnorm",
    )(o, w_b, x2, g.reshape(1, d), b.reshape(1, d))


def _fox_layer(x2, w_in, b_f, w_out, ln_g, ln_b, batch, seq):
    dq = A_HEADS * A_HEAD_DIM
    qkv, fl = _proj0(x2, w_in[:, :3 * dq].astype(BF16), w_in[:, 3 * dq:])
    f = _forget_cumsum(fl, b_f, batch, seq)
    f_rows = f.reshape(batch, seq, A_HEADS).transpose(0, 2, 1).reshape(batch * A_HEADS, 1, seq)
    o = _fox_attention(qkv, f_rows, batch, seq)
    return _outproj_ln(o, w_out.astype(BF16), x2, ln_g, ln_b)


def _router_kernel(x_ref, rw_ref, rb_ref, sel_ref, gate_ref, rank_ref, cnt_ref, carry_sc, *, tm):
    @pl.when(pl.program_id(0) == 0)
    def _():
        carry_sc[...] = jnp.zeros_like(carry_sc)

    e = N_EXPERTS
    scores = jax.nn.sigmoid(_dot_x3(x_ref[...], rw_ref[...]))
    lane = lax.broadcasted_iota(I32, (tm, e), 1)
    slot = lax.broadcasted_iota(I32, (tm, TOP_K), 1)
    work = scores + rb_ref[...]
    chosen = jnp.zeros((tm, e), F32)
    sel = jnp.zeros((tm, TOP_K), I32)
    gate = jnp.zeros((tm, TOP_K), F32)
    idxs = []
    for k in range(TOP_K):
        mx = jnp.max(work, axis=1, keepdims=True)
        idx = jnp.min(jnp.where(work == mx, lane, e), axis=1, keepdims=True)
        hit = lane == idx
        gk = jnp.sum(jnp.where(hit, scores, 0.0), axis=1, keepdims=True)
        work = jnp.where(hit, -jnp.inf, work)
        chosen = jnp.where(hit, 1.0, chosen)
        sel = jnp.where(slot == k, idx, sel)
        gate = jnp.where(slot == k, gk, gate)
        idxs.append(idx)
    gate = gate / jnp.sum(gate, axis=1, keepdims=True) * ROUTED_SCALE

    row = lax.broadcasted_iota(I32, (tm, tm), 0)
    col = lax.broadcasted_iota(I32, (tm, tm), 1)
    before = _dot((col < row).astype(BF16), chosen.astype(BF16)) + carry_sc[...]
    rank = jnp.zeros((tm, TOP_K), F32)
    for k in range(TOP_K):
        rk = jnp.sum(jnp.where(lane == idxs[k], before, 0.0), axis=1, keepdims=True)
        rank = jnp.where(slot == k, rk, rank)
    total = carry_sc[...] + jnp.sum(chosen, axis=0, keepdims=True)
    carry_sc[...] = total
    sel_ref[...] = sel
    gate_ref[...] = gate
    rank_ref[...] = rank.astype(I32)
    cnt_ref[...] = total.astype(I32)


def _router(x2, rw, rb):
    n, d = x2.shape
    tm = 512
    e = N_EXPERTS
    return pl.pallas_call(
        functools.partial(_router_kernel, tm=tm),
        out_shape=(jax.ShapeDtypeStruct((n, TOP_K), I32),
                   jax.ShapeDtypeStruct((n, TOP_K), F32),
                   jax.ShapeDtypeStruct((n, TOP_K), I32),
                   jax.ShapeDtypeStruct((1, e), I32)),
        grid=(n // tm,),
        in_specs=[pl.BlockSpec((tm, d), lambda i: (i, 0)),
                  pl.BlockSpec((d, e), lambda i: (0, 0)),
                  pl.BlockSpec((1, e), lambda i: (0, 0))],
        out_specs=(pl.BlockSpec((tm, TOP_K), lambda i: (i, 0)),
                   pl.BlockSpec((tm, TOP_K), lambda i: (i, 0)),
                   pl.BlockSpec((tm, TOP_K), lambda i: (i, 0)),
                   pl.BlockSpec((1, e), lambda i: (0, 0))),
        scratch_shapes=[pltpu.VMEM((1, e), F32)],
        compiler_params=_cparams(("arbitrary",)),
        name="moe_router",
    )(x2, rw, rb.reshape(1, e))


def _pack_bf16_pairs(y):
    w = y.shape[1] // 2
    lo = lax.bitcast_convert_type(y[:, :w].astype(BF16).astype(F32), U32)
    hi = lax.bitcast_convert_type(y[:, w:].astype(BF16).astype(F32), U32)
    return lax.shift_right_logical(lo, jnp.uint32(16)) | hi


def _unpack_bf16_pairs(words):
    lo = lax.bitcast_convert_type(lax.shift_left(words, jnp.uint32(16)), F32).astype(BF16)
    hi = lax.bitcast_convert_type(words & jnp.uint32(0xFFFF0000), F32).astype(BF16)
    return lo, hi


def _slab_load(ref, rows):
    return jnp.concatenate([ref[pl.ds(c, rows, stride=SLAB), :] for c in range(SLAB)], axis=1)


def _slab_store(ref, rows, val):
    for c in range(SLAB):
        ref[pl.ds(c, rows, stride=SLAB), :] = val[:, c * LANES:(c + 1) * LANES]


def _dispatch_kernel(pad_ref, dest_hbm, x_ref, xs_hbm, dest_sm, slab_sc, zero_sc, sem_idx, sem_zero, sem_row,
                     *, tt):
    i = pl.program_id(0)

    def zero_copy(e):
        start = pl.multiple_of(pad_ref[e] * SLAB, SLAB)
        return pltpu.make_async_copy(zero_sc, xs_hbm.at[pl.ds(start, EXPERT_ROWS * SLAB)], sem_zero)

    @pl.when(i == 0)
    def _():
        zero_sc[...] = jnp.zeros_like(zero_sc)

        def start(e, c):
            @pl.when(pad_ref[N_EXPERTS + e] > 0)
            def _():
                zero_copy(e).start()
            return c

        def wait(e, c):
            @pl.when(pad_ref[N_EXPERTS + e] > 0)
            def _():
                zero_copy(e).wait()
            return c

        lax.fori_loop(0, N_EXPERTS, start, 0)
        lax.fori_loop(0, N_EXPERTS, wait, 0)

    idx_copy = pltpu.make_async_copy(dest_hbm.at[i], dest_sm, sem_idx)
    idx_copy.start()
    _slab_store(slab_sc, tt, _pack_bf16_pairs(x_ref[...]))
    idx_copy.wait()

    def row_copy(t, dst_row):
        src = pl.multiple_of(t * SLAB, SLAB)
        dst = pl.multiple_of(dst_row * SLAB, SLAB)
        return pltpu.make_async_copy(slab_sc.at[pl.ds(src, SLAB)], xs_hbm.at[pl.ds(dst, SLAB)], sem_row)

    def start(t, c):
        for k in range(TOP_K):
            row_copy(t, dest_sm[t * TOP_K + k]).start(priority=k % 2)
        return c

    def wait(t, c):
        for k in range(TOP_K):
            row_copy(0, 0).wait()
        return c

    lax.fori_loop(0, tt, start, 0)
    lax.fori_loop(0, tt, wait, 0)


def _dispatch(pad_info, dest, x2, p_rows):
    n, d = x2.shape
    tt = 512
    assert TOP_K == 8
    assert d == 2 * SLAB * LANES
    dest2 = dest.reshape(n // tt, tt * TOP_K)
    grid_spec = pltpu.PrefetchScalarGridSpec(
        num_scalar_prefetch=1,
        grid=(n // tt,),
        in_specs=[pl.BlockSpec(memory_space=pl.ANY),
                  pl.BlockSpec((tt, d), lambda i, pad: (i, 0))],
        out_specs=pl.BlockSpec(memory_space=pl.ANY),
        scratch_shapes=[pltpu.SMEM((tt * TOP_K,), I32),
                        pltpu.VMEM((tt * SLAB, LANES), U32),
                        pltpu.VMEM((EXPERT_ROWS * SLAB, LANES), U32),
                        pltpu.SemaphoreType.DMA, pltpu.SemaphoreType.DMA, pltpu.SemaphoreType.DMA],
    )
    return pl.pallas_call(
        functools.partial(_dispatch_kernel, tt=tt),
        out_shape=jax.ShapeDtypeStruct((p_rows * SLAB, LANES), U32),
        grid_spec=grid_spec,
        compiler_params=_cparams(("arbitrary",)),
        name="moe_dispatch",
    )(pad_info, dest2, x2)


def _expert_mlp_kernel(be_ref, nu_ref, xs_ref, wg_ref, wu_ref, wd_ref, ys_ref, wg_sc, wu_sc, wd_sc):
    b = pl.program_id(0)
    e = be_ref[b]
    first = jnp.logical_or(b == 0, e != be_ref[jnp.maximum(b - 1, 0)])
    active = b < nu_ref[0]

    @pl.when(jnp.logical_and(active, first))
    def _():
        wg_sc[...] = wg_ref[...].astype(BF16)
        wu_sc[...] = wu_ref[...].astype(BF16)
        wd_sc[...] = wd_ref[...].astype(BF16)

    @pl.when(active)
    def _():
        r = EXPERT_ROWS
        half = wg_sc.shape[0] // 2
        x_lo, x_hi = _unpack_bf16_pairs(_slab_load(xs_ref, r))
        g = _dot(x_lo, wg_sc[:half]) + _dot(x_hi, wg_sc[half:])
        u = _dot(x_lo, wu_sc[:half]) + _dot(x_hi, wu_sc[half:])
        h = (g * jax.nn.sigmoid(g) * u).astype(BF16)
        _slab_store(ys_ref, r, _pack_bf16_pairs(_dot(h, wd_sc[...])))


def _expert_mlp(blk_e, n_used, xs, w_gate, w_up, w_down, layer):
    r = EXPERT_ROWS
    nblk = xs.shape[0] // (r * SLAB)
    d, de = w_gate.shape[2], w_gate.shape[3]

    def row_map(b, be, nu):
        return (jnp.minimum(b, nu[0] - 1), 0)

    def w_map(b, be, nu):
        return (layer, be[b], 0, 0)

    grid_spec = pltpu.PrefetchScalarGridSpec(
        num_scalar_prefetch=2,
        grid=(nblk,),
        in_specs=[pl.BlockSpec((r * SLAB, LANES), row_map),
                  pl.BlockSpec((None, None, d, de), w_map),
                  pl.BlockSpec((None, None, d, de), w_map),
                  pl.BlockSpec((None, None, de, d), w_map)],
        out_specs=pl.BlockSpec((r * SLAB, LANES), row_map),
        scratch_shapes=[pltpu.VMEM((d, de), BF16), pltpu.VMEM((d, de), BF16), pltpu.VMEM((de, d), BF16)],
    )
    return pl.pallas_call(
        _expert_mlp_kernel,
        out_shape=jax.ShapeDtypeStruct(xs.shape, U32),
        grid_spec=grid_spec,
        compiler_params=_cparams(("arbitrary",)),
        name="moe_experts",
    )(blk_e, n_used, xs, w_gate, w_up, w_down)


def _combine_kernel(dest_hbm, ys_hbm, x_ref, gate_ref, sg_ref, su_ref, sd_ref, g_ref, b_ref,
                    y_ref, dest_sm, rows_sc, sem_idx, sem_row, *, tt, n_tiles):
    i = pl.program_id(0)
    slot = lax.rem(i, 2)
    n_rows = tt * TOP_K

    def idx_copy(tile, s):
        return pltpu.make_async_copy(dest_hbm.at[tile], dest_sm.at[pl.ds(pl.multiple_of(s * n_rows, n_rows), n_rows)],
                                     sem_idx.at[s])

    def row_copy(s, t, k, src_row):
        return pltpu.make_async_copy(ys_hbm.at[pl.ds(pl.multiple_of(src_row * SLAB, SLAB), SLAB)],
                                     rows_sc.at[s, k, pl.ds(pl.multiple_of(t * SLAB, SLAB), SLAB)], sem_row.at[s])

    def start_rows(s):
        def start(t, c):
            for k in range(TOP_K):
                row_copy(s, t, k, dest_sm[s * n_rows + t * TOP_K + k]).start(priority=k % 2)
            return c
        lax.fori_loop(0, tt, start, 0)

    @pl.when(i == 0)
    def _():
        idx_copy(0, 0).start()
        idx_copy(0, 0).wait()
        start_rows(0)
        if n_tiles > 1:
            idx_copy(1, 1).start()

    @pl.when(i + 1 < n_tiles)
    def _():
        idx_copy(i + 1, 1 - slot).wait()
        start_rows(1 - slot)

    @pl.when(i + 2 < n_tiles)
    def _():
        idx_copy(i + 2, slot).start()

    x = x_ref[...]
    xb = x.astype(BF16)
    g = _dot(xb, sg_ref[...])
    u = _dot(xb, su_ref[...])
    h = (g * jax.nn.sigmoid(g) * u).astype(BF16)
    z = DEEPNORM_ALPHA * x + _dot(h, sd_ref[...])

    def wait(t, c):
        for k in range(TOP_K):
            row_copy(slot, 0, 0, 0).wait()
        return c
    lax.fori_loop(0, tt, wait, 0)

    gate = gate_ref[...]
    lo_pieces, hi_pieces = [], []
    for c in range(SLAB):
        acc_lo = jnp.zeros((tt, LANES), F32)
        acc_hi = jnp.zeros((tt, LANES), F32)
        for k in range(TOP_K):
            words = rows_sc[slot, k, pl.ds(c, tt, stride=SLAB), :]
            gk = gate[:, k:k + 1]
            acc_lo = acc_lo + gk * lax.bitcast_convert_type(lax.shift_left(words, jnp.uint32(16)), F32)
            acc_hi = acc_hi + gk * lax.bitcast_convert_type(words & jnp.uint32(0xFFFF0000), F32)
        lo_pieces.append(acc_lo)
        hi_pieces.append(acc_hi)
    routed = jnp.concatenate(lo_pieces + hi_pieces, axis=1)
    y_ref[...] = _layer_norm_rows(z + routed, g_ref[...], b_ref[...])


def _combine(dest, ys, x2, gate, sg_b, su_b, sd_b, ln_g, ln_b):
    n, d = x2.shape
    ds_ = sg_b.shape[1]
    tt = 256
    assert TOP_K == 8
    n_tiles = n // tt
    dest2 = dest.reshape(n_tiles, tt * TOP_K)
    return pl.pallas_call(
        functools.partial(_combine_kernel, tt=tt, n_tiles=n_tiles),
        out_shape=jax.ShapeDtypeStruct((n, d), F32),
        grid=(n_tiles,),
        in_specs=[pl.BlockSpec(memory_space=pl.ANY),
                  pl.BlockSpec(memory_space=pl.ANY),
                  pl.BlockSpec((tt, d), lambda i: (i, 0)),
                  pl.BlockSpec((tt, TOP_K), lambda i: (i, 0)),
                  pl.BlockSpec((d, ds_), lambda i: (0, 0)),
                  pl.BlockSpec((d, ds_), lambda i: (0, 0)),
                  pl.BlockSpec((ds_, d), lambda i: (0, 0)),
                  pl.BlockSpec((1, d), lambda i: (0, 0)),
                  pl.BlockSpec((1, d), lambda i: (0, 0))],
        out_specs=pl.BlockSpec((tt, d), lambda i: (i, 0)),
        scratch_shapes=[pltpu.SMEM((2 * tt * TOP_K,), I32),
                        pltpu.VMEM((2, TOP_K, tt * SLAB, LANES), U32),
                        pltpu.SemaphoreType.DMA((2,)), pltpu.SemaphoreType.DMA((2,))],
        compiler_params=_cparams(("arbitrary",)),
        name="moe_combine",
    )(dest2, ys, x2, gate, sg_b, su_b, sd_b, ln_g.reshape(1, d), ln_b.reshape(1, d))


def _moe_layer(x2, rw, rb, w_gate, w_up, w_down, layer, sh_gate, sh_up, sh_down, ln_g, ln_b):
    n, d = x2.shape
    e = N_EXPERTS
    r = EXPERT_ROWS
    sel, gate, rank, counts = _router(x2, rw, rb)
    counts = counts.reshape(e)
    padded = (counts + r - 1) // r * r
    pend = jnp.cumsum(padded)
    pstart = pend - padded
    dest = rank + jnp.sum(jnp.where(sel[..., None] == jnp.arange(e, dtype=I32), pstart.astype(I32), 0), axis=-1)
    p_rows = n * TOP_K + e * r
    nblk = p_rows // r
    blk_start = jnp.arange(nblk, dtype=I32) * r
    blk_e = jnp.minimum(jnp.sum((pend[None, :] <= blk_start[:, None]).astype(I32), axis=1), e - 1)
    n_used = (pend[-1] // r).astype(I32).reshape(1)
    pad_info = jnp.concatenate([jnp.maximum(pend - r, 0), padded]).astype(I32)
    xs = _dispatch(pad_info, dest.astype(I32), x2, p_rows)
    ys = _expert_mlp(blk_e, n_used, xs, w_gate, w_up, w_down, layer)
    return _combine(dest.astype(I32), ys, x2, gate, sh_gate.astype(BF16), sh_up.astype(BF16),
                    sh_down.astype(BF16), ln_g, ln_b)


FAR_BUCKET = REL_BUCKETS // 2 - 1


def _n_near_offsets(tq, tk):
    return (tk + REL_MAX_DIST - 1 + tq - 1) // tq


def _t5_bucket(rel):
    half = REL_BUCKETS // 2
    max_exact = half // 2
    n = jnp.abs(rel)
    large = max_exact + (jnp.log(jnp.maximum(n, 1).astype(F32) / max_exact)
                         / math.log(REL_MAX_DIST / max_exact) * (half - max_exact)).astype(I32)
    large = jnp.minimum(large, half - 1)
    return jnp.where(rel > 0, half, 0) + jnp.where(n < max_exact, n, large)


def _bias_tiles_kernel(rb_ref, out_ref, *, tq, tk):
    offset = -tq * pl.program_id(0)
    row = lax.broadcasted_iota(I32, (tq, tk), 0)
    col = lax.broadcasted_iota(I32, (tq, tk), 1)
    bucket = _t5_bucket(offset + col - row)
    for h in range(B_HEADS):
        far = rb_ref[FAR_BUCKET * B_HEADS + h]
        acc = jnp.zeros((tq, tk), F32)
        for b in range(REL_BUCKETS):
            acc = jnp.where(bucket == b, (rb_ref[b * B_HEADS + h] - far) * LOG2E, acc)
        out_ref[0, h] = acc


def _bias_tiles(rel_bias, tq, tk):
    n_off = _n_near_offsets(tq, tk)
    grid_spec = pltpu.PrefetchScalarGridSpec(
        num_scalar_prefetch=1,
        grid=(n_off,),
        in_specs=[],
        out_specs=pl.BlockSpec((1, B_HEADS, tq, tk), lambda i, rb: (i, 0, 0, 0)),
    )
    return pl.pallas_call(
        functools.partial(_bias_tiles_kernel, tq=tq, tk=tk),
        out_shape=jax.ShapeDtypeStruct((n_off, B_HEADS, tq, tk), F32),
        grid_spec=grid_spec,
        compiler_params=_cparams(("arbitrary",)),
        name="dsa_bias_tiles",
    )(rel_bias.reshape(-1))


def _proj1_kernel(x_ref, w_ref, qn_ref, kvn_ref, cq_ref, ckv_ref, ckvt_ref, ki_ref, wi_ref,
                  *, o1, o2, o3, wscale):
    proj = _dot(x_ref[...].astype(BF16), w_ref[...])
    cq_ref[...] = _rms_norm_rows(proj[:, :o1], qn_ref[...]).astype(BF16)
    ckv = _rms_norm_rows(proj[:, o1:o2], kvn_ref[...])
    ckv_ref[...] = ckv.astype(BF16)
    ckvt_ref[...] = ckv.T.astype(BF16)
    ki_ref[...] = proj[:, o2:o3].astype(BF16)
    wi_ref[...] = proj[:, o3:] * wscale


def _proj1(x2, w_b, q_norm, kv_norm, ql, kvl, batch, seq):
    n, d = x2.shape
    wout = w_b.shape[1]
    o1, o2, o3 = ql, ql + kvl, ql + kvl + IDX_DIM
    tm = 512
    nt = seq // tm
    kern = functools.partial(_proj1_kernel, o1=o1, o2=o2, o3=o3,
                             wscale=(IDX_HEADS ** -0.5) * (IDX_DIM ** -0.5))
    return pl.pallas_call(
        kern,
        out_shape=(jax.ShapeDtypeStruct((n, ql), BF16), jax.ShapeDtypeStruct((n, kvl), BF16),
                   jax.ShapeDtypeStruct((batch * kvl, seq), BF16),
                   jax.ShapeDtypeStruct((n, IDX_DIM), BF16), jax.ShapeDtypeStruct((n, IDX_HEADS), F32)),
        grid=(n // tm,),
        in_specs=[pl.BlockSpec((tm, d), lambda i: (i, 0)),
                  pl.BlockSpec((d, wout), lambda i: (0, 0)),
                  pl.BlockSpec((1, ql), lambda i: (0, 0)),
                  pl.BlockSpec((1, kvl), lambda i: (0, 0))],
        out_specs=(pl.BlockSpec((tm, ql), lambda i: (i, 0)), pl.BlockSpec((tm, kvl), lambda i: (i, 0)),
                   pl.BlockSpec((kvl, tm), lambda i: (i // nt, i % nt)),
                   pl.BlockSpec((tm, IDX_DIM), lambda i: (i, 0)), pl.BlockSpec((tm, IDX_HEADS), lambda i: (i, 0))),
        compiler_params=_cparams(("arbitrary",)),
        name="dsa_proj",
    )(x2, w_b, q_norm.reshape(1, ql), kv_norm.reshape(1, kvl))


def _qside_kernel(cq_ref, wuq_ref, wuk_ref, wiq_ref, ql_ref, qi_ref, *, kvl, scale):
    cq = cq_ref[...]
    q = _dot(cq, wuq_ref[...]).astype(BF16)
    for h in range(B_HEADS):
        qh = q[:, h * B_HEAD_DIM:(h + 1) * B_HEAD_DIM]
        ql_ref[h] = (_dot_nt(qh, wuk_ref[h]) * scale).astype(BF16)
    qi_ref[...] = _dot(cq, wiq_ref[...]).astype(BF16)


def _qside(cq, w_uq_b, w_uk_b, w_iq_b):
    n, ql = cq.shape
    kvl = w_uk_b.shape[1]
    tm = 512
    kern = functools.partial(_qside_kernel, kvl=kvl, scale=B_HEAD_DIM ** -0.5 * LOG2E)
    return pl.pallas_call(
        kern,
        out_shape=(jax.ShapeDtypeStruct((B_HEADS, n, kvl), BF16),
                   jax.ShapeDtypeStruct((n, IDX_HEADS * IDX_DIM), BF16)),
        grid=(n // tm,),
        in_specs=[pl.BlockSpec((tm, ql), lambda i: (i, 0)),
                  pl.BlockSpec(w_uq_b.shape, lambda i: (0, 0)),
                  pl.BlockSpec(w_uk_b.shape, lambda i: (0, 0, 0)),
                  pl.BlockSpec(w_iq_b.shape, lambda i: (0, 0))],
        out_specs=(pl.BlockSpec((B_HEADS, tm, kvl), lambda i: (0, i, 0)),
                   pl.BlockSpec((tm, IDX_HEADS * IDX_DIM), lambda i: (i, 0))),
        compiler_params=_cparams(("arbitrary",)),
        name="dsa_qside",
    )(cq, w_uq_b, w_uk_b, w_iq_b)


INT_MIN = -2 ** 31
KEY_NEG_INF = (0xFF800000 ^ 0x7FFFFFFF) - 2 ** 32


def _ordered_key(v):
    bits = lax.bitcast_convert_type(v, I32)
    return bits ^ (lax.shift_right_arithmetic(bits, 31) & 0x7FFFFFFF)


def _indexer_kernel(qi_ref, wi_ref, ki_ref, mask_ref, key_sc, *, tq, tkc, seq, topk):
    i = pl.program_id(1)
    n_chunks = (i * tq + tq + tkc - 1) // tkc
    row = lax.broadcasted_iota(I32, (tq, 1), 0)
    limit = (lax.shift_right_logical(i * tq + row, 6) + 1) * CHUNK
    w = wi_ref[...]
    key_sc[...] = jnp.full(key_sc.shape, KEY_NEG_INF, I32)

    def score_chunk(c, carry):
        start = pl.multiple_of(c * tkc, tkc)
        k = ki_ref[pl.ds(start, tkc), :]
        acc = jnp.zeros((tq, tkc), F32)
        for h in range(IDX_HEADS):
            sc = _dot_nt(qi_ref[:, h * IDX_DIM:(h + 1) * IDX_DIM], k)
            acc = acc + jnp.maximum(sc, 0.0) * w[:, h:h + 1]
        kpos = start + lax.broadcasted_iota(I32, (tq, tkc), 1)
        key_sc[:, pl.ds(start, tkc)] = _ordered_key(jnp.where(kpos < limit, acc, -jnp.inf))
        return carry

    lax.fori_loop(0, n_chunks, score_chunk, 0)

    def count(pred_fn):
        def body(c, acc):
            start = pl.multiple_of(c * tkc, tkc)
            hit = pred_fn(key_sc[:, pl.ds(start, tkc)]).astype(I32)
            part = hit[:, 0:LANES]
            for q in range(1, tkc // LANES):
                part = part + hit[:, q * LANES:(q + 1) * LANES]
            return acc + part
        acc = lax.fori_loop(0, n_chunks, body, jnp.zeros((tq, LANES), I32))
        return jnp.sum(acc, axis=1, keepdims=True)

    def bisect(it, prefix):
        cand = prefix + lax.shift_left(jnp.int32(1), 31 - it)
        cnt = count(lambda kk: kk >= cand)
        return jnp.where(cnt >= topk, cand, prefix)

    thr = lax.fori_loop(0, 32, bisect, jnp.full((tq, 1), INT_MIN, I32))
    n_gt = count(lambda kk: kk > thr)
    n_eq = count(lambda kk: kk == thr)
    need = topk - n_gt
    tie_break = jnp.max(jnp.where(jnp.logical_and(thr > KEY_NEG_INF, n_eq > need), 1, 0)) > 0

    mask_ref[...] = jnp.full(mask_ref.shape, NEG_BIG, F32)

    @pl.when(jnp.logical_not(tie_break))
    def _():
        def write(c, carry):
            start = pl.multiple_of(c * tkc, tkc)
            kk = key_sc[:, pl.ds(start, tkc)]
            sel = jnp.logical_and(kk >= thr, kk > KEY_NEG_INF)
            mask_ref[:, pl.ds(start, tkc)] = jnp.where(sel, 0.0, NEG_BIG)
            return carry
        lax.fori_loop(0, n_chunks, write, 0)

    @pl.when(tie_break)
    def _():
        r_ = lax.broadcasted_iota(I32, (LANES, LANES), 0)
        c_ = lax.broadcasted_iota(I32, (LANES, LANES), 1)
        upper = (r_ < c_).astype(BF16)

        def write(c, seen):
            start = pl.multiple_of(c * LANES, LANES)
            kk = key_sc[:, pl.ds(start, LANES)]
            eq = kk == thr
            before = seen + _dot(eq.astype(BF16), upper)
            sel = jnp.logical_or(kk > thr, jnp.logical_and(eq, before < need.astype(F32)))
            sel = jnp.logical_and(sel, kk > KEY_NEG_INF)
            mask_ref[:, pl.ds(start, LANES)] = jnp.where(sel, 0.0, NEG_BIG)
            return seen + jnp.sum(eq.astype(F32), axis=1, keepdims=True)
        lax.fori_loop(0, n_chunks * (tkc // LANES), write, jnp.zeros((tq, 1), F32))


def _indexer(qidx, widx, kidx, batch, seq, topk):
    tq = IDX_TQ
    tkc = min(IDX_TKC, seq)
    nq = seq // tq
    kern = functools.partial(_indexer_kernel, tq=tq, tkc=tkc, seq=seq, topk=topk)
    return pl.pallas_call(
        kern,
        out_shape=jax.ShapeDtypeStruct((batch * seq, seq), F32),
        grid=(batch, nq),
        in_specs=[pl.BlockSpec((tq, IDX_HEADS * IDX_DIM), lambda b, i: (b * nq + i, 0)),
                  pl.BlockSpec((tq, IDX_HEADS), lambda b, i: (b * nq + i, 0)),
                  pl.BlockSpec((seq, IDX_DIM), lambda b, i: (b, 0))],
        out_specs=pl.BlockSpec((tq, seq), lambda b, i: (b * nq + i, 0)),
        scratch_shapes=[pltpu.VMEM((tq, seq), I32)],
        compiler_params=_cparams(("arbitrary", "arbitrary")),
        name="dsa_indexer",
    )(qidx, widx, kidx)


def _dsa_attn_kernel(qi_ref, kj_ref, ql_ref, kt_ref, kv_ref, mask_ref, bias_ref, wuv_ref, o_ref,
                     m_sc, l_sc, acc_sc, *, tq, tk, kvl, n_near):
    p = pl.program_id(1)
    i = qi_ref[p]
    j = kj_ref[p]
    nh = B_HEADS
    behind = (i * tq - j * tk) // tq

    @pl.when(j == 0)
    def _():
        m_sc[...] = jnp.full_like(m_sc, -jnp.inf)
        l_sc[...] = jnp.zeros_like(l_sc)
        acc_sc[...] = jnp.zeros_like(acc_sc)

    def step(near):
        kt = kt_ref[...]
        kv = kv_ref[...]
        msk = mask_ref[...]
        for h in range(nh):
            s = _dot(ql_ref[h], kt) + msk
            if near:
                s = s + bias_ref[behind, h]
            m_prev = m_sc[h]
            m_new = jnp.maximum(m_prev, jnp.max(s, axis=1, keepdims=True))
            alpha = jnp.exp2(m_prev - m_new)
            pexp = jnp.exp2(s - _lane_tile(m_new, tk))
            l_sc[h] = alpha * l_sc[h] + jnp.sum(pexp, axis=1, keepdims=True)
            acc_sc[h] = _lane_tile(alpha, kvl) * acc_sc[h] + _dot(pexp.astype(BF16), kv)
            m_sc[h] = m_new

    @pl.when(behind >= n_near)
    def _():
        step(False)

    @pl.when(behind < n_near)
    def _():
        step(True)

    @pl.when(j == (i * tq + tq - 1) // tk)
    def _():
        for h in range(nh):
            o_lat = (acc_sc[h] / _lane_tile(l_sc[h], kvl)).astype(BF16)
            o_ref[:, h * B_V_DIM:(h + 1) * B_V_DIM] = _dot(o_lat, wuv_ref[h]).astype(o_ref.dtype)


def _dsa_attention(ql, ckvt, ckv, mask, bias_tiles, w_uv_b, batch, seq):
    tq, tk = DSA_TQ, DSA_TK
    kvl = ckv.shape[1]
    nq = seq // tq
    nk = seq // tk
    last = [(i * tq + tq - 1) // tk for i in range(nq)]
    qi = np.concatenate([np.full(last[i] + 1, i) for i in range(nq)]).astype(np.int32)
    kj = np.concatenate([np.arange(last[i] + 1) for i in range(nq)]).astype(np.int32)
    kern = functools.partial(_dsa_attn_kernel, tq=tq, tk=tk, kvl=kvl, n_near=bias_tiles.shape[0])
    grid_spec = pltpu.PrefetchScalarGridSpec(
        num_scalar_prefetch=2,
        grid=(batch, len(qi)),
        in_specs=[
            pl.BlockSpec((B_HEADS, tq, kvl), lambda b, p, qi, kj: (0, b * nq + qi[p], 0)),
            pl.BlockSpec((kvl, tk), lambda b, p, qi, kj: (b, kj[p])),
            pl.BlockSpec((tk, kvl), lambda b, p, qi, kj: (b * nk + kj[p], 0)),
            pl.BlockSpec((tq, tk), lambda b, p, qi, kj: (b * nq + qi[p], kj[p])),
            pl.BlockSpec(bias_tiles.shape, lambda b, p, qi, kj: (0, 0, 0, 0)),
            pl.BlockSpec(w_uv_b.shape, lambda b, p, qi, kj: (0, 0, 0)),
        ],
        out_specs=pl.BlockSpec((tq, B_HEADS * B_V_DIM), lambda b, p, qi, kj: (b * nq + qi[p], 0)),
        scratch_shapes=[pltpu.VMEM((B_HEADS, tq, LANES), F32), pltpu.VMEM((B_HEADS, tq, LANES), F32),
                        pltpu.VMEM((B_HEADS, tq, kvl), F32)],
    )
    return pl.pallas_call(
        kern,
        out_shape=jax.ShapeDtypeStruct((batch * seq, B_HEADS * B_V_DIM), BF16),
        grid_spec=grid_spec,
        compiler_params=_cparams(("arbitrary", "arbitrary")),
        name="dsa_attention",
    )(jnp.asarray(qi), jnp.asarray(kj), ql, ckvt, ckv, mask, bias_tiles, w_uv_b)


def _dsa_layer(x2, w_in, q_norm, kv_norm, w_uq, w_iq, w_uk, w_uv, w_out, rel_bias, ln_g, ln_b, batch, seq):
    ql_dim = q_norm.shape[0]
    kvl = kv_norm.shape[0]
    topk = min(IDX_TOPK, seq // 4)
    cq, ckv, ckvt, kidx, widx = _proj1(x2, w_in.astype(BF16), q_norm, kv_norm, ql_dim, kvl, batch, seq)
    ql, qidx = _qside(cq, w_uq.astype(BF16), w_uk.astype(BF16), w_iq.astype(BF16))
    mask = _indexer(qidx, widx, kidx, batch, seq, topk)
    bias_tiles = _bias_tiles(rel_bias, DSA_TQ, DSA_TK)
    o = _dsa_attention(ql, ckvt, ckv, mask, bias_tiles, w_uv.astype(BF16), batch, seq)
    return _outproj_ln(o, w_out.astype(BF16), x2, ln_g, ln_b)


def kernel(x, a_w_in, a_b_f, a_w_out, b_w_in, b_q_norm, b_kv_norm, b_w_uq, b_w_iq, b_w_uk, b_w_uv, b_w_out,
           rel_bias, ln1_g, ln1_b, ln2_g, ln2_b, router_w, router_b, w_gate, w_up, w_down, sh_gate, sh_up,
           sh_down):
    batch, seq, d = x.shape
    x2 = x.reshape(batch * seq, d)
    x2 = _fox_layer(x2, a_w_in[0], a_b_f[0], a_w_out[0], ln1_g[0], ln1_b[0], batch, seq)
    x2 = _moe_layer(x2, router_w[0], router_b[0], w_gate, w_up, w_down, 0, sh_gate[0], sh_up[0],
                    sh_down[0], ln2_g[0], ln2_b[0])
    x2 = _dsa_layer(x2, b_w_in[0], b_q_norm[0], b_kv_norm[0], b_w_uq[0], b_w_iq[0], b_w_uk[0], b_w_uv[0],
                    b_w_out[0], rel_bias, ln1_g[1], ln1_b[1], batch, seq)
    x2 = _moe_layer(x2, router_w[1], router_b[1], w_gate, w_up, w_down, 1, sh_gate[1], sh_up[1],
                    sh_down[1], ln2_g[1], ln2_b[1])
    return x2.reshape(batch, seq, d)
```

```python
import functools
import math

import numpy as np
import jax
import jax.numpy as jnp
from jax import lax
from jax.experimental import pallas as pl
from jax.experimental.pallas import tpu as pltpu

BF16 = jnp.bfloat16
F32 = jnp.float32
I32 = jnp.int32
U32 = jnp.uint32

A_HEADS = 16
A_HEAD_DIM = 128
B_HEADS = 16
B_HEAD_DIM = 128
B_V_DIM = 128
IDX_HEADS = 16
IDX_DIM = 64
IDX_TOPK = 256
CHUNK = 64
REL_BUCKETS = 32
REL_MAX_DIST = 128
N_EXPERTS = 64
TOP_K = 8
ROUTED_SCALE = 2.5
DEPTH = 2
DEEPNORM_ALPHA = (2 * DEPTH) ** 0.25
LN_EPS = 1e-5
RMS_EPS = 1e-6

LANES = 128
SLAB = 8
VMEM_LIMIT = 56 * 1024 * 1024
LOG2E = math.log2(math.e)
NEG_BIG = -1e30

EXPERT_ROWS = 512
FOX_TQ = 512
FOX_TK = 512
FOX_HEADS_PER_STEP = 4
DSA_TQ = 256
DSA_TK = 256
IDX_TQ = 256
IDX_TKC = 512


def _cparams(sem, vmem=VMEM_LIMIT):
    return pltpu.CompilerParams(dimension_semantics=sem, vmem_limit_bytes=vmem)


def _dot(a, b):
    return jnp.dot(a, b, preferred_element_type=F32)


def _dot_nt(a, b):
    return lax.dot_general(a, b, (((1,), (1,)), ((), ())), preferred_element_type=F32)


def _lane_tile(a, width):
    return a if width == LANES else jnp.concatenate([a] * (width // LANES), axis=1)


def _split2(a):
    hi = a.astype(BF16)
    lo = (a - hi.astype(F32)).astype(BF16)
    return hi, lo


def _split3(a):
    hi = a.astype(BF16)
    r = a - hi.astype(F32)
    mid = r.astype(BF16)
    lo = (r - mid.astype(F32)).astype(BF16)
    return hi, mid, lo


def _dot_x3(a, b):
    ah, al = _split2(a)
    bh, bl = _split2(b)
    return _dot(ah, bh) + (_dot(ah, bl) + _dot(al, bh))


def _layer_norm_rows(z, g, b):
    mu = jnp.mean(z, axis=-1, keepdims=True)
    d = z - mu
    var = jnp.mean(d * d, axis=-1, keepdims=True)
    return d * lax.rsqrt(var + LN_EPS) * g + b


def _rms_norm_rows(z, g):
    ms = jnp.mean(z * z, axis=-1, keepdims=True)
    return z * lax.rsqrt(ms + RMS_EPS) * g


def _proj0_kernel(x_ref, w_ref, wf_ref, qkv_ref, fl_ref, xb_sc, *, n_q_blocks, q_scale):
    j = pl.program_id(1)

    @pl.when(j == 0)
    def _():
        x = x_ref[...]
        xb_sc[...] = x.astype(BF16)
        fl_ref[...] = _dot_x3(x, wf_ref[...])

    acc = _dot(xb_sc[...], w_ref[...])
    scale = jnp.where(j < n_q_blocks, q_scale, 1.0).astype(F32)
    qkv_ref[...] = (acc * scale).astype(BF16)


def _proj0(x2, w_qkv_b, w_f):
    n, d = x2.shape
    nout = w_qkv_b.shape[1]
    tm, tn = 1024, 512
    dq = A_HEADS * A_HEAD_DIM
    kern = functools.partial(_proj0_kernel, n_q_blocks=dq // tn, q_scale=A_HEAD_DIM ** -0.5 * LOG2E)
    return pl.pallas_call(
        kern,
        out_shape=(jax.ShapeDtypeStruct((n, nout), BF16),
                   jax.ShapeDtypeStruct((n, A_HEADS), F32)),
        grid=(n // tm, nout // tn),
        in_specs=[pl.BlockSpec((tm, d), lambda i, j: (i, 0)),
                  pl.BlockSpec((d, tn), lambda i, j: (0, j)),
                  pl.BlockSpec((d, A_HEADS), lambda i, j: (0, 0))],
        out_specs=(pl.BlockSpec((tm, tn), lambda i, j: (i, j)),
                   pl.BlockSpec((tm, A_HEADS), lambda i, j: (i, 0))),
        scratch_shapes=[pltpu.VMEM((tm, d), BF16)],
        compiler_params=_cparams(("arbitrary", "arbitrary")),
        name="fox_proj",
    )(x2, w_qkv_b, w_f)


def _forget_cumsum_kernel(fl_ref, bf_ref, f_ref, carry_sc, *, t):
    @pl.when(pl.program_id(1) == 0)
    def _():
        carry_sc[...] = jnp.zeros_like(carry_sc)

    z = fl_ref[...] + bf_ref[...]
    logf = jnp.minimum(z, 0.0) - jnp.log1p(jnp.exp(-jnp.abs(z)))
    row = lax.broadcasted_iota(I32, (t, t), 0)
    col = lax.broadcasted_iota(I32, (t, t), 1)
    tri = (col <= row).astype(BF16)
    hi, mid, lo = _split3(logf)
    cs = _dot(tri, hi) + (_dot(tri, mid) + _dot(tri, lo)) + carry_sc[...]
    f_ref[...] = cs * LOG2E
    carry_sc[...] = cs[t - 1:t, :]


def _forget_cumsum(fl, b_f, batch, seq):
    t = 256
    nb = seq // t
    return pl.pallas_call(
        functools.partial(_forget_cumsum_kernel, t=t),
        out_shape=jax.ShapeDtypeStruct(fl.shape, F32),
        grid=(batch, nb),
        in_specs=[pl.BlockSpec((t, A_HEADS), lambda b, i: (b * nb + i, 0)),
                  pl.BlockSpec((1, A_HEADS), lambda b, i: (0, 0))],
        out_specs=pl.BlockSpec((t, A_HEADS), lambda b, i: (b * nb + i, 0)),
        scratch_shapes=[pltpu.VMEM((1, A_HEADS), F32)],
        compiler_params=_cparams(("arbitrary", "arbitrary")),
        name="fox_forget_cumsum",
    )(fl, b_f.reshape(1, A_HEADS))


def _fox_attn_kernel(qi_ref, kj_ref, q_ref, k_ref, v_ref, fk_ref, o_ref, m_sc, l_sc, acc_sc, *, tq, tk, hp):
    p = pl.program_id(2)
    i = qi_ref[p]
    j = kj_ref[p]
    dh = A_HEAD_DIM

    @pl.when(j == 0)
    def _():
        m_sc[...] = jnp.full_like(m_sc, -jnp.inf)
        l_sc[...] = jnp.zeros_like(l_sc)
        acc_sc[...] = jnp.zeros_like(acc_sc)

    def step(diag):
        for hh in range(hp):
            cols = slice(hh * dh, (hh + 1) * dh)
            s = _dot_nt(q_ref[:, cols], k_ref[:, cols]) - fk_ref[hh]
            if diag:
                row = lax.broadcasted_iota(I32, (tq, tk), 0)
                col = lax.broadcasted_iota(I32, (tq, tk), 1)
                s = jnp.where(col <= row, s, -jnp.inf)
            m_prev = m_sc[hh]
            m_new = jnp.maximum(m_prev, jnp.max(s, axis=1, keepdims=True))
            alpha = jnp.exp2(m_prev - m_new)
            pexp = jnp.exp2(s - _lane_tile(m_new, tk))
            l_sc[hh] = alpha * l_sc[hh] + jnp.sum(pexp, axis=1, keepdims=True)
            acc_sc[hh] = alpha * acc_sc[hh] + _dot(pexp.astype(BF16), v_ref[:, cols])
            m_sc[hh] = m_new

    @pl.when(j < i)
    def _():
        step(False)

    @pl.when(j == i)
    def _():
        step(True)
        for hh in range(hp):
            o_ref[:, hh * dh:(hh + 1) * dh] = (acc_sc[hh] / l_sc[hh]).astype(o_ref.dtype)


def _fox_attention(qkv, f_rows, batch, seq):
    t = min(FOX_TQ, seq)
    nq = seq // t
    qi = np.concatenate([np.full(i + 1, i) for i in range(nq)]).astype(np.int32)
    kj = np.concatenate([np.arange(i + 1) for i in range(nq)]).astype(np.int32)
    hp = FOX_HEADS_PER_STEP
    hg = A_HEADS // hp
    dh = A_HEAD_DIM
    w = hp * dh
    kern = functools.partial(_fox_attn_kernel, tq=t, tk=t, hp=hp)
    grid_spec = pltpu.PrefetchScalarGridSpec(
        num_scalar_prefetch=2,
        grid=(batch, hg, len(qi)),
        in_specs=[
            pl.BlockSpec((t, w), lambda b, h, p, qi, kj: (b * nq + qi[p], h)),
            pl.BlockSpec((t, w), lambda b, h, p, qi, kj: (b * nq + kj[p], hg + h)),
            pl.BlockSpec((t, w), lambda b, h, p, qi, kj: (b * nq + kj[p], 2 * hg + h)),
            pl.BlockSpec((hp, 1, t), lambda b, h, p, qi, kj: (b * hg + h, 0, kj[p])),
        ],
        out_specs=pl.BlockSpec((t, w), lambda b, h, p, qi, kj: (b * nq + qi[p], h)),
        scratch_shapes=[pltpu.VMEM((hp, t, dh), F32), pltpu.VMEM((hp, t, dh), F32),
                        pltpu.VMEM((hp, t, dh), F32)],
    )
    return pl.pallas_call(
        kern,
        out_shape=jax.ShapeDtypeStruct((batch * seq, A_HEADS * dh), BF16),
        grid_spec=grid_spec,
        compiler_params=_cparams(("arbitrary", "arbitrary", "arbitrary")),
        name="fox_attention",
    )(jnp.asarray(qi), jnp.asarray(kj), qkv, qkv, qkv, f_rows)


def _outproj_ln_kernel(o_ref, w_ref, x_ref, g_ref, b_ref, y_ref):
    z = DEEPNORM_ALPHA * x_ref[...] + _dot(o_ref[...], w_ref[...])
    y_ref[...] = _layer_norm_rows(z, g_ref[...], b_ref[...])


def _outproj_ln(o, w_b, x2, g, b):
    n, d = x2.shape
    k = o.shape[1]
    tm = 512
    return pl.pallas_call(
        _outproj_ln_kernel,
        out_shape=jax.ShapeDtypeStruct((n, d), F32),
        grid=(n // tm,),
        in_specs=[pl.BlockSpec((tm, k), lambda i: (i, 0)),
                  pl.BlockSpec((k, d), lambda i: (0, 0)),
                  pl.BlockSpec((tm, d), lambda i: (i, 0)),
                  pl.BlockSpec((1, d), lambda i: (0, 0)),
                  pl.BlockSpec((1, d), lambda i: (0, 0))],
        out_specs=pl.BlockSpec((tm, d), lambda i: (i, 0)),
        compiler_params=_cparams(("arbitrary",)),
        name="outproj_deepnorm",
    )(o, w_b, x2, g.reshape(1, d), b.reshape(1, d))


def _fox_layer(x2, w_in, b_f, w_out, ln_g, ln_b, batch, seq):
    dq = A_HEADS * A_HEAD_DIM
    qkv, fl = _proj0(x2, w_in[:, :3 * dq].astype(BF16), w_in[:, 3 * dq:])
    f = _forget_cumsum(fl, b_f, batch, seq)
    f_rows = f.reshape(batch, seq, A_HEADS).transpose(0, 2, 1).reshape(batch * A_HEADS, 1, seq)
    o = _fox_attention(qkv, f_rows, batch, seq)
    return _outproj_ln(o, w_out.astype(BF16), x2, ln_g, ln_b)


def _router_kernel(x_ref, rw_ref, rb_ref, sel_ref, gate_ref, rank_ref, cnt_ref, carry_sc, *, tm):
    @pl.when(pl.program_id(0) == 0)
    def _():
        carry_sc[...] = jnp.zeros_like(carry_sc)

    e = N_EXPERTS
    scores = jax.nn.sigmoid(_dot_x3(x_ref[...], rw_ref[...]))
    lane = lax.broadcasted_iota(I32, (tm, e), 1)
    slot = lax.broadcasted_iota(I32, (tm, TOP_K), 1)
    work = scores + rb_ref[...]
    chosen = jnp.zeros((tm, e), F32)
    sel = jnp.zeros((tm, TOP_K), I32)
    gate = jnp.zeros((tm, TOP_K), F32)
    idxs = []
    for k in range(TOP_K):
        mx = jnp.max(work, axis=1, keepdims=True)
        idx = jnp.min(jnp.where(work == mx, lane, e), axis=1, keepdims=True)
        hit = lane == idx
        gk = jnp.sum(jnp.where(hit, scores, 0.0), axis=1, keepdims=True)
        work = jnp.where(hit, -jnp.inf, work)
        chosen = jnp.where(hit, 1.0, chosen)
        sel = jnp.where(slot == k, idx, sel)
        gate = jnp.where(slot == k, gk, gate)
        idxs.append(idx)
    gate = gate / jnp.sum(gate, axis=1, keepdims=True) * ROUTED_SCALE

    row = lax.broadcasted_iota(I32, (tm, tm), 0)
    col = lax.broadcasted_iota(I32, (tm, tm), 1)
    before = _dot((col < row).astype(BF16), chosen.astype(BF16)) + carry_sc[...]
    rank = jnp.zeros((tm, TOP_K), F32)
    for k in range(TOP_K):
        rk = jnp.sum(jnp.where(lane == idxs[k], before, 0.0), axis=1, keepdims=True)
        rank = jnp.where(slot == k, rk, rank)
    total = carry_sc[...] + jnp.sum(chosen, axis=0, keepdims=True)
    carry_sc[...] = total
    sel_ref[...] = sel
    gate_ref[...] = gate
    rank_ref[...] = rank.astype(I32)
    cnt_ref[...] = total.astype(I32)


def _router(x2, rw, rb):
    n, d = x2.shape
    tm = 512
    e = N_EXPERTS
    return pl.pallas_call(
        functools.partial(_router_kernel, tm=tm),
        out_shape=(jax.ShapeDtypeStruct((n, TOP_K), I32),
                   jax.ShapeDtypeStruct((n, TOP_K), F32),
                   jax.ShapeDtypeStruct((n, TOP_K), I32),
                   jax.ShapeDtypeStruct((1, e), I32)),
        grid=(n // tm,),
        in_specs=[pl.BlockSpec((tm, d), lambda i: (i, 0)),
                  pl.BlockSpec((d, e), lambda i: (0, 0)),
                  pl.BlockSpec((1, e), lambda i: (0, 0))],
        out_specs=(pl.BlockSpec((tm, TOP_K), lambda i: (i, 0)),
                   pl.BlockSpec((tm, TOP_K), lambda i: (i, 0)),
                   pl.BlockSpec((tm, TOP_K), lambda i: (i, 0)),
                   pl.BlockSpec((1, e), lambda i: (0, 0))),
        scratch_shapes=[pltpu.VMEM((1, e), F32)],
        compiler_params=_cparams(("arbitrary",)),
        name="moe_router",
    )(x2, rw, rb.reshape(1, e))


def _pack_bf16_pairs(y):
    w = y.shape[1] // 2
    lo = lax.bitcast_convert_type(y[:, :w].astype(BF16).astype(F32), U32)
    hi = lax.bitcast_convert_type(y[:, w:].astype(BF16).astype(F32), U32)
    return lax.shift_right_logical(lo, jnp.uint32(16)) | hi


def _unpack_bf16_pairs(words):
    lo = lax.bitcast_convert_type(lax.shift_left(words, jnp.uint32(16)), F32).astype(BF16)
    hi = lax.bitcast_convert_type(words & jnp.uint32(0xFFFF0000), F32).astype(BF16)
    return lo, hi


def _slab_load(ref, rows):
    return jnp.concatenate([ref[pl.ds(c, rows, stride=SLAB), :] for c in range(SLAB)], axis=1)


def _slab_store(ref, rows, val):
    for c in range(SLAB):
        ref[pl.ds(c, rows, stride=SLAB), :] = val[:, c * LANES:(c + 1) * LANES]


def _dispatch_kernel(pad_ref, dest_hbm, x_ref, xs_hbm, dest_sm, slab_sc, zero_sc, sem_idx, sem_zero, sem_row,
                     *, tt):
    i = pl.program_id(0)

    def zero_copy(e):
        start = pl.multiple_of(pad_ref[e] * SLAB, SLAB)
        return pltpu.make_async_copy(zero_sc, xs_hbm.at[pl.ds(start, EXPERT_ROWS * SLAB)], sem_zero)

    @pl.when(i == 0)
    def _():
        zero_sc[...] = jnp.zeros_like(zero_sc)

        def start(e, c):
            @pl.when(pad_ref[N_EXPERTS + e] > 0)
            def _():
                zero_copy(e).start()
            return c

        def wait(e, c):
            @pl.when(pad_ref[N_EXPERTS + e] > 0)
            def _():
                zero_copy(e).wait()
            return c

        lax.fori_loop(0, N_EXPERTS, start, 0)
        lax.fori_loop(0, N_EXPERTS, wait, 0)

    idx_copy = pltpu.make_async_copy(dest_hbm.at[i], dest_sm, sem_idx)
    idx_copy.start()
    _slab_store(slab_sc, tt, _pack_bf16_pairs(x_ref[...]))
    idx_copy.wait()

    def row_copy(t, dst_row):
        src = pl.multiple_of(t * SLAB, SLAB)
        dst = pl.multiple_of(dst_row * SLAB, SLAB)
        return pltpu.make_async_copy(slab_sc.at[pl.ds(src, SLAB)], xs_hbm.at[pl.ds(dst, SLAB)], sem_row)

    def start(t, c):
        for k in range(TOP_K):
            row_copy(t, dest_sm[t * TOP_K + k]).start(priority=k % 2)
        return c

    def wait(t, c):
        for k in range(TOP_K):
            row_copy(0, 0).wait()
        return c

    lax.fori_loop(0, tt, start, 0)
    lax.fori_loop(0, tt, wait, 0)


def _dispatch(pad_info, dest, x2, p_rows):
    n, d = x2.shape
    tt = 512
    assert TOP_K == 8
    assert d == 2 * SLAB * LANES
    dest2 = dest.reshape(n // tt, tt * TOP_K)
    grid_spec = pltpu.PrefetchScalarGridSpec(
        num_scalar_prefetch=1,
        grid=(n // tt,),
        in_specs=[pl.BlockSpec(memory_space=pl.ANY),
                  pl.BlockSpec((tt, d), lambda i, pad: (i, 0))],
        out_specs=pl.BlockSpec(memory_space=pl.ANY),
        scratch_shapes=[pltpu.SMEM((tt * TOP_K,), I32),
                        pltpu.VMEM((tt * SLAB, LANES), U32),
                        pltpu.VMEM((EXPERT_ROWS * SLAB, LANES), U32),
                        pltpu.SemaphoreType.DMA, pltpu.SemaphoreType.DMA, pltpu.SemaphoreType.DMA],
    )
    return pl.pallas_call(
        functools.partial(_dispatch_kernel, tt=tt),
        out_shape=jax.ShapeDtypeStruct((p_rows * SLAB, LANES), U32),
        grid_spec=grid_spec,
        compiler_params=_cparams(("arbitrary",)),
        name="moe_dispatch",
    )(pad_info, dest2, x2)


def _expert_mlp_kernel(be_ref, nu_ref, xs_ref, wg_ref, wu_ref, wd_ref, ys_ref, wg_sc, wu_sc, wd_sc):
    b = pl.program_id(0)
    e = be_ref[b]
    first = jnp.logical_or(b == 0, e != be_ref[jnp.maximum(b - 1, 0)])
    active = b < nu_ref[0]

    @pl.when(jnp.logical_and(active, first))
    def _():
        wg_sc[...] = wg_ref[...].astype(BF16)
        wu_sc[...] = wu_ref[...].astype(BF16)
        wd_sc[...] = wd_ref[...].astype(BF16)

    @pl.when(active)
    def _():
        r = EXPERT_ROWS
        half = wg_sc.shape[0] // 2
        x_lo, x_hi = _unpack_bf16_pairs(_slab_load(xs_ref, r))
        g = _dot(x_lo, wg_sc[:half]) + _dot(x_hi, wg_sc[half:])
        u = _dot(x_lo, wu_sc[:half]) + _dot(x_hi, wu_sc[half:])
        h = (g * jax.nn.sigmoid(g) * u).astype(BF16)
        _slab_store(ys_ref, r, _pack_bf16_pairs(_dot(h, wd_sc[...])))


def _expert_mlp(blk_e, n_used, xs, w_gate, w_up, w_down, layer):
    r = EXPERT_ROWS
    nblk = xs.shape[0] // (r * SLAB)
    d, de = w_gate.shape[2], w_gate.shape[3]

    def row_map(b, be, nu):
        return (jnp.minimum(b, nu[0] - 1), 0)

    def w_map(b, be, nu):
        return (layer, be[b], 0, 0)

    grid_spec = pltpu.PrefetchScalarGridSpec(
        num_scalar_prefetch=2,
        grid=(nblk,),
        in_specs=[pl.BlockSpec((r * SLAB, LANES), row_map),
                  pl.BlockSpec((None, None, d, de), w_map),
                  pl.BlockSpec((None, None, d, de), w_map),
                  pl.BlockSpec((None, None, de, d), w_map)],
        out_specs=pl.BlockSpec((r * SLAB, LANES), row_map),
        scratch_shapes=[pltpu.VMEM((d, de), BF16), pltpu.VMEM((d, de), BF16), pltpu.VMEM((de, d), BF16)],
    )
    return pl.pallas_call(
        _expert_mlp_kernel,
        out_shape=jax.ShapeDtypeStruct(xs.shape, U32),
        grid_spec=grid_spec,
        compiler_params=_cparams(("arbitrary",)),
        name="moe_experts",
    )(blk_e, n_used, xs, w_gate, w_up, w_down)


def _combine_kernel(dest_hbm, ys_hbm, x_ref, gate_ref, sg_ref, su_ref, sd_ref, g_ref, b_ref,
                    y_ref, dest_sm, rows_sc, sem_idx, sem_row, *, tt, n_tiles):
    i = pl.program_id(0)
    slot = lax.rem(i, 2)
    n_rows = tt * TOP_K

    def idx_copy(tile, s):
        return pltpu.make_async_copy(dest_hbm.at[tile], dest_sm.at[pl.ds(pl.multiple_of(s * n_rows, n_rows), n_rows)],
                                     sem_idx.at[s])

    def row_copy(s, t, k, src_row):
        return pltpu.make_async_copy(ys_hbm.at[pl.ds(pl.multiple_of(src_row * SLAB, SLAB), SLAB)],
                                     rows_sc.at[s, k, pl.ds(pl.multiple_of(t * SLAB, SLAB), SLAB)], sem_row.at[s])

    def start_rows(s):
        def start(t, c):
            for k in range(TOP_K):
                row_copy(s, t, k, dest_sm[s * n_rows + t * TOP_K + k]).start(priority=k % 2)
            return c
        lax.fori_loop(0, tt, start, 0)

    @pl.when(i == 0)
    def _():
        idx_copy(0, 0).start()
        idx_copy(0, 0).wait()
        start_rows(0)
        if n_tiles > 1:
            idx_copy(1, 1).start()

    @pl.when(i + 1 < n_tiles)
    def _():
        idx_copy(i + 1, 1 - slot).wait()
        start_rows(1 - slot)

    @pl.when(i + 2 < n_tiles)
    def _():
        idx_copy(i + 2, slot).start()

    x = x_ref[...]
    xb = x.astype(BF16)
    g = _dot(xb, sg_ref[...])
    u = _dot(xb, su_ref[...])
    h = (g * jax.nn.sigmoid(g) * u).astype(BF16)
    z = DEEPNORM_ALPHA * x + _dot(h, sd_ref[...])

    def wait(t, c):
        for k in range(TOP_K):
            row_copy(slot, 0, 0, 0).wait()
        return c
    lax.fori_loop(0, tt, wait, 0)

    gate = gate_ref[...]
    lo_pieces, hi_pieces = [], []
    for c in range(SLAB):
        acc_lo = jnp.zeros((tt, LANES), F32)
        acc_hi = jnp.zeros((tt, LANES), F32)
        for k in range(TOP_K):
            words = rows_sc[slot, k, pl.ds(c, tt, stride=SLAB), :]
            gk = gate[:, k:k + 1]
            acc_lo = acc_lo + gk * lax.bitcast_convert_type(lax.shift_left(words, jnp.uint32(16)), F32)
            acc_hi = acc_hi + gk * lax.bitcast_convert_type(words & jnp.uint32(0xFFFF0000), F32)
        lo_pieces.append(acc_lo)
        hi_pieces.append(acc_hi)
    routed = jnp.concatenate(lo_pieces + hi_pieces, axis=1)
    y_ref[...] = _layer_norm_rows(z + routed, g_ref[...], b_ref[...])


def _combine(dest, ys, x2, gate, sg_b, su_b, sd_b, ln_g, ln_b):
    n, d = x2.shape
    ds_ = sg_b.shape[1]
    tt = 256
    assert TOP_K == 8
    n_tiles = n // tt
    dest2 = dest.reshape(n_tiles, tt * TOP_K)
    return pl.pallas_call(
        functools.partial(_combine_kernel, tt=tt, n_tiles=n_tiles),
        out_shape=jax.ShapeDtypeStruct((n, d), F32),
        grid=(n_tiles,),
        in_specs=[pl.BlockSpec(memory_space=pl.ANY),
                  pl.BlockSpec(memory_space=pl.ANY),
                  pl.BlockSpec((tt, d), lambda i: (i, 0)),
                  pl.BlockSpec((tt, TOP_K), lambda i: (i, 0)),
                  pl.BlockSpec((d, ds_), lambda i: (0, 0)),
                  pl.BlockSpec((d, ds_), lambda i: (0, 0)),
                  pl.BlockSpec((ds_, d), lambda i: (0, 0)),
                  pl.BlockSpec((1, d), lambda i: (0, 0)),
                  pl.BlockSpec((1, d), lambda i: (0, 0))],
        out_specs=pl.BlockSpec((tt, d), lambda i: (i, 0)),
        scratch_shapes=[pltpu.SMEM((2 * tt * TOP_K,), I32),
                        pltpu.VMEM((2, TOP_K, tt * SLAB, LANES), U32),
                        pltpu.SemaphoreType.DMA((2,)), pltpu.SemaphoreType.DMA((2,))],
        compiler_params=_cparams(("arbitrary",)),
        name="moe_combine",
    )(dest2, ys, x2, gate, sg_b, su_b, sd_b, ln_g.reshape(1, d), ln_b.reshape(1, d))


def _moe_layer(x2, rw, rb, w_gate, w_up, w_down, layer, sh_gate, sh_up, sh_down, ln_g, ln_b):
    n, d = x2.shape
    e = N_EXPERTS
    r = EXPERT_ROWS
    sel, gate, rank, counts = _router(x2, rw, rb)
    counts = counts.reshape(e)
    padded = (counts + r - 1) // r * r
    pend = jnp.cumsum(padded)
    pstart = pend - padded
    dest = rank + jnp.sum(jnp.where(sel[..., None] == jnp.arange(e, dtype=I32), pstart.astype(I32), 0), axis=-1)
    p_rows = n * TOP_K + e * r
    nblk = p_rows // r
    blk_start = jnp.arange(nblk, dtype=I32) * r
    blk_e = jnp.minimum(jnp.sum((pend[None, :] <= blk_start[:, None]).astype(I32), axis=1), e - 1)
    n_used = (pend[-1] // r).astype(I32).reshape(1)
    pad_info = jnp.concatenate([jnp.maximum(pend - r, 0), padded]).astype(I32)
    xs = _dispatch(pad_info, dest.astype(I32), x2, p_rows)
    ys = _expert_mlp(blk_e, n_used, xs, w_gate, w_up, w_down, layer)
    return _combine(dest.astype(I32), ys, x2, gate, sh_gate.astype(BF16), sh_up.astype(BF16),
                    sh_down.astype(BF16), ln_g, ln_b)


FAR_BUCKET = REL_BUCKETS // 2 - 1


def _n_near_offsets(tq, tk):
    return (tk + REL_MAX_DIST - 1 + tq - 1) // tq


def _t5_bucket(rel):
    half = REL_BUCKETS // 2
    max_exact = half // 2
    n = jnp.abs(rel)
    large = max_exact + (jnp.log(jnp.maximum(n, 1).astype(F32) / max_exact)
                         / math.log(REL_MAX_DIST / max_exact) * (half - max_exact)).astype(I32)
    large = jnp.minimum(large, half - 1)
    return jnp.where(rel > 0, half, 0) + jnp.where(n < max_exact, n, large)


def _bias_tiles_kernel(rb_ref, out_ref, *, tq, tk):
    offset = -tq * pl.program_id(0)
    key = lax.broadcasted_iota(I32, (tk, tq), 0)
    qry = lax.broadcasted_iota(I32, (tk, tq), 1)
    bucket = _t5_bucket(offset + key - qry)
    for h in range(B_HEADS):
        far = rb_ref[FAR_BUCKET * B_HEADS + h]
        acc = jnp.zeros((tk, tq), F32)
        for b in range(REL_BUCKETS):
            acc = jnp.where(bucket == b, (rb_ref[b * B_HEADS + h] - far) * LOG2E, acc)
        out_ref[0, h] = acc


def _bias_tiles(rel_bias, tq, tk):
    n_off = _n_near_offsets(tq, tk)
    grid_spec = pltpu.PrefetchScalarGridSpec(
        num_scalar_prefetch=1,
        grid=(n_off,),
        in_specs=[],
        out_specs=pl.BlockSpec((1, B_HEADS, tk, tq), lambda i, rb: (i, 0, 0, 0)),
    )
    return pl.pallas_call(
        functools.partial(_bias_tiles_kernel, tq=tq, tk=tk),
        out_shape=jax.ShapeDtypeStruct((n_off, B_HEADS, tk, tq), F32),
        grid_spec=grid_spec,
        compiler_params=_cparams(("arbitrary",)),
        name="dsa_bias_tiles",
    )(rel_bias.reshape(-1))


def _proj1_kernel(x_ref, w_ref, wit_ref, qn_ref, kvn_ref, cq_ref, ckv_ref, ckvt_ref, ki_ref, wi_ref,
                  *, o1, o2, o3, wscale):
    x = x_ref[...]
    proj = _dot(x.astype(BF16), w_ref[...])
    cq_ref[...] = _rms_norm_rows(proj[:, :o1], qn_ref[...]).astype(BF16)
    ckv = _rms_norm_rows(proj[:, o1:o2], kvn_ref[...])
    ckv_ref[...] = ckv.astype(BF16)
    ckvt_ref[...] = ckv.T.astype(BF16)
    ki_ref[...] = proj[:, o2:o3].astype(BF16)
    wi_ref[...] = _dot_nt(wit_ref[...], x) * wscale


def _proj1(x2, w_in, q_norm, kv_norm, ql, kvl, batch, seq):
    n, d = x2.shape
    o1, o2, o3 = ql, ql + kvl, ql + kvl + IDX_DIM
    tm = 512
    nt = seq // tm
    w_main = w_in[:, :o3].astype(BF16)
    w_idx_t = w_in[:, o3:].T
    kern = functools.partial(_proj1_kernel, o1=o1, o2=o2, o3=o3,
                             wscale=(IDX_HEADS ** -0.5) * (IDX_DIM ** -0.5))
    return pl.pallas_call(
        kern,
        out_shape=(jax.ShapeDtypeStruct((n, ql), BF16), jax.ShapeDtypeStruct((n, kvl), BF16),
                   jax.ShapeDtypeStruct((batch * kvl, seq), BF16),
                   jax.ShapeDtypeStruct((n, IDX_DIM), BF16), jax.ShapeDtypeStruct((IDX_HEADS, n), F32)),
        grid=(n // tm,),
        in_specs=[pl.BlockSpec((tm, d), lambda i: (i, 0)),
                  pl.BlockSpec((d, o3), lambda i: (0, 0)),
                  pl.BlockSpec((IDX_HEADS, d), lambda i: (0, 0)),
                  pl.BlockSpec((1, ql), lambda i: (0, 0)),
                  pl.BlockSpec((1, kvl), lambda i: (0, 0))],
        out_specs=(pl.BlockSpec((tm, ql), lambda i: (i, 0)), pl.BlockSpec((tm, kvl), lambda i: (i, 0)),
                   pl.BlockSpec((kvl, tm), lambda i: (i // nt, i % nt)),
                   pl.BlockSpec((tm, IDX_DIM), lambda i: (i, 0)), pl.BlockSpec((IDX_HEADS, tm), lambda i: (0, i))),
        compiler_params=_cparams(("arbitrary",)),
        name="dsa_proj",
    )(x2, w_main, w_idx_t, q_norm.reshape(1, ql), kv_norm.reshape(1, kvl))


def _qside_kernel(cq_ref, wuq_ref, wuk_ref, wiq_ref, ql_ref, qi_ref, *, kvl, scale):
    cq = cq_ref[...]
    q = _dot(cq, wuq_ref[...]).astype(BF16)
    for h in range(B_HEADS):
        qh = q[:, h * B_HEAD_DIM:(h + 1) * B_HEAD_DIM]
        ql_ref[h] = (_dot_nt(wuk_ref[h], qh) * scale).astype(BF16)
    qi_ref[...] = _dot(cq, wiq_ref[...]).astype(BF16)


def _qside(cq, w_uq_b, w_uk_b, w_iq_b):
    n, ql = cq.shape
    kvl = w_uk_b.shape[1]
    tm = 512
    kern = functools.partial(_qside_kernel, kvl=kvl, scale=B_HEAD_DIM ** -0.5 * LOG2E)
    return pl.pallas_call(
        kern,
        out_shape=(jax.ShapeDtypeStruct((B_HEADS, kvl, n), BF16),
                   jax.ShapeDtypeStruct((n, IDX_HEADS * IDX_DIM), BF16)),
        grid=(n // tm,),
        in_specs=[pl.BlockSpec((tm, ql), lambda i: (i, 0)),
                  pl.BlockSpec(w_uq_b.shape, lambda i: (0, 0)),
                  pl.BlockSpec(w_uk_b.shape, lambda i: (0, 0, 0)),
                  pl.BlockSpec(w_iq_b.shape, lambda i: (0, 0))],
        out_specs=(pl.BlockSpec((B_HEADS, kvl, tm), lambda i: (0, 0, i)),
                   pl.BlockSpec((tm, IDX_HEADS * IDX_DIM), lambda i: (i, 0))),
        compiler_params=_cparams(("arbitrary",)),
        name="dsa_qside",
    )(cq, w_uq_b, w_uk_b, w_iq_b)


INT_MIN = -2 ** 31
KEY_NEG_INF = (0xFF800000 ^ 0x7FFFFFFF) - 2 ** 32


def _ordered_key(v):
    bits = lax.bitcast_convert_type(v, I32)
    return bits ^ (lax.shift_right_arithmetic(bits, 31) & 0x7FFFFFFF)


def _indexer_kernel(qi_ref, wi_ref, ki_ref, mask_ref, key_sc, *, tq, tkc, seq, topk):
    i = pl.program_id(1)
    n_chunks = (i * tq + tq + tkc - 1) // tkc
    qpos = i * tq + lax.broadcasted_iota(I32, (1, tq), 1)
    limit = (lax.shift_right_logical(qpos, 6) + 1) * CHUNK
    w = wi_ref[...]
    key_sc[...] = jnp.full(key_sc.shape, KEY_NEG_INF, I32)

    def score_chunk(c, carry):
        start = pl.multiple_of(c * tkc, tkc)
        k = ki_ref[pl.ds(start, tkc), :]
        acc = jnp.zeros((tkc, tq), F32)
        for h in range(IDX_HEADS):
            sc = _dot_nt(k, qi_ref[:, h * IDX_DIM:(h + 1) * IDX_DIM])
            acc = acc + jnp.maximum(sc, 0.0) * w[h:h + 1, :]
        kpos = start + lax.broadcasted_iota(I32, (tkc, tq), 0)
        key_sc[pl.ds(start, tkc), :] = _ordered_key(jnp.where(kpos < limit, acc, -jnp.inf))
        return carry

    lax.fori_loop(0, n_chunks, score_chunk, 0)

    def count(pred_fn):
        def body(c, acc):
            start = pl.multiple_of(c * tkc, tkc)
            hit = pred_fn(key_sc[pl.ds(start, tkc), :]).astype(I32)
            return acc + jnp.sum(hit.reshape(tkc // SLAB, SLAB, tq), axis=0)
        acc = lax.fori_loop(0, n_chunks, body, jnp.zeros((SLAB, tq), I32))
        return jnp.sum(acc, axis=0, keepdims=True)

    def bisect(it, prefix):
        cand = prefix + lax.shift_left(jnp.int32(1), 31 - it)
        cnt = count(lambda kk: kk >= cand)
        return jnp.where(cnt >= topk, cand, prefix)

    thr = lax.fori_loop(0, 32, bisect, jnp.full((1, tq), INT_MIN, I32))
    n_gt = count(lambda kk: kk > thr)
    n_eq = count(lambda kk: kk == thr)
    need = topk - n_gt
    tie_break = jnp.max(jnp.where(jnp.logical_and(thr > KEY_NEG_INF, n_eq > need), 1, 0)) > 0

    mask_ref[...] = jnp.full(mask_ref.shape, NEG_BIG, F32)

    @pl.when(jnp.logical_not(tie_break))
    def _():
        def write(c, carry):
            start = pl.multiple_of(c * tkc, tkc)
            kk = key_sc[pl.ds(start, tkc), :]
            sel = jnp.logical_and(kk >= thr, kk > KEY_NEG_INF)
            mask_ref[pl.ds(start, tkc), :] = jnp.where(sel, 0.0, NEG_BIG)
            return carry
        lax.fori_loop(0, n_chunks, write, 0)

    @pl.when(tie_break)
    def _():
        r_ = lax.broadcasted_iota(I32, (LANES, LANES), 0)
        c_ = lax.broadcasted_iota(I32, (LANES, LANES), 1)
        lower = (c_ < r_).astype(BF16)

        def write(c, seen):
            start = pl.multiple_of(c * LANES, LANES)
            kk = key_sc[pl.ds(start, LANES), :]
            eq = kk == thr
            before = seen + _dot(lower, eq.astype(BF16))
            sel = jnp.logical_or(kk > thr, jnp.logical_and(eq, before < need.astype(F32)))
            sel = jnp.logical_and(sel, kk > KEY_NEG_INF)
            mask_ref[pl.ds(start, LANES), :] = jnp.where(sel, 0.0, NEG_BIG)
            return seen + jnp.sum(eq.astype(F32), axis=0, keepdims=True)
        lax.fori_loop(0, n_chunks * (tkc // LANES), write, jnp.zeros((1, tq), F32))


def _indexer(qidx, widx_t, kidx, batch, seq, topk):
    tq = IDX_TQ
    tkc = min(IDX_TKC, seq)
    nq = seq // tq
    kern = functools.partial(_indexer_kernel, tq=tq, tkc=tkc, seq=seq, topk=topk)
    return pl.pallas_call(
        kern,
        out_shape=jax.ShapeDtypeStruct((batch * seq, seq), F32),
        grid=(batch, nq),
        in_specs=[pl.BlockSpec((tq, IDX_HEADS * IDX_DIM), lambda b, i: (b * nq + i, 0)),
                  pl.BlockSpec((IDX_HEADS, tq), lambda b, i: (0, b * nq + i)),
                  pl.BlockSpec((seq, IDX_DIM), lambda b, i: (b, 0))],
        out_specs=pl.BlockSpec((seq, tq), lambda b, i: (b, i)),
        scratch_shapes=[pltpu.VMEM((seq, tq), I32)],
        compiler_params=_cparams(("arbitrary", "arbitrary")),
        name="dsa_indexer",
    )(qidx, widx_t, kidx)


def _dsa_attn_kernel(qi_ref, kj_ref, qlt_ref, kv_ref, kvt_ref, mask_ref, bias_ref, wuvt_ref, o_ref,
                     m_sc, l_sc, acc_sc, *, tq, tk, n_near):
    p = pl.program_id(1)
    i = qi_ref[p]
    j = kj_ref[p]
    nh = B_HEADS
    behind = (i * tq - j * tk) // tq

    @pl.when(j == 0)
    def _():
        m_sc[...] = jnp.full_like(m_sc, -jnp.inf)
        l_sc[...] = jnp.zeros_like(l_sc)
        acc_sc[...] = jnp.zeros_like(acc_sc)

    def step(near):
        kv = kv_ref[...]
        kvt = kvt_ref[...]
        msk = mask_ref[...]
        for h in range(nh):
            s = _dot(kv, qlt_ref[h]) + msk
            if near:
                s = s + bias_ref[behind, h]
            m_prev = m_sc[h]
            m_new = jnp.maximum(m_prev, jnp.max(s, axis=0, keepdims=True))
            alpha = jnp.exp2(m_prev - m_new)
            pexp = jnp.exp2(s - m_new)
            l_sc[h] = alpha * l_sc[h] + jnp.sum(pexp, axis=0, keepdims=True)
            acc_sc[h] = alpha * acc_sc[h] + _dot(kvt, pexp.astype(BF16))
            m_sc[h] = m_new

    @pl.when(behind >= n_near)
    def _():
        step(False)

    @pl.when(behind < n_near)
    def _():
        step(True)

    @pl.when(j == (i * tq + tq - 1) // tk)
    def _():
        for h in range(nh):
            o_lat_t = (acc_sc[h] / l_sc[h]).astype(BF16)
            o_t = _dot(wuvt_ref[h], o_lat_t)
            o_ref[:, h * B_V_DIM:(h + 1) * B_V_DIM] = o_t.T.astype(o_ref.dtype)


def _dsa_attention(qlt, ckv, ckvt, mask_t, bias_tiles, w_uvt_b, batch, seq):
    tq, tk = DSA_TQ, DSA_TK
    kvl = ckv.shape[1]
    nq = seq // tq
    nk = seq // tk
    last = [(i * tq + tq - 1) // tk for i in range(nq)]
    qi = np.concatenate([np.full(last[i] + 1, i) for i in range(nq)]).astype(np.int32)
    kj = np.concatenate([np.arange(last[i] + 1) for i in range(nq)]).astype(np.int32)
    kern = functools.partial(_dsa_attn_kernel, tq=tq, tk=tk, n_near=bias_tiles.shape[0])
    grid_spec = pltpu.PrefetchScalarGridSpec(
        num_scalar_prefetch=2,
        grid=(batch, len(qi)),
        in_specs=[
            pl.BlockSpec((B_HEADS, kvl, tq), lambda b, p, qi, kj: (0, 0, b * nq + qi[p])),
            pl.BlockSpec((tk, kvl), lambda b, p, qi, kj: (b * nk + kj[p], 0)),
            pl.BlockSpec((kvl, tk), lambda b, p, qi, kj: (b, kj[p])),
            pl.BlockSpec((tk, tq), lambda b, p, qi, kj: (b * nk + kj[p], qi[p])),
            pl.BlockSpec(bias_tiles.shape, lambda b, p, qi, kj: (0, 0, 0, 0)),
            pl.BlockSpec(w_uvt_b.shape, lambda b, p, qi, kj: (0, 0, 0)),
        ],
        out_specs=pl.BlockSpec((tq, B_HEADS * B_V_DIM), lambda b, p, qi, kj: (b * nq + qi[p], 0)),
        scratch_shapes=[pltpu.VMEM((B_HEADS, 1, tq), F32), pltpu.VMEM((B_HEADS, 1, tq), F32),
                        pltpu.VMEM((B_HEADS, kvl, tq), F32)],
    )
    return pl.pallas_call(
        kern,
        out_shape=jax.ShapeDtypeStruct((batch * seq, B_HEADS * B_V_DIM), BF16),
        grid_spec=grid_spec,
        compiler_params=_cparams(("arbitrary", "arbitrary")),
        name="dsa_attention",
    )(jnp.asarray(qi), jnp.asarray(kj), qlt, ckv, ckvt, mask_t, bias_tiles, w_uvt_b)


def _dsa_layer(x2, w_in, q_norm, kv_norm, w_uq, w_iq, w_uk, w_uv, w_out, rel_bias, ln_g, ln_b, batch, seq):
    ql_dim = q_norm.shape[0]
    kvl = kv_norm.shape[0]
    topk = min(IDX_TOPK, seq // 4)
    cq, ckv, ckvt, kidx, widx_t = _proj1(x2, w_in, q_norm, kv_norm, ql_dim, kvl, batch, seq)
    qlt, qidx = _qside(cq, w_uq.astype(BF16), w_uk.astype(BF16), w_iq.astype(BF16))
    mask_t = _indexer(qidx, widx_t, kidx, batch, seq, topk)
    bias_tiles = _bias_tiles(rel_bias, DSA_TQ, DSA_TK)
    w_uvt = jnp.swapaxes(w_uv, 1, 2).astype(BF16)
    o = _dsa_attention(qlt, ckv, ckvt, mask_t, bias_tiles, w_uvt, batch, seq)
    return _outproj_ln(o, w_out.astype(BF16), x2, ln_g, ln_b)


def kernel(x, a_w_in, a_b_f, a_w_out, b_w_in, b_q_norm, b_kv_norm, b_w_uq, b_w_iq, b_w_uk, b_w_uv, b_w_out,
           rel_bias, ln1_g, ln1_b, ln2_g, ln2_b, router_w, router_b, w_gate, w_up, w_down, sh_gate, sh_up,
           sh_down):
    batch, seq, d = x.shape
    x2 = x.reshape(batch * seq, d)
    x2 = _fox_layer(x2, a_w_in[0], a_b_f[0], a_w_out[0], ln1_g[0], ln1_b[0], batch, seq)
    x2 = _moe_layer(x2, router_w[0], router_b[0], w_gate, w_up, w_down, 0, sh_gate[0], sh_up[0],
                    sh_down[0], ln2_g[0], ln2_b[0])
    x2 = _dsa_layer(x2, b_w_in[0], b_q_norm[0], b_kv_norm[0], b_w_uq[0], b_w_iq[0], b_w_uk[0], b_w_uv[0],
                    b_w_out[0], rel_bias, ln1_g[1], ln1_b[1], batch, seq)
    x2 = _moe_layer(x2, router_w[1], router_b[1], w_gate, w_up, w_down, 1, sh_gate[1], sh_up[1],
                    sh_down[1], ln2_g[1], ln2_b[1])
    return x2.reshape(batch, seq, d)
```

```python
import functools
import math

import numpy as np
import jax
import jax.numpy as jnp
from jax import lax
from jax.experimental import pallas as pl
from jax.experimental.pallas import tpu as pltpu

BF16 = jnp.bfloat16
F32 = jnp.float32
I32 = jnp.int32
U32 = jnp.uint32

A_HEADS = 16
A_HEAD_DIM = 128
B_HEADS = 16
B_HEAD_DIM = 128
B_V_DIM = 128
IDX_HEADS = 16
IDX_DIM = 64
IDX_TOPK = 256
CHUNK = 64
REL_BUCKETS = 32
REL_MAX_DIST = 128
N_EXPERTS = 64
TOP_K = 8
ROUTED_SCALE = 2.5
DEPTH = 2
DEEPNORM_ALPHA = (2 * DEPTH) ** 0.25
LN_EPS = 1e-5
RMS_EPS = 1e-6

LANES = 128
SLAB = 8
VMEM_LIMIT = 56 * 1024 * 1024
LOG2E = math.log2(math.e)
NEG_BIG = -1e30

EXPERT_ROWS = 512
FOX_TQ = 512
FOX_TK = 512
FOX_HEADS_PER_STEP = 4
DSA_TQ = 256
DSA_TK = 256
IDX_TQ = 256
IDX_TKC = 512


def _cparams(sem, vmem=VMEM_LIMIT):
    return pltpu.CompilerParams(dimension_semantics=sem, vmem_limit_bytes=vmem)


def _dot(a, b):
    return jnp.dot(a, b, preferred_element_type=F32)


def _dot_nt(a, b):
    return lax.dot_general(a, b, (((1,), (1,)), ((), ())), preferred_element_type=F32)


def _lane_tile(a, width):
    return a if width == LANES else jnp.concatenate([a] * (width // LANES), axis=1)


def _split2(a):
    hi = a.astype(BF16)
    lo = (a - hi.astype(F32)).astype(BF16)
    return hi, lo


def _split3(a):
    hi = a.astype(BF16)
    r = a - hi.astype(F32)
    mid = r.astype(BF16)
    lo = (r - mid.astype(F32)).astype(BF16)
    return hi, mid, lo


def _dot_x3(a, b):
    ah, al = _split2(a)
    bh, bl = _split2(b)
    return _dot(ah, bh) + (_dot(ah, bl) + _dot(al, bh))


def _layer_norm_rows(z, g, b):
    mu = jnp.mean(z, axis=-1, keepdims=True)
    d = z - mu
    var = jnp.mean(d * d, axis=-1, keepdims=True)
    return d * lax.rsqrt(var + LN_EPS) * g + b


def _rms_norm_rows(z, g):
    ms = jnp.mean(z * z, axis=-1, keepdims=True)
    return z * lax.rsqrt(ms + RMS_EPS) * g


def _proj0_kernel(x_ref, w_ref, wf_ref, qkv_ref, fl_ref, xb_sc, *, n_q_blocks, q_scale):
    j = pl.program_id(1)

    @pl.when(j == 0)
    def _():
        x = x_ref[...]
        xb_sc[...] = x.astype(BF16)
        fl_ref[...] = _dot_x3(x, wf_ref[...])

    acc = _dot(xb_sc[...], w_ref[...])
    scale = jnp.where(j < n_q_blocks, q_scale, 1.0).astype(F32)
    qkv_ref[...] = (acc * scale).astype(BF16)


def _proj0(x2, w_qkv_b, w_f):
    n, d = x2.shape
    nout = w_qkv_b.shape[1]
    tm, tn = 1024, 512
    dq = A_HEADS * A_HEAD_DIM
    kern = functools.partial(_proj0_kernel, n_q_blocks=dq // tn, q_scale=A_HEAD_DIM ** -0.5 * LOG2E)
    return pl.pallas_call(
        kern,
        out_shape=(jax.ShapeDtypeStruct((n, nout), BF16),
                   jax.ShapeDtypeStruct((n, A_HEADS), F32)),
        grid=(n // tm, nout // tn),
        in_specs=[pl.BlockSpec((tm, d), lambda i, j: (i, 0)),
                  pl.BlockSpec((d, tn), lambda i, j: (0, j)),
                  pl.BlockSpec((d, A_HEADS), lambda i, j: (0, 0))],
        out_specs=(pl.BlockSpec((tm, tn), lambda i, j: (i, j)),
                   pl.BlockSpec((tm, A_HEADS), lambda i, j: (i, 0))),
        scratch_shapes=[pltpu.VMEM((tm, d), BF16)],
        compiler_params=_cparams(("arbitrary", "arbitrary")),
        name="fox_proj",
    )(x2, w_qkv_b, w_f)


def _forget_cumsum_kernel(fl_ref, bf_ref, f_ref, carry_sc, *, t):
    @pl.when(pl.program_id(1) == 0)
    def _():
        carry_sc[...] = jnp.zeros_like(carry_sc)

    z = fl_ref[...] + bf_ref[...]
    logf = jnp.minimum(z, 0.0) - jnp.log1p(jnp.exp(-jnp.abs(z)))
    row = lax.broadcasted_iota(I32, (t, t), 0)
    col = lax.broadcasted_iota(I32, (t, t), 1)
    tri = (col <= row).astype(BF16)
    hi, mid, lo = _split3(logf)
    cs = _dot(tri, hi) + (_dot(tri, mid) + _dot(tri, lo)) + carry_sc[...]
    f_ref[...] = cs * LOG2E
    carry_sc[...] = cs[t - 1:t, :]


def _forget_cumsum(fl, b_f, batch, seq):
    t = 256
    nb = seq // t
    return pl.pallas_call(
        functools.partial(_forget_cumsum_kernel, t=t),
        out_shape=jax.ShapeDtypeStruct(fl.shape, F32),
        grid=(batch, nb),
        in_specs=[pl.BlockSpec((t, A_HEADS), lambda b, i: (b * nb + i, 0)),
                  pl.BlockSpec((1, A_HEADS), lambda b, i: (0, 0))],
        out_specs=pl.BlockSpec((t, A_HEADS), lambda b, i: (b * nb + i, 0)),
        scratch_shapes=[pltpu.VMEM((1, A_HEADS), F32)],
        compiler_params=_cparams(("arbitrary", "arbitrary")),
        name="fox_forget_cumsum",
    )(fl, b_f.reshape(1, A_HEADS))


def _fox_attn_kernel(qi_ref, kj_ref, q_ref, k_ref, v_ref, fk_ref, o_ref, m_sc, l_sc, acc_sc, *, tq, tk, hp):
    p = pl.program_id(2)
    i = qi_ref[p]
    j = kj_ref[p]
    dh = A_HEAD_DIM

    @pl.when(j == 0)
    def _():
        m_sc[...] = jnp.full_like(m_sc, -jnp.inf)
        l_sc[...] = jnp.zeros_like(l_sc)
        acc_sc[...] = jnp.zeros_like(acc_sc)

    def step(diag):
        for hh in range(hp):
            cols = slice(hh * dh, (hh + 1) * dh)
            s = _dot_nt(q_ref[:, cols], k_ref[:, cols]) - fk_ref[hh]
            if diag:
                row = lax.broadcasted_iota(I32, (tq, tk), 0)
                col = lax.broadcasted_iota(I32, (tq, tk), 1)
                s = jnp.where(col <= row, s, -jnp.inf)
            m_prev = m_sc[hh]
            m_new = jnp.maximum(m_prev, jnp.max(s, axis=1, keepdims=True))
            alpha = jnp.exp2(m_prev - m_new)
            pexp = jnp.exp2(s - _lane_tile(m_new, tk))
            l_sc[hh] = alpha * l_sc[hh] + jnp.sum(pexp, axis=1, keepdims=True)
            acc_sc[hh] = alpha * acc_sc[hh] + _dot(pexp.astype(BF16), v_ref[:, cols])
            m_sc[hh] = m_new

    @pl.when(j < i)
    def _():
        step(False)

    @pl.when(j == i)
    def _():
        step(True)
        for hh in range(hp):
            o_ref[:, hh * dh:(hh + 1) * dh] = (acc_sc[hh] / l_sc[hh]).astype(o_ref.dtype)


def _fox_attention(qkv, f_rows, batch, seq):
    t = min(FOX_TQ, seq)
    nq = seq // t
    qi = np.concatenate([np.full(i + 1, i) for i in range(nq)]).astype(np.int32)
    kj = np.concatenate([np.arange(i + 1) for i in range(nq)]).astype(np.int32)
    hp = FOX_HEADS_PER_STEP
    hg = A_HEADS // hp
    dh = A_HEAD_DIM
    w = hp * dh
    kern = functools.partial(_fox_attn_kernel, tq=t, tk=t, hp=hp)
    grid_spec = pltpu.PrefetchScalarGridSpec(
        num_scalar_prefetch=2,
        grid=(batch, hg, len(qi)),
        in_specs=[
            pl.BlockSpec((t, w), lambda b, h, p, qi, kj: (b * nq + qi[p], h)),
            pl.BlockSpec((t, w), lambda b, h, p, qi, kj: (b * nq + kj[p], hg + h)),
            pl.BlockSpec((t, w), lambda b, h, p, qi, kj: (b * nq + kj[p], 2 * hg + h)),
            pl.BlockSpec((hp, 1, t), lambda b, h, p, qi, kj: (b * hg + h, 0, kj[p])),
        ],
        out_specs=pl.BlockSpec((t, w), lambda b, h, p, qi, kj: (b * nq + qi[p], h)),
        scratch_shapes=[pltpu.VMEM((hp, t, dh), F32), pltpu.VMEM((hp, t, dh), F32),
                        pltpu.VMEM((hp, t, dh), F32)],
    )
    return pl.pallas_call(
        kern,
        out_shape=jax.ShapeDtypeStruct((batch * seq, A_HEADS * dh), BF16),
        grid_spec=grid_spec,
        compiler_params=_cparams(("arbitrary", "arbitrary", "arbitrary")),
        name="fox_attention",
    )(jnp.asarray(qi), jnp.asarray(kj), qkv, qkv, qkv, f_rows)


def _outproj_ln_kernel(o_ref, w_ref, x_ref, g_ref, b_ref, y_ref):
    z = DEEPNORM_ALPHA * x_ref[...] + _dot(o_ref[...], w_ref[...])
    y_ref[...] = _layer_norm_rows(z, g_ref[...], b_ref[...])


def _outproj_ln(o, w_b, x2, g, b):
    n, d = x2.shape
    k = o.shape[1]
    tm = 512
    return pl.pallas_call(
        _outproj_ln_kernel,
        out_shape=jax.ShapeDtypeStruct((n, d), F32),
        grid=(n // tm,),
        in_specs=[pl.BlockSpec((tm, k), lambda i: (i, 0)),
                  pl.BlockSpec((k, d), lambda i: (0, 0)),
                  pl.BlockSpec((tm, d), lambda i: (i, 0)),
                  pl.BlockSpec((1, d), lambda i: (0, 0)),
                  pl.BlockSpec((1, d), lambda i: (0, 0))],
        out_specs=pl.BlockSpec((tm, d), lambda i: (i, 0)),
        compiler_params=_cparams(("arbitrary",)),
        name="outproj_deepnorm",
    )(o, w_b, x2, g.reshape(1, d), b.reshape(1, d))


def _fox_layer(x2, w_in, b_f, w_out, ln_g, ln_b, batch, seq):
    dq = A_HEADS * A_HEAD_DIM
    qkv, fl = _proj0(x2, w_in[:, :3 * dq].astype(BF16), w_in[:, 3 * dq:])
    f = _forget_cumsum(fl, b_f, batch, seq)
    f_rows = f.reshape(batch, seq, A_HEADS).transpose(0, 2, 1).reshape(batch * A_HEADS, 1, seq)
    o = _fox_attention(qkv, f_rows, batch, seq)
    return _outproj_ln(o, w_out.astype(BF16), x2, ln_g, ln_b)


def _router_kernel(x_ref, rw_ref, rb_ref, sel_ref, gate_ref, rank_ref, cnt_ref, carry_sc, *, tm):
    @pl.when(pl.program_id(0) == 0)
    def _():
        carry_sc[...] = jnp.zeros_like(carry_sc)

    e = N_EXPERTS
    scores = jax.nn.sigmoid(_dot_x3(x_ref[...], rw_ref[...]))
    lane = lax.broadcasted_iota(I32, (tm, e), 1)
    slot = lax.broadcasted_iota(I32, (tm, TOP_K), 1)
    work = scores + rb_ref[...]
    chosen = jnp.zeros((tm, e), F32)
    sel = jnp.zeros((tm, TOP_K), I32)
    gate = jnp.zeros((tm, TOP_K), F32)
    idxs = []
    for k in range(TOP_K):
        mx = jnp.max(work, axis=1, keepdims=True)
        idx = jnp.min(jnp.where(work == mx, lane, e), axis=1, keepdims=True)
        hit = lane == idx
        gk = jnp.sum(jnp.where(hit, scores, 0.0), axis=1, keepdims=True)
        work = jnp.where(hit, -jnp.inf, work)
        chosen = jnp.where(hit, 1.0, chosen)
        sel = jnp.where(slot == k, idx, sel)
        gate = jnp.where(slot == k, gk, gate)
        idxs.append(idx)
    gate = gate / jnp.sum(gate, axis=1, keepdims=True) * ROUTED_SCALE

    row = lax.broadcasted_iota(I32, (tm, tm), 0)
    col = lax.broadcasted_iota(I32, (tm, tm), 1)
    before = _dot((col < row).astype(BF16), chosen.astype(BF16)) + carry_sc[...]
    rank = jnp.zeros((tm, TOP_K), F32)
    for k in range(TOP_K):
        rk = jnp.sum(jnp.where(lane == idxs[k], before, 0.0), axis=1, keepdims=True)
        rank = jnp.where(slot == k, rk, rank)
    total = carry_sc[...] + jnp.sum(chosen, axis=0, keepdims=True)
    carry_sc[...] = total
    sel_ref[...] = sel
    gate_ref[...] = gate
    rank_ref[...] = rank.astype(I32)
    cnt_ref[...] = total.astype(I32)


def _router(x2, rw, rb):
    n, d = x2.shape
    tm = 512
    e = N_EXPERTS
    return pl.pallas_call(
        functools.partial(_router_kernel, tm=tm),
        out_shape=(jax.ShapeDtypeStruct((n, TOP_K), I32),
                   jax.ShapeDtypeStruct((n, TOP_K), F32),
                   jax.ShapeDtypeStruct((n, TOP_K), I32),
                   jax.ShapeDtypeStruct((1, e), I32)),
        grid=(n // tm,),
        in_specs=[pl.BlockSpec((tm, d), lambda i: (i, 0)),
                  pl.BlockSpec((d, e), lambda i: (0, 0)),
                  pl.BlockSpec((1, e), lambda i: (0, 0))],
        out_specs=(pl.BlockSpec((tm, TOP_K), lambda i: (i, 0)),
                   pl.BlockSpec((tm, TOP_K), lambda i: (i, 0)),
                   pl.BlockSpec((tm, TOP_K), lambda i: (i, 0)),
                   pl.BlockSpec((1, e), lambda i: (0, 0))),
        scratch_shapes=[pltpu.VMEM((1, e), F32)],
        compiler_params=_cparams(("arbitrary",)),
        name="moe_router",
    )(x2, rw, rb.reshape(1, e))


def _pack_bf16_pairs(y):
    w = y.shape[1] // 2
    lo = lax.bitcast_convert_type(y[:, :w].astype(BF16).astype(F32), U32)
    hi = lax.bitcast_convert_type(y[:, w:].astype(BF16).astype(F32), U32)
    return lax.shift_right_logical(lo, jnp.uint32(16)) | hi


def _unpack_bf16_pairs(words):
    lo = lax.bitcast_convert_type(lax.shift_left(words, jnp.uint32(16)), F32).astype(BF16)
    hi = lax.bitcast_convert_type(words & jnp.uint32(0xFFFF0000), F32).astype(BF16)
    return lo, hi


def _slab_load(ref, rows):
    return jnp.concatenate([ref[pl.ds(c, rows, stride=SLAB), :] for c in range(SLAB)], axis=1)


def _slab_store(ref, rows, val):
    for c in range(SLAB):
        ref[pl.ds(c, rows, stride=SLAB), :] = val[:, c * LANES:(c + 1) * LANES]


def _dispatch_kernel(pad_ref, dest_hbm, x_ref, xs_hbm, dest_sm, slab_sc, zero_sc, sem_idx, sem_zero, sem_row,
                     *, tt, n_tiles):
    i = pl.program_id(0)

    def zero_copy(e):
        start = pl.multiple_of(pad_ref[e] * SLAB, SLAB)
        return pltpu.make_async_copy(zero_sc, xs_hbm.at[pl.ds(start, EXPERT_ROWS * SLAB)], sem_zero)

    @pl.when(i == 0)
    def _():
        zero_sc[...] = jnp.zeros_like(zero_sc)

        def start(e, c):
            @pl.when(pad_ref[N_EXPERTS + e] > 0)
            def _():
                zero_copy(e).start()
            return c

        def wait(e, c):
            @pl.when(pad_ref[N_EXPERTS + e] > 0)
            def _():
                zero_copy(e).wait()
            return c

        lax.fori_loop(0, N_EXPERTS, start, 0)
        lax.fori_loop(0, N_EXPERTS, wait, 0)

    slot = lax.rem(i, 2)
    idx_copy = pltpu.make_async_copy(dest_hbm.at[i], dest_sm, sem_idx)
    idx_copy.start()

    def row_copy(s, t, dst_row):
        src = pl.multiple_of(t * SLAB, SLAB)
        dst = pl.multiple_of(dst_row * SLAB, SLAB)
        return pltpu.make_async_copy(slab_sc.at[s, pl.ds(src, SLAB)], xs_hbm.at[pl.ds(dst, SLAB)], sem_row.at[s])

    def wait_rows(s):
        def wait(t, c):
            for k in range(TOP_K):
                row_copy(s, 0, 0).wait()
            return c
        lax.fori_loop(0, tt, wait, 0, unroll=2)

    @pl.when(i >= 2)
    def _():
        wait_rows(slot)

    _slab_store(slab_sc.at[slot], tt, _pack_bf16_pairs(x_ref[...]))
    idx_copy.wait()

    def start(t, c):
        for k in range(TOP_K):
            row_copy(slot, t, dest_sm[t * TOP_K + k]).start(priority=k % 2)
        return c

    lax.fori_loop(0, tt, start, 0, unroll=2)

    @pl.when(i == n_tiles - 1)
    def _():
        if n_tiles > 1:
            wait_rows(1 - slot)
        wait_rows(slot)


def _dispatch(pad_info, dest, x2, p_rows):
    n, d = x2.shape
    tt = 512
    assert TOP_K == 8
    assert d == 2 * SLAB * LANES
    dest2 = dest.reshape(n // tt, tt * TOP_K)
    grid_spec = pltpu.PrefetchScalarGridSpec(
        num_scalar_prefetch=1,
        grid=(n // tt,),
        in_specs=[pl.BlockSpec(memory_space=pl.ANY),
                  pl.BlockSpec((tt, d), lambda i, pad: (i, 0))],
        out_specs=pl.BlockSpec(memory_space=pl.ANY),
        scratch_shapes=[pltpu.SMEM((tt * TOP_K,), I32),
                        pltpu.VMEM((2, tt * SLAB, LANES), U32),
                        pltpu.VMEM((EXPERT_ROWS * SLAB, LANES), U32),
                        pltpu.SemaphoreType.DMA, pltpu.SemaphoreType.DMA, pltpu.SemaphoreType.DMA((2,))],
    )
    return pl.pallas_call(
        functools.partial(_dispatch_kernel, tt=tt, n_tiles=n // tt),
        out_shape=jax.ShapeDtypeStruct((p_rows * SLAB, LANES), U32),
        grid_spec=grid_spec,
        compiler_params=_cparams(("arbitrary",)),
        name="moe_dispatch",
    )(pad_info, dest2, x2)


def _expert_mlp_kernel(be_ref, nu_ref, xs_ref, wg_ref, wu_ref, wd_ref, ys_ref, wg_sc, wu_sc, wd_sc):
    b = pl.program_id(0)
    e = be_ref[b]
    first = jnp.logical_or(b == 0, e != be_ref[jnp.maximum(b - 1, 0)])
    active = b < nu_ref[0]

    @pl.when(jnp.logical_and(active, first))
    def _():
        wg_sc[...] = wg_ref[...].astype(BF16)
        wu_sc[...] = wu_ref[...].astype(BF16)
        wd_sc[...] = wd_ref[...].astype(BF16)

    @pl.when(active)
    def _():
        r = EXPERT_ROWS
        half = wg_sc.shape[0] // 2
        x_lo, x_hi = _unpack_bf16_pairs(_slab_load(xs_ref, r))
        g = _dot(x_lo, wg_sc[:half]) + _dot(x_hi, wg_sc[half:])
        u = _dot(x_lo, wu_sc[:half]) + _dot(x_hi, wu_sc[half:])
        h = (g * jax.nn.sigmoid(g) * u).astype(BF16)
        _slab_store(ys_ref, r, _pack_bf16_pairs(_dot(h, wd_sc[...])))


def _expert_mlp(blk_e, n_used, xs, w_gate, w_up, w_down, layer):
    r = EXPERT_ROWS
    nblk = xs.shape[0] // (r * SLAB)
    d, de = w_gate.shape[2], w_gate.shape[3]

    def row_map(b, be, nu):
        return (jnp.minimum(b, nu[0] - 1), 0)

    def w_map(b, be, nu):
        return (layer, be[b], 0, 0)

    grid_spec = pltpu.PrefetchScalarGridSpec(
        num_scalar_prefetch=2,
        grid=(nblk,),
        in_specs=[pl.BlockSpec((r * SLAB, LANES), row_map),
                  pl.BlockSpec((None, None, d, de), w_map),
                  pl.BlockSpec((None, None, d, de), w_map),
                  pl.BlockSpec((None, None, de, d), w_map)],
        out_specs=pl.BlockSpec((r * SLAB, LANES), row_map),
        scratch_shapes=[pltpu.VMEM((d, de), BF16), pltpu.VMEM((d, de), BF16), pltpu.VMEM((de, d), BF16)],
    )
    return pl.pallas_call(
        _expert_mlp_kernel,
        out_shape=jax.ShapeDtypeStruct(xs.shape, U32),
        grid_spec=grid_spec,
        compiler_params=_cparams(("arbitrary",)),
        name="moe_experts",
    )(blk_e, n_used, xs, w_gate, w_up, w_down)


def _combine_kernel(dest_hbm, ys_hbm, x_ref, gate_ref, sg_ref, su_ref, sd_ref, g_ref, b_ref,
                    y_ref, dest_sm, rows_sc, sem_idx, sem_row, *, tt, n_tiles):
    i = pl.program_id(0)
    slot = lax.rem(i, 2)
    n_rows = tt * TOP_K

    def idx_copy(tile, s):
        return pltpu.make_async_copy(dest_hbm.at[tile], dest_sm.at[pl.ds(pl.multiple_of(s * n_rows, n_rows), n_rows)],
                                     sem_idx.at[s])

    def row_copy(s, t, k, src_row):
        return pltpu.make_async_copy(ys_hbm.at[pl.ds(pl.multiple_of(src_row * SLAB, SLAB), SLAB)],
                                     rows_sc.at[s, k, pl.ds(pl.multiple_of(t * SLAB, SLAB), SLAB)], sem_row.at[s])

    def start_rows(s):
        def start(t, c):
            for k in range(TOP_K):
                row_copy(s, t, k, dest_sm[s * n_rows + t * TOP_K + k]).start(priority=k % 2)
            return c
        lax.fori_loop(0, tt, start, 0, unroll=2)

    @pl.when(i == 0)
    def _():
        idx_copy(0, 0).start()
        idx_copy(0, 0).wait()
        start_rows(0)
        if n_tiles > 1:
            idx_copy(1, 1).start()

    @pl.when(i + 1 < n_tiles)
    def _():
        idx_copy(i + 1, 1 - slot).wait()
        start_rows(1 - slot)

    @pl.when(i + 2 < n_tiles)
    def _():
        idx_copy(i + 2, slot).start()

    x = x_ref[...]
    xb = x.astype(BF16)
    g = _dot(xb, sg_ref[...])
    u = _dot(xb, su_ref[...])
    h = (g * jax.nn.sigmoid(g) * u).astype(BF16)
    z = DEEPNORM_ALPHA * x + _dot(h, sd_ref[...])

    def wait(t, c):
        for k in range(TOP_K):
            row_copy(slot, 0, 0, 0).wait()
        return c
    lax.fori_loop(0, tt, wait, 0, unroll=2)

    gate = gate_ref[...]
    lo_pieces, hi_pieces = [], []
    for c in range(SLAB):
        acc_lo = jnp.zeros((tt, LANES), F32)
        acc_hi = jnp.zeros((tt, LANES), F32)
        for k in range(TOP_K):
            words = rows_sc[slot, k, pl.ds(c, tt, stride=SLAB), :]
            gk = gate[:, k:k + 1]
            acc_lo = acc_lo + gk * lax.bitcast_convert_type(lax.shift_left(words, jnp.uint32(16)), F32)
            acc_hi = acc_hi + gk * lax.bitcast_convert_type(words & jnp.uint32(0xFFFF0000), F32)
        lo_pieces.append(acc_lo)
        hi_pieces.append(acc_hi)
    routed = jnp.concatenate(lo_pieces + hi_pieces, axis=1)
    y_ref[...] = _layer_norm_rows(z + routed, g_ref[...], b_ref[...])


def _combine(dest, ys, x2, gate, sg_b, su_b, sd_b, ln_g, ln_b):
    n, d = x2.shape
    ds_ = sg_b.shape[1]
    tt = 256
    assert TOP_K == 8
    n_tiles = n // tt
    dest2 = dest.reshape(n_tiles, tt * TOP_K)
    return pl.pallas_call(
        functools.partial(_combine_kernel, tt=tt, n_tiles=n_tiles),
        out_shape=jax.ShapeDtypeStruct((n, d), F32),
        grid=(n_tiles,),
        in_specs=[pl.BlockSpec(memory_space=pl.ANY),
                  pl.BlockSpec(memory_space=pl.ANY),
                  pl.BlockSpec((tt, d), lambda i: (i, 0)),
                  pl.BlockSpec((tt, TOP_K), lambda i: (i, 0)),
                  pl.BlockSpec((d, ds_), lambda i: (0, 0)),
                  pl.BlockSpec((d, ds_), lambda i: (0, 0)),
                  pl.BlockSpec((ds_, d), lambda i: (0, 0)),
                  pl.BlockSpec((1, d), lambda i: (0, 0)),
                  pl.BlockSpec((1, d), lambda i: (0, 0))],
        out_specs=pl.BlockSpec((tt, d), lambda i: (i, 0)),
        scratch_shapes=[pltpu.SMEM((2 * tt * TOP_K,), I32),
                        pltpu.VMEM((2, TOP_K, tt * SLAB, LANES), U32),
                        pltpu.SemaphoreType.DMA((2,)), pltpu.SemaphoreType.DMA((2,))],
        compiler_params=_cparams(("arbitrary",)),
        name="moe_combine",
    )(dest2, ys, x2, gate, sg_b, su_b, sd_b, ln_g.reshape(1, d), ln_b.reshape(1, d))


def _moe_layer(x2, rw, rb, w_gate, w_up, w_down, layer, sh_gate, sh_up, sh_down, ln_g, ln_b):
    n, d = x2.shape
    e = N_EXPERTS
    r = EXPERT_ROWS
    sel, gate, rank, counts = _router(x2, rw, rb)
    counts = counts.reshape(e)
    padded = (counts + r - 1) // r * r
    pend = jnp.cumsum(padded)
    pstart = pend - padded
    dest = rank + jnp.sum(jnp.where(sel[..., None] == jnp.arange(e, dtype=I32), pstart.astype(I32), 0), axis=-1)
    p_rows = n * TOP_K + e * r
    nblk = p_rows // r
    blk_start = jnp.arange(nblk, dtype=I32) * r
    blk_e = jnp.minimum(jnp.sum((pend[None, :] <= blk_start[:, None]).astype(I32), axis=1), e - 1)
    n_used = (pend[-1] // r).astype(I32).reshape(1)
    pad_info = jnp.concatenate([jnp.maximum(pend - r, 0), padded]).astype(I32)
    xs = _dispatch(pad_info, dest.astype(I32), x2, p_rows)
    ys = _expert_mlp(blk_e, n_used, xs, w_gate, w_up, w_down, layer)
    return _combine(dest.astype(I32), ys, x2, gate, sh_gate.astype(BF16), sh_up.astype(BF16),
                    sh_down.astype(BF16), ln_g, ln_b)


FAR_BUCKET = REL_BUCKETS // 2 - 1


def _n_near_offsets(tq, tk):
    return (tk + REL_MAX_DIST - 1 + tq - 1) // tq


def _t5_bucket(rel):
    half = REL_BUCKETS // 2
    max_exact = half // 2
    n = jnp.abs(rel)
    large = max_exact + (jnp.log(jnp.maximum(n, 1).astype(F32) / max_exact)
                         / math.log(REL_MAX_DIST / max_exact) * (half - max_exact)).astype(I32)
    large = jnp.minimum(large, half - 1)
    return jnp.where(rel > 0, half, 0) + jnp.where(n < max_exact, n, large)


def _bias_tiles_kernel(rb_ref, out_ref, *, tq, tk):
    offset = -tq * pl.program_id(0)
    key = lax.broadcasted_iota(I32, (tk, tq), 0)
    qry = lax.broadcasted_iota(I32, (tk, tq), 1)
    bucket = _t5_bucket(offset + key - qry)
    for h in range(B_HEADS):
        far = rb_ref[FAR_BUCKET * B_HEADS + h]
        acc = jnp.zeros((tk, tq), F32)
        for b in range(REL_BUCKETS):
            acc = jnp.where(bucket == b, (rb_ref[b * B_HEADS + h] - far) * LOG2E, acc)
        out_ref[0, h] = acc


def _bias_tiles(rel_bias, tq, tk):
    n_off = _n_near_offsets(tq, tk)
    grid_spec = pltpu.PrefetchScalarGridSpec(
        num_scalar_prefetch=1,
        grid=(n_off,),
        in_specs=[],
        out_specs=pl.BlockSpec((1, B_HEADS, tk, tq), lambda i, rb: (i, 0, 0, 0)),
    )
    return pl.pallas_call(
        functools.partial(_bias_tiles_kernel, tq=tq, tk=tk),
        out_shape=jax.ShapeDtypeStruct((n_off, B_HEADS, tk, tq), F32),
        grid_spec=grid_spec,
        compiler_params=_cparams(("arbitrary",)),
        name="dsa_bias_tiles",
    )(rel_bias.reshape(-1))


def _proj1_kernel(x_ref, w_ref, wit_ref, qn_ref, kvn_ref, cq_ref, ckv_ref, ckvt_ref, ki_ref, wi_ref,
                  *, o1, o2, o3, wscale):
    x = x_ref[...]
    proj = _dot(x.astype(BF16), w_ref[...])
    cq_ref[...] = _rms_norm_rows(proj[:, :o1], qn_ref[...]).astype(BF16)
    ckv = _rms_norm_rows(proj[:, o1:o2], kvn_ref[...])
    ckv_ref[...] = ckv.astype(BF16)
    ckvt_ref[...] = ckv.T.astype(BF16)
    ki_ref[...] = proj[:, o2:o3].astype(BF16)
    wi_ref[...] = _dot_nt(wit_ref[...], x) * wscale


def _proj1(x2, w_in, q_norm, kv_norm, ql, kvl, batch, seq):
    n, d = x2.shape
    o1, o2, o3 = ql, ql + kvl, ql + kvl + IDX_DIM
    tm = 512
    nt = seq // tm
    w_main = w_in[:, :o3].astype(BF16)
    w_idx_t = w_in[:, o3:].T
    kern = functools.partial(_proj1_kernel, o1=o1, o2=o2, o3=o3,
                             wscale=(IDX_HEADS ** -0.5) * (IDX_DIM ** -0.5))
    return pl.pallas_call(
        kern,
        out_shape=(jax.ShapeDtypeStruct((n, ql), BF16), jax.ShapeDtypeStruct((n, kvl), BF16),
                   jax.ShapeDtypeStruct((batch * kvl, seq), BF16),
                   jax.ShapeDtypeStruct((n, IDX_DIM), BF16), jax.ShapeDtypeStruct((IDX_HEADS, n), F32)),
        grid=(n // tm,),
        in_specs=[pl.BlockSpec((tm, d), lambda i: (i, 0)),
                  pl.BlockSpec((d, o3), lambda i: (0, 0)),
                  pl.BlockSpec((IDX_HEADS, d), lambda i: (0, 0)),
                  pl.BlockSpec((1, ql), lambda i: (0, 0)),
                  pl.BlockSpec((1, kvl), lambda i: (0, 0))],
        out_specs=(pl.BlockSpec((tm, ql), lambda i: (i, 0)), pl.BlockSpec((tm, kvl), lambda i: (i, 0)),
                   pl.BlockSpec((kvl, tm), lambda i: (i // nt, i % nt)),
                   pl.BlockSpec((tm, IDX_DIM), lambda i: (i, 0)), pl.BlockSpec((IDX_HEADS, tm), lambda i: (0, i))),
        compiler_params=_cparams(("arbitrary",)),
        name="dsa_proj",
    )(x2, w_main, w_idx_t, q_norm.reshape(1, ql), kv_norm.reshape(1, kvl))


def _qside_kernel(cq_ref, wuq_ref, wuk_ref, wiq_ref, ql_ref, qi_ref, *, kvl, scale):
    cq = cq_ref[...]
    q = _dot(cq, wuq_ref[...]).astype(BF16)
    for h in range(B_HEADS):
        qh = q[:, h * B_HEAD_DIM:(h + 1) * B_HEAD_DIM]
        ql_ref[h] = (_dot_nt(wuk_ref[h], qh) * scale).astype(BF16)
    qi_ref[...] = _dot(cq, wiq_ref[...]).astype(BF16)


def _qside(cq, w_uq_b, w_uk_b, w_iq_b):
    n, ql = cq.shape
    kvl = w_uk_b.shape[1]
    tm = 512
    kern = functools.partial(_qside_kernel, kvl=kvl, scale=B_HEAD_DIM ** -0.5 * LOG2E)
    return pl.pallas_call(
        kern,
        out_shape=(jax.ShapeDtypeStruct((B_HEADS, kvl, n), BF16),
                   jax.ShapeDtypeStruct((n, IDX_HEADS * IDX_DIM), BF16)),
        grid=(n // tm,),
        in_specs=[pl.BlockSpec((tm, ql), lambda i: (i, 0)),
                  pl.BlockSpec(w_uq_b.shape, lambda i: (0, 0)),
                  pl.BlockSpec(w_uk_b.shape, lambda i: (0, 0, 0)),
                  pl.BlockSpec(w_iq_b.shape, lambda i: (0, 0))],
        out_specs=(pl.BlockSpec((B_HEADS, kvl, tm), lambda i: (0, 0, i)),
                   pl.BlockSpec((tm, IDX_HEADS * IDX_DIM), lambda i: (i, 0))),
        compiler_params=_cparams(("arbitrary",)),
        name="dsa_qside",
    )(cq, w_uq_b, w_uk_b, w_iq_b)


INT_MIN = -2 ** 31
KEY_NEG_INF = (0xFF800000 ^ 0x7FFFFFFF) - 2 ** 32


def _ordered_key(v):
    bits = lax.bitcast_convert_type(v, I32)
    return bits ^ (lax.shift_right_arithmetic(bits, 31) & 0x7FFFFFFF)


def _indexer_kernel(qi_ref, wi_ref, ki_ref, mask_ref, key_sc, *, tq, tkc, seq, topk):
    i = pl.program_id(1)
    n_chunks = (i * tq + tq + tkc - 1) // tkc
    qpos = i * tq + lax.broadcasted_iota(I32, (1, tq), 1)
    limit = (lax.shift_right_logical(qpos, 6) + 1) * CHUNK
    w = wi_ref[...]
    key_sc[...] = jnp.full(key_sc.shape, KEY_NEG_INF, I32)

    def score_chunk(c, carry):
        start = pl.multiple_of(c * tkc, tkc)
        k = ki_ref[pl.ds(start, tkc), :]
        acc = jnp.zeros((tkc, tq), F32)
        for h in range(IDX_HEADS):
            sc = _dot_nt(k, qi_ref[:, h * IDX_DIM:(h + 1) * IDX_DIM])
            acc = acc + jnp.maximum(sc, 0.0) * w[h:h + 1, :]
        kpos = start + lax.broadcasted_iota(I32, (tkc, tq), 0)
        key_sc[pl.ds(start, tkc), :] = _ordered_key(jnp.where(kpos < limit, acc, -jnp.inf))
        return carry

    lax.fori_loop(0, n_chunks, score_chunk, 0)

    def count(pred_fn):
        def body(c, acc):
            start = pl.multiple_of(c * tkc, tkc)
            hit = pred_fn(key_sc[pl.ds(start, tkc), :]).astype(I32)
            return acc + jnp.sum(hit.reshape(tkc // SLAB, SLAB, tq), axis=0)
        acc = lax.fori_loop(0, n_chunks, body, jnp.zeros((SLAB, tq), I32))
        return jnp.sum(acc, axis=0, keepdims=True)

    def bisect(it, prefix):
        cand = prefix + lax.shift_left(jnp.int32(1), 31 - it)
        cnt = count(lambda kk: kk >= cand)
        return jnp.where(cnt >= topk, cand, prefix)

    thr = lax.fori_loop(0, 32, bisect, jnp.full((1, tq), INT_MIN, I32))
    n_gt = count(lambda kk: kk > thr)
    n_eq = count(lambda kk: kk == thr)
    need = topk - n_gt
    tie_break = jnp.max(jnp.where(jnp.logical_and(thr > KEY_NEG_INF, n_eq > need), 1, 0)) > 0

    mask_ref[...] = jnp.full(mask_ref.shape, NEG_BIG, F32)

    @pl.when(jnp.logical_not(tie_break))
    def _():
        def write(c, carry):
            start = pl.multiple_of(c * tkc, tkc)
            kk = key_sc[pl.ds(start, tkc), :]
            sel = jnp.logical_and(kk >= thr, kk > KEY_NEG_INF)
            mask_ref[pl.ds(start, tkc), :] = jnp.where(sel, 0.0, NEG_BIG)
            return carry
        lax.fori_loop(0, n_chunks, write, 0)

    @pl.when(tie_break)
    def _():
        r_ = lax.broadcasted_iota(I32, (LANES, LANES), 0)
        c_ = lax.broadcasted_iota(I32, (LANES, LANES), 1)
        lower = (c_ < r_).astype(BF16)

        def write(c, seen):
            start = pl.multiple_of(c * LANES, LANES)
            kk = key_sc[pl.ds(start, LANES), :]
            eq = kk == thr
            before = seen + _dot(lower, eq.astype(BF16))
            sel = jnp.logical_or(kk > thr, jnp.logical_and(eq, before < need.astype(F32)))
            sel = jnp.logical_and(sel, kk > KEY_NEG_INF)
            mask_ref[pl.ds(start, LANES), :] = jnp.where(sel, 0.0, NEG_BIG)
            return seen + jnp.sum(eq.astype(F32), axis=0, keepdims=True)
        lax.fori_loop(0, n_chunks * (tkc // LANES), write, jnp.zeros((1, tq), F32))


def _indexer(qidx, widx_t, kidx, batch, seq, topk):
    tq = IDX_TQ
    tkc = min(IDX_TKC, seq)
    nq = seq // tq
    kern = functools.partial(_indexer_kernel, tq=tq, tkc=tkc, seq=seq, topk=topk)
    return pl.pallas_call(
        kern,
        out_shape=jax.ShapeDtypeStruct((batch * seq, seq), F32),
        grid=(batch, nq),
        in_specs=[pl.BlockSpec((tq, IDX_HEADS * IDX_DIM), lambda b, i: (b * nq + i, 0)),
                  pl.BlockSpec((IDX_HEADS, tq), lambda b, i: (0, b * nq + i)),
                  pl.BlockSpec((seq, IDX_DIM), lambda b, i: (b, 0))],
        out_specs=pl.BlockSpec((seq, tq), lambda b, i: (b, i)),
        scratch_shapes=[pltpu.VMEM((seq, tq), I32)],
        compiler_params=_cparams(("arbitrary", "arbitrary")),
        name="dsa_indexer",
    )(qidx, widx_t, kidx)


def _dsa_attn_kernel(qi_ref, kj_ref, qlt_ref, kv_ref, kvt_ref, mask_ref, bias_ref, wuvt_ref, o_ref,
                     m_sc, l_sc, acc_sc, *, tq, tk, n_near):
    p = pl.program_id(1)
    i = qi_ref[p]
    j = kj_ref[p]
    nh = B_HEADS
    behind = (i * tq - j * tk) // tq

    @pl.when(j == 0)
    def _():
        m_sc[...] = jnp.full_like(m_sc, -jnp.inf)
        l_sc[...] = jnp.zeros_like(l_sc)
        acc_sc[...] = jnp.zeros_like(acc_sc)

    def step(near):
        kv = kv_ref[...]
        kvt = kvt_ref[...]
        msk = mask_ref[...]
        for h in range(nh):
            s = _dot(kv, qlt_ref[h]) + msk
            if near:
                s = s + bias_ref[behind, h]
            m_prev = m_sc[h]
            m_new = jnp.maximum(m_prev, jnp.max(s, axis=0, keepdims=True))
            alpha = jnp.exp2(m_prev - m_new)
            pexp = jnp.exp2(s - m_new)
            l_sc[h] = alpha * l_sc[h] + jnp.sum(pexp, axis=0, keepdims=True)
            acc_sc[h] = alpha * acc_sc[h] + _dot(kvt, pexp.astype(BF16))
            m_sc[h] = m_new

    @pl.when(behind >= n_near)
    def _():
        step(False)

    @pl.when(behind < n_near)
    def _():
        step(True)

    @pl.when(j == (i * tq + tq - 1) // tk)
    def _():
        for h in range(nh):
            o_lat_t = (acc_sc[h] / l_sc[h]).astype(BF16)
            o_t = _dot(wuvt_ref[h], o_lat_t)
            o_ref[:, h * B_V_DIM:(h + 1) * B_V_DIM] = o_t.T.astype(o_ref.dtype)


def _dsa_attention(qlt, ckv, ckvt, mask_t, bias_tiles, w_uvt_b, batch, seq):
    tq, tk = DSA_TQ, DSA_TK
    kvl = ckv.shape[1]
    nq = seq // tq
    nk = seq // tk
    last = [(i * tq + tq - 1) // tk for i in range(nq)]
    qi = np.concatenate([np.full(last[i] + 1, i) for i in range(nq)]).astype(np.int32)
    kj = np.concatenate([np.arange(last[i] + 1) for i in range(nq)]).astype(np.int32)
    kern = functools.partial(_dsa_attn_kernel, tq=tq, tk=tk, n_near=bias_tiles.shape[0])
    grid_spec = pltpu.PrefetchScalarGridSpec(
        num_scalar_prefetch=2,
        grid=(batch, len(qi)),
        in_specs=[
            pl.BlockSpec((B_HEADS, kvl, tq), lambda b, p, qi, kj: (0, 0, b * nq + qi[p])),
            pl.BlockSpec((tk, kvl), lambda b, p, qi, kj: (b * nk + kj[p], 0)),
            pl.BlockSpec((kvl, tk), lambda b, p, qi, kj: (b, kj[p])),
            pl.BlockSpec((tk, tq), lambda b, p, qi, kj: (b * nk + kj[p], qi[p])),
            pl.BlockSpec(bias_tiles.shape, lambda b, p, qi, kj: (0, 0, 0, 0)),
            pl.BlockSpec(w_uvt_b.shape, lambda b, p, qi, kj: (0, 0, 0)),
        ],
        out_specs=pl.BlockSpec((tq, B_HEADS * B_V_DIM), lambda b, p, qi, kj: (b * nq + qi[p], 0)),
        scratch_shapes=[pltpu.VMEM((B_HEADS, 1, tq), F32), pltpu.VMEM((B_HEADS, 1, tq), F32),
                        pltpu.VMEM((B_HEADS, kvl, tq), F32)],
    )
    return pl.pallas_call(
        kern,
        out_shape=jax.ShapeDtypeStruct((batch * seq, B_HEADS * B_V_DIM), BF16),
        grid_spec=grid_spec,
        compiler_params=_cparams(("arbitrary", "arbitrary")),
        name="dsa_attention",
    )(jnp.asarray(qi), jnp.asarray(kj), qlt, ckv, ckvt, mask_t, bias_tiles, w_uvt_b)


def _dsa_layer(x2, w_in, q_norm, kv_norm, w_uq, w_iq, w_uk, w_uv, w_out, rel_bias, ln_g, ln_b, batch, seq):
    ql_dim = q_norm.shape[0]
    kvl = kv_norm.shape[0]
    topk = min(IDX_TOPK, seq // 4)
    cq, ckv, ckvt, kidx, widx_t = _proj1(x2, w_in, q_norm, kv_norm, ql_dim, kvl, batch, seq)
    qlt, qidx = _qside(cq, w_uq.astype(BF16), w_uk.astype(BF16), w_iq.astype(BF16))
    mask_t = _indexer(qidx, widx_t, kidx, batch, seq, topk)
    bias_tiles = _bias_tiles(rel_bias, DSA_TQ, DSA_TK)
    w_uvt = jnp.swapaxes(w_uv, 1, 2).astype(BF16)
    o = _dsa_attention(qlt, ckv, ckvt, mask_t, bias_tiles, w_uvt, batch, seq)
    return _outproj_ln(o, w_out.astype(BF16), x2, ln_g, ln_b)


def kernel(x, a_w_in, a_b_f, a_w_out, b_w_in, b_q_norm, b_kv_norm, b_w_uq, b_w_iq, b_w_uk, b_w_uv, b_w_out,
           rel_bias, ln1_g, ln1_b, ln2_g, ln2_b, router_w, router_b, w_gate, w_up, w_down, sh_gate, sh_up,
           sh_down):
    batch, seq, d = x.shape
    x2 = x.reshape(batch * seq, d)
    x2 = _fox_layer(x2, a_w_in[0], a_b_f[0], a_w_out[0], ln1_g[0], ln1_b[0], batch, seq)
    x2 = _moe_layer(x2, router_w[0], router_b[0], w_gate, w_up, w_down, 0, sh_gate[0], sh_up[0],
                    sh_down[0], ln2_g[0], ln2_b[0])
    x2 = _dsa_layer(x2, b_w_in[0], b_q_norm[0], b_kv_norm[0], b_w_uq[0], b_w_iq[0], b_w_uk[0], b_w_uv[0],
                    b_w_out[0], rel_bias, ln1_g[1], ln1_b[1], batch, seq)
    x2 = _moe_layer(x2, router_w[1], router_b[1], w_gate, w_up, w_down, 1, sh_gate[1], sh_up[1],
                    sh_down[1], ln2_g[1], ln2_b[1])
    return x2.reshape(batch, seq, d)
```

```python
import functools
import math

import numpy as np
import jax
import jax.numpy as jnp
from jax import lax
from jax.experimental import pallas as pl
from jax.experimental.pallas import tpu as pltpu

BF16 = jnp.bfloat16
F32 = jnp.float32
I32 = jnp.int32
U32 = jnp.uint32

A_HEADS = 16
A_HEAD_DIM = 128
B_HEADS = 16
B_HEAD_DIM = 128
B_V_DIM = 128
IDX_HEADS = 16
IDX_DIM = 64
IDX_TOPK = 256
CHUNK = 64
REL_BUCKETS = 32
REL_MAX_DIST = 128
N_EXPERTS = 64
TOP_K = 8
ROUTED_SCALE = 2.5
DEPTH = 2
DEEPNORM_ALPHA = (2 * DEPTH) ** 0.25
LN_EPS = 1e-5
RMS_EPS = 1e-6

LANES = 128
SLAB = 8
VMEM_LIMIT = 56 * 1024 * 1024
LOG2E = math.log2(math.e)
NEG_BIG = -1e30

EXPERT_ROWS = 512
FOX_TQ = 512
FOX_TK = 512
FOX_HEADS_PER_STEP = 8
DSA_TQ = 256
DSA_TK = 256
IDX_TQ = 256
IDX_TKC = 512


def _cparams(sem, vmem=VMEM_LIMIT):
    return pltpu.CompilerParams(dimension_semantics=sem, vmem_limit_bytes=vmem)


def _dot(a, b):
    return jnp.dot(a, b, preferred_element_type=F32)


def _dot_nt(a, b):
    return lax.dot_general(a, b, (((1,), (1,)), ((), ())), preferred_element_type=F32)


def _lane_tile(a, width):
    return a if width == LANES else jnp.concatenate([a] * (width // LANES), axis=1)


def _split2(a):
    hi = a.astype(BF16)
    lo = (a - hi.astype(F32)).astype(BF16)
    return hi, lo


def _split3(a):
    hi = a.astype(BF16)
    r = a - hi.astype(F32)
    mid = r.astype(BF16)
    lo = (r - mid.astype(F32)).astype(BF16)
    return hi, mid, lo


def _dot_x3(a, b):
    ah, al = _split2(a)
    bh, bl = _split2(b)
    return _dot(ah, bh) + (_dot(ah, bl) + _dot(al, bh))


def _layer_norm_rows(z, g, b):
    mu = jnp.mean(z, axis=-1, keepdims=True)
    d = z - mu
    var = jnp.mean(d * d, axis=-1, keepdims=True)
    return d * lax.rsqrt(var + LN_EPS) * g + b


def _rms_norm_rows(z, g):
    ms = jnp.mean(z * z, axis=-1, keepdims=True)
    return z * lax.rsqrt(ms + RMS_EPS) * g


def _proj0_kernel(x_ref, w_ref, wf_ref, qkv_ref, fl_ref, xb_sc, *, n_q_blocks, q_scale):
    j = pl.program_id(1)

    @pl.when(j == 0)
    def _():
        x = x_ref[...]
        xb_sc[...] = x.astype(BF16)
        fl_ref[...] = _dot_x3(x, wf_ref[...])

    acc = _dot(xb_sc[...], w_ref[...])
    scale = jnp.where(j < n_q_blocks, q_scale, 1.0).astype(F32)
    qkv_ref[...] = (acc * scale).astype(BF16)


def _proj0(x2, w_qkv_b, w_f):
    n, d = x2.shape
    nout = w_qkv_b.shape[1]
    tm, tn = 1024, 512
    dq = A_HEADS * A_HEAD_DIM
    kern = functools.partial(_proj0_kernel, n_q_blocks=dq // tn, q_scale=A_HEAD_DIM ** -0.5 * LOG2E)
    return pl.pallas_call(
        kern,
        out_shape=(jax.ShapeDtypeStruct((n, nout), BF16),
                   jax.ShapeDtypeStruct((n, A_HEADS), F32)),
        grid=(n // tm, nout // tn),
        in_specs=[pl.BlockSpec((tm, d), lambda i, j: (i, 0)),
                  pl.BlockSpec((d, tn), lambda i, j: (0, j)),
                  pl.BlockSpec((d, A_HEADS), lambda i, j: (0, 0))],
        out_specs=(pl.BlockSpec((tm, tn), lambda i, j: (i, j)),
                   pl.BlockSpec((tm, A_HEADS), lambda i, j: (i, 0))),
        scratch_shapes=[pltpu.VMEM((tm, d), BF16)],
        compiler_params=_cparams(("arbitrary", "arbitrary")),
        name="fox_proj",
    )(x2, w_qkv_b, w_f)


def _forget_cumsum_kernel(fl_ref, bf_ref, f_ref, carry_sc, *, t):
    @pl.when(pl.program_id(1) == 0)
    def _():
        carry_sc[...] = jnp.zeros_like(carry_sc)

    z = fl_ref[...] + bf_ref[...]
    logf = jnp.minimum(z, 0.0) - jnp.log1p(jnp.exp(-jnp.abs(z)))
    row = lax.broadcasted_iota(I32, (t, t), 0)
    col = lax.broadcasted_iota(I32, (t, t), 1)
    tri = (col <= row).astype(BF16)
    hi, mid, lo = _split3(logf)
    cs = _dot(tri, hi) + (_dot(tri, mid) + _dot(tri, lo)) + carry_sc[...]
    f_ref[...] = cs * LOG2E
    carry_sc[...] = cs[t - 1:t, :]


def _forget_cumsum(fl, b_f, batch, seq):
    t = 256
    nb = seq // t
    return pl.pallas_call(
        functools.partial(_forget_cumsum_kernel, t=t),
        out_shape=jax.ShapeDtypeStruct(fl.shape, F32),
        grid=(batch, nb),
        in_specs=[pl.BlockSpec((t, A_HEADS), lambda b, i: (b * nb + i, 0)),
                  pl.BlockSpec((1, A_HEADS), lambda b, i: (0, 0))],
        out_specs=pl.BlockSpec((t, A_HEADS), lambda b, i: (b * nb + i, 0)),
        scratch_shapes=[pltpu.VMEM((1, A_HEADS), F32)],
        compiler_params=_cparams(("arbitrary", "arbitrary")),
        name="fox_forget_cumsum",
    )(fl, b_f.reshape(1, A_HEADS))


def _fox_attn_kernel(qi_ref, kj_ref, q_ref, k_ref, v_ref, fk_ref, o_ref, m_sc, l_sc, acc_sc, *, tq, tk, hp):
    p = pl.program_id(2)
    i = qi_ref[p]
    j = kj_ref[p]
    dh = A_HEAD_DIM

    @pl.when(j == 0)
    def _():
        m_sc[...] = jnp.full_like(m_sc, -jnp.inf)
        l_sc[...] = jnp.zeros_like(l_sc)
        acc_sc[...] = jnp.zeros_like(acc_sc)

    def step(diag):
        for hh in range(hp):
            cols = slice(hh * dh, (hh + 1) * dh)
            s = _dot_nt(q_ref[:, cols], k_ref[:, cols]) - fk_ref[hh]
            if diag:
                row = lax.broadcasted_iota(I32, (tq, tk), 0)
                col = lax.broadcasted_iota(I32, (tq, tk), 1)
                s = jnp.where(col <= row, s, -jnp.inf)
            m_prev = m_sc[hh]
            m_new = jnp.maximum(m_prev, jnp.max(s, axis=1, keepdims=True))
            alpha = jnp.exp2(m_prev - m_new)
            pexp = jnp.exp2(s - _lane_tile(m_new, tk))
            l_sc[hh] = alpha * l_sc[hh] + jnp.sum(pexp, axis=1, keepdims=True)
            acc_sc[hh] = alpha * acc_sc[hh] + _dot(pexp.astype(BF16), v_ref[:, cols])
            m_sc[hh] = m_new

    @pl.when(j < i)
    def _():
        step(False)

    @pl.when(j == i)
    def _():
        step(True)
        for hh in range(hp):
            o_ref[:, hh * dh:(hh + 1) * dh] = (acc_sc[hh] / l_sc[hh]).astype(o_ref.dtype)


def _fox_attention(qkv, f_rows, batch, seq):
    t = min(FOX_TQ, seq)
    nq = seq // t
    qi = np.concatenate([np.full(i + 1, i) for i in range(nq)]).astype(np.int32)
    kj = np.concatenate([np.arange(i + 1) for i in range(nq)]).astype(np.int32)
    hp = FOX_HEADS_PER_STEP
    hg = A_HEADS // hp
    dh = A_HEAD_DIM
    w = hp * dh
    kern = functools.partial(_fox_attn_kernel, tq=t, tk=t, hp=hp)
    grid_spec = pltpu.PrefetchScalarGridSpec(
        num_scalar_prefetch=2,
        grid=(batch, hg, len(qi)),
        in_specs=[
            pl.BlockSpec((t, w), lambda b, h, p, qi, kj: (b * nq + qi[p], h)),
            pl.BlockSpec((t, w), lambda b, h, p, qi, kj: (b * nq + kj[p], hg + h)),
            pl.BlockSpec((t, w), lambda b, h, p, qi, kj: (b * nq + kj[p], 2 * hg + h)),
            pl.BlockSpec((hp, 1, t), lambda b, h, p, qi, kj: (b * hg + h, 0, kj[p])),
        ],
        out_specs=pl.BlockSpec((t, w), lambda b, h, p, qi, kj: (b * nq + qi[p], h)),
        scratch_shapes=[pltpu.VMEM((hp, t, dh), F32), pltpu.VMEM((hp, t, dh), F32),
                        pltpu.VMEM((hp, t, dh), F32)],
    )
    return pl.pallas_call(
        kern,
        out_shape=jax.ShapeDtypeStruct((batch * seq, A_HEADS * dh), BF16),
        grid_spec=grid_spec,
        compiler_params=_cparams(("arbitrary", "arbitrary", "arbitrary")),
        name="fox_attention",
    )(jnp.asarray(qi), jnp.asarray(kj), qkv, qkv, qkv, f_rows)


def _outproj_ln_kernel(o_ref, w_ref, x_ref, g_ref, b_ref, y_ref):
    z = DEEPNORM_ALPHA * x_ref[...] + _dot(o_ref[...], w_ref[...])
    y_ref[...] = _layer_norm_rows(z, g_ref[...], b_ref[...])


def _outproj_ln(o, w_b, x2, g, b):
    n, d = x2.shape
    k = o.shape[1]
    tm = 512
    return pl.pallas_call(
        _outproj_ln_kernel,
        out_shape=jax.ShapeDtypeStruct((n, d), F32),
        grid=(n // tm,),
        in_specs=[pl.BlockSpec((tm, k), lambda i: (i, 0)),
                  pl.BlockSpec((k, d), lambda i: (0, 0)),
                  pl.BlockSpec((tm, d), lambda i: (i, 0)),
                  pl.BlockSpec((1, d), lambda i: (0, 0)),
                  pl.BlockSpec((1, d), lambda i: (0, 0))],
        out_specs=pl.BlockSpec((tm, d), lambda i: (i, 0)),
        compiler_params=_cparams(("arbitrary",)),
        name="outproj_deepnorm",
    )(o, w_b, x2, g.reshape(1, d), b.reshape(1, d))


def _fox_layer(x2, w_in, b_f, w_out, ln_g, ln_b, batch, seq):
    dq = A_HEADS * A_HEAD_DIM
    qkv, fl = _proj0(x2, w_in[:, :3 * dq].astype(BF16), w_in[:, 3 * dq:])
    f = _forget_cumsum(fl, b_f, batch, seq)
    f_rows = f.reshape(batch, seq, A_HEADS).transpose(0, 2, 1).reshape(batch * A_HEADS, 1, seq)
    o = _fox_attention(qkv, f_rows, batch, seq)
    return _outproj_ln(o, w_out.astype(BF16), x2, ln_g, ln_b)


def _router_kernel(x_ref, rw_ref, rb_ref, sel_ref, gate_ref, rank_ref, cnt_ref, carry_sc, *, tm):
    @pl.when(pl.program_id(0) == 0)
    def _():
        carry_sc[...] = jnp.zeros_like(carry_sc)

    e = N_EXPERTS
    scores = jax.nn.sigmoid(_dot_x3(x_ref[...], rw_ref[...]))
    lane = lax.broadcasted_iota(I32, (tm, e), 1)
    slot = lax.broadcasted_iota(I32, (tm, TOP_K), 1)
    work = scores + rb_ref[...]
    chosen = jnp.zeros((tm, e), F32)
    sel = jnp.zeros((tm, TOP_K), I32)
    gate = jnp.zeros((tm, TOP_K), F32)
    idxs = []
    for k in range(TOP_K):
        mx = jnp.max(work, axis=1, keepdims=True)
        idx = jnp.min(jnp.where(work == mx, lane, e), axis=1, keepdims=True)
        hit = lane == idx
        gk = jnp.sum(jnp.where(hit, scores, 0.0), axis=1, keepdims=True)
        work = jnp.where(hit, -jnp.inf, work)
        chosen = jnp.where(hit, 1.0, chosen)
        sel = jnp.where(slot == k, idx, sel)
        gate = jnp.where(slot == k, gk, gate)
        idxs.append(idx)
    gate = gate / jnp.sum(gate, axis=1, keepdims=True) * ROUTED_SCALE

    row = lax.broadcasted_iota(I32, (tm, tm), 0)
    col = lax.broadcasted_iota(I32, (tm, tm), 1)
    before = _dot((col < row).astype(BF16), chosen.astype(BF16)) + carry_sc[...]
    rank = jnp.zeros((tm, TOP_K), F32)
    for k in range(TOP_K):
        rk = jnp.sum(jnp.where(lane == idxs[k], before, 0.0), axis=1, keepdims=True)
        rank = jnp.where(slot == k, rk, rank)
    total = carry_sc[...] + jnp.sum(chosen, axis=0, keepdims=True)
    carry_sc[...] = total
    sel_ref[...] = sel
    gate_ref[...] = gate
    rank_ref[...] = rank.astype(I32)
    cnt_ref[...] = total.astype(I32)


def _router(x2, rw, rb):
    n, d = x2.shape
    tm = 512
    e = N_EXPERTS
    return pl.pallas_call(
        functools.partial(_router_kernel, tm=tm),
        out_shape=(jax.ShapeDtypeStruct((n, TOP_K), I32),
                   jax.ShapeDtypeStruct((n, TOP_K), F32),
                   jax.ShapeDtypeStruct((n, TOP_K), I32),
                   jax.ShapeDtypeStruct((1, e), I32)),
        grid=(n // tm,),
        in_specs=[pl.BlockSpec((tm, d), lambda i: (i, 0)),
                  pl.BlockSpec((d, e), lambda i: (0, 0)),
                  pl.BlockSpec((1, e), lambda i: (0, 0))],
        out_specs=(pl.BlockSpec((tm, TOP_K), lambda i: (i, 0)),
                   pl.BlockSpec((tm, TOP_K), lambda i: (i, 0)),
                   pl.BlockSpec((tm, TOP_K), lambda i: (i, 0)),
                   pl.BlockSpec((1, e), lambda i: (0, 0))),
        scratch_shapes=[pltpu.VMEM((1, e), F32)],
        compiler_params=_cparams(("arbitrary",)),
        name="moe_router",
    )(x2, rw, rb.reshape(1, e))


def _pack_bf16_pairs(y):
    w = y.shape[1] // 2
    lo = lax.bitcast_convert_type(y[:, :w].astype(BF16).astype(F32), U32)
    hi = lax.bitcast_convert_type(y[:, w:].astype(BF16).astype(F32), U32)
    return lax.shift_right_logical(lo, jnp.uint32(16)) | hi


def _unpack_bf16_pairs(words):
    lo = lax.bitcast_convert_type(lax.shift_left(words, jnp.uint32(16)), F32).astype(BF16)
    hi = lax.bitcast_convert_type(words & jnp.uint32(0xFFFF0000), F32).astype(BF16)
    return lo, hi


def _slab_load(ref, rows):
    return jnp.concatenate([ref[pl.ds(c, rows, stride=SLAB), :] for c in range(SLAB)], axis=1)


def _slab_store(ref, rows, val):
    for c in range(SLAB):
        ref[pl.ds(c, rows, stride=SLAB), :] = val[:, c * LANES:(c + 1) * LANES]


def _dispatch_kernel(pad_ref, dest_hbm, x_ref, xs_hbm, dest_sm, slab_sc, zero_sc, sem_idx, sem_zero, sem_row,
                     *, tt, n_tiles):
    i = pl.program_id(0)

    def zero_copy(e):
        start = pl.multiple_of(pad_ref[e] * SLAB, SLAB)
        return pltpu.make_async_copy(zero_sc, xs_hbm.at[pl.ds(start, EXPERT_ROWS * SLAB)], sem_zero)

    @pl.when(i == 0)
    def _():
        zero_sc[...] = jnp.zeros_like(zero_sc)

        def start(e, c):
            @pl.when(pad_ref[N_EXPERTS + e] > 0)
            def _():
                zero_copy(e).start()
            return c

        def wait(e, c):
            @pl.when(pad_ref[N_EXPERTS + e] > 0)
            def _():
                zero_copy(e).wait()
            return c

        lax.fori_loop(0, N_EXPERTS, start, 0)
        lax.fori_loop(0, N_EXPERTS, wait, 0)

    slot = lax.rem(i, 2)
    idx_copy = pltpu.make_async_copy(dest_hbm.at[i], dest_sm, sem_idx)
    idx_copy.start()

    def row_copy(s, t, dst_row):
        src = pl.multiple_of(t * SLAB, SLAB)
        dst = pl.multiple_of(dst_row * SLAB, SLAB)
        return pltpu.make_async_copy(slab_sc.at[s, pl.ds(src, SLAB)], xs_hbm.at[pl.ds(dst, SLAB)], sem_row.at[s])

    def wait_rows(s):
        def wait(t, c):
            for k in range(TOP_K):
                row_copy(s, 0, 0).wait()
            return c
        lax.fori_loop(0, tt, wait, 0, unroll=2)

    @pl.when(i >= 2)
    def _():
        wait_rows(slot)

    _slab_store(slab_sc.at[slot], tt, _pack_bf16_pairs(x_ref[...]))
    idx_copy.wait()

    def start(t, c):
        for k in range(TOP_K):
            row_copy(slot, t, dest_sm[t * TOP_K + k]).start(priority=k % 2)
        return c

    lax.fori_loop(0, tt, start, 0, unroll=2)

    @pl.when(i == n_tiles - 1)
    def _():
        if n_tiles > 1:
            wait_rows(1 - slot)
        wait_rows(slot)


def _dispatch(pad_info, dest, x2, p_rows):
    n, d = x2.shape
    tt = 512
    assert TOP_K == 8
    assert d == 2 * SLAB * LANES
    dest2 = dest.reshape(n // tt, tt * TOP_K)
    grid_spec = pltpu.PrefetchScalarGridSpec(
        num_scalar_prefetch=1,
        grid=(n // tt,),
        in_specs=[pl.BlockSpec(memory_space=pl.ANY),
                  pl.BlockSpec((tt, d), lambda i, pad: (i, 0))],
        out_specs=pl.BlockSpec(memory_space=pl.ANY),
        scratch_shapes=[pltpu.SMEM((tt * TOP_K,), I32),
                        pltpu.VMEM((2, tt * SLAB, LANES), U32),
                        pltpu.VMEM((EXPERT_ROWS * SLAB, LANES), U32),
                        pltpu.SemaphoreType.DMA, pltpu.SemaphoreType.DMA, pltpu.SemaphoreType.DMA((2,))],
    )
    return pl.pallas_call(
        functools.partial(_dispatch_kernel, tt=tt, n_tiles=n // tt),
        out_shape=jax.ShapeDtypeStruct((p_rows * SLAB, LANES), U32),
        grid_spec=grid_spec,
        compiler_params=_cparams(("arbitrary",)),
        name="moe_dispatch",
    )(pad_info, dest2, x2)


def _expert_mlp_kernel(be_ref, nu_ref, xs_ref, wg_ref, wu_ref, wd_ref, ys_ref, wg_sc, wu_sc, wd_sc):
    b = pl.program_id(0)
    e = be_ref[b]
    first = jnp.logical_or(b == 0, e != be_ref[jnp.maximum(b - 1, 0)])
    active = b < nu_ref[0]

    @pl.when(jnp.logical_and(active, first))
    def _():
        wg_sc[...] = wg_ref[...].astype(BF16)
        wu_sc[...] = wu_ref[...].astype(BF16)
        wd_sc[...] = wd_ref[...].astype(BF16)

    @pl.when(active)
    def _():
        r = EXPERT_ROWS
        half = wg_sc.shape[0] // 2
        x_lo, x_hi = _unpack_bf16_pairs(_slab_load(xs_ref, r))
        g = _dot(x_lo, wg_sc[:half]) + _dot(x_hi, wg_sc[half:])
        u = _dot(x_lo, wu_sc[:half]) + _dot(x_hi, wu_sc[half:])
        h = (g * jax.nn.sigmoid(g) * u).astype(BF16)
        _slab_store(ys_ref, r, _pack_bf16_pairs(_dot(h, wd_sc[...])))


def _expert_mlp(blk_e, n_used, xs, w_gate, w_up, w_down, layer):
    r = EXPERT_ROWS
    nblk = xs.shape[0] // (r * SLAB)
    d, de = w_gate.shape[2], w_gate.shape[3]

    def row_map(b, be, nu):
        return (jnp.minimum(b, nu[0] - 1), 0)

    def w_map(b, be, nu):
        return (layer, be[b], 0, 0)

    grid_spec = pltpu.PrefetchScalarGridSpec(
        num_scalar_prefetch=2,
        grid=(nblk,),
        in_specs=[pl.BlockSpec((r * SLAB, LANES), row_map),
                  pl.BlockSpec((None, None, d, de), w_map),
                  pl.BlockSpec((None, None, d, de), w_map),
                  pl.BlockSpec((None, None, de, d), w_map)],
        out_specs=pl.BlockSpec((r * SLAB, LANES), row_map),
        scratch_shapes=[pltpu.VMEM((d, de), BF16), pltpu.VMEM((d, de), BF16), pltpu.VMEM((de, d), BF16)],
    )
    return pl.pallas_call(
        _expert_mlp_kernel,
        out_shape=jax.ShapeDtypeStruct(xs.shape, U32),
        grid_spec=grid_spec,
        compiler_params=_cparams(("arbitrary",)),
        name="moe_experts",
    )(blk_e, n_used, xs, w_gate, w_up, w_down)


def _combine_kernel(dest_hbm, ys_hbm, x_ref, gate_ref, sg_ref, su_ref, sd_ref, g_ref, b_ref,
                    y_ref, dest_sm, rows_sc, sem_idx, sem_row, *, tt, n_tiles):
    i = pl.program_id(0)
    slot = lax.rem(i, 2)
    n_rows = tt * TOP_K

    def idx_copy(tile, s):
        return pltpu.make_async_copy(dest_hbm.at[tile], dest_sm.at[pl.ds(pl.multiple_of(s * n_rows, n_rows), n_rows)],
                                     sem_idx.at[s])

    def row_copy(s, t, k, src_row):
        return pltpu.make_async_copy(ys_hbm.at[pl.ds(pl.multiple_of(src_row * SLAB, SLAB), SLAB)],
                                     rows_sc.at[s, k, pl.ds(pl.multiple_of(t * SLAB, SLAB), SLAB)], sem_row.at[s])

    def start_rows(s):
        def start(t, c):
            for k in range(TOP_K):
                row_copy(s, t, k, dest_sm[s * n_rows + t * TOP_K + k]).start(priority=k % 2)
            return c
        lax.fori_loop(0, tt, start, 0, unroll=2)

    @pl.when(i == 0)
    def _():
        idx_copy(0, 0).start()
        idx_copy(0, 0).wait()
        start_rows(0)
        if n_tiles > 1:
            idx_copy(1, 1).start()

    @pl.when(i + 1 < n_tiles)
    def _():
        idx_copy(i + 1, 1 - slot).wait()
        start_rows(1 - slot)

    @pl.when(i + 2 < n_tiles)
    def _():
        idx_copy(i + 2, slot).start()

    x = x_ref[...]
    xb = x.astype(BF16)
    g = _dot(xb, sg_ref[...])
    u = _dot(xb, su_ref[...])
    h = (g * jax.nn.sigmoid(g) * u).astype(BF16)
    z = DEEPNORM_ALPHA * x + _dot(h, sd_ref[...])

    def wait(t, c):
        for k in range(TOP_K):
            row_copy(slot, 0, 0, 0).wait()
        return c
    lax.fori_loop(0, tt, wait, 0, unroll=2)

    gate = gate_ref[...]
    lo_pieces, hi_pieces = [], []
    for c in range(SLAB):
        acc_lo = jnp.zeros((tt, LANES), F32)
        acc_hi = jnp.zeros((tt, LANES), F32)
        for k in range(TOP_K):
            words = rows_sc[slot, k, pl.ds(c, tt, stride=SLAB), :]
            gk = gate[:, k:k + 1]
            acc_lo = acc_lo + gk * lax.bitcast_convert_type(lax.shift_left(words, jnp.uint32(16)), F32)
            acc_hi = acc_hi + gk * lax.bitcast_convert_type(words & jnp.uint32(0xFFFF0000), F32)
        lo_pieces.append(acc_lo)
        hi_pieces.append(acc_hi)
    routed = jnp.concatenate(lo_pieces + hi_pieces, axis=1)
    y_ref[...] = _layer_norm_rows(z + routed, g_ref[...], b_ref[...])


def _combine(dest, ys, x2, gate, sg_b, su_b, sd_b, ln_g, ln_b):
    n, d = x2.shape
    ds_ = sg_b.shape[1]
    tt = 256
    assert TOP_K == 8
    n_tiles = n // tt
    dest2 = dest.reshape(n_tiles, tt * TOP_K)
    return pl.pallas_call(
        functools.partial(_combine_kernel, tt=tt, n_tiles=n_tiles),
        out_shape=jax.ShapeDtypeStruct((n, d), F32),
        grid=(n_tiles,),
        in_specs=[pl.BlockSpec(memory_space=pl.ANY),
                  pl.BlockSpec(memory_space=pl.ANY),
                  pl.BlockSpec((tt, d), lambda i: (i, 0)),
                  pl.BlockSpec((tt, TOP_K), lambda i: (i, 0)),
                  pl.BlockSpec((d, ds_), lambda i: (0, 0)),
                  pl.BlockSpec((d, ds_), lambda i: (0, 0)),
                  pl.BlockSpec((ds_, d), lambda i: (0, 0)),
                  pl.BlockSpec((1, d), lambda i: (0, 0)),
                  pl.BlockSpec((1, d), lambda i: (0, 0))],
        out_specs=pl.BlockSpec((tt, d), lambda i: (i, 0)),
        scratch_shapes=[pltpu.SMEM((2 * tt * TOP_K,), I32),
                        pltpu.VMEM((2, TOP_K, tt * SLAB, LANES), U32),
                        pltpu.SemaphoreType.DMA((2,)), pltpu.SemaphoreType.DMA((2,))],
        compiler_params=_cparams(("arbitrary",)),
        name="moe_combine",
    )(dest2, ys, x2, gate, sg_b, su_b, sd_b, ln_g.reshape(1, d), ln_b.reshape(1, d))


def _moe_layer(x2, rw, rb, w_gate, w_up, w_down, layer, sh_gate, sh_up, sh_down, ln_g, ln_b):
    n, d = x2.shape
    e = N_EXPERTS
    r = EXPERT_ROWS
    sel, gate, rank, counts = _router(x2, rw, rb)
    counts = counts.reshape(e)
    padded = (counts + r - 1) // r * r
    pend = jnp.cumsum(padded)
    pstart = pend - padded
    dest = rank + jnp.sum(jnp.where(sel[..., None] == jnp.arange(e, dtype=I32), pstart.astype(I32), 0), axis=-1)
    p_rows = n * TOP_K + e * r
    nblk = p_rows // r
    blk_start = jnp.arange(nblk, dtype=I32) * r
    blk_e = jnp.minimum(jnp.sum((pend[None, :] <= blk_start[:, None]).astype(I32), axis=1), e - 1)
    n_used = (pend[-1] // r).astype(I32).reshape(1)
    pad_info = jnp.concatenate([jnp.maximum(pend - r, 0), padded]).astype(I32)
    xs = _dispatch(pad_info, dest.astype(I32), x2, p_rows)
    ys = _expert_mlp(blk_e, n_used, xs, w_gate, w_up, w_down, layer)
    return _combine(dest.astype(I32), ys, x2, gate, sh_gate.astype(BF16), sh_up.astype(BF16),
                    sh_down.astype(BF16), ln_g, ln_b)


FAR_BUCKET = REL_BUCKETS // 2 - 1


def _n_near_offsets(tq, tk):
    return (tk + REL_MAX_DIST - 1 + tq - 1) // tq


def _t5_bucket(rel):
    half = REL_BUCKETS // 2
    max_exact = half // 2
    n = jnp.abs(rel)
    large = max_exact + (jnp.log(jnp.maximum(n, 1).astype(F32) / max_exact)
                         / math.log(REL_MAX_DIST / max_exact) * (half - max_exact)).astype(I32)
    large = jnp.minimum(large, half - 1)
    return jnp.where(rel > 0, half, 0) + jnp.where(n < max_exact, n, large)


def _bias_tiles_kernel(rb_ref, out_ref, *, tq, tk):
    offset = -tq * pl.program_id(0)
    key = lax.broadcasted_iota(I32, (tk, tq), 0)
    qry = lax.broadcasted_iota(I32, (tk, tq), 1)
    bucket = _t5_bucket(offset + key - qry)
    for h in range(B_HEADS):
        far = rb_ref[FAR_BUCKET * B_HEADS + h]
        acc = jnp.zeros((tk, tq), F32)
        for b in range(REL_BUCKETS):
            acc = jnp.where(bucket == b, (rb_ref[b * B_HEADS + h] - far) * LOG2E, acc)
        out_ref[0, h] = acc


def _bias_tiles(rel_bias, tq, tk):
    n_off = _n_near_offsets(tq, tk)
    grid_spec = pltpu.PrefetchScalarGridSpec(
        num_scalar_prefetch=1,
        grid=(n_off,),
        in_specs=[],
        out_specs=pl.BlockSpec((1, B_HEADS, tk, tq), lambda i, rb: (i, 0, 0, 0)),
    )
    return pl.pallas_call(
        functools.partial(_bias_tiles_kernel, tq=tq, tk=tk),
        out_shape=jax.ShapeDtypeStruct((n_off, B_HEADS, tk, tq), F32),
        grid_spec=grid_spec,
        compiler_params=_cparams(("arbitrary",)),
        name="dsa_bias_tiles",
    )(rel_bias.reshape(-1))


def _proj1_kernel(x_ref, w_ref, wit_ref, qn_ref, kvn_ref, cq_ref, ckv_ref, ckvt_ref, ki_ref, wi_ref,
                  *, o1, o2, o3, wscale):
    x = x_ref[...]
    proj = _dot(x.astype(BF16), w_ref[...])
    cq_ref[...] = _rms_norm_rows(proj[:, :o1], qn_ref[...]).astype(BF16)
    ckv = _rms_norm_rows(proj[:, o1:o2], kvn_ref[...])
    ckv_ref[...] = ckv.astype(BF16)
    ckvt_ref[...] = ckv.T.astype(BF16)
    ki_ref[...] = proj[:, o2:o3].astype(BF16)
    wi_ref[...] = _dot_nt(wit_ref[...], x) * wscale


def _proj1(x2, w_in, q_norm, kv_norm, ql, kvl, batch, seq):
    n, d = x2.shape
    o1, o2, o3 = ql, ql + kvl, ql + kvl + IDX_DIM
    tm = 512
    nt = seq // tm
    w_main = w_in[:, :o3].astype(BF16)
    w_idx_t = w_in[:, o3:].T
    kern = functools.partial(_proj1_kernel, o1=o1, o2=o2, o3=o3,
                             wscale=(IDX_HEADS ** -0.5) * (IDX_DIM ** -0.5))
    return pl.pallas_call(
        kern,
        out_shape=(jax.ShapeDtypeStruct((n, ql), BF16), jax.ShapeDtypeStruct((n, kvl), BF16),
                   jax.ShapeDtypeStruct((batch * kvl, seq), BF16),
                   jax.ShapeDtypeStruct((n, IDX_DIM), BF16), jax.ShapeDtypeStruct((IDX_HEADS, n), F32)),
        grid=(n // tm,),
        in_specs=[pl.BlockSpec((tm, d), lambda i: (i, 0)),
                  pl.BlockSpec((d, o3), lambda i: (0, 0)),
                  pl.BlockSpec((IDX_HEADS, d), lambda i: (0, 0)),
                  pl.BlockSpec((1, ql), lambda i: (0, 0)),
                  pl.BlockSpec((1, kvl), lambda i: (0, 0))],
        out_specs=(pl.BlockSpec((tm, ql), lambda i: (i, 0)), pl.BlockSpec((tm, kvl), lambda i: (i, 0)),
                   pl.BlockSpec((kvl, tm), lambda i: (i // nt, i % nt)),
                   pl.BlockSpec((tm, IDX_DIM), lambda i: (i, 0)), pl.BlockSpec((IDX_HEADS, tm), lambda i: (0, i))),
        compiler_params=_cparams(("arbitrary",)),
        name="dsa_proj",
    )(x2, w_main, w_idx_t, q_norm.reshape(1, ql), kv_norm.reshape(1, kvl))


def _qside_kernel(cq_ref, wuq_ref, wuk_ref, wiq_ref, ql_ref, qi_ref, *, kvl, scale):
    cq = cq_ref[...]
    q = _dot(cq, wuq_ref[...]).astype(BF16)
    for h in range(B_HEADS):
        qh = q[:, h * B_HEAD_DIM:(h + 1) * B_HEAD_DIM]
        ql_ref[h] = (_dot_nt(wuk_ref[h], qh) * scale).astype(BF16)
    qi_ref[...] = _dot(cq, wiq_ref[...]).astype(BF16)


def _qside(cq, w_uq_b, w_uk_b, w_iq_b):
    n, ql = cq.shape
    kvl = w_uk_b.shape[1]
    tm = 512
    kern = functools.partial(_qside_kernel, kvl=kvl, scale=B_HEAD_DIM ** -0.5 * LOG2E)
    return pl.pallas_call(
        kern,
        out_shape=(jax.ShapeDtypeStruct((B_HEADS, kvl, n), BF16),
                   jax.ShapeDtypeStruct((n, IDX_HEADS * IDX_DIM), BF16)),
        grid=(n // tm,),
        in_specs=[pl.BlockSpec((tm, ql), lambda i: (i, 0)),
                  pl.BlockSpec(w_uq_b.shape, lambda i: (0, 0)),
                  pl.BlockSpec(w_uk_b.shape, lambda i: (0, 0, 0)),
                  pl.BlockSpec(w_iq_b.shape, lambda i: (0, 0))],
        out_specs=(pl.BlockSpec((B_HEADS, kvl, tm), lambda i: (0, 0, i)),
                   pl.BlockSpec((tm, IDX_HEADS * IDX_DIM), lambda i: (i, 0))),
        compiler_params=_cparams(("arbitrary",)),
        name="dsa_qside",
    )(cq, w_uq_b, w_uk_b, w_iq_b)


INT_MIN = -2 ** 31
KEY_NEG_INF = (0xFF800000 ^ 0x7FFFFFFF) - 2 ** 32


def _ordered_key(v):
    bits = lax.bitcast_convert_type(v, I32)
    return bits ^ (lax.shift_right_arithmetic(bits, 31) & 0x7FFFFFFF)


def _indexer_kernel(qi_ref, wi_ref, ki_ref, mask_ref, key_sc, *, tq, tkc, seq, topk):
    i = pl.program_id(1)
    n_chunks = (i * tq + tq + tkc - 1) // tkc
    qpos = i * tq + lax.broadcasted_iota(I32, (1, tq), 1)
    limit = (lax.shift_right_logical(qpos, 6) + 1) * CHUNK
    w = wi_ref[...]
    key_sc[...] = jnp.full(key_sc.shape, KEY_NEG_INF, I32)

    def score_chunk(c, carry):
        start = pl.multiple_of(c * tkc, tkc)
        k = ki_ref[pl.ds(start, tkc), :]
        acc = jnp.zeros((tkc, tq), F32)
        for h in range(IDX_HEADS):
            sc = _dot_nt(k, qi_ref[:, h * IDX_DIM:(h + 1) * IDX_DIM])
            acc = acc + jnp.maximum(sc, 0.0) * w[h:h + 1, :]
        kpos = start + lax.broadcasted_iota(I32, (tkc, tq), 0)
        key_sc[pl.ds(start, tkc), :] = _ordered_key(jnp.where(kpos < limit, acc, -jnp.inf))
        return carry

    lax.fori_loop(0, n_chunks, score_chunk, 0)

    def count(pred_fn):
        def body(c, acc):
            start = pl.multiple_of(c * tkc, tkc)
            hit = pred_fn(key_sc[pl.ds(start, tkc), :]).astype(I32)
            return acc + jnp.sum(hit.reshape(tkc // SLAB, SLAB, tq), axis=0)
        acc = lax.fori_loop(0, n_chunks, body, jnp.zeros((SLAB, tq), I32))
        return jnp.sum(acc, axis=0, keepdims=True)

    def bisect(it, prefix):
        cand = prefix + lax.shift_left(jnp.int32(1), 31 - it)
        cnt = count(lambda kk: kk >= cand)
        return jnp.where(cnt >= topk, cand, prefix)

    thr = lax.fori_loop(0, 32, bisect, jnp.full((1, tq), INT_MIN, I32))
    n_gt = count(lambda kk: kk > thr)
    n_eq = count(lambda kk: kk == thr)
    need = topk - n_gt
    tie_break = jnp.max(jnp.where(jnp.logical_and(thr > KEY_NEG_INF, n_eq > need), 1, 0)) > 0

    mask_ref[...] = jnp.full(mask_ref.shape, NEG_BIG, F32)

    @pl.when(jnp.logical_not(tie_break))
    def _():
        def write(c, carry):
            start = pl.multiple_of(c * tkc, tkc)
            kk = key_sc[pl.ds(start, tkc), :]
            sel = jnp.logical_and(kk >= thr, kk > KEY_NEG_INF)
            mask_ref[pl.ds(start, tkc), :] = jnp.where(sel, 0.0, NEG_BIG)
            return carry
        lax.fori_loop(0, n_chunks, write, 0)

    @pl.when(tie_break)
    def _():
        r_ = lax.broadcasted_iota(I32, (LANES, LANES), 0)
        c_ = lax.broadcasted_iota(I32, (LANES, LANES), 1)
        lower = (c_ < r_).astype(BF16)

        def write(c, seen):
            start = pl.multiple_of(c * LANES, LANES)
            kk = key_sc[pl.ds(start, LANES), :]
            eq = kk == thr
            before = seen + _dot(lower, eq.astype(BF16))
            sel = jnp.logical_or(kk > thr, jnp.logical_and(eq, before < need.astype(F32)))
            sel = jnp.logical_and(sel, kk > KEY_NEG_INF)
            mask_ref[pl.ds(start, LANES), :] = jnp.where(sel, 0.0, NEG_BIG)
            return seen + jnp.sum(eq.astype(F32), axis=0, keepdims=True)
        lax.fori_loop(0, n_chunks * (tkc // LANES), write, jnp.zeros((1, tq), F32))


def _indexer(qidx, widx_t, kidx, batch, seq, topk):
    tq = IDX_TQ
    tkc = min(IDX_TKC, seq)
    nq = seq // tq
    kern = functools.partial(_indexer_kernel, tq=tq, tkc=tkc, seq=seq, topk=topk)
    return pl.pallas_call(
        kern,
        out_shape=jax.ShapeDtypeStruct((batch * seq, seq), F32),
        grid=(batch, nq),
        in_specs=[pl.BlockSpec((tq, IDX_HEADS * IDX_DIM), lambda b, i: (b * nq + i, 0)),
                  pl.BlockSpec((IDX_HEADS, tq), lambda b, i: (0, b * nq + i)),
                  pl.BlockSpec((seq, IDX_DIM), lambda b, i: (b, 0))],
        out_specs=pl.BlockSpec((seq, tq), lambda b, i: (b, i)),
        scratch_shapes=[pltpu.VMEM((seq, tq), I32)],
        compiler_params=_cparams(("arbitrary", "arbitrary")),
        name="dsa_indexer",
    )(qidx, widx_t, kidx)


def _dsa_attn_kernel(qi_ref, kj_ref, qlt_ref, kv_ref, kvt_ref, mask_ref, bias_ref, wuvt_ref, o_ref,
                     m_sc, l_sc, acc_sc, *, tq, tk, n_near):
    p = pl.program_id(1)
    i = qi_ref[p]
    j = kj_ref[p]
    nh = B_HEADS
    behind = (i * tq - j * tk) // tq

    @pl.when(j == 0)
    def _():
        m_sc[...] = jnp.full_like(m_sc, -jnp.inf)
        l_sc[...] = jnp.zeros_like(l_sc)
        acc_sc[...] = jnp.zeros_like(acc_sc)

    def step(near):
        kv = kv_ref[...]
        kvt = kvt_ref[...]
        msk = mask_ref[...]
        for h in range(nh):
            s = _dot(kv, qlt_ref[h]) + msk
            if near:
                s = s + bias_ref[behind, h]
            m_prev = m_sc[h]
            m_new = jnp.maximum(m_prev, jnp.max(s, axis=0, keepdims=True))
            alpha = jnp.exp2(m_prev - m_new)
            pexp = jnp.exp2(s - m_new)
            l_sc[h] = alpha * l_sc[h] + jnp.sum(pexp, axis=0, keepdims=True)
            acc_sc[h] = alpha * acc_sc[h] + _dot(kvt, pexp.astype(BF16))
            m_sc[h] = m_new

    @pl.when(behind >= n_near)
    def _():
        step(False)

    @pl.when(behind < n_near)
    def _():
        step(True)

    @pl.when(j == (i * tq + tq - 1) // tk)
    def _():
        for h in range(nh):
            o_lat_t = (acc_sc[h] / l_sc[h]).astype(BF16)
            o_t = _dot(wuvt_ref[h], o_lat_t)
            o_ref[:, h * B_V_DIM:(h + 1) * B_V_DIM] = o_t.T.astype(o_ref.dtype)


def _dsa_attention(qlt, ckv, ckvt, mask_t, bias_tiles, w_uvt_b, batch, seq):
    tq, tk = DSA_TQ, DSA_TK
    kvl = ckv.shape[1]
    nq = seq // tq
    nk = seq // tk
    last = [(i * tq + tq - 1) // tk for i in range(nq)]
    qi = np.concatenate([np.full(last[i] + 1, i) for i in range(nq)]).astype(np.int32)
    kj = np.concatenate([np.arange(last[i] + 1) for i in range(nq)]).astype(np.int32)
    kern = functools.partial(_dsa_attn_kernel, tq=tq, tk=tk, n_near=bias_tiles.shape[0])
    grid_spec = pltpu.PrefetchScalarGridSpec(
        num_scalar_prefetch=2,
        grid=(batch, len(qi)),
        in_specs=[
            pl.BlockSpec((B_HEADS, kvl, tq), lambda b, p, qi, kj: (0, 0, b * nq + qi[p])),
            pl.BlockSpec((tk, kvl), lambda b, p, qi, kj: (b * nk + kj[p], 0)),
            pl.BlockSpec((kvl, tk), lambda b, p, qi, kj: (b, kj[p])),
            pl.BlockSpec((tk, tq), lambda b, p, qi, kj: (b * nk + kj[p], qi[p])),
            pl.BlockSpec(bias_tiles.shape, lambda b, p, qi, kj: (0, 0, 0, 0)),
            pl.BlockSpec(w_uvt_b.shape, lambda b, p, qi, kj: (0, 0, 0)),
        ],
        out_specs=pl.BlockSpec((tq, B_HEADS * B_V_DIM), lambda b, p, qi, kj: (b * nq + qi[p], 0)),
        scratch_shapes=[pltpu.VMEM((B_HEADS, 1, tq), F32), pltpu.VMEM((B_HEADS, 1, tq), F32),
                        pltpu.VMEM((B_HEADS, kvl, tq), F32)],
    )
    return pl.pallas_call(
        kern,
        out_shape=jax.ShapeDtypeStruct((batch * seq, B_HEADS * B_V_DIM), BF16),
        grid_spec=grid_spec,
        compiler_params=_cparams(("arbitrary", "arbitrary")),
        name="dsa_attention",
    )(jnp.asarray(qi), jnp.asarray(kj), qlt, ckv, ckvt, mask_t, bias_tiles, w_uvt_b)


def _dsa_layer(x2, w_in, q_norm, kv_norm, w_uq, w_iq, w_uk, w_uv, w_out, rel_bias, ln_g, ln_b, batch, seq):
    ql_dim = q_norm.shape[0]
    kvl = kv_norm.shape[0]
    topk = min(IDX_TOPK, seq // 4)
    cq, ckv, ckvt, kidx, widx_t = _proj1(x2, w_in, q_norm, kv_norm, ql_dim, kvl, batch, seq)
    qlt, qidx = _qside(cq, w_uq.astype(BF16), w_uk.astype(BF16), w_iq.astype(BF16))
    mask_t = _indexer(qidx, widx_t, kidx, batch, seq, topk)
    bias_tiles = _bias_tiles(rel_bias, DSA_TQ, DSA_TK)
    w_uvt = jnp.swapaxes(w_uv, 1, 2).astype(BF16)
    o = _dsa_attention(qlt, ckv, ckvt, mask_t, bias_tiles, w_uvt, batch, seq)
    return _outproj_ln(o, w_out.astype(BF16), x2, ln_g, ln_b)


def kernel(x, a_w_in, a_b_f, a_w_out, b_w_in, b_q_norm, b_kv_norm, b_w_uq, b_w_iq, b_w_uk, b_w_uv, b_w_out,
           rel_bias, ln1_g, ln1_b, ln2_g, ln2_b, router_w, router_b, w_gate, w_up, w_down, sh_gate, sh_up,
           sh_down):
    batch, seq, d = x.shape
    x2 = x.reshape(batch * seq, d)
    x2 = _fox_layer(x2, a_w_in[0], a_b_f[0], a_w_out[0], ln1_g[0], ln1_b[0], batch, seq)
    x2 = _moe_layer(x2, router_w[0], router_b[0], w_gate, w_up, w_down, 0, sh_gate[0], sh_up[0],
                    sh_down[0], ln2_g[0], ln2_b[0])
    x2 = _dsa_layer(x2, b_w_in[0], b_q_norm[0], b_kv_norm[0], b_w_uq[0], b_w_iq[0], b_w_uk[0], b_w_uv[0],
                    b_w_out[0], rel_bias, ln1_g[1], ln1_b[1], batch, seq)
    x2 = _moe_layer(x2, router_w[1], router_b[1], w_gate, w_up, w_down, 1, sh_gate[1], sh_up[1],
                    sh_down[1], ln2_g[1], ln2_b[1])
    return x2.reshape(batch, seq, d)
```

```python
import functools
import math

import numpy as np
import jax
import jax.numpy as jnp
from jax import lax
from jax.experimental import pallas as pl
from jax.experimental.pallas import tpu as pltpu

BF16 = jnp.bfloat16
F32 = jnp.float32
I32 = jnp.int32
U32 = jnp.uint32

A_HEADS = 16
A_HEAD_DIM = 128
B_HEADS = 16
B_HEAD_DIM = 128
B_V_DIM = 128
IDX_HEADS = 16
IDX_DIM = 64
IDX_TOPK = 256
CHUNK = 64
REL_BUCKETS = 32
REL_MAX_DIST = 128
N_EXPERTS = 64
TOP_K = 8
ROUTED_SCALE = 2.5
DEPTH = 2
DEEPNORM_ALPHA = (2 * DEPTH) ** 0.25
LN_EPS = 1e-5
RMS_EPS = 1e-6

LANES = 128
SLAB = 8
HALF_BITS = 16
HIGH_HALF_MASK = 0xFFFF0000
VMEM_LIMIT = 56 * 1024 * 1024
LOG2E = math.log2(math.e)
NEG_BIG = -1e30

EXPERT_ROWS = 512
ROW_TILE = 512
PROJ0_TM, PROJ0_TN = 1024, 1024
CUMSUM_T = 256
DISPATCH_TT = 512
COMBINE_TT = 256
FOX_T = 512
FOX_HEADS_PER_STEP = 8
DSA_TQ = 256
DSA_TK = 256
IDX_TQ = 256
IDX_TKC = 512


def _cparams(sem, vmem=VMEM_LIMIT):
    return pltpu.CompilerParams(dimension_semantics=sem, vmem_limit_bytes=vmem)


def _dot(a, b):
    return jnp.dot(a, b, preferred_element_type=F32)


def _dot_nt(a, b):
    return lax.dot_general(a, b, (((1,), (1,)), ((), ())), preferred_element_type=F32)


def _lane_tile(a, width):
    return a if width == LANES else jnp.concatenate([a] * (width // LANES), axis=1)


def _split2(a):
    hi = a.astype(BF16)
    lo = (a - hi.astype(F32)).astype(BF16)
    return hi, lo


def _split3(a):
    hi = a.astype(BF16)
    r = a - hi.astype(F32)
    mid = r.astype(BF16)
    lo = (r - mid.astype(F32)).astype(BF16)
    return hi, mid, lo


def _dot_x3(a, b):
    ah, al = _split2(a)
    bh, bl = _split2(b)
    return _dot(ah, bh) + (_dot(ah, bl) + _dot(al, bh))


def _layer_norm_rows(z, g, b):
    mu = jnp.mean(z, axis=-1, keepdims=True)
    d = z - mu
    var = jnp.mean(d * d, axis=-1, keepdims=True)
    return d * lax.rsqrt(var + LN_EPS) * g + b


def _rms_norm_rows(z, g):
    ms = jnp.mean(z * z, axis=-1, keepdims=True)
    return z * lax.rsqrt(ms + RMS_EPS) * g


def _proj0_kernel(x_ref, w_ref, wf_ref, qkv_ref, fl_ref, xb_sc, *, n_q_blocks, q_scale):
    j = pl.program_id(1)

    @pl.when(j == 0)
    def _():
        x = x_ref[...]
        xb_sc[...] = x.astype(BF16)
        fl_ref[...] = _dot_x3(x, wf_ref[...])

    acc = _dot(xb_sc[...], w_ref[...])
    scale = jnp.where(j < n_q_blocks, q_scale, 1.0).astype(F32)
    qkv_ref[...] = (acc * scale).astype(BF16)


def _proj0(x2, w_qkv_b, w_f):
    n, d = x2.shape
    nout = w_qkv_b.shape[1]
    tm, tn = PROJ0_TM, PROJ0_TN
    dq = A_HEADS * A_HEAD_DIM
    kern = functools.partial(_proj0_kernel, n_q_blocks=dq // tn, q_scale=A_HEAD_DIM ** -0.5 * LOG2E)
    return pl.pallas_call(
        kern,
        out_shape=(jax.ShapeDtypeStruct((n, nout), BF16),
                   jax.ShapeDtypeStruct((n, A_HEADS), F32)),
        grid=(n // tm, nout // tn),
        in_specs=[pl.BlockSpec((tm, d), lambda i, j: (i, 0)),
                  pl.BlockSpec((d, tn), lambda i, j: (0, j)),
                  pl.BlockSpec((d, A_HEADS), lambda i, j: (0, 0))],
        out_specs=(pl.BlockSpec((tm, tn), lambda i, j: (i, j)),
                   pl.BlockSpec((tm, A_HEADS), lambda i, j: (i, 0))),
        scratch_shapes=[pltpu.VMEM((tm, d), BF16)],
        compiler_params=_cparams(("arbitrary", "arbitrary")),
        name="fox_proj",
    )(x2, w_qkv_b, w_f)


def _forget_cumsum_kernel(fl_ref, bf_ref, f_ref, carry_sc, *, t):
    @pl.when(pl.program_id(1) == 0)
    def _():
        carry_sc[...] = jnp.zeros_like(carry_sc)

    z = fl_ref[...] + bf_ref[...]
    logf = jnp.minimum(z, 0.0) - jnp.log1p(jnp.exp(-jnp.abs(z)))
    row = lax.broadcasted_iota(I32, (t, t), 0)
    col = lax.broadcasted_iota(I32, (t, t), 1)
    tri = (col <= row).astype(BF16)
    hi, mid, lo = _split3(logf)
    cs = _dot(tri, hi) + (_dot(tri, mid) + _dot(tri, lo)) + carry_sc[...]
    f_ref[...] = cs * LOG2E
    carry_sc[...] = cs[t - 1:t, :]


def _forget_cumsum(fl, b_f, batch, seq):
    t = CUMSUM_T
    nb = seq // t
    return pl.pallas_call(
        functools.partial(_forget_cumsum_kernel, t=t),
        out_shape=jax.ShapeDtypeStruct(fl.shape, F32),
        grid=(batch, nb),
        in_specs=[pl.BlockSpec((t, A_HEADS), lambda b, i: (b * nb + i, 0)),
                  pl.BlockSpec((1, A_HEADS), lambda b, i: (0, 0))],
        out_specs=pl.BlockSpec((t, A_HEADS), lambda b, i: (b * nb + i, 0)),
        scratch_shapes=[pltpu.VMEM((1, A_HEADS), F32)],
        compiler_params=_cparams(("arbitrary", "arbitrary")),
        name="fox_forget_cumsum",
    )(fl, b_f.reshape(1, A_HEADS))


def _fox_attn_kernel(qi_ref, kj_ref, q_ref, k_ref, v_ref, fk_ref, o_ref, m_sc, l_sc, acc_sc, *, tq, tk, hp):
    p = pl.program_id(2)
    i = qi_ref[p]
    j = kj_ref[p]
    dh = A_HEAD_DIM

    @pl.when(j == 0)
    def _():
        m_sc[...] = jnp.full_like(m_sc, -jnp.inf)
        l_sc[...] = jnp.zeros_like(l_sc)
        acc_sc[...] = jnp.zeros_like(acc_sc)

    def step(diag):
        for hh in range(hp):
            cols = slice(hh * dh, (hh + 1) * dh)
            s = _dot_nt(q_ref[:, cols], k_ref[:, cols]) - fk_ref[hh]
            if diag:
                row = lax.broadcasted_iota(I32, (tq, tk), 0)
                col = lax.broadcasted_iota(I32, (tq, tk), 1)
                s = jnp.where(col <= row, s, -jnp.inf)
            m_prev = m_sc[hh]
            m_new = jnp.maximum(m_prev, jnp.max(s, axis=1, keepdims=True))
            alpha = jnp.exp2(m_prev - m_new)
            pexp = jnp.exp2(s - _lane_tile(m_new, tk))
            l_sc[hh] = alpha * l_sc[hh] + jnp.sum(pexp, axis=1, keepdims=True)
            acc_sc[hh] = alpha * acc_sc[hh] + _dot(pexp.astype(BF16), v_ref[:, cols])
            m_sc[hh] = m_new

    @pl.when(j < i)
    def _():
        step(False)

    @pl.when(j == i)
    def _():
        step(True)
        for hh in range(hp):
            o_ref[:, hh * dh:(hh + 1) * dh] = (acc_sc[hh] / l_sc[hh]).astype(o_ref.dtype)


def _fox_attention(qkv, f_rows, batch, seq):
    t = min(FOX_T, seq)
    nq = seq // t
    qi = np.concatenate([np.full(i + 1, i) for i in range(nq)]).astype(np.int32)
    kj = np.concatenate([np.arange(i + 1) for i in range(nq)]).astype(np.int32)
    hp = FOX_HEADS_PER_STEP
    hg = A_HEADS // hp
    dh = A_HEAD_DIM
    w = hp * dh
    kern = functools.partial(_fox_attn_kernel, tq=t, tk=t, hp=hp)
    grid_spec = pltpu.PrefetchScalarGridSpec(
        num_scalar_prefetch=2,
        grid=(batch, hg, len(qi)),
        in_specs=[
            pl.BlockSpec((t, w), lambda b, h, p, qi, kj: (b * nq + qi[p], h)),
            pl.BlockSpec((t, w), lambda b, h, p, qi, kj: (b * nq + kj[p], hg + h)),
            pl.BlockSpec((t, w), lambda b, h, p, qi, kj: (b * nq + kj[p], 2 * hg + h)),
            pl.BlockSpec((hp, 1, t), lambda b, h, p, qi, kj: (b * hg + h, 0, kj[p])),
        ],
        out_specs=pl.BlockSpec((t, w), lambda b, h, p, qi, kj: (b * nq + qi[p], h)),
        scratch_shapes=[pltpu.VMEM((hp, t, dh), F32), pltpu.VMEM((hp, t, dh), F32),
                        pltpu.VMEM((hp, t, dh), F32)],
    )
    return pl.pallas_call(
        kern,
        out_shape=jax.ShapeDtypeStruct((batch * seq, A_HEADS * dh), BF16),
        grid_spec=grid_spec,
        compiler_params=_cparams(("arbitrary", "arbitrary", "arbitrary")),
        name="fox_attention",
    )(jnp.asarray(qi), jnp.asarray(kj), qkv, qkv, qkv, f_rows)


def _outproj_ln_kernel(o_ref, w_ref, x_ref, g_ref, b_ref, y_ref):
    z = DEEPNORM_ALPHA * x_ref[...] + _dot(o_ref[...], w_ref[...])
    y_ref[...] = _layer_norm_rows(z, g_ref[...], b_ref[...])


def _outproj_ln(o, w_b, x2, g, b):
    n, d = x2.shape
    k = o.shape[1]
    tm = ROW_TILE
    return pl.pallas_call(
        _outproj_ln_kernel,
        out_shape=jax.ShapeDtypeStruct((n, d), F32),
        grid=(n // tm,),
        in_specs=[pl.BlockSpec((tm, k), lambda i: (i, 0)),
                  pl.BlockSpec((k, d), lambda i: (0, 0)),
                  pl.BlockSpec((tm, d), lambda i: (i, 0)),
                  pl.BlockSpec((1, d), lambda i: (0, 0)),
                  pl.BlockSpec((1, d), lambda i: (0, 0))],
        out_specs=pl.BlockSpec((tm, d), lambda i: (i, 0)),
        compiler_params=_cparams(("arbitrary",)),
        name="outproj_deepnorm",
    )(o, w_b, x2, g.reshape(1, d), b.reshape(1, d))


def _fox_layer(x2, w_in, b_f, w_out, ln_g, ln_b, batch, seq):
    dq = A_HEADS * A_HEAD_DIM
    qkv, fl = _proj0(x2, w_in[:, :3 * dq].astype(BF16), w_in[:, 3 * dq:])
    f = _forget_cumsum(fl, b_f, batch, seq)
    f_rows = f.reshape(batch, seq, A_HEADS).transpose(0, 2, 1).reshape(batch * A_HEADS, 1, seq)
    o = _fox_attention(qkv, f_rows, batch, seq)
    return _outproj_ln(o, w_out.astype(BF16), x2, ln_g, ln_b)


def _router_kernel(x_ref, rw_ref, rb_ref, sel_ref, gate_ref, rank_ref, cnt_ref, carry_sc, *, tm):
    @pl.when(pl.program_id(0) == 0)
    def _():
        carry_sc[...] = jnp.zeros_like(carry_sc)

    e = N_EXPERTS
    scores = jax.nn.sigmoid(_dot_x3(x_ref[...], rw_ref[...]))
    lane = lax.broadcasted_iota(I32, (tm, e), 1)
    slot = lax.broadcasted_iota(I32, (tm, TOP_K), 1)
    work = scores + rb_ref[...]
    chosen = jnp.zeros((tm, e), F32)
    sel = jnp.zeros((tm, TOP_K), I32)
    gate = jnp.zeros((tm, TOP_K), F32)
    idxs = []
    for k in range(TOP_K):
        mx = jnp.max(work, axis=1, keepdims=True)
        idx = jnp.min(jnp.where(work == mx, lane, e), axis=1, keepdims=True)
        hit = lane == idx
        gk = jnp.sum(jnp.where(hit, scores, 0.0), axis=1, keepdims=True)
        work = jnp.where(hit, -jnp.inf, work)
        chosen = jnp.where(hit, 1.0, chosen)
        sel = jnp.where(slot == k, idx, sel)
        gate = jnp.where(slot == k, gk, gate)
        idxs.append(idx)
    gate = gate / jnp.sum(gate, axis=1, keepdims=True) * ROUTED_SCALE

    row = lax.broadcasted_iota(I32, (tm, tm), 0)
    col = lax.broadcasted_iota(I32, (tm, tm), 1)
    before = _dot((col < row).astype(BF16), chosen.astype(BF16)) + carry_sc[...]
    rank = jnp.zeros((tm, TOP_K), F32)
    for k in range(TOP_K):
        rk = jnp.sum(jnp.where(lane == idxs[k], before, 0.0), axis=1, keepdims=True)
        rank = jnp.where(slot == k, rk, rank)
    total = carry_sc[...] + jnp.sum(chosen, axis=0, keepdims=True)
    carry_sc[...] = total
    sel_ref[...] = sel
    gate_ref[...] = gate
    rank_ref[...] = rank.astype(I32)
    cnt_ref[...] = total.astype(I32)


def _router(x2, rw, rb):
    n, d = x2.shape
    tm = ROW_TILE
    e = N_EXPERTS
    return pl.pallas_call(
        functools.partial(_router_kernel, tm=tm),
        out_shape=(jax.ShapeDtypeStruct((n, TOP_K), I32),
                   jax.ShapeDtypeStruct((n, TOP_K), F32),
                   jax.ShapeDtypeStruct((n, TOP_K), I32),
                   jax.ShapeDtypeStruct((1, e), I32)),
        grid=(n // tm,),
        in_specs=[pl.BlockSpec((tm, d), lambda i: (i, 0)),
                  pl.BlockSpec((d, e), lambda i: (0, 0)),
                  pl.BlockSpec((1, e), lambda i: (0, 0))],
        out_specs=(pl.BlockSpec((tm, TOP_K), lambda i: (i, 0)),
                   pl.BlockSpec((tm, TOP_K), lambda i: (i, 0)),
                   pl.BlockSpec((tm, TOP_K), lambda i: (i, 0)),
                   pl.BlockSpec((1, e), lambda i: (0, 0))),
        scratch_shapes=[pltpu.VMEM((1, e), F32)],
        compiler_params=_cparams(("arbitrary",)),
        name="moe_router",
    )(x2, rw, rb.reshape(1, e))


def _pack_bf16_pairs(y):
    w = y.shape[1] // 2
    lo = lax.bitcast_convert_type(y[:, :w].astype(BF16).astype(F32), U32)
    hi = lax.bitcast_convert_type(y[:, w:].astype(BF16).astype(F32), U32)
    return lax.shift_right_logical(lo, jnp.uint32(HALF_BITS)) | hi


def _low_half_f32(words):
    return lax.bitcast_convert_type(lax.shift_left(words, jnp.uint32(HALF_BITS)), F32)


def _high_half_f32(words):
    return lax.bitcast_convert_type(words & jnp.uint32(HIGH_HALF_MASK), F32)


def _unpack_bf16_pairs(words):
    return _low_half_f32(words).astype(BF16), _high_half_f32(words).astype(BF16)


def _slab_load(ref, rows):
    return jnp.concatenate([ref[pl.ds(c, rows, stride=SLAB), :] for c in range(SLAB)], axis=1)


def _slab_store(ref, rows, val):
    for c in range(SLAB):
        ref[pl.ds(c, rows, stride=SLAB), :] = val[:, c * LANES:(c + 1) * LANES]


def _dispatch_kernel(pad_ref, dest_hbm, x_ref, xs_hbm, dest_sm, slab_sc, zero_sc, sem_idx, sem_zero, sem_row,
                     *, tt, n_tiles):
    i = pl.program_id(0)

    def zero_copy(e):
        start = pl.multiple_of(pad_ref[e] * SLAB, SLAB)
        return pltpu.make_async_copy(zero_sc, xs_hbm.at[pl.ds(start, EXPERT_ROWS * SLAB)], sem_zero)

    @pl.when(i == 0)
    def _():
        zero_sc[...] = jnp.zeros_like(zero_sc)

        def start(e, c):
            @pl.when(pad_ref[N_EXPERTS + e] > 0)
            def _():
                zero_copy(e).start()
            return c

        def wait(e, c):
            @pl.when(pad_ref[N_EXPERTS + e] > 0)
            def _():
                zero_copy(e).wait()
            return c

        lax.fori_loop(0, N_EXPERTS, start, 0)
        lax.fori_loop(0, N_EXPERTS, wait, 0)

    slot = lax.rem(i, 2)
    idx_copy = pltpu.make_async_copy(dest_hbm.at[i], dest_sm, sem_idx)
    idx_copy.start()

    def row_copy(s, t, dst_row):
        src = pl.multiple_of(t * SLAB, SLAB)
        dst = pl.multiple_of(dst_row * SLAB, SLAB)
        return pltpu.make_async_copy(slab_sc.at[s, pl.ds(src, SLAB)], xs_hbm.at[pl.ds(dst, SLAB)], sem_row.at[s])

    def wait_rows(s):
        def wait(t, c):
            for k in range(TOP_K):
                row_copy(s, 0, 0).wait()
            return c
        lax.fori_loop(0, tt, wait, 0, unroll=2)

    @pl.when(i >= 2)
    def _():
        wait_rows(slot)

    _slab_store(slab_sc.at[slot], tt, _pack_bf16_pairs(x_ref[...]))
    idx_copy.wait()

    def start(t, c):
        for k in range(TOP_K):
            row_copy(slot, t, dest_sm[t * TOP_K + k]).start(priority=k % 2)
        return c

    lax.fori_loop(0, tt, start, 0, unroll=2)

    @pl.when(i == n_tiles - 1)
    def _():
        if n_tiles > 1:
            wait_rows(1 - slot)
        wait_rows(slot)


def _dispatch(pad_info, dest, x2, p_rows):
    n, d = x2.shape
    tt = DISPATCH_TT
    assert d == 2 * SLAB * LANES
    dest2 = dest.reshape(n // tt, tt * TOP_K)
    grid_spec = pltpu.PrefetchScalarGridSpec(
        num_scalar_prefetch=1,
        grid=(n // tt,),
        in_specs=[pl.BlockSpec(memory_space=pl.ANY),
                  pl.BlockSpec((tt, d), lambda i, pad: (i, 0))],
        out_specs=pl.BlockSpec(memory_space=pl.ANY),
        scratch_shapes=[pltpu.SMEM((tt * TOP_K,), I32),
                        pltpu.VMEM((2, tt * SLAB, LANES), U32),
                        pltpu.VMEM((EXPERT_ROWS * SLAB, LANES), U32),
                        pltpu.SemaphoreType.DMA, pltpu.SemaphoreType.DMA, pltpu.SemaphoreType.DMA((2,))],
    )
    return pl.pallas_call(
        functools.partial(_dispatch_kernel, tt=tt, n_tiles=n // tt),
        out_shape=jax.ShapeDtypeStruct((p_rows * SLAB, LANES), U32),
        grid_spec=grid_spec,
        compiler_params=_cparams(("arbitrary",)),
        name="moe_dispatch",
    )(pad_info, dest2, x2)


def _expert_mlp_kernel(be_ref, nu_ref, xs_ref, wg_ref, wu_ref, wd_ref, ys_ref, wg_sc, wu_sc, wd_sc):
    b = pl.program_id(0)
    e = be_ref[b]
    first = jnp.logical_or(b == 0, e != be_ref[jnp.maximum(b - 1, 0)])
    active = b < nu_ref[0]

    @pl.when(jnp.logical_and(active, first))
    def _():
        wg_sc[...] = wg_ref[...].astype(BF16)
        wu_sc[...] = wu_ref[...].astype(BF16)
        wd_sc[...] = wd_ref[...].astype(BF16)

    @pl.when(active)
    def _():
        r = EXPERT_ROWS
        x = jnp.concatenate(_unpack_bf16_pairs(_slab_load(xs_ref, r)), axis=1)
        g = _dot(x, wg_sc[...])
        u = _dot(x, wu_sc[...])
        h = (g * jax.nn.sigmoid(g) * u).astype(BF16)
        _slab_store(ys_ref, r, _pack_bf16_pairs(_dot(h, wd_sc[...])))


def _expert_mlp(blk_e, n_used, xs, w_gate, w_up, w_down, layer):
    r = EXPERT_ROWS
    nblk = xs.shape[0] // (r * SLAB)
    d, de = w_gate.shape[2], w_gate.shape[3]

    def row_map(b, be, nu):
        return (jnp.minimum(b, nu[0] - 1), 0)

    def w_map(b, be, nu):
        return (layer, be[b], 0, 0)

    grid_spec = pltpu.PrefetchScalarGridSpec(
        num_scalar_prefetch=2,
        grid=(nblk,),
        in_specs=[pl.BlockSpec((r * SLAB, LANES), row_map),
                  pl.BlockSpec((None, None, d, de), w_map),
                  pl.BlockSpec((None, None, d, de), w_map),
                  pl.BlockSpec((None, None, de, d), w_map)],
        out_specs=pl.BlockSpec((r * SLAB, LANES), row_map),
        scratch_shapes=[pltpu.VMEM((d, de), BF16), pltpu.VMEM((d, de), BF16), pltpu.VMEM((de, d), BF16)],
    )
    return pl.pallas_call(
        _expert_mlp_kernel,
        out_shape=jax.ShapeDtypeStruct(xs.shape, U32),
        grid_spec=grid_spec,
        compiler_params=_cparams(("arbitrary",)),
        name="moe_experts",
    )(blk_e, n_used, xs, w_gate, w_up, w_down)


def _combine_kernel(dest_hbm, ys_hbm, x_ref, gate_ref, sg_ref, su_ref, sd_ref, g_ref, b_ref,
                    y_ref, dest_sm, rows_sc, sem_idx, sem_row, *, tt, n_tiles):
    i = pl.program_id(0)
    slot = lax.rem(i, 2)
    n_rows = tt * TOP_K

    def idx_copy(tile, s):
        return pltpu.make_async_copy(dest_hbm.at[tile], dest_sm.at[pl.ds(pl.multiple_of(s * n_rows, n_rows), n_rows)],
                                     sem_idx.at[s])

    def row_copy(s, t, k, src_row):
        return pltpu.make_async_copy(ys_hbm.at[pl.ds(pl.multiple_of(src_row * SLAB, SLAB), SLAB)],
                                     rows_sc.at[s, k, pl.ds(pl.multiple_of(t * SLAB, SLAB), SLAB)], sem_row.at[s])

    def start_rows(s):
        def start(t, c):
            for k in range(TOP_K):
                row_copy(s, t, k, dest_sm[s * n_rows + t * TOP_K + k]).start(priority=k % 2)
            return c
        lax.fori_loop(0, tt, start, 0, unroll=2)

    @pl.when(i == 0)
    def _():
        idx_copy(0, 0).start()
        idx_copy(0, 0).wait()
        start_rows(0)
        if n_tiles > 1:
            idx_copy(1, 1).start()

    @pl.when(i + 1 < n_tiles)
    def _():
        idx_copy(i + 1, 1 - slot).wait()
        start_rows(1 - slot)

    @pl.when(i + 2 < n_tiles)
    def _():
        idx_copy(i + 2, slot).start()

    x = x_ref[...]
    xb = x.astype(BF16)
    g = _dot(xb, sg_ref[...])
    u = _dot(xb, su_ref[...])
    h = (g * jax.nn.sigmoid(g) * u).astype(BF16)
    z = DEEPNORM_ALPHA * x + _dot(h, sd_ref[...])

    def wait(t, c):
        for k in range(TOP_K):
            row_copy(slot, 0, 0, 0).wait()
        return c
    lax.fori_loop(0, tt, wait, 0, unroll=2)

    gate = gate_ref[...]
    lo_pieces, hi_pieces = [], []
    for c in range(SLAB):
        acc_lo = jnp.zeros((tt, LANES), F32)
        acc_hi = jnp.zeros((tt, LANES), F32)
        for k in range(TOP_K):
            words = rows_sc[slot, k, pl.ds(c, tt, stride=SLAB), :]
            gk = gate[:, k:k + 1]
            acc_lo = acc_lo + gk * _low_half_f32(words)
            acc_hi = acc_hi + gk * _high_half_f32(words)
        lo_pieces.append(acc_lo)
        hi_pieces.append(acc_hi)
    routed = jnp.concatenate(lo_pieces + hi_pieces, axis=1)
    y_ref[...] = _layer_norm_rows(z + routed, g_ref[...], b_ref[...])


def _combine(dest, ys, x2, gate, sg_b, su_b, sd_b, ln_g, ln_b):
    n, d = x2.shape
    ds_ = sg_b.shape[1]
    tt = COMBINE_TT
    n_tiles = n // tt
    dest2 = dest.reshape(n_tiles, tt * TOP_K)
    return pl.pallas_call(
        functools.partial(_combine_kernel, tt=tt, n_tiles=n_tiles),
        out_shape=jax.ShapeDtypeStruct((n, d), F32),
        grid=(n_tiles,),
        in_specs=[pl.BlockSpec(memory_space=pl.ANY),
                  pl.BlockSpec(memory_space=pl.ANY),
                  pl.BlockSpec((tt, d), lambda i: (i, 0)),
                  pl.BlockSpec((tt, TOP_K), lambda i: (i, 0)),
                  pl.BlockSpec((d, ds_), lambda i: (0, 0)),
                  pl.BlockSpec((d, ds_), lambda i: (0, 0)),
                  pl.BlockSpec((ds_, d), lambda i: (0, 0)),
                  pl.BlockSpec((1, d), lambda i: (0, 0)),
                  pl.BlockSpec((1, d), lambda i: (0, 0))],
        out_specs=pl.BlockSpec((tt, d), lambda i: (i, 0)),
        scratch_shapes=[pltpu.SMEM((2 * tt * TOP_K,), I32),
                        pltpu.VMEM((2, TOP_K, tt * SLAB, LANES), U32),
                        pltpu.SemaphoreType.DMA((2,)), pltpu.SemaphoreType.DMA((2,))],
        compiler_params=_cparams(("arbitrary",)),
        name="moe_combine",
    )(dest2, ys, x2, gate, sg_b, su_b, sd_b, ln_g.reshape(1, d), ln_b.reshape(1, d))


def _moe_layer(x2, rw, rb, w_gate, w_up, w_down, layer, sh_gate, sh_up, sh_down, ln_g, ln_b):
    n, d = x2.shape
    e = N_EXPERTS
    r = EXPERT_ROWS
    sel, gate, rank, counts = _router(x2, rw, rb)
    counts = counts.reshape(e)
    padded = (counts + r - 1) // r * r
    pend = jnp.cumsum(padded)
    pstart = pend - padded
    dest = rank + jnp.sum(jnp.where(sel[..., None] == jnp.arange(e, dtype=I32), pstart.astype(I32), 0), axis=-1)
    p_rows = n * TOP_K + e * r
    nblk = p_rows // r
    blk_start = jnp.arange(nblk, dtype=I32) * r
    blk_e = jnp.minimum(jnp.sum((pend[None, :] <= blk_start[:, None]).astype(I32), axis=1), e - 1)
    n_used = (pend[-1] // r).astype(I32).reshape(1)
    pad_info = jnp.concatenate([jnp.maximum(pend - r, 0), padded]).astype(I32)
    xs = _dispatch(pad_info, dest.astype(I32), x2, p_rows)
    ys = _expert_mlp(blk_e, n_used, xs, w_gate, w_up, w_down, layer)
    return _combine(dest.astype(I32), ys, x2, gate, sh_gate.astype(BF16), sh_up.astype(BF16),
                    sh_down.astype(BF16), ln_g, ln_b)


FAR_BUCKET = REL_BUCKETS // 2 - 1


def _n_near_offsets(tq, tk):
    return (tk + REL_MAX_DIST - 1 + tq - 1) // tq


def _t5_bucket(rel):
    half = REL_BUCKETS // 2
    max_exact = half // 2
    n = jnp.abs(rel)
    large = max_exact + (jnp.log(jnp.maximum(n, 1).astype(F32) / max_exact)
                         / math.log(REL_MAX_DIST / max_exact) * (half - max_exact)).astype(I32)
    large = jnp.minimum(large, half - 1)
    return jnp.where(rel > 0, half, 0) + jnp.where(n < max_exact, n, large)


def _bias_tiles_kernel(rb_ref, out_ref, *, tq, tk):
    offset = -tq * pl.program_id(0)
    key = lax.broadcasted_iota(I32, (tk, tq), 0)
    qry = lax.broadcasted_iota(I32, (tk, tq), 1)
    bucket = _t5_bucket(offset + key - qry)
    for h in range(B_HEADS):
        far = rb_ref[FAR_BUCKET * B_HEADS + h]
        acc = jnp.zeros((tk, tq), F32)
        for b in range(REL_BUCKETS):
            acc = jnp.where(bucket == b, (rb_ref[b * B_HEADS + h] - far) * LOG2E, acc)
        out_ref[0, h] = acc


def _bias_tiles(rel_bias, tq, tk):
    n_off = _n_near_offsets(tq, tk)
    grid_spec = pltpu.PrefetchScalarGridSpec(
        num_scalar_prefetch=1,
        grid=(n_off,),
        in_specs=[],
        out_specs=pl.BlockSpec((1, B_HEADS, tk, tq), lambda i, rb: (i, 0, 0, 0)),
    )
    return pl.pallas_call(
        functools.partial(_bias_tiles_kernel, tq=tq, tk=tk),
        out_shape=jax.ShapeDtypeStruct((n_off, B_HEADS, tk, tq), F32),
        grid_spec=grid_spec,
        compiler_params=_cparams(("arbitrary",)),
        name="dsa_bias_tiles",
    )(rel_bias.reshape(-1))


def _proj1_kernel(x_ref, w_ref, wit_ref, qn_ref, kvn_ref, cq_ref, ckv_ref, ckvt_ref, ki_ref, wi_ref,
                  *, o1, o2, o3, wscale):
    x = x_ref[...]
    proj = _dot(x.astype(BF16), w_ref[...])
    cq_ref[...] = _rms_norm_rows(proj[:, :o1], qn_ref[...]).astype(BF16)
    ckv = _rms_norm_rows(proj[:, o1:o2], kvn_ref[...])
    ckv_ref[...] = ckv.astype(BF16)
    ckvt_ref[...] = ckv.T.astype(BF16)
    ki_ref[...] = proj[:, o2:o3].astype(BF16)
    wi_ref[...] = _dot_nt(wit_ref[...], x) * wscale


def _proj1(x2, w_in, q_norm, kv_norm, ql, kvl, batch, seq):
    n, d = x2.shape
    o1, o2, o3 = ql, ql + kvl, ql + kvl + IDX_DIM
    tm = ROW_TILE
    nt = seq // tm
    w_main = w_in[:, :o3].astype(BF16)
    w_idx_t = w_in[:, o3:].T
    kern = functools.partial(_proj1_kernel, o1=o1, o2=o2, o3=o3,
                             wscale=(IDX_HEADS ** -0.5) * (IDX_DIM ** -0.5))
    return pl.pallas_call(
        kern,
        out_shape=(jax.ShapeDtypeStruct((n, ql), BF16), jax.ShapeDtypeStruct((n, kvl), BF16),
                   jax.ShapeDtypeStruct((batch * kvl, seq), BF16),
                   jax.ShapeDtypeStruct((n, IDX_DIM), BF16), jax.ShapeDtypeStruct((IDX_HEADS, n), F32)),
        grid=(n // tm,),
        in_specs=[pl.BlockSpec((tm, d), lambda i: (i, 0)),
                  pl.BlockSpec((d, o3), lambda i: (0, 0)),
                  pl.BlockSpec((IDX_HEADS, d), lambda i: (0, 0)),
                  pl.BlockSpec((1, ql), lambda i: (0, 0)),
                  pl.BlockSpec((1, kvl), lambda i: (0, 0))],
        out_specs=(pl.BlockSpec((tm, ql), lambda i: (i, 0)), pl.BlockSpec((tm, kvl), lambda i: (i, 0)),
                   pl.BlockSpec((kvl, tm), lambda i: (i // nt, i % nt)),
                   pl.BlockSpec((tm, IDX_DIM), lambda i: (i, 0)), pl.BlockSpec((IDX_HEADS, tm), lambda i: (0, i))),
        compiler_params=_cparams(("arbitrary",)),
        name="dsa_proj",
    )(x2, w_main, w_idx_t, q_norm.reshape(1, ql), kv_norm.reshape(1, kvl))


def _qside_kernel(cq_ref, wuq_ref, wuk_ref, wiq_ref, ql_ref, qi_ref, *, scale):
    cq = cq_ref[...]
    q = _dot(cq, wuq_ref[...]).astype(BF16)
    for h in range(B_HEADS):
        qh = q[:, h * B_HEAD_DIM:(h + 1) * B_HEAD_DIM]
        ql_ref[h] = (_dot_nt(wuk_ref[h], qh) * scale).astype(BF16)
    qi_ref[...] = _dot(cq, wiq_ref[...]).astype(BF16)


def _qside(cq, w_uq_b, w_uk_b, w_iq_b):
    n, ql = cq.shape
    kvl = w_uk_b.shape[1]
    tm = ROW_TILE
    kern = functools.partial(_qside_kernel, scale=B_HEAD_DIM ** -0.5 * LOG2E)
    return pl.pallas_call(
        kern,
        out_shape=(jax.ShapeDtypeStruct((B_HEADS, kvl, n), BF16),
                   jax.ShapeDtypeStruct((n, IDX_HEADS * IDX_DIM), BF16)),
        grid=(n // tm,),
        in_specs=[pl.BlockSpec((tm, ql), lambda i: (i, 0)),
                  pl.BlockSpec(w_uq_b.shape, lambda i: (0, 0)),
                  pl.BlockSpec(w_uk_b.shape, lambda i: (0, 0, 0)),
                  pl.BlockSpec(w_iq_b.shape, lambda i: (0, 0))],
        out_specs=(pl.BlockSpec((B_HEADS, kvl, tm), lambda i: (0, 0, i)),
                   pl.BlockSpec((tm, IDX_HEADS * IDX_DIM), lambda i: (i, 0))),
        compiler_params=_cparams(("arbitrary",)),
        name="dsa_qside",
    )(cq, w_uq_b, w_uk_b, w_iq_b)


INT_MIN = -2 ** 31
KEY_NEG_INF = (0xFF800000 ^ 0x7FFFFFFF) - 2 ** 32


def _ordered_key(v):
    bits = lax.bitcast_convert_type(v, I32)
    return bits ^ (lax.shift_right_arithmetic(bits, 31) & 0x7FFFFFFF)


def _indexer_kernel(qi_ref, wi_ref, ki_ref, mask_ref, key_sc, *, tq, tkc, topk):
    i = pl.program_id(1)
    n_chunks = (i * tq + tq + tkc - 1) // tkc
    qpos = i * tq + lax.broadcasted_iota(I32, (1, tq), 1)
    limit = (lax.shift_right_logical(qpos, CHUNK.bit_length() - 1) + 1) * CHUNK
    w = wi_ref[...]
    key_sc[...] = jnp.full(key_sc.shape, KEY_NEG_INF, I32)

    def score_chunk(c, carry):
        start = pl.multiple_of(c * tkc, tkc)
        k = ki_ref[pl.ds(start, tkc), :]
        acc = jnp.zeros((tkc, tq), F32)
        for h in range(IDX_HEADS):
            sc = _dot_nt(k, qi_ref[:, h * IDX_DIM:(h + 1) * IDX_DIM])
            acc = acc + jnp.maximum(sc, 0.0) * w[h:h + 1, :]
        kpos = start + lax.broadcasted_iota(I32, (tkc, tq), 0)
        key_sc[pl.ds(start, tkc), :] = _ordered_key(jnp.where(kpos < limit, acc, -jnp.inf))
        return carry

    lax.fori_loop(0, n_chunks, score_chunk, 0)

    def count(pred_fn):
        def body(c, acc):
            start = pl.multiple_of(c * tkc, tkc)
            hit = pred_fn(key_sc[pl.ds(start, tkc), :]).astype(I32)
            return acc + jnp.sum(hit.reshape(tkc // SLAB, SLAB, tq), axis=0)
        acc = lax.fori_loop(0, n_chunks, body, jnp.zeros((SLAB, tq), I32))
        return jnp.sum(acc, axis=0, keepdims=True)

    def bisect(it, prefix):
        cand = prefix + lax.shift_left(jnp.int32(1), 31 - it)
        cnt = count(lambda kk: kk >= cand)
        return jnp.where(cnt >= topk, cand, prefix)

    thr = lax.fori_loop(0, 32, bisect, jnp.full((1, tq), INT_MIN, I32))
    n_gt = count(lambda kk: kk > thr)
    n_eq = count(lambda kk: kk == thr)
    need = topk - n_gt
    tie_break = jnp.max(jnp.where(jnp.logical_and(thr > KEY_NEG_INF, n_eq > need), 1, 0)) > 0

    mask_ref[...] = jnp.full(mask_ref.shape, NEG_BIG, F32)

    @pl.when(jnp.logical_not(tie_break))
    def _():
        def write(c, carry):
            start = pl.multiple_of(c * tkc, tkc)
            kk = key_sc[pl.ds(start, tkc), :]
            sel = jnp.logical_and(kk >= thr, kk > KEY_NEG_INF)
            mask_ref[pl.ds(start, tkc), :] = jnp.where(sel, 0.0, NEG_BIG)
            return carry
        lax.fori_loop(0, n_chunks, write, 0)

    @pl.when(tie_break)
    def _():
        r_ = lax.broadcasted_iota(I32, (LANES, LANES), 0)
        c_ = lax.broadcasted_iota(I32, (LANES, LANES), 1)
        lower = (c_ < r_).astype(BF16)

        def write(c, seen):
            start = pl.multiple_of(c * LANES, LANES)
            kk = key_sc[pl.ds(start, LANES), :]
            eq = kk == thr
            before = seen + _dot(lower, eq.astype(BF16))
            sel = jnp.logical_or(kk > thr, jnp.logical_and(eq, before < need.astype(F32)))
            sel = jnp.logical_and(sel, kk > KEY_NEG_INF)
            mask_ref[pl.ds(start, LANES), :] = jnp.where(sel, 0.0, NEG_BIG)
            return seen + jnp.sum(eq.astype(F32), axis=0, keepdims=True)
        lax.fori_loop(0, n_chunks * (tkc // LANES), write, jnp.zeros((1, tq), F32))


def _indexer(qidx, widx_t, kidx, batch, seq, topk):
    tq = IDX_TQ
    tkc = min(IDX_TKC, seq)
    nq = seq // tq
    kern = functools.partial(_indexer_kernel, tq=tq, tkc=tkc, topk=topk)
    return pl.pallas_call(
        kern,
        out_shape=jax.ShapeDtypeStruct((batch * seq, seq), F32),
        grid=(batch, nq),
        in_specs=[pl.BlockSpec((tq, IDX_HEADS * IDX_DIM), lambda b, i: (b * nq + i, 0)),
                  pl.BlockSpec((IDX_HEADS, tq), lambda b, i: (0, b * nq + i)),
                  pl.BlockSpec((seq, IDX_DIM), lambda b, i: (b, 0))],
        out_specs=pl.BlockSpec((seq, tq), lambda b, i: (b, i)),
        scratch_shapes=[pltpu.VMEM((seq, tq), I32)],
        compiler_params=_cparams(("arbitrary", "arbitrary")),
        name="dsa_indexer",
    )(qidx, widx_t, kidx)


def _dsa_attn_kernel(qi_ref, kj_ref, qlt_ref, kv_ref, kvt_ref, mask_ref, bias_ref, wuvt_ref, o_ref,
                     m_sc, l_sc, acc_sc, *, tq, tk, n_near):
    p = pl.program_id(1)
    i = qi_ref[p]
    j = kj_ref[p]
    nh = B_HEADS
    behind = (i * tq - j * tk) // tq

    @pl.when(j == 0)
    def _():
        m_sc[...] = jnp.full_like(m_sc, -jnp.inf)
        l_sc[...] = jnp.zeros_like(l_sc)
        acc_sc[...] = jnp.zeros_like(acc_sc)

    def step(near):
        kv = kv_ref[...]
        kvt = kvt_ref[...]
        msk = mask_ref[...]
        for h in range(nh):
            s = _dot(kv, qlt_ref[h]) + msk
            if near:
                s = s + bias_ref[behind, h]
            m_prev = m_sc[h]
            m_new = jnp.maximum(m_prev, jnp.max(s, axis=0, keepdims=True))
            alpha = jnp.exp2(m_prev - m_new)
            pexp = jnp.exp2(s - m_new)
            l_sc[h] = alpha * l_sc[h] + jnp.sum(pexp, axis=0, keepdims=True)
            acc_sc[h] = alpha * acc_sc[h] + _dot(kvt, pexp.astype(BF16))
            m_sc[h] = m_new

    @pl.when(behind >= n_near)
    def _():
        step(False)

    @pl.when(behind < n_near)
    def _():
        step(True)

    @pl.when(j == (i * tq + tq - 1) // tk)
    def _():
        for h in range(nh):
            o_lat_t = (acc_sc[h] / l_sc[h]).astype(BF16)
            o_t = _dot(wuvt_ref[h], o_lat_t)
            o_ref[:, h * B_V_DIM:(h + 1) * B_V_DIM] = o_t.T.astype(o_ref.dtype)


def _dsa_attention(qlt, ckv, ckvt, mask_t, bias_tiles, w_uvt_b, batch, seq):
    tq, tk = DSA_TQ, DSA_TK
    kvl = ckv.shape[1]
    nq = seq // tq
    nk = seq // tk
    last = [(i * tq + tq - 1) // tk for i in range(nq)]
    qi = np.concatenate([np.full(last[i] + 1, i) for i in range(nq)]).astype(np.int32)
    kj = np.concatenate([np.arange(last[i] + 1) for i in range(nq)]).astype(np.int32)
    kern = functools.partial(_dsa_attn_kernel, tq=tq, tk=tk, n_near=bias_tiles.shape[0])
    grid_spec = pltpu.PrefetchScalarGridSpec(
        num_scalar_prefetch=2,
        grid=(batch, len(qi)),
        in_specs=[
            pl.BlockSpec((B_HEADS, kvl, tq), lambda b, p, qi, kj: (0, 0, b * nq + qi[p])),
            pl.BlockSpec((tk, kvl), lambda b, p, qi, kj: (b * nk + kj[p], 0)),
            pl.BlockSpec((kvl, tk), lambda b, p, qi, kj: (b, kj[p])),
            pl.BlockSpec((tk, tq), lambda b, p, qi, kj: (b * nk + kj[p], qi[p])),
            pl.BlockSpec(bias_tiles.shape, lambda b, p, qi, kj: (0, 0, 0, 0)),
            pl.BlockSpec(w_uvt_b.shape, lambda b, p, qi, kj: (0, 0, 0)),
        ],
        out_specs=pl.BlockSpec((tq, B_HEADS * B_V_DIM), lambda b, p, qi, kj: (b * nq + qi[p], 0)),
        scratch_shapes=[pltpu.VMEM((B_HEADS, 1, tq), F32), pltpu.VMEM((B_HEADS, 1, tq), F32),
                        pltpu.VMEM((B_HEADS, kvl, tq), F32)],
    )
    return pl.pallas_call(
        kern,
        out_shape=jax.ShapeDtypeStruct((batch * seq, B_HEADS * B_V_DIM), BF16),
        grid_spec=grid_spec,
        compiler_params=_cparams(("arbitrary", "arbitrary")),
        name="dsa_attention",
    )(jnp.asarray(qi), jnp.asarray(kj), qlt, ckv, ckvt, mask_t, bias_tiles, w_uvt_b)


def _dsa_layer(x2, w_in, q_norm, kv_norm, w_uq, w_iq, w_uk, w_uv, w_out, rel_bias, ln_g, ln_b, batch, seq):
    ql_dim = q_norm.shape[0]
    kvl = kv_norm.shape[0]
    topk = min(IDX_TOPK, seq // 4)
    cq, ckv, ckvt, kidx, widx_t = _proj1(x2, w_in, q_norm, kv_norm, ql_dim, kvl, batch, seq)
    qlt, qidx = _qside(cq, w_uq.astype(BF16), w_uk.astype(BF16), w_iq.astype(BF16))
    mask_t = _indexer(qidx, widx_t, kidx, batch, seq, topk)
    bias_tiles = _bias_tiles(rel_bias, DSA_TQ, DSA_TK)
    w_uvt = jnp.swapaxes(w_uv, 1, 2).astype(BF16)
    o = _dsa_attention(qlt, ckv, ckvt, mask_t, bias_tiles, w_uvt, batch, seq)
    return _outproj_ln(o, w_out.astype(BF16), x2, ln_g, ln_b)


def kernel(x, a_w_in, a_b_f, a_w_out, b_w_in, b_q_norm, b_kv_norm, b_w_uq, b_w_iq, b_w_uk, b_w_uv, b_w_out,
           rel_bias, ln1_g, ln1_b, ln2_g, ln2_b, router_w, router_b, w_gate, w_up, w_down, sh_gate, sh_up,
           sh_down):
    batch, seq, d = x.shape
    x2 = x.reshape(batch * seq, d)
    x2 = _fox_layer(x2, a_w_in[0], a_b_f[0], a_w_out[0], ln1_g[0], ln1_b[0], batch, seq)
    x2 = _moe_layer(x2, router_w[0], router_b[0], w_gate, w_up, w_down, 0, sh_gate[0], sh_up[0],
                    sh_down[0], ln2_g[0], ln2_b[0])
    x2 = _dsa_layer(x2, b_w_in[0], b_q_norm[0], b_kv_norm[0], b_w_uq[0], b_w_iq[0], b_w_uk[0], b_w_uv[0],
                    b_w_out[0], rel_bias, ln1_g[1], ln1_b[1], batch, seq)
    x2 = _moe_layer(x2, router_w[1], router_b[1], w_gate, w_up, w_down, 1, sh_gate[1], sh_up[1],
                    sh_down[1], ln2_g[1], ln2_b[1])
    return x2.reshape(batch, seq, d)
```

```python
import functools
import math

import numpy as np
import jax
import jax.numpy as jnp
from jax import lax
from jax.experimental import pallas as pl
from jax.experimental.pallas import tpu as pltpu

BF16 = jnp.bfloat16
F32 = jnp.float32
I32 = jnp.int32
U32 = jnp.uint32

A_HEADS = 16
A_HEAD_DIM = 128
B_HEADS = 16
B_HEAD_DIM = 128
B_V_DIM = 128
IDX_HEADS = 16
IDX_DIM = 64
IDX_TOPK = 256
CHUNK = 64
REL_BUCKETS = 32
REL_MAX_DIST = 128
N_EXPERTS = 64
TOP_K = 8
ROUTED_SCALE = 2.5
DEPTH = 2
DEEPNORM_ALPHA = (2 * DEPTH) ** 0.25
LN_EPS = 1e-5
RMS_EPS = 1e-6

LANES = 128
SLAB = 8
HALF_BITS = 16
HIGH_HALF_MASK = 0xFFFF0000
VMEM_LIMIT = 56 * 1024 * 1024
LOG2E = math.log2(math.e)
NEG_BIG = -1e30

EXPERT_ROWS = 512
ROW_TILE = 512
PROJ0_TM, PROJ0_TN = 1024, 1024
CUMSUM_T = 256
DISPATCH_TT = 512
COMBINE_TT = 256
FOX_T = 512
FOX_HEADS_PER_STEP = 8
DSA_TQ = 256
DSA_TK = 512
IDX_TQ = 256
IDX_TKC = 512


def _cparams(sem, vmem=VMEM_LIMIT):
    return pltpu.CompilerParams(dimension_semantics=sem, vmem_limit_bytes=vmem)


def _dot(a, b):
    return jnp.dot(a, b, preferred_element_type=F32)


def _dot_nt(a, b):
    return lax.dot_general(a, b, (((1,), (1,)), ((), ())), preferred_element_type=F32)


def _lane_tile(a, width):
    return a if width == LANES else jnp.concatenate([a] * (width // LANES), axis=1)


def _split2(a):
    hi = a.astype(BF16)
    lo = (a - hi.astype(F32)).astype(BF16)
    return hi, lo


def _split3(a):
    hi = a.astype(BF16)
    r = a - hi.astype(F32)
    mid = r.astype(BF16)
    lo = (r - mid.astype(F32)).astype(BF16)
    return hi, mid, lo


def _dot_x3(a, b):
    ah, al = _split2(a)
    bh, bl = _split2(b)
    return _dot(ah, bh) + (_dot(ah, bl) + _dot(al, bh))


def _layer_norm_rows(z, g, b):
    mu = jnp.mean(z, axis=-1, keepdims=True)
    d = z - mu
    var = jnp.mean(d * d, axis=-1, keepdims=True)
    return d * lax.rsqrt(var + LN_EPS) * g + b


def _rms_norm_rows(z, g):
    ms = jnp.mean(z * z, axis=-1, keepdims=True)
    return z * lax.rsqrt(ms + RMS_EPS) * g


def _proj0_kernel(x_ref, w_ref, wf_ref, qkv_ref, fl_ref, xb_sc, *, n_q_blocks, q_scale):
    j = pl.program_id(1)

    @pl.when(j == 0)
    def _():
        x = x_ref[...]
        xb_sc[...] = x.astype(BF16)
        fl_ref[...] = _dot_x3(x, wf_ref[...])

    acc = _dot(xb_sc[...], w_ref[...])
    scale = jnp.where(j < n_q_blocks, q_scale, 1.0).astype(F32)
    qkv_ref[...] = (acc * scale).astype(BF16)


def _proj0(x2, w_qkv_b, w_f):
    n, d = x2.shape
    nout = w_qkv_b.shape[1]
    tm, tn = PROJ0_TM, PROJ0_TN
    dq = A_HEADS * A_HEAD_DIM
    kern = functools.partial(_proj0_kernel, n_q_blocks=dq // tn, q_scale=A_HEAD_DIM ** -0.5 * LOG2E)
    return pl.pallas_call(
        kern,
        out_shape=(jax.ShapeDtypeStruct((n, nout), BF16),
                   jax.ShapeDtypeStruct((n, A_HEADS), F32)),
        grid=(n // tm, nout // tn),
        in_specs=[pl.BlockSpec((tm, d), lambda i, j: (i, 0)),
                  pl.BlockSpec((d, tn), lambda i, j: (0, j)),
                  pl.BlockSpec((d, A_HEADS), lambda i, j: (0, 0))],
        out_specs=(pl.BlockSpec((tm, tn), lambda i, j: (i, j)),
                   pl.BlockSpec((tm, A_HEADS), lambda i, j: (i, 0))),
        scratch_shapes=[pltpu.VMEM((tm, d), BF16)],
        compiler_params=_cparams(("arbitrary", "arbitrary")),
        name="fox_proj",
    )(x2, w_qkv_b, w_f)


def _forget_cumsum_kernel(fl_ref, bf_ref, f_ref, carry_sc, *, t):
    @pl.when(pl.program_id(1) == 0)
    def _():
        carry_sc[...] = jnp.zeros_like(carry_sc)

    z = fl_ref[...] + bf_ref[...]
    logf = jnp.minimum(z, 0.0) - jnp.log1p(jnp.exp(-jnp.abs(z)))
    row = lax.broadcasted_iota(I32, (t, t), 0)
    col = lax.broadcasted_iota(I32, (t, t), 1)
    tri = (col <= row).astype(BF16)
    hi, mid, lo = _split3(logf)
    cs = _dot(tri, hi) + (_dot(tri, mid) + _dot(tri, lo)) + carry_sc[...]
    f_ref[...] = cs * LOG2E
    carry_sc[...] = cs[t - 1:t, :]


def _forget_cumsum(fl, b_f, batch, seq):
    t = CUMSUM_T
    nb = seq // t
    return pl.pallas_call(
        functools.partial(_forget_cumsum_kernel, t=t),
        out_shape=jax.ShapeDtypeStruct(fl.shape, F32),
        grid=(batch, nb),
        in_specs=[pl.BlockSpec((t, A_HEADS), lambda b, i: (b * nb + i, 0)),
                  pl.BlockSpec((1, A_HEADS), lambda b, i: (0, 0))],
        out_specs=pl.BlockSpec((t, A_HEADS), lambda b, i: (b * nb + i, 0)),
        scratch_shapes=[pltpu.VMEM((1, A_HEADS), F32)],
        compiler_params=_cparams(("arbitrary", "arbitrary")),
        name="fox_forget_cumsum",
    )(fl, b_f.reshape(1, A_HEADS))


def _fox_attn_kernel(qi_ref, kj_ref, q_ref, k_ref, v_ref, fk_ref, o_ref, m_sc, l_sc, acc_sc, *, tq, tk, hp):
    p = pl.program_id(2)
    i = qi_ref[p]
    j = kj_ref[p]
    dh = A_HEAD_DIM

    @pl.when(j == 0)
    def _():
        m_sc[...] = jnp.full_like(m_sc, -jnp.inf)
        l_sc[...] = jnp.zeros_like(l_sc)
        acc_sc[...] = jnp.zeros_like(acc_sc)

    def step(diag):
        for hh in range(hp):
            cols = slice(hh * dh, (hh + 1) * dh)
            s = _dot_nt(q_ref[:, cols], k_ref[:, cols]) - fk_ref[hh]
            if diag:
                row = lax.broadcasted_iota(I32, (tq, tk), 0)
                col = lax.broadcasted_iota(I32, (tq, tk), 1)
                s = jnp.where(col <= row, s, -jnp.inf)
            m_prev = m_sc[hh]
            m_new = jnp.maximum(m_prev, jnp.max(s, axis=1, keepdims=True))
            alpha = jnp.exp2(m_prev - m_new)
            pexp = jnp.exp2(s - _lane_tile(m_new, tk))
            l_sc[hh] = alpha * l_sc[hh] + jnp.sum(pexp, axis=1, keepdims=True)
            acc_sc[hh] = alpha * acc_sc[hh] + _dot(pexp.astype(BF16), v_ref[:, cols])
            m_sc[hh] = m_new

    @pl.when(j < i)
    def _():
        step(False)

    @pl.when(j == i)
    def _():
        step(True)
        for hh in range(hp):
            o_ref[:, hh * dh:(hh + 1) * dh] = (acc_sc[hh] / l_sc[hh]).astype(o_ref.dtype)


def _fox_attention(qkv, f_rows, batch, seq):
    t = min(FOX_T, seq)
    nq = seq // t
    qi = np.concatenate([np.full(i + 1, i) for i in range(nq)]).astype(np.int32)
    kj = np.concatenate([np.arange(i + 1) for i in range(nq)]).astype(np.int32)
    hp = FOX_HEADS_PER_STEP
    hg = A_HEADS // hp
    dh = A_HEAD_DIM
    w = hp * dh
    kern = functools.partial(_fox_attn_kernel, tq=t, tk=t, hp=hp)
    grid_spec = pltpu.PrefetchScalarGridSpec(
        num_scalar_prefetch=2,
        grid=(batch, hg, len(qi)),
        in_specs=[
            pl.BlockSpec((t, w), lambda b, h, p, qi, kj: (b * nq + qi[p], h)),
            pl.BlockSpec((t, w), lambda b, h, p, qi, kj: (b * nq + kj[p], hg + h)),
            pl.BlockSpec((t, w), lambda b, h, p, qi, kj: (b * nq + kj[p], 2 * hg + h)),
            pl.BlockSpec((hp, 1, t), lambda b, h, p, qi, kj: (b * hg + h, 0, kj[p])),
        ],
        out_specs=pl.BlockSpec((t, w), lambda b, h, p, qi, kj: (b * nq + qi[p], h)),
        scratch_shapes=[pltpu.VMEM((hp, t, dh), F32), pltpu.VMEM((hp, t, dh), F32),
                        pltpu.VMEM((hp, t, dh), F32)],
    )
    return pl.pallas_call(
        kern,
        out_shape=jax.ShapeDtypeStruct((batch * seq, A_HEADS * dh), BF16),
        grid_spec=grid_spec,
        compiler_params=_cparams(("arbitrary", "arbitrary", "arbitrary")),
        name="fox_attention",
    )(jnp.asarray(qi), jnp.asarray(kj), qkv, qkv, qkv, f_rows)


def _outproj_ln_kernel(o_ref, w_ref, x_ref, g_ref, b_ref, y_ref):
    z = DEEPNORM_ALPHA * x_ref[...] + _dot(o_ref[...], w_ref[...])
    y_ref[...] = _layer_norm_rows(z, g_ref[...], b_ref[...])


def _outproj_ln(o, w_b, x2, g, b):
    n, d = x2.shape
    k = o.shape[1]
    tm = ROW_TILE
    return pl.pallas_call(
        _outproj_ln_kernel,
        out_shape=jax.ShapeDtypeStruct((n, d), F32),
        grid=(n // tm,),
        in_specs=[pl.BlockSpec((tm, k), lambda i: (i, 0)),
                  pl.BlockSpec((k, d), lambda i: (0, 0)),
                  pl.BlockSpec((tm, d), lambda i: (i, 0)),
                  pl.BlockSpec((1, d), lambda i: (0, 0)),
                  pl.BlockSpec((1, d), lambda i: (0, 0))],
        out_specs=pl.BlockSpec((tm, d), lambda i: (i, 0)),
        compiler_params=_cparams(("arbitrary",)),
        name="outproj_deepnorm",
    )(o, w_b, x2, g.reshape(1, d), b.reshape(1, d))


def _fox_layer(x2, w_in, b_f, w_out, ln_g, ln_b, batch, seq):
    dq = A_HEADS * A_HEAD_DIM
    qkv, fl = _proj0(x2, w_in[:, :3 * dq].astype(BF16), w_in[:, 3 * dq:])
    f = _forget_cumsum(fl, b_f, batch, seq)
    f_rows = f.reshape(batch, seq, A_HEADS).transpose(0, 2, 1).reshape(batch * A_HEADS, 1, seq)
    o = _fox_attention(qkv, f_rows, batch, seq)
    return _outproj_ln(o, w_out.astype(BF16), x2, ln_g, ln_b)


def _router_kernel(x_ref, rw_ref, rb_ref, sel_ref, gate_ref, rank_ref, cnt_ref, carry_sc, *, tm):
    @pl.when(pl.program_id(0) == 0)
    def _():
        carry_sc[...] = jnp.zeros_like(carry_sc)

    e = N_EXPERTS
    scores = jax.nn.sigmoid(_dot_x3(x_ref[...], rw_ref[...]))
    lane = lax.broadcasted_iota(I32, (tm, e), 1)
    slot = lax.broadcasted_iota(I32, (tm, TOP_K), 1)
    work = scores + rb_ref[...]
    chosen = jnp.zeros((tm, e), F32)
    sel = jnp.zeros((tm, TOP_K), I32)
    gate = jnp.zeros((tm, TOP_K), F32)
    idxs = []
    for k in range(TOP_K):
        mx = jnp.max(work, axis=1, keepdims=True)
        idx = jnp.min(jnp.where(work == mx, lane, e), axis=1, keepdims=True)
        hit = lane == idx
        gk = jnp.sum(jnp.where(hit, scores, 0.0), axis=1, keepdims=True)
        work = jnp.where(hit, -jnp.inf, work)
        chosen = jnp.where(hit, 1.0, chosen)
        sel = jnp.where(slot == k, idx, sel)
        gate = jnp.where(slot == k, gk, gate)
        idxs.append(idx)
    gate = gate / jnp.sum(gate, axis=1, keepdims=True) * ROUTED_SCALE

    row = lax.broadcasted_iota(I32, (tm, tm), 0)
    col = lax.broadcasted_iota(I32, (tm, tm), 1)
    before = _dot((col < row).astype(BF16), chosen.astype(BF16)) + carry_sc[...]
    rank = jnp.zeros((tm, TOP_K), F32)
    for k in range(TOP_K):
        rk = jnp.sum(jnp.where(lane == idxs[k], before, 0.0), axis=1, keepdims=True)
        rank = jnp.where(slot == k, rk, rank)
    total = carry_sc[...] + jnp.sum(chosen, axis=0, keepdims=True)
    carry_sc[...] = total
    sel_ref[...] = sel
    gate_ref[...] = gate
    rank_ref[...] = rank.astype(I32)
    cnt_ref[...] = total.astype(I32)


def _router(x2, rw, rb):
    n, d = x2.shape
    tm = ROW_TILE
    e = N_EXPERTS
    return pl.pallas_call(
        functools.partial(_router_kernel, tm=tm),
        out_shape=(jax.ShapeDtypeStruct((n, TOP_K), I32),
                   jax.ShapeDtypeStruct((n, TOP_K), F32),
                   jax.ShapeDtypeStruct((n, TOP_K), I32),
                   jax.ShapeDtypeStruct((1, e), I32)),
        grid=(n // tm,),
        in_specs=[pl.BlockSpec((tm, d), lambda i: (i, 0)),
                  pl.BlockSpec((d, e), lambda i: (0, 0)),
                  pl.BlockSpec((1, e), lambda i: (0, 0))],
        out_specs=(pl.BlockSpec((tm, TOP_K), lambda i: (i, 0)),
                   pl.BlockSpec((tm, TOP_K), lambda i: (i, 0)),
                   pl.BlockSpec((tm, TOP_K), lambda i: (i, 0)),
                   pl.BlockSpec((1, e), lambda i: (0, 0))),
        scratch_shapes=[pltpu.VMEM((1, e), F32)],
        compiler_params=_cparams(("arbitrary",)),
        name="moe_router",
    )(x2, rw, rb.reshape(1, e))


def _pack_bf16_pairs(y):
    w = y.shape[1] // 2
    lo = lax.bitcast_convert_type(y[:, :w].astype(BF16).astype(F32), U32)
    hi = lax.bitcast_convert_type(y[:, w:].astype(BF16).astype(F32), U32)
    return lax.shift_right_logical(lo, jnp.uint32(HALF_BITS)) | hi


def _low_half_f32(words):
    return lax.bitcast_convert_type(lax.shift_left(words, jnp.uint32(HALF_BITS)), F32)


def _high_half_f32(words):
    return lax.bitcast_convert_type(words & jnp.uint32(HIGH_HALF_MASK), F32)


def _unpack_bf16_pairs(words):
    return _low_half_f32(words).astype(BF16), _high_half_f32(words).astype(BF16)


def _slab_load(ref, rows):
    return jnp.concatenate([ref[pl.ds(c, rows, stride=SLAB), :] for c in range(SLAB)], axis=1)


def _slab_store(ref, rows, val):
    for c in range(SLAB):
        ref[pl.ds(c, rows, stride=SLAB), :] = val[:, c * LANES:(c + 1) * LANES]


def _dispatch_kernel(pad_ref, dest_hbm, x_ref, xs_hbm, dest_sm, slab_sc, zero_sc, sem_idx, sem_zero, sem_row,
                     *, tt, n_tiles):
    i = pl.program_id(0)

    def zero_copy(e):
        start = pl.multiple_of(pad_ref[e] * SLAB, SLAB)
        return pltpu.make_async_copy(zero_sc, xs_hbm.at[pl.ds(start, EXPERT_ROWS * SLAB)], sem_zero)

    @pl.when(i == 0)
    def _():
        zero_sc[...] = jnp.zeros_like(zero_sc)

        def start(e, c):
            @pl.when(pad_ref[N_EXPERTS + e] > 0)
            def _():
                zero_copy(e).start()
            return c

        def wait(e, c):
            @pl.when(pad_ref[N_EXPERTS + e] > 0)
            def _():
                zero_copy(e).wait()
            return c

        lax.fori_loop(0, N_EXPERTS, start, 0)
        lax.fori_loop(0, N_EXPERTS, wait, 0)

    slot = lax.rem(i, 2)
    idx_copy = pltpu.make_async_copy(dest_hbm.at[i], dest_sm, sem_idx)
    idx_copy.start()

    def row_copy(s, t, dst_row):
        src = pl.multiple_of(t * SLAB, SLAB)
        dst = pl.multiple_of(dst_row * SLAB, SLAB)
        return pltpu.make_async_copy(slab_sc.at[s, pl.ds(src, SLAB)], xs_hbm.at[pl.ds(dst, SLAB)], sem_row.at[s])

    def wait_rows(s):
        def wait(t, c):
            for k in range(TOP_K):
                row_copy(s, 0, 0).wait()
            return c
        lax.fori_loop(0, tt, wait, 0, unroll=2)

    @pl.when(i >= 2)
    def _():
        wait_rows(slot)

    _slab_store(slab_sc.at[slot], tt, _pack_bf16_pairs(x_ref[...]))
    idx_copy.wait()

    def start(t, c):
        for k in range(TOP_K):
            row_copy(slot, t, dest_sm[t * TOP_K + k]).start(priority=k % 2)
        return c

    lax.fori_loop(0, tt, start, 0, unroll=2)

    @pl.when(i == n_tiles - 1)
    def _():
        if n_tiles > 1:
            wait_rows(1 - slot)
        wait_rows(slot)


def _dispatch(pad_info, dest, x2, p_rows):
    n, d = x2.shape
    tt = DISPATCH_TT
    assert d == 2 * SLAB * LANES
    dest2 = dest.reshape(n // tt, tt * TOP_K)
    grid_spec = pltpu.PrefetchScalarGridSpec(
        num_scalar_prefetch=1,
        grid=(n // tt,),
        in_specs=[pl.BlockSpec(memory_space=pl.ANY),
                  pl.BlockSpec((tt, d), lambda i, pad: (i, 0))],
        out_specs=pl.BlockSpec(memory_space=pl.ANY),
        scratch_shapes=[pltpu.SMEM((tt * TOP_K,), I32),
                        pltpu.VMEM((2, tt * SLAB, LANES), U32),
                        pltpu.VMEM((EXPERT_ROWS * SLAB, LANES), U32),
                        pltpu.SemaphoreType.DMA, pltpu.SemaphoreType.DMA, pltpu.SemaphoreType.DMA((2,))],
    )
    return pl.pallas_call(
        functools.partial(_dispatch_kernel, tt=tt, n_tiles=n // tt),
        out_shape=jax.ShapeDtypeStruct((p_rows * SLAB, LANES), U32),
        grid_spec=grid_spec,
        compiler_params=_cparams(("arbitrary",)),
        name="moe_dispatch",
    )(pad_info, dest2, x2)


def _expert_mlp_kernel(be_ref, nu_ref, xs_ref, wg_ref, wu_ref, wd_ref, ys_ref, wg_sc, wu_sc, wd_sc):
    b = pl.program_id(0)
    e = be_ref[b]
    first = jnp.logical_or(b == 0, e != be_ref[jnp.maximum(b - 1, 0)])
    active = b < nu_ref[0]

    @pl.when(jnp.logical_and(active, first))
    def _():
        wg_sc[...] = wg_ref[...].astype(BF16)
        wu_sc[...] = wu_ref[...].astype(BF16)
        wd_sc[...] = wd_ref[...].astype(BF16)

    @pl.when(active)
    def _():
        r = EXPERT_ROWS
        x = jnp.concatenate(_unpack_bf16_pairs(_slab_load(xs_ref, r)), axis=1)
        g = _dot(x, wg_sc[...])
        u = _dot(x, wu_sc[...])
        h = (g * jax.nn.sigmoid(g) * u).astype(BF16)
        _slab_store(ys_ref, r, _pack_bf16_pairs(_dot(h, wd_sc[...])))


def _expert_mlp(blk_e, n_used, xs, w_gate, w_up, w_down, layer):
    r = EXPERT_ROWS
    nblk = xs.shape[0] // (r * SLAB)
    d, de = w_gate.shape[2], w_gate.shape[3]

    def row_map(b, be, nu):
        return (jnp.minimum(b, nu[0] - 1), 0)

    def w_map(b, be, nu):
        return (layer, be[b], 0, 0)

    grid_spec = pltpu.PrefetchScalarGridSpec(
        num_scalar_prefetch=2,
        grid=(nblk,),
        in_specs=[pl.BlockSpec((r * SLAB, LANES), row_map),
                  pl.BlockSpec((None, None, d, de), w_map),
                  pl.BlockSpec((None, None, d, de), w_map),
                  pl.BlockSpec((None, None, de, d), w_map)],
        out_specs=pl.BlockSpec((r * SLAB, LANES), row_map),
        scratch_shapes=[pltpu.VMEM((d, de), BF16), pltpu.VMEM((d, de), BF16), pltpu.VMEM((de, d), BF16)],
    )
    return pl.pallas_call(
        _expert_mlp_kernel,
        out_shape=jax.ShapeDtypeStruct(xs.shape, U32),
        grid_spec=grid_spec,
        compiler_params=_cparams(("arbitrary",)),
        name="moe_experts",
    )(blk_e, n_used, xs, w_gate, w_up, w_down)


def _combine_kernel(dest_hbm, ys_hbm, x_ref, gate_ref, sg_ref, su_ref, sd_ref, g_ref, b_ref,
                    y_ref, dest_sm, rows_sc, sem_idx, sem_row, *, tt, n_tiles):
    i = pl.program_id(0)
    slot = lax.rem(i, 2)
    n_rows = tt * TOP_K

    def idx_copy(tile, s):
        return pltpu.make_async_copy(dest_hbm.at[tile], dest_sm.at[pl.ds(pl.multiple_of(s * n_rows, n_rows), n_rows)],
                                     sem_idx.at[s])

    def row_copy(s, t, k, src_row):
        return pltpu.make_async_copy(ys_hbm.at[pl.ds(pl.multiple_of(src_row * SLAB, SLAB), SLAB)],
                                     rows_sc.at[s, k, pl.ds(pl.multiple_of(t * SLAB, SLAB), SLAB)], sem_row.at[s])

    def start_rows(s):
        def start(t, c):
            for k in range(TOP_K):
                row_copy(s, t, k, dest_sm[s * n_rows + t * TOP_K + k]).start(priority=k % 2)
            return c
        lax.fori_loop(0, tt, start, 0, unroll=2)

    @pl.when(i == 0)
    def _():
        idx_copy(0, 0).start()
        idx_copy(0, 0).wait()
        start_rows(0)
        if n_tiles > 1:
            idx_copy(1, 1).start()

    @pl.when(i + 1 < n_tiles)
    def _():
        idx_copy(i + 1, 1 - slot).wait()
        start_rows(1 - slot)

    @pl.when(i + 2 < n_tiles)
    def _():
        idx_copy(i + 2, slot).start()

    x = x_ref[...]
    xb = x.astype(BF16)
    g = _dot(xb, sg_ref[...])
    u = _dot(xb, su_ref[...])
    h = (g * jax.nn.sigmoid(g) * u).astype(BF16)
    z = DEEPNORM_ALPHA * x + _dot(h, sd_ref[...])

    def wait(t, c):
        for k in range(TOP_K):
            row_copy(slot, 0, 0, 0).wait()
        return c
    lax.fori_loop(0, tt, wait, 0, unroll=2)

    gate = gate_ref[...]
    lo_pieces, hi_pieces = [], []
    for c in range(SLAB):
        acc_lo = jnp.zeros((tt, LANES), F32)
        acc_hi = jnp.zeros((tt, LANES), F32)
        for k in range(TOP_K):
            words = rows_sc[slot, k, pl.ds(c, tt, stride=SLAB), :]
            gk = gate[:, k:k + 1]
            acc_lo = acc_lo + gk * _low_half_f32(words)
            acc_hi = acc_hi + gk * _high_half_f32(words)
        lo_pieces.append(acc_lo)
        hi_pieces.append(acc_hi)
    routed = jnp.concatenate(lo_pieces + hi_pieces, axis=1)
    y_ref[...] = _layer_norm_rows(z + routed, g_ref[...], b_ref[...])


def _combine(dest, ys, x2, gate, sg_b, su_b, sd_b, ln_g, ln_b):
    n, d = x2.shape
    ds_ = sg_b.shape[1]
    tt = COMBINE_TT
    n_tiles = n // tt
    dest2 = dest.reshape(n_tiles, tt * TOP_K)
    return pl.pallas_call(
        functools.partial(_combine_kernel, tt=tt, n_tiles=n_tiles),
        out_shape=jax.ShapeDtypeStruct((n, d), F32),
        grid=(n_tiles,),
        in_specs=[pl.BlockSpec(memory_space=pl.ANY),
                  pl.BlockSpec(memory_space=pl.ANY),
                  pl.BlockSpec((tt, d), lambda i: (i, 0)),
                  pl.BlockSpec((tt, TOP_K), lambda i: (i, 0)),
                  pl.BlockSpec((d, ds_), lambda i: (0, 0)),
                  pl.BlockSpec((d, ds_), lambda i: (0, 0)),
                  pl.BlockSpec((ds_, d), lambda i: (0, 0)),
                  pl.BlockSpec((1, d), lambda i: (0, 0)),
                  pl.BlockSpec((1, d), lambda i: (0, 0))],
        out_specs=pl.BlockSpec((tt, d), lambda i: (i, 0)),
        scratch_shapes=[pltpu.SMEM((2 * tt * TOP_K,), I32),
                        pltpu.VMEM((2, TOP_K, tt * SLAB, LANES), U32),
                        pltpu.SemaphoreType.DMA((2,)), pltpu.SemaphoreType.DMA((2,))],
        compiler_params=_cparams(("arbitrary",)),
        name="moe_combine",
    )(dest2, ys, x2, gate, sg_b, su_b, sd_b, ln_g.reshape(1, d), ln_b.reshape(1, d))


def _moe_layer(x2, rw, rb, w_gate, w_up, w_down, layer, sh_gate, sh_up, sh_down, ln_g, ln_b):
    n, d = x2.shape
    e = N_EXPERTS
    r = EXPERT_ROWS
    sel, gate, rank, counts = _router(x2, rw, rb)
    counts = counts.reshape(e)
    padded = (counts + r - 1) // r * r
    pend = jnp.cumsum(padded)
    pstart = pend - padded
    dest = rank + jnp.sum(jnp.where(sel[..., None] == jnp.arange(e, dtype=I32), pstart.astype(I32), 0), axis=-1)
    p_rows = n * TOP_K + e * r
    nblk = p_rows // r
    blk_start = jnp.arange(nblk, dtype=I32) * r
    blk_e = jnp.minimum(jnp.sum((pend[None, :] <= blk_start[:, None]).astype(I32), axis=1), e - 1)
    n_used = (pend[-1] // r).astype(I32).reshape(1)
    pad_info = jnp.concatenate([jnp.maximum(pend - r, 0), padded]).astype(I32)
    xs = _dispatch(pad_info, dest.astype(I32), x2, p_rows)
    ys = _expert_mlp(blk_e, n_used, xs, w_gate, w_up, w_down, layer)
    return _combine(dest.astype(I32), ys, x2, gate, sh_gate.astype(BF16), sh_up.astype(BF16),
                    sh_down.astype(BF16), ln_g, ln_b)


FAR_BUCKET = REL_BUCKETS // 2 - 1


def _n_near_offsets(tq, tk):
    return (tk + REL_MAX_DIST - 1 + tq - 1) // tq


def _t5_bucket(rel):
    half = REL_BUCKETS // 2
    max_exact = half // 2
    n = jnp.abs(rel)
    large = max_exact + (jnp.log(jnp.maximum(n, 1).astype(F32) / max_exact)
                         / math.log(REL_MAX_DIST / max_exact) * (half - max_exact)).astype(I32)
    large = jnp.minimum(large, half - 1)
    return jnp.where(rel > 0, half, 0) + jnp.where(n < max_exact, n, large)


def _bias_tiles_kernel(rb_ref, out_ref, *, tq, tk):
    offset = -tq * pl.program_id(0)
    key = lax.broadcasted_iota(I32, (tk, tq), 0)
    qry = lax.broadcasted_iota(I32, (tk, tq), 1)
    bucket = _t5_bucket(offset + key - qry)
    for h in range(B_HEADS):
        far = rb_ref[FAR_BUCKET * B_HEADS + h]
        acc = jnp.zeros((tk, tq), F32)
        for b in range(REL_BUCKETS):
            acc = jnp.where(bucket == b, (rb_ref[b * B_HEADS + h] - far) * LOG2E, acc)
        out_ref[0, h] = acc


def _bias_tiles(rel_bias, tq, tk):
    n_off = _n_near_offsets(tq, tk)
    grid_spec = pltpu.PrefetchScalarGridSpec(
        num_scalar_prefetch=1,
        grid=(n_off,),
        in_specs=[],
        out_specs=pl.BlockSpec((1, B_HEADS, tk, tq), lambda i, rb: (i, 0, 0, 0)),
    )
    return pl.pallas_call(
        functools.partial(_bias_tiles_kernel, tq=tq, tk=tk),
        out_shape=jax.ShapeDtypeStruct((n_off, B_HEADS, tk, tq), F32),
        grid_spec=grid_spec,
        compiler_params=_cparams(("arbitrary",)),
        name="dsa_bias_tiles",
    )(rel_bias.reshape(-1))


def _proj1_kernel(x_ref, w_ref, wit_ref, qn_ref, kvn_ref, cq_ref, ckv_ref, ckvt_ref, ki_ref, wi_ref,
                  *, o1, o2, o3, wscale):
    x = x_ref[...]
    proj = _dot(x.astype(BF16), w_ref[...])
    cq_ref[...] = _rms_norm_rows(proj[:, :o1], qn_ref[...]).astype(BF16)
    ckv = _rms_norm_rows(proj[:, o1:o2], kvn_ref[...])
    ckv_ref[...] = ckv.astype(BF16)
    ckvt_ref[...] = ckv.T.astype(BF16)
    ki_ref[...] = proj[:, o2:o3].astype(BF16)
    wi_ref[...] = _dot_nt(wit_ref[...], x) * wscale


def _proj1(x2, w_in, q_norm, kv_norm, ql, kvl, batch, seq):
    n, d = x2.shape
    o1, o2, o3 = ql, ql + kvl, ql + kvl + IDX_DIM
    tm = ROW_TILE
    nt = seq // tm
    w_main = w_in[:, :o3].astype(BF16)
    w_idx_t = w_in[:, o3:].T
    kern = functools.partial(_proj1_kernel, o1=o1, o2=o2, o3=o3,
                             wscale=(IDX_HEADS ** -0.5) * (IDX_DIM ** -0.5))
    return pl.pallas_call(
        kern,
        out_shape=(jax.ShapeDtypeStruct((n, ql), BF16), jax.ShapeDtypeStruct((n, kvl), BF16),
                   jax.ShapeDtypeStruct((batch * kvl, seq), BF16),
                   jax.ShapeDtypeStruct((n, IDX_DIM), BF16), jax.ShapeDtypeStruct((IDX_HEADS, n), F32)),
        grid=(n // tm,),
        in_specs=[pl.BlockSpec((tm, d), lambda i: (i, 0)),
                  pl.BlockSpec((d, o3), lambda i: (0, 0)),
                  pl.BlockSpec((IDX_HEADS, d), lambda i: (0, 0)),
                  pl.BlockSpec((1, ql), lambda i: (0, 0)),
                  pl.BlockSpec((1, kvl), lambda i: (0, 0))],
        out_specs=(pl.BlockSpec((tm, ql), lambda i: (i, 0)), pl.BlockSpec((tm, kvl), lambda i: (i, 0)),
                   pl.BlockSpec((kvl, tm), lambda i: (i // nt, i % nt)),
                   pl.BlockSpec((tm, IDX_DIM), lambda i: (i, 0)), pl.BlockSpec((IDX_HEADS, tm), lambda i: (0, i))),
        compiler_params=_cparams(("arbitrary",)),
        name="dsa_proj",
    )(x2, w_main, w_idx_t, q_norm.reshape(1, ql), kv_norm.reshape(1, kvl))


def _qside_kernel(cq_ref, wuq_ref, wuk_ref, wiq_ref, ql_ref, qi_ref, *, scale):
    cq = cq_ref[...]
    q = _dot(cq, wuq_ref[...]).astype(BF16)
    for h in range(B_HEADS):
        qh = q[:, h * B_HEAD_DIM:(h + 1) * B_HEAD_DIM]
        ql_ref[h] = (_dot_nt(wuk_ref[h], qh) * scale).astype(BF16)
    qi_ref[...] = _dot(cq, wiq_ref[...]).astype(BF16)


def _qside(cq, w_uq_b, w_uk_b, w_iq_b):
    n, ql = cq.shape
    kvl = w_uk_b.shape[1]
    tm = ROW_TILE
    kern = functools.partial(_qside_kernel, scale=B_HEAD_DIM ** -0.5 * LOG2E)
    return pl.pallas_call(
        kern,
        out_shape=(jax.ShapeDtypeStruct((B_HEADS, kvl, n), BF16),
                   jax.ShapeDtypeStruct((n, IDX_HEADS * IDX_DIM), BF16)),
        grid=(n // tm,),
        in_specs=[pl.BlockSpec((tm, ql), lambda i: (i, 0)),
                  pl.BlockSpec(w_uq_b.shape, lambda i: (0, 0)),
                  pl.BlockSpec(w_uk_b.shape, lambda i: (0, 0, 0)),
                  pl.BlockSpec(w_iq_b.shape, lambda i: (0, 0))],
        out_specs=(pl.BlockSpec((B_HEADS, kvl, tm), lambda i: (0, 0, i)),
                   pl.BlockSpec((tm, IDX_HEADS * IDX_DIM), lambda i: (i, 0))),
        compiler_params=_cparams(("arbitrary",)),
        name="dsa_qside",
    )(cq, w_uq_b, w_uk_b, w_iq_b)


INT_MIN = -2 ** 31
KEY_NEG_INF = (0xFF800000 ^ 0x7FFFFFFF) - 2 ** 32


def _ordered_key(v):
    bits = lax.bitcast_convert_type(v, I32)
    return bits ^ (lax.shift_right_arithmetic(bits, 31) & 0x7FFFFFFF)


def _indexer_kernel(qi_ref, wi_ref, ki_ref, mask_ref, key_sc, *, tq, tkc, topk):
    i = pl.program_id(1)
    n_chunks = (i * tq + tq + tkc - 1) // tkc
    qpos = i * tq + lax.broadcasted_iota(I32, (1, tq), 1)
    limit = (lax.shift_right_logical(qpos, CHUNK.bit_length() - 1) + 1) * CHUNK
    w = wi_ref[...]
    key_sc[...] = jnp.full(key_sc.shape, KEY_NEG_INF, I32)

    def score_chunk(c, carry):
        start = pl.multiple_of(c * tkc, tkc)
        k = ki_ref[pl.ds(start, tkc), :]
        acc = jnp.zeros((tkc, tq), F32)
        for h in range(IDX_HEADS):
            sc = _dot_nt(k, qi_ref[:, h * IDX_DIM:(h + 1) * IDX_DIM])
            acc = acc + jnp.maximum(sc, 0.0) * w[h:h + 1, :]
        kpos = start + lax.broadcasted_iota(I32, (tkc, tq), 0)
        key_sc[pl.ds(start, tkc), :] = _ordered_key(jnp.where(kpos < limit, acc, -jnp.inf))
        return carry

    lax.fori_loop(0, n_chunks, score_chunk, 0)

    def count(pred_fn):
        def body(c, acc):
            start = pl.multiple_of(c * tkc, tkc)
            hit = pred_fn(key_sc[pl.ds(start, tkc), :]).astype(I32)
            return acc + jnp.sum(hit.reshape(tkc // SLAB, SLAB, tq), axis=0)
        acc = lax.fori_loop(0, n_chunks, body, jnp.zeros((SLAB, tq), I32))
        return jnp.sum(acc, axis=0, keepdims=True)

    def bisect(it, prefix):
        cand = prefix + lax.shift_left(jnp.int32(1), 31 - it)
        cnt = count(lambda kk: kk >= cand)
        return jnp.where(cnt >= topk, cand, prefix)

    thr = lax.fori_loop(0, 32, bisect, jnp.full((1, tq), INT_MIN, I32))
    n_gt = count(lambda kk: kk > thr)
    n_eq = count(lambda kk: kk == thr)
    need = topk - n_gt
    tie_break = jnp.max(jnp.where(jnp.logical_and(thr > KEY_NEG_INF, n_eq > need), 1, 0)) > 0

    mask_ref[...] = jnp.full(mask_ref.shape, NEG_BIG, F32)

    @pl.when(jnp.logical_not(tie_break))
    def _():
        def write(c, carry):
            start = pl.multiple_of(c * tkc, tkc)
            kk = key_sc[pl.ds(start, tkc), :]
            sel = jnp.logical_and(kk >= thr, kk > KEY_NEG_INF)
            mask_ref[pl.ds(start, tkc), :] = jnp.where(sel, 0.0, NEG_BIG)
            return carry
        lax.fori_loop(0, n_chunks, write, 0)

    @pl.when(tie_break)
    def _():
        r_ = lax.broadcasted_iota(I32, (LANES, LANES), 0)
        c_ = lax.broadcasted_iota(I32, (LANES, LANES), 1)
        lower = (c_ < r_).astype(BF16)

        def write(c, seen):
            start = pl.multiple_of(c * LANES, LANES)
            kk = key_sc[pl.ds(start, LANES), :]
            eq = kk == thr
            before = seen + _dot(lower, eq.astype(BF16))
            sel = jnp.logical_or(kk > thr, jnp.logical_and(eq, before < need.astype(F32)))
            sel = jnp.logical_and(sel, kk > KEY_NEG_INF)
            mask_ref[pl.ds(start, LANES), :] = jnp.where(sel, 0.0, NEG_BIG)
            return seen + jnp.sum(eq.astype(F32), axis=0, keepdims=True)
        lax.fori_loop(0, n_chunks * (tkc // LANES), write, jnp.zeros((1, tq), F32))


def _indexer(qidx, widx_t, kidx, batch, seq, topk):
    tq = IDX_TQ
    tkc = min(IDX_TKC, seq)
    nq = seq // tq
    kern = functools.partial(_indexer_kernel, tq=tq, tkc=tkc, topk=topk)
    return pl.pallas_call(
        kern,
        out_shape=jax.ShapeDtypeStruct((batch * seq, seq), F32),
        grid=(batch, nq),
        in_specs=[pl.BlockSpec((tq, IDX_HEADS * IDX_DIM), lambda b, i: (b * nq + i, 0)),
                  pl.BlockSpec((IDX_HEADS, tq), lambda b, i: (0, b * nq + i)),
                  pl.BlockSpec((seq, IDX_DIM), lambda b, i: (b, 0))],
        out_specs=pl.BlockSpec((seq, tq), lambda b, i: (b, i)),
        scratch_shapes=[pltpu.VMEM((seq, tq), I32)],
        compiler_params=_cparams(("arbitrary", "arbitrary")),
        name="dsa_indexer",
    )(qidx, widx_t, kidx)


def _dsa_attn_kernel(qi_ref, kj_ref, qlt_ref, kv_ref, kvt_ref, mask_ref, bias_ref, wuvt_ref, o_ref,
                     m_sc, l_sc, acc_sc, *, tq, tk, n_near):
    p = pl.program_id(1)
    i = qi_ref[p]
    j = kj_ref[p]
    nh = B_HEADS
    behind = (i * tq - j * tk) // tq

    @pl.when(j == 0)
    def _():
        m_sc[...] = jnp.full_like(m_sc, -jnp.inf)
        l_sc[...] = jnp.zeros_like(l_sc)
        acc_sc[...] = jnp.zeros_like(acc_sc)

    def step(near):
        kv = kv_ref[...]
        kvt = kvt_ref[...]
        msk = mask_ref[...]
        for h in range(nh):
            s = _dot(kv, qlt_ref[h]) + msk
            if near:
                s = s + bias_ref[behind, h]
            m_prev = m_sc[h]
            m_new = jnp.maximum(m_prev, jnp.max(s, axis=0, keepdims=True))
            alpha = jnp.exp2(m_prev - m_new)
            pexp = jnp.exp2(s - m_new)
            l_sc[h] = alpha * l_sc[h] + jnp.sum(pexp, axis=0, keepdims=True)
            acc_sc[h] = alpha * acc_sc[h] + _dot(kvt, pexp.astype(BF16))
            m_sc[h] = m_new

    @pl.when(behind >= n_near)
    def _():
        step(False)

    @pl.when(behind < n_near)
    def _():
        step(True)

    @pl.when(j == (i * tq + tq - 1) // tk)
    def _():
        for h in range(nh):
            o_lat_t = (acc_sc[h] / l_sc[h]).astype(BF16)
            o_t = _dot(wuvt_ref[h], o_lat_t)
            o_ref[:, h * B_V_DIM:(h + 1) * B_V_DIM] = o_t.T.astype(o_ref.dtype)


def _dsa_attention(qlt, ckv, ckvt, mask_t, bias_tiles, w_uvt_b, batch, seq):
    tq, tk = DSA_TQ, DSA_TK
    kvl = ckv.shape[1]
    nq = seq // tq
    nk = seq // tk
    last = [(i * tq + tq - 1) // tk for i in range(nq)]
    qi = np.concatenate([np.full(last[i] + 1, i) for i in range(nq)]).astype(np.int32)
    kj = np.concatenate([np.arange(last[i] + 1) for i in range(nq)]).astype(np.int32)
    kern = functools.partial(_dsa_attn_kernel, tq=tq, tk=tk, n_near=bias_tiles.shape[0])
    grid_spec = pltpu.PrefetchScalarGridSpec(
        num_scalar_prefetch=2,
        grid=(batch, len(qi)),
        in_specs=[
            pl.BlockSpec((B_HEADS, kvl, tq), lambda b, p, qi, kj: (0, 0, b * nq + qi[p])),
            pl.BlockSpec((tk, kvl), lambda b, p, qi, kj: (b * nk + kj[p], 0)),
            pl.BlockSpec((kvl, tk), lambda b, p, qi, kj: (b, kj[p])),
            pl.BlockSpec((tk, tq), lambda b, p, qi, kj: (b * nk + kj[p], qi[p])),
            pl.BlockSpec(bias_tiles.shape, lambda b, p, qi, kj: (0, 0, 0, 0), pipeline_mode=pl.Buffered(1)),
            pl.BlockSpec(w_uvt_b.shape, lambda b, p, qi, kj: (0, 0, 0), pipeline_mode=pl.Buffered(1)),
        ],
        out_specs=pl.BlockSpec((tq, B_HEADS * B_V_DIM), lambda b, p, qi, kj: (b * nq + qi[p], 0)),
        scratch_shapes=[pltpu.VMEM((B_HEADS, 1, tq), F32), pltpu.VMEM((B_HEADS, 1, tq), F32),
                        pltpu.VMEM((B_HEADS, kvl, tq), F32)],
    )
    return pl.pallas_call(
        kern,
        out_shape=jax.ShapeDtypeStruct((batch * seq, B_HEADS * B_V_DIM), BF16),
        grid_spec=grid_spec,
        compiler_params=_cparams(("arbitrary", "arbitrary")),
        name="dsa_attention",
    )(jnp.asarray(qi), jnp.asarray(kj), qlt, ckv, ckvt, mask_t, bias_tiles, w_uvt_b)


def _dsa_layer(x2, w_in, q_norm, kv_norm, w_uq, w_iq, w_uk, w_uv, w_out, rel_bias, ln_g, ln_b, batch, seq):
    ql_dim = q_norm.shape[0]
    kvl = kv_norm.shape[0]
    topk = min(IDX_TOPK, seq // 4)
    cq, ckv, ckvt, kidx, widx_t = _proj1(x2, w_in, q_norm, kv_norm, ql_dim, kvl, batch, seq)
    qlt, qidx = _qside(cq, w_uq.astype(BF16), w_uk.astype(BF16), w_iq.astype(BF16))
    mask_t = _indexer(qidx, widx_t, kidx, batch, seq, topk)
    bias_tiles = _bias_tiles(rel_bias, DSA_TQ, DSA_TK)
    w_uvt = jnp.swapaxes(w_uv, 1, 2).astype(BF16)
    o = _dsa_attention(qlt, ckv, ckvt, mask_t, bias_tiles, w_uvt, batch, seq)
    return _outproj_ln(o, w_out.astype(BF16), x2, ln_g, ln_b)


def kernel(x, a_w_in, a_b_f, a_w_out, b_w_in, b_q_norm, b_kv_norm, b_w_uq, b_w_iq, b_w_uk, b_w_uv, b_w_out,
           rel_bias, ln1_g, ln1_b, ln2_g, ln2_b, router_w, router_b, w_gate, w_up, w_down, sh_gate, sh_up,
           sh_down):
    batch, seq, d = x.shape
    x2 = x.reshape(batch * seq, d)
    x2 = _fox_layer(x2, a_w_in[0], a_b_f[0], a_w_out[0], ln1_g[0], ln1_b[0], batch, seq)
    x2 = _moe_layer(x2, router_w[0], router_b[0], w_gate, w_up, w_down, 0, sh_gate[0], sh_up[0],
                    sh_down[0], ln2_g[0], ln2_b[0])
    x2 = _dsa_layer(x2, b_w_in[0], b_q_norm[0], b_kv_norm[0], b_w_uq[0], b_w_iq[0], b_w_uk[0], b_w_uv[0],
                    b_w_out[0], rel_bias, ln1_g[1], ln1_b[1], batch, seq)
    x2 = _moe_layer(x2, router_w[1], router_b[1], w_gate, w_up, w_down, 1, sh_gate[1], sh_up[1],
                    sh_down[1], ln2_g[1], ln2_b[1])
    return x2.reshape(batch, seq, d)
```

```python
import functools
import math

import numpy as np
import jax
import jax.numpy as jnp
from jax import lax
from jax.experimental import pallas as pl
from jax.experimental.pallas import tpu as pltpu

BF16 = jnp.bfloat16
F32 = jnp.float32
I32 = jnp.int32
U32 = jnp.uint32

A_HEADS = 16
A_HEAD_DIM = 128
B_HEADS = 16
B_HEAD_DIM = 128
B_V_DIM = 128
IDX_HEADS = 16
IDX_DIM = 64
IDX_TOPK = 256
CHUNK = 64
REL_BUCKETS = 32
REL_MAX_DIST = 128
N_EXPERTS = 64
TOP_K = 8
ROUTED_SCALE = 2.5
DEPTH = 2
DEEPNORM_ALPHA = (2 * DEPTH) ** 0.25
LN_EPS = 1e-5
RMS_EPS = 1e-6

LANES = 128
SLAB = 8
HALF_BITS = 16
HIGH_HALF_MASK = 0xFFFF0000
VMEM_LIMIT = 56 * 1024 * 1024
LOG2E = math.log2(math.e)
NEG_BIG = -1e30

EXPERT_ROWS = 512
ROW_TILE = 512
PROJ0_TM, PROJ0_TN = 1024, 1024
CUMSUM_T = 256
DISPATCH_TT = 512
COMBINE_TT = 256
FOX_T = 512
FOX_HEADS_PER_STEP = 8
DSA_TQ = 256
DSA_TK = 512
IDX_TQ = 256
IDX_TKC = 512


def _cparams(sem, vmem=VMEM_LIMIT):
    return pltpu.CompilerParams(dimension_semantics=sem, vmem_limit_bytes=vmem)


def _dot(a, b):
    return jnp.dot(a, b, preferred_element_type=F32)


def _dot_nt(a, b):
    return lax.dot_general(a, b, (((1,), (1,)), ((), ())), preferred_element_type=F32)


def _lane_tile(a, width):
    return a if width == LANES else jnp.concatenate([a] * (width // LANES), axis=1)


def _split2(a):
    hi = a.astype(BF16)
    lo = (a - hi.astype(F32)).astype(BF16)
    return hi, lo


def _split3(a):
    hi = a.astype(BF16)
    r = a - hi.astype(F32)
    mid = r.astype(BF16)
    lo = (r - mid.astype(F32)).astype(BF16)
    return hi, mid, lo


def _dot_x3(a, b):
    ah, al = _split2(a)
    bh, bl = _split2(b)
    return _dot(ah, bh) + (_dot(ah, bl) + _dot(al, bh))


def _layer_norm_rows(z, g, b):
    mu = jnp.mean(z, axis=-1, keepdims=True)
    d = z - mu
    var = jnp.mean(d * d, axis=-1, keepdims=True)
    return d * lax.rsqrt(var + LN_EPS) * g + b


def _rms_norm_rows(z, g):
    ms = jnp.mean(z * z, axis=-1, keepdims=True)
    return z * lax.rsqrt(ms + RMS_EPS) * g


def _proj0_kernel(x_ref, w_ref, wf_ref, qkv_ref, fl_ref, xb_sc, *, n_q_blocks, q_scale):
    j = pl.program_id(1)

    @pl.when(j == 0)
    def _():
        x = x_ref[...]
        xb_sc[...] = x.astype(BF16)
        fl_ref[...] = _dot_x3(x, wf_ref[...])

    acc = _dot(xb_sc[...], w_ref[...])
    scale = jnp.where(j < n_q_blocks, q_scale, 1.0).astype(F32)
    qkv_ref[...] = (acc * scale).astype(BF16)


def _proj0(x2, w_qkv_b, w_f):
    n, d = x2.shape
    nout = w_qkv_b.shape[1]
    tm, tn = PROJ0_TM, PROJ0_TN
    dq = A_HEADS * A_HEAD_DIM
    kern = functools.partial(_proj0_kernel, n_q_blocks=dq // tn, q_scale=A_HEAD_DIM ** -0.5 * LOG2E)
    return pl.pallas_call(
        kern,
        out_shape=(jax.ShapeDtypeStruct((n, nout), BF16),
                   jax.ShapeDtypeStruct((n, A_HEADS), F32)),
        grid=(n // tm, nout // tn),
        in_specs=[pl.BlockSpec((tm, d), lambda i, j: (i, 0)),
                  pl.BlockSpec((d, tn), lambda i, j: (0, j)),
                  pl.BlockSpec((d, A_HEADS), lambda i, j: (0, 0))],
        out_specs=(pl.BlockSpec((tm, tn), lambda i, j: (i, j)),
                   pl.BlockSpec((tm, A_HEADS), lambda i, j: (i, 0))),
        scratch_shapes=[pltpu.VMEM((tm, d), BF16)],
        compiler_params=_cparams(("arbitrary", "arbitrary")),
        name="fox_proj",
    )(x2, w_qkv_b, w_f)


def _forget_cumsum_kernel(fl_ref, bf_ref, f_ref, carry_sc, *, t):
    @pl.when(pl.program_id(1) == 0)
    def _():
        carry_sc[...] = jnp.zeros_like(carry_sc)

    z = fl_ref[...] + bf_ref[...]
    logf = jnp.minimum(z, 0.0) - jnp.log1p(jnp.exp(-jnp.abs(z)))
    row = lax.broadcasted_iota(I32, (t, t), 0)
    col = lax.broadcasted_iota(I32, (t, t), 1)
    tri = (col <= row).astype(BF16)
    hi, mid, lo = _split3(logf)
    cs = _dot(tri, hi) + (_dot(tri, mid) + _dot(tri, lo)) + carry_sc[...]
    f_ref[...] = cs * LOG2E
    carry_sc[...] = cs[t - 1:t, :]


def _forget_cumsum(fl, b_f, batch, seq):
    t = CUMSUM_T
    nb = seq // t
    return pl.pallas_call(
        functools.partial(_forget_cumsum_kernel, t=t),
        out_shape=jax.ShapeDtypeStruct(fl.shape, F32),
        grid=(batch, nb),
        in_specs=[pl.BlockSpec((t, A_HEADS), lambda b, i: (b * nb + i, 0)),
                  pl.BlockSpec((1, A_HEADS), lambda b, i: (0, 0))],
        out_specs=pl.BlockSpec((t, A_HEADS), lambda b, i: (b * nb + i, 0)),
        scratch_shapes=[pltpu.VMEM((1, A_HEADS), F32)],
        compiler_params=_cparams(("arbitrary", "arbitrary")),
        name="fox_forget_cumsum",
    )(fl, b_f.reshape(1, A_HEADS))


def _fox_attn_kernel(qi_ref, kj_ref, q_ref, k_ref, v_ref, fk_ref, o_ref, m_sc, l_sc, acc_sc, *, tq, tk, hp):
    p = pl.program_id(2)
    i = qi_ref[p]
    j = kj_ref[p]
    dh = A_HEAD_DIM

    @pl.when(j == 0)
    def _():
        m_sc[...] = jnp.full_like(m_sc, -jnp.inf)
        l_sc[...] = jnp.zeros_like(l_sc)
        acc_sc[...] = jnp.zeros_like(acc_sc)

    def step(diag):
        for hh in range(hp):
            cols = slice(hh * dh, (hh + 1) * dh)
            s = _dot_nt(q_ref[:, cols], k_ref[:, cols]) - fk_ref[hh]
            if diag:
                row = lax.broadcasted_iota(I32, (tq, tk), 0)
                col = lax.broadcasted_iota(I32, (tq, tk), 1)
                s = jnp.where(col <= row, s, -jnp.inf)
            m_prev = m_sc[hh]
            m_new = jnp.maximum(m_prev, jnp.max(s, axis=1, keepdims=True))
            alpha = jnp.exp2(m_prev - m_new)
            pexp = jnp.exp2(s - _lane_tile(m_new, tk))
            l_sc[hh] = alpha * l_sc[hh] + jnp.sum(pexp, axis=1, keepdims=True)
            acc_sc[hh] = alpha * acc_sc[hh] + _dot(pexp.astype(BF16), v_ref[:, cols])
            m_sc[hh] = m_new

    @pl.when(j < i)
    def _():
        step(False)

    @pl.when(j == i)
    def _():
        step(True)
        for hh in range(hp):
            o_ref[:, hh * dh:(hh + 1) * dh] = (acc_sc[hh] / l_sc[hh]).astype(o_ref.dtype)


def _fox_attention(qkv, f_rows, batch, seq):
    t = min(FOX_T, seq)
    nq = seq // t
    qi = np.concatenate([np.full(i + 1, i) for i in range(nq)]).astype(np.int32)
    kj = np.concatenate([np.arange(i + 1) for i in range(nq)]).astype(np.int32)
    hp = FOX_HEADS_PER_STEP
    hg = A_HEADS // hp
    dh = A_HEAD_DIM
    w = hp * dh
    kern = functools.partial(_fox_attn_kernel, tq=t, tk=t, hp=hp)
    grid_spec = pltpu.PrefetchScalarGridSpec(
        num_scalar_prefetch=2,
        grid=(batch, hg, len(qi)),
        in_specs=[
            pl.BlockSpec((t, w), lambda b, h, p, qi, kj: (b * nq + qi[p], h)),
            pl.BlockSpec((t, w), lambda b, h, p, qi, kj: (b * nq + kj[p], hg + h)),
            pl.BlockSpec((t, w), lambda b, h, p, qi, kj: (b * nq + kj[p], 2 * hg + h)),
            pl.BlockSpec((hp, 1, t), lambda b, h, p, qi, kj: (b * hg + h, 0, kj[p])),
        ],
        out_specs=pl.BlockSpec((t, w), lambda b, h, p, qi, kj: (b * nq + qi[p], h)),
        scratch_shapes=[pltpu.VMEM((hp, t, dh), F32), pltpu.VMEM((hp, t, dh), F32),
                        pltpu.VMEM((hp, t, dh), F32)],
    )
    return pl.pallas_call(
        kern,
        out_shape=jax.ShapeDtypeStruct((batch * seq, A_HEADS * dh), BF16),
        grid_spec=grid_spec,
        compiler_params=_cparams(("arbitrary", "arbitrary", "arbitrary")),
        name="fox_attention",
    )(jnp.asarray(qi), jnp.asarray(kj), qkv, qkv, qkv, f_rows)


def _outproj_ln_kernel(o_ref, w_ref, x_ref, g_ref, b_ref, y_ref):
    z = DEEPNORM_ALPHA * x_ref[...] + _dot(o_ref[...], w_ref[...])
    y_ref[...] = _layer_norm_rows(z, g_ref[...], b_ref[...])


def _outproj_ln(o, w_b, x2, g, b):
    n, d = x2.shape
    k = o.shape[1]
    tm = ROW_TILE
    return pl.pallas_call(
        _outproj_ln_kernel,
        out_shape=jax.ShapeDtypeStruct((n, d), F32),
        grid=(n // tm,),
        in_specs=[pl.BlockSpec((tm, k), lambda i: (i, 0)),
                  pl.BlockSpec((k, d), lambda i: (0, 0)),
                  pl.BlockSpec((tm, d), lambda i: (i, 0)),
                  pl.BlockSpec((1, d), lambda i: (0, 0)),
                  pl.BlockSpec((1, d), lambda i: (0, 0))],
        out_specs=pl.BlockSpec((tm, d), lambda i: (i, 0)),
        compiler_params=_cparams(("arbitrary",)),
        name="outproj_deepnorm",
    )(o, w_b, x2, g.reshape(1, d), b.reshape(1, d))


def _fox_layer(x2, w_in, b_f, w_out, ln_g, ln_b, batch, seq):
    dq = A_HEADS * A_HEAD_DIM
    qkv, fl = _proj0(x2, w_in[:, :3 * dq].astype(BF16), w_in[:, 3 * dq:])
    f = _forget_cumsum(fl, b_f, batch, seq)
    f_rows = f.reshape(batch, seq, A_HEADS).transpose(0, 2, 1).reshape(batch * A_HEADS, 1, seq)
    o = _fox_attention(qkv, f_rows, batch, seq)
    return _outproj_ln(o, w_out.astype(BF16), x2, ln_g, ln_b)


def _router_kernel(x_ref, rw_ref, rb_ref, sel_ref, gate_ref, rank_ref, cnt_ref, carry_sc, *, tm):
    @pl.when(pl.program_id(0) == 0)
    def _():
        carry_sc[...] = jnp.zeros_like(carry_sc)

    e = N_EXPERTS
    scores = jax.nn.sigmoid(_dot_x3(x_ref[...], rw_ref[...]))
    lane = lax.broadcasted_iota(I32, (tm, e), 1)
    slot = lax.broadcasted_iota(I32, (tm, TOP_K), 1)
    work = scores + rb_ref[...]
    chosen = jnp.zeros((tm, e), F32)
    sel = jnp.zeros((tm, TOP_K), I32)
    gate = jnp.zeros((tm, TOP_K), F32)
    idxs = []
    for k in range(TOP_K):
        mx = jnp.max(work, axis=1, keepdims=True)
        idx = jnp.min(jnp.where(work == mx, lane, e), axis=1, keepdims=True)
        hit = lane == idx
        gk = jnp.sum(jnp.where(hit, scores, 0.0), axis=1, keepdims=True)
        work = jnp.where(hit, -jnp.inf, work)
        chosen = jnp.where(hit, 1.0, chosen)
        sel = jnp.where(slot == k, idx, sel)
        gate = jnp.where(slot == k, gk, gate)
        idxs.append(idx)
    gate = gate / jnp.sum(gate, axis=1, keepdims=True) * ROUTED_SCALE

    row = lax.broadcasted_iota(I32, (tm, tm), 0)
    col = lax.broadcasted_iota(I32, (tm, tm), 1)
    before = _dot((col < row).astype(BF16), chosen.astype(BF16)) + carry_sc[...]
    rank = jnp.zeros((tm, TOP_K), F32)
    for k in range(TOP_K):
        rk = jnp.sum(jnp.where(lane == idxs[k], before, 0.0), axis=1, keepdims=True)
        rank = jnp.where(slot == k, rk, rank)
    total = carry_sc[...] + jnp.sum(chosen, axis=0, keepdims=True)
    carry_sc[...] = total
    sel_ref[...] = sel
    gate_ref[...] = gate
    rank_ref[...] = rank.astype(I32)
    cnt_ref[...] = total.astype(I32)


def _router(x2, rw, rb):
    n, d = x2.shape
    tm = ROW_TILE
    e = N_EXPERTS
    return pl.pallas_call(
        functools.partial(_router_kernel, tm=tm),
        out_shape=(jax.ShapeDtypeStruct((n, TOP_K), I32),
                   jax.ShapeDtypeStruct((n, TOP_K), F32),
                   jax.ShapeDtypeStruct((n, TOP_K), I32),
                   jax.ShapeDtypeStruct((1, e), I32)),
        grid=(n // tm,),
        in_specs=[pl.BlockSpec((tm, d), lambda i: (i, 0)),
                  pl.BlockSpec((d, e), lambda i: (0, 0)),
                  pl.BlockSpec((1, e), lambda i: (0, 0))],
        out_specs=(pl.BlockSpec((tm, TOP_K), lambda i: (i, 0)),
                   pl.BlockSpec((tm, TOP_K), lambda i: (i, 0)),
                   pl.BlockSpec((tm, TOP_K), lambda i: (i, 0)),
                   pl.BlockSpec((1, e), lambda i: (0, 0))),
        scratch_shapes=[pltpu.VMEM((1, e), F32)],
        compiler_params=_cparams(("arbitrary",)),
        name="moe_router",
    )(x2, rw, rb.reshape(1, e))


def _pack_bf16_pairs(y):
    w = y.shape[1] // 2
    lo = lax.bitcast_convert_type(y[:, :w].astype(BF16).astype(F32), U32)
    hi = lax.bitcast_convert_type(y[:, w:].astype(BF16).astype(F32), U32)
    return lax.shift_right_logical(lo, jnp.uint32(HALF_BITS)) | hi


def _low_half_f32(words):
    return lax.bitcast_convert_type(lax.shift_left(words, jnp.uint32(HALF_BITS)), F32)


def _high_half_f32(words):
    return lax.bitcast_convert_type(words & jnp.uint32(HIGH_HALF_MASK), F32)


def _unpack_bf16_pairs(words):
    return _low_half_f32(words).astype(BF16), _high_half_f32(words).astype(BF16)


def _slab_load(ref, rows):
    return jnp.concatenate([ref[pl.ds(c, rows, stride=SLAB), :] for c in range(SLAB)], axis=1)


def _slab_store(ref, rows, val):
    for c in range(SLAB):
        ref[pl.ds(c, rows, stride=SLAB), :] = val[:, c * LANES:(c + 1) * LANES]


def _dispatch_kernel(pad_ref, dest_hbm, x_ref, xs_hbm, dest_sm, slab_sc, zero_sc, sem_idx, sem_zero, sem_row,
                     *, tt, n_tiles):
    i = pl.program_id(0)

    def zero_copy(e):
        start = pl.multiple_of(pad_ref[e] * SLAB, SLAB)
        return pltpu.make_async_copy(zero_sc, xs_hbm.at[pl.ds(start, EXPERT_ROWS * SLAB)], sem_zero)

    @pl.when(i == 0)
    def _():
        zero_sc[...] = jnp.zeros_like(zero_sc)

        def start(e, c):
            @pl.when(pad_ref[N_EXPERTS + e] > 0)
            def _():
                zero_copy(e).start()
            return c

        def wait(e, c):
            @pl.when(pad_ref[N_EXPERTS + e] > 0)
            def _():
                zero_copy(e).wait()
            return c

        lax.fori_loop(0, N_EXPERTS, start, 0)
        lax.fori_loop(0, N_EXPERTS, wait, 0)

    slot = lax.rem(i, 2)
    idx_copy = pltpu.make_async_copy(dest_hbm.at[i], dest_sm, sem_idx)
    idx_copy.start()

    def row_copy(s, t, dst_row):
        src = pl.multiple_of(t * SLAB, SLAB)
        dst = pl.multiple_of(dst_row * SLAB, SLAB)
        return pltpu.make_async_copy(slab_sc.at[s, pl.ds(src, SLAB)], xs_hbm.at[pl.ds(dst, SLAB)], sem_row.at[s])

    def wait_rows(s):
        def wait(t, c):
            for k in range(TOP_K):
                row_copy(s, 0, 0).wait()
            return c
        lax.fori_loop(0, tt, wait, 0, unroll=2)

    @pl.when(i >= 2)
    def _():
        wait_rows(slot)

    _slab_store(slab_sc.at[slot], tt, _pack_bf16_pairs(x_ref[...]))
    idx_copy.wait()

    def start(t, c):
        for k in range(TOP_K):
            row_copy(slot, t, dest_sm[t * TOP_K + k]).start(priority=k % 2)
        return c

    lax.fori_loop(0, tt, start, 0, unroll=2)

    @pl.when(i == n_tiles - 1)
    def _():
        if n_tiles > 1:
            wait_rows(1 - slot)
        wait_rows(slot)


def _dispatch(pad_info, dest, x2, p_rows):
    n, d = x2.shape
    tt = DISPATCH_TT
    assert d == 2 * SLAB * LANES
    dest2 = dest.reshape(n // tt, tt * TOP_K)
    grid_spec = pltpu.PrefetchScalarGridSpec(
        num_scalar_prefetch=1,
        grid=(n // tt,),
        in_specs=[pl.BlockSpec(memory_space=pl.ANY),
                  pl.BlockSpec((tt, d), lambda i, pad: (i, 0))],
        out_specs=pl.BlockSpec(memory_space=pl.ANY),
        scratch_shapes=[pltpu.SMEM((tt * TOP_K,), I32),
                        pltpu.VMEM((2, tt * SLAB, LANES), U32),
                        pltpu.VMEM((EXPERT_ROWS * SLAB, LANES), U32),
                        pltpu.SemaphoreType.DMA, pltpu.SemaphoreType.DMA, pltpu.SemaphoreType.DMA((2,))],
    )
    return pl.pallas_call(
        functools.partial(_dispatch_kernel, tt=tt, n_tiles=n // tt),
        out_shape=jax.ShapeDtypeStruct((p_rows * SLAB, LANES), U32),
        grid_spec=grid_spec,
        compiler_params=_cparams(("arbitrary",)),
        name="moe_dispatch",
    )(pad_info, dest2, x2)


def _expert_mlp_kernel(be_ref, nu_ref, slot_ref, next_ref, xs_ref, wg_hbm, wu_hbm, wd_hbm, ys_ref,
                       wg_f, wu_f, wd_f, wg_sc, wu_sc, wd_sc, sem, *, layer):
    b = pl.program_id(0)
    e = be_ref[b]
    first = jnp.logical_or(b == 0, e != be_ref[jnp.maximum(b - 1, 0)])
    active = b < nu_ref[0]

    def fetch(expert, s):
        return (pltpu.make_async_copy(wg_hbm.at[layer, expert], wg_f.at[s], sem.at[s]),
                pltpu.make_async_copy(wu_hbm.at[layer, expert], wu_f.at[s], sem.at[s]),
                pltpu.make_async_copy(wd_hbm.at[layer, expert], wd_f.at[s], sem.at[s]))

    @pl.when(jnp.logical_and(active, first))
    def _():
        s = slot_ref[e]

        @pl.when(b == 0)
        def _():
            for c in fetch(e, s):
                c.start()

        for c in fetch(e, s):
            c.wait()
        nxt = next_ref[e]

        @pl.when(nxt >= 0)
        def _():
            for c in fetch(nxt, 1 - s):
                c.start()

        wg_sc[...] = wg_f[s].astype(BF16)
        wu_sc[...] = wu_f[s].astype(BF16)
        wd_sc[...] = wd_f[s].astype(BF16)

    @pl.when(active)
    def _():
        r = EXPERT_ROWS
        x = jnp.concatenate(_unpack_bf16_pairs(_slab_load(xs_ref, r)), axis=1)
        g = _dot(x, wg_sc[...])
        u = _dot(x, wu_sc[...])
        h = (g * jax.nn.sigmoid(g) * u).astype(BF16)
        _slab_store(ys_ref, r, _pack_bf16_pairs(_dot(h, wd_sc[...])))


def _expert_mlp(blk_e, n_used, w_slot, w_next, xs, w_gate, w_up, w_down, layer):
    r = EXPERT_ROWS
    nblk = xs.shape[0] // (r * SLAB)
    d, de = w_gate.shape[2], w_gate.shape[3]

    def row_map(b, be, nu, ws, wn):
        return (jnp.minimum(b, nu[0] - 1), 0)

    grid_spec = pltpu.PrefetchScalarGridSpec(
        num_scalar_prefetch=4,
        grid=(nblk,),
        in_specs=[pl.BlockSpec((r * SLAB, LANES), row_map),
                  pl.BlockSpec(memory_space=pl.ANY),
                  pl.BlockSpec(memory_space=pl.ANY),
                  pl.BlockSpec(memory_space=pl.ANY)],
        out_specs=pl.BlockSpec((r * SLAB, LANES), row_map),
        scratch_shapes=[pltpu.VMEM((2, d, de), F32), pltpu.VMEM((2, d, de), F32), pltpu.VMEM((2, de, d), F32),
                        pltpu.VMEM((d, de), BF16), pltpu.VMEM((d, de), BF16), pltpu.VMEM((de, d), BF16),
                        pltpu.SemaphoreType.DMA((2,))],
    )
    return pl.pallas_call(
        functools.partial(_expert_mlp_kernel, layer=layer),
        out_shape=jax.ShapeDtypeStruct(xs.shape, U32),
        grid_spec=grid_spec,
        compiler_params=_cparams(("arbitrary",)),
        name="moe_experts",
    )(blk_e, n_used, w_slot, w_next, xs, w_gate, w_up, w_down)


def _combine_kernel(dest_hbm, ys_hbm, x_ref, gate_ref, sg_ref, su_ref, sd_ref, g_ref, b_ref,
                    y_ref, dest_sm, rows_sc, sem_idx, sem_row, *, tt, n_tiles):
    i = pl.program_id(0)
    slot = lax.rem(i, 2)
    n_rows = tt * TOP_K

    def idx_copy(tile, s):
        return pltpu.make_async_copy(dest_hbm.at[tile], dest_sm.at[pl.ds(pl.multiple_of(s * n_rows, n_rows), n_rows)],
                                     sem_idx.at[s])

    def row_copy(s, t, k, src_row):
        return pltpu.make_async_copy(ys_hbm.at[pl.ds(pl.multiple_of(src_row * SLAB, SLAB), SLAB)],
                                     rows_sc.at[s, k, pl.ds(pl.multiple_of(t * SLAB, SLAB), SLAB)], sem_row.at[s])

    def start_rows(s):
        def start(t, c):
            for k in range(TOP_K):
                row_copy(s, t, k, dest_sm[s * n_rows + t * TOP_K + k]).start(priority=k % 2)
            return c
        lax.fori_loop(0, tt, start, 0, unroll=2)

    @pl.when(i == 0)
    def _():
        idx_copy(0, 0).start()
        idx_copy(0, 0).wait()
        start_rows(0)
        if n_tiles > 1:
            idx_copy(1, 1).start()

    @pl.when(i + 1 < n_tiles)
    def _():
        idx_copy(i + 1, 1 - slot).wait()
        start_rows(1 - slot)

    @pl.when(i + 2 < n_tiles)
    def _():
        idx_copy(i + 2, slot).start()

    x = x_ref[...]
    xb = x.astype(BF16)
    g = _dot(xb, sg_ref[...])
    u = _dot(xb, su_ref[...])
    h = (g * jax.nn.sigmoid(g) * u).astype(BF16)
    z = DEEPNORM_ALPHA * x + _dot(h, sd_ref[...])

    def wait(t, c):
        for k in range(TOP_K):
            row_copy(slot, 0, 0, 0).wait()
        return c
    lax.fori_loop(0, tt, wait, 0, unroll=2)

    gate = gate_ref[...]
    lo_pieces, hi_pieces = [], []
    for c in range(SLAB):
        acc_lo = jnp.zeros((tt, LANES), F32)
        acc_hi = jnp.zeros((tt, LANES), F32)
        for k in range(TOP_K):
            words = rows_sc[slot, k, pl.ds(c, tt, stride=SLAB), :]
            gk = gate[:, k:k + 1]
            acc_lo = acc_lo + gk * _low_half_f32(words)
            acc_hi = acc_hi + gk * _high_half_f32(words)
        lo_pieces.append(acc_lo)
        hi_pieces.append(acc_hi)
    routed = jnp.concatenate(lo_pieces + hi_pieces, axis=1)
    y_ref[...] = _layer_norm_rows(z + routed, g_ref[...], b_ref[...])


def _combine(dest, ys, x2, gate, sg_b, su_b, sd_b, ln_g, ln_b):
    n, d = x2.shape
    ds_ = sg_b.shape[1]
    tt = COMBINE_TT
    n_tiles = n // tt
    dest2 = dest.reshape(n_tiles, tt * TOP_K)
    return pl.pallas_call(
        functools.partial(_combine_kernel, tt=tt, n_tiles=n_tiles),
        out_shape=jax.ShapeDtypeStruct((n, d), F32),
        grid=(n_tiles,),
        in_specs=[pl.BlockSpec(memory_space=pl.ANY),
                  pl.BlockSpec(memory_space=pl.ANY),
                  pl.BlockSpec((tt, d), lambda i: (i, 0)),
                  pl.BlockSpec((tt, TOP_K), lambda i: (i, 0)),
                  pl.BlockSpec((d, ds_), lambda i: (0, 0)),
                  pl.BlockSpec((d, ds_), lambda i: (0, 0)),
                  pl.BlockSpec((ds_, d), lambda i: (0, 0)),
                  pl.BlockSpec((1, d), lambda i: (0, 0)),
                  pl.BlockSpec((1, d), lambda i: (0, 0))],
        out_specs=pl.BlockSpec((tt, d), lambda i: (i, 0)),
        scratch_shapes=[pltpu.SMEM((2 * tt * TOP_K,), I32),
                        pltpu.VMEM((2, TOP_K, tt * SLAB, LANES), U32),
                        pltpu.SemaphoreType.DMA((2,)), pltpu.SemaphoreType.DMA((2,))],
        compiler_params=_cparams(("arbitrary",)),
        name="moe_combine",
    )(dest2, ys, x2, gate, sg_b, su_b, sd_b, ln_g.reshape(1, d), ln_b.reshape(1, d))


def _moe_layer(x2, rw, rb, w_gate, w_up, w_down, layer, sh_gate, sh_up, sh_down, ln_g, ln_b):
    n, d = x2.shape
    e = N_EXPERTS
    r = EXPERT_ROWS
    sel, gate, rank, counts = _router(x2, rw, rb)
    counts = counts.reshape(e)
    padded = (counts + r - 1) // r * r
    pend = jnp.cumsum(padded)
    pstart = pend - padded
    dest = rank + jnp.sum(jnp.where(sel[..., None] == jnp.arange(e, dtype=I32), pstart.astype(I32), 0), axis=-1)
    p_rows = n * TOP_K + e * r
    nblk = p_rows // r
    blk_start = jnp.arange(nblk, dtype=I32) * r
    blk_e = jnp.minimum(jnp.sum((pend[None, :] <= blk_start[:, None]).astype(I32), axis=1), e - 1)
    n_used = (pend[-1] // r).astype(I32).reshape(1)
    pad_info = jnp.concatenate([jnp.maximum(pend - r, 0), padded]).astype(I32)
    used = padded > 0
    eid = jnp.arange(e, dtype=I32)
    w_slot = ((jnp.cumsum(used.astype(I32)) - 1) & 1).astype(I32)
    later = lax.cummin(jnp.where(used, eid, e), reverse=True)
    w_next = jnp.concatenate([later[1:], jnp.full((1,), e, I32)])
    w_next = jnp.where(w_next < e, w_next, -1).astype(I32)
    xs = _dispatch(pad_info, dest.astype(I32), x2, p_rows)
    ys = _expert_mlp(blk_e, n_used, w_slot, w_next, xs, w_gate, w_up, w_down, layer)
    return _combine(dest.astype(I32), ys, x2, gate, sh_gate.astype(BF16), sh_up.astype(BF16),
                    sh_down.astype(BF16), ln_g, ln_b)


FAR_BUCKET = REL_BUCKETS // 2 - 1


def _n_near_offsets(tq, tk):
    return (tk + REL_MAX_DIST - 1 + tq - 1) // tq


def _t5_bucket(rel):
    half = REL_BUCKETS // 2
    max_exact = half // 2
    n = jnp.abs(rel)
    large = max_exact + (jnp.log(jnp.maximum(n, 1).astype(F32) / max_exact)
                         / math.log(REL_MAX_DIST / max_exact) * (half - max_exact)).astype(I32)
    large = jnp.minimum(large, half - 1)
    return jnp.where(rel > 0, half, 0) + jnp.where(n < max_exact, n, large)


def _bias_tiles_kernel(rb_ref, out_ref, *, tq, tk):
    offset = -tq * pl.program_id(0)
    key = lax.broadcasted_iota(I32, (tk, tq), 0)
    qry = lax.broadcasted_iota(I32, (tk, tq), 1)
    bucket = _t5_bucket(offset + key - qry)
    for h in range(B_HEADS):
        far = rb_ref[FAR_BUCKET * B_HEADS + h]
        acc = jnp.zeros((tk, tq), F32)
        for b in range(REL_BUCKETS):
            acc = jnp.where(bucket == b, (rb_ref[b * B_HEADS + h] - far) * LOG2E, acc)
        out_ref[0, h] = acc


def _bias_tiles(rel_bias, tq, tk):
    n_off = _n_near_offsets(tq, tk)
    grid_spec = pltpu.PrefetchScalarGridSpec(
        num_scalar_prefetch=1,
        grid=(n_off,),
        in_specs=[],
        out_specs=pl.BlockSpec((1, B_HEADS, tk, tq), lambda i, rb: (i, 0, 0, 0)),
    )
    return pl.pallas_call(
        functools.partial(_bias_tiles_kernel, tq=tq, tk=tk),
        out_shape=jax.ShapeDtypeStruct((n_off, B_HEADS, tk, tq), F32),
        grid_spec=grid_spec,
        compiler_params=_cparams(("arbitrary",)),
        name="dsa_bias_tiles",
    )(rel_bias.reshape(-1))


def _proj1_kernel(x_ref, w_ref, wit_ref, qn_ref, kvn_ref, cq_ref, ckv_ref, ckvt_ref, ki_ref, wi_ref,
                  *, o1, o2, o3, wscale):
    x = x_ref[...]
    proj = _dot(x.astype(BF16), w_ref[...])
    cq_ref[...] = _rms_norm_rows(proj[:, :o1], qn_ref[...]).astype(BF16)
    ckv = _rms_norm_rows(proj[:, o1:o2], kvn_ref[...])
    ckv_ref[...] = ckv.astype(BF16)
    ckvt_ref[...] = ckv.T.astype(BF16)
    ki_ref[...] = proj[:, o2:o3].astype(BF16)
    wi_ref[...] = _dot_nt(wit_ref[...], x) * wscale


def _proj1(x2, w_in, q_norm, kv_norm, ql, kvl, batch, seq):
    n, d = x2.shape
    o1, o2, o3 = ql, ql + kvl, ql + kvl + IDX_DIM
    tm = ROW_TILE
    nt = seq // tm
    w_main = w_in[:, :o3].astype(BF16)
    w_idx_t = w_in[:, o3:].T
    kern = functools.partial(_proj1_kernel, o1=o1, o2=o2, o3=o3,
                             wscale=(IDX_HEADS ** -0.5) * (IDX_DIM ** -0.5))
    return pl.pallas_call(
        kern,
        out_shape=(jax.ShapeDtypeStruct((n, ql), BF16), jax.ShapeDtypeStruct((n, kvl), BF16),
                   jax.ShapeDtypeStruct((batch * kvl, seq), BF16),
                   jax.ShapeDtypeStruct((n, IDX_DIM), BF16), jax.ShapeDtypeStruct((IDX_HEADS, n), F32)),
        grid=(n // tm,),
        in_specs=[pl.BlockSpec((tm, d), lambda i: (i, 0)),
                  pl.BlockSpec((d, o3), lambda i: (0, 0)),
                  pl.BlockSpec((IDX_HEADS, d), lambda i: (0, 0)),
                  pl.BlockSpec((1, ql), lambda i: (0, 0)),
                  pl.BlockSpec((1, kvl), lambda i: (0, 0))],
        out_specs=(pl.BlockSpec((tm, ql), lambda i: (i, 0)), pl.BlockSpec((tm, kvl), lambda i: (i, 0)),
                   pl.BlockSpec((kvl, tm), lambda i: (i // nt, i % nt)),
                   pl.BlockSpec((tm, IDX_DIM), lambda i: (i, 0)), pl.BlockSpec((IDX_HEADS, tm), lambda i: (0, i))),
        compiler_params=_cparams(("arbitrary",)),
        name="dsa_proj",
    )(x2, w_main, w_idx_t, q_norm.reshape(1, ql), kv_norm.reshape(1, kvl))


def _qside_kernel(cq_ref, wuq_ref, wuk_ref, wiq_ref, ql_ref, qi_ref, *, scale):
    cq = cq_ref[...]
    q = _dot(cq, wuq_ref[...]).astype(BF16)
    for h in range(B_HEADS):
        qh = q[:, h * B_HEAD_DIM:(h + 1) * B_HEAD_DIM]
        ql_ref[h] = (_dot_nt(wuk_ref[h], qh) * scale).astype(BF16)
    qi_ref[...] = _dot(cq, wiq_ref[...]).astype(BF16)


def _qside(cq, w_uq_b, w_uk_b, w_iq_b):
    n, ql = cq.shape
    kvl = w_uk_b.shape[1]
    tm = ROW_TILE
    kern = functools.partial(_qside_kernel, scale=B_HEAD_DIM ** -0.5 * LOG2E)
    return pl.pallas_call(
        kern,
        out_shape=(jax.ShapeDtypeStruct((B_HEADS, kvl, n), BF16),
                   jax.ShapeDtypeStruct((n, IDX_HEADS * IDX_DIM), BF16)),
        grid=(n // tm,),
        in_specs=[pl.BlockSpec((tm, ql), lambda i: (i, 0)),
                  pl.BlockSpec(w_uq_b.shape, lambda i: (0, 0)),
                  pl.BlockSpec(w_uk_b.shape, lambda i: (0, 0, 0)),
                  pl.BlockSpec(w_iq_b.shape, lambda i: (0, 0))],
        out_specs=(pl.BlockSpec((B_HEADS, kvl, tm), lambda i: (0, 0, i)),
                   pl.BlockSpec((tm, IDX_HEADS * IDX_DIM), lambda i: (i, 0))),
        compiler_params=_cparams(("arbitrary",)),
        name="dsa_qside",
    )(cq, w_uq_b, w_uk_b, w_iq_b)


INT_MIN = -2 ** 31
KEY_NEG_INF = (0xFF800000 ^ 0x7FFFFFFF) - 2 ** 32


def _ordered_key(v):
    bits = lax.bitcast_convert_type(v, I32)
    return bits ^ (lax.shift_right_arithmetic(bits, 31) & 0x7FFFFFFF)


def _indexer_kernel(qi_ref, wi_ref, ki_ref, mask_ref, key_sc, *, tq, tkc, topk):
    i = pl.program_id(1)
    n_chunks = (i * tq + tq + tkc - 1) // tkc
    qpos = i * tq + lax.broadcasted_iota(I32, (1, tq), 1)
    limit = (lax.shift_right_logical(qpos, CHUNK.bit_length() - 1) + 1) * CHUNK
    w = wi_ref[...]
    key_sc[...] = jnp.full(key_sc.shape, KEY_NEG_INF, I32)

    def score_chunk(c, carry):
        start = pl.multiple_of(c * tkc, tkc)
        k = ki_ref[pl.ds(start, tkc), :]
        acc = jnp.zeros((tkc, tq), F32)
        for h in range(IDX_HEADS):
            sc = _dot_nt(k, qi_ref[:, h * IDX_DIM:(h + 1) * IDX_DIM])
            acc = acc + jnp.maximum(sc, 0.0) * w[h:h + 1, :]
        kpos = start + lax.broadcasted_iota(I32, (tkc, tq), 0)
        key_sc[pl.ds(start, tkc), :] = _ordered_key(jnp.where(kpos < limit, acc, -jnp.inf))
        return carry

    lax.fori_loop(0, n_chunks, score_chunk, 0)

    def count(pred_fn):
        def body(c, acc):
            start = pl.multiple_of(c * tkc, tkc)
            hit = pred_fn(key_sc[pl.ds(start, tkc), :]).astype(I32)
            return acc + jnp.sum(hit.reshape(tkc // SLAB, SLAB, tq), axis=0)
        acc = lax.fori_loop(0, n_chunks, body, jnp.zeros((SLAB, tq), I32))
        return jnp.sum(acc, axis=0, keepdims=True)

    def bisect(it, prefix):
        cand = prefix + lax.shift_left(jnp.int32(1), 31 - it)
        cnt = count(lambda kk: kk >= cand)
        return jnp.where(cnt >= topk, cand, prefix)

    thr = lax.fori_loop(0, 32, bisect, jnp.full((1, tq), INT_MIN, I32))
    n_gt = count(lambda kk: kk > thr)
    n_eq = count(lambda kk: kk == thr)
    need = topk - n_gt
    tie_break = jnp.max(jnp.where(jnp.logical_and(thr > KEY_NEG_INF, n_eq > need), 1, 0)) > 0

    mask_ref[...] = jnp.full(mask_ref.shape, NEG_BIG, F32)

    @pl.when(jnp.logical_not(tie_break))
    def _():
        def write(c, carry):
            start = pl.multiple_of(c * tkc, tkc)
            kk = key_sc[pl.ds(start, tkc), :]
            sel = jnp.logical_and(kk >= thr, kk > KEY_NEG_INF)
            mask_ref[pl.ds(start, tkc), :] = jnp.where(sel, 0.0, NEG_BIG)
            return carry
        lax.fori_loop(0, n_chunks, write, 0)

    @pl.when(tie_break)
    def _():
        r_ = lax.broadcasted_iota(I32, (LANES, LANES), 0)
        c_ = lax.broadcasted_iota(I32, (LANES, LANES), 1)
        lower = (c_ < r_).astype(BF16)

        def write(c, seen):
            start = pl.multiple_of(c * LANES, LANES)
            kk = key_sc[pl.ds(start, LANES), :]
            eq = kk == thr
            before = seen + _dot(lower, eq.astype(BF16))
            sel = jnp.logical_or(kk > thr, jnp.logical_and(eq, before < need.astype(F32)))
            sel = jnp.logical_and(sel, kk > KEY_NEG_INF)
            mask_ref[pl.ds(start, LANES), :] = jnp.where(sel, 0.0, NEG_BIG)
            return seen + jnp.sum(eq.astype(F32), axis=0, keepdims=True)
        lax.fori_loop(0, n_chunks * (tkc // LANES), write, jnp.zeros((1, tq), F32))


def _indexer(qidx, widx_t, kidx, batch, seq, topk):
    tq = IDX_TQ
    tkc = min(IDX_TKC, seq)
    nq = seq // tq
    kern = functools.partial(_indexer_kernel, tq=tq, tkc=tkc, topk=topk)
    return pl.pallas_call(
        kern,
        out_shape=jax.ShapeDtypeStruct((batch * seq, seq), F32),
        grid=(batch, nq),
        in_specs=[pl.BlockSpec((tq, IDX_HEADS * IDX_DIM), lambda b, i: (b * nq + i, 0)),
                  pl.BlockSpec((IDX_HEADS, tq), lambda b, i: (0, b * nq + i)),
                  pl.BlockSpec((seq, IDX_DIM), lambda b, i: (b, 0))],
        out_specs=pl.BlockSpec((seq, tq), lambda b, i: (b, i)),
        scratch_shapes=[pltpu.VMEM((seq, tq), I32)],
        compiler_params=_cparams(("arbitrary", "arbitrary")),
        name="dsa_indexer",
    )(qidx, widx_t, kidx)


def _dsa_attn_kernel(qi_ref, kj_ref, qlt_ref, kv_ref, kvt_ref, mask_ref, bias_ref, wuvt_ref, o_ref,
                     m_sc, l_sc, acc_sc, *, tq, tk, n_near):
    p = pl.program_id(1)
    i = qi_ref[p]
    j = kj_ref[p]
    nh = B_HEADS
    behind = (i * tq - j * tk) // tq

    @pl.when(j == 0)
    def _():
        m_sc[...] = jnp.full_like(m_sc, -jnp.inf)
        l_sc[...] = jnp.zeros_like(l_sc)
        acc_sc[...] = jnp.zeros_like(acc_sc)

    def step(near):
        kv = kv_ref[...]
        kvt = kvt_ref[...]
        msk = mask_ref[...]
        for h in range(nh):
            s = _dot(kv, qlt_ref[h]) + msk
            if near:
                s = s + bias_ref[behind, h]
            m_prev = m_sc[h]
            m_new = jnp.maximum(m_prev, jnp.max(s, axis=0, keepdims=True))
            alpha = jnp.exp2(m_prev - m_new)
            pexp = jnp.exp2(s - m_new)
            l_sc[h] = alpha * l_sc[h] + jnp.sum(pexp, axis=0, keepdims=True)
            acc_sc[h] = alpha * acc_sc[h] + _dot(kvt, pexp.astype(BF16))
            m_sc[h] = m_new

    @pl.when(behind >= n_near)
    def _():
        step(False)

    @pl.when(behind < n_near)
    def _():
        step(True)

    @pl.when(j == (i * tq + tq - 1) // tk)
    def _():
        for h in range(nh):
            o_lat_t = (acc_sc[h] / l_sc[h]).astype(BF16)
            o_t = _dot(wuvt_ref[h], o_lat_t)
            o_ref[:, h * B_V_DIM:(h + 1) * B_V_DIM] = o_t.T.astype(o_ref.dtype)


def _dsa_attention(qlt, ckv, ckvt, mask_t, bias_tiles, w_uvt_b, batch, seq):
    tq, tk = DSA_TQ, DSA_TK
    kvl = ckv.shape[1]
    nq = seq // tq
    nk = seq // tk
    last = [(i * tq + tq - 1) // tk for i in range(nq)]
    qi = np.concatenate([np.full(last[i] + 1, i) for i in range(nq)]).astype(np.int32)
    kj = np.concatenate([np.arange(last[i] + 1) for i in range(nq)]).astype(np.int32)
    kern = functools.partial(_dsa_attn_kernel, tq=tq, tk=tk, n_near=bias_tiles.shape[0])
    grid_spec = pltpu.PrefetchScalarGridSpec(
        num_scalar_prefetch=2,
        grid=(batch, len(qi)),
        in_specs=[
            pl.BlockSpec((B_HEADS, kvl, tq), lambda b, p, qi, kj: (0, 0, b * nq + qi[p])),
            pl.BlockSpec((tk, kvl), lambda b, p, qi, kj: (b * nk + kj[p], 0)),
            pl.BlockSpec((kvl, tk), lambda b, p, qi, kj: (b, kj[p])),
            pl.BlockSpec((tk, tq), lambda b, p, qi, kj: (b * nk + kj[p], qi[p])),
            pl.BlockSpec(bias_tiles.shape, lambda b, p, qi, kj: (0, 0, 0, 0), pipeline_mode=pl.Buffered(1)),
            pl.BlockSpec(w_uvt_b.shape, lambda b, p, qi, kj: (0, 0, 0), pipeline_mode=pl.Buffered(1)),
        ],
        out_specs=pl.BlockSpec((tq, B_HEADS * B_V_DIM), lambda b, p, qi, kj: (b * nq + qi[p], 0)),
        scratch_shapes=[pltpu.VMEM((B_HEADS, 1, tq), F32), pltpu.VMEM((B_HEADS, 1, tq), F32),
                        pltpu.VMEM((B_HEADS, kvl, tq), F32)],
    )
    return pl.pallas_call(
        kern,
        out_shape=jax.ShapeDtypeStruct((batch * seq, B_HEADS * B_V_DIM), BF16),
        grid_spec=grid_spec,
        compiler_params=_cparams(("arbitrary", "arbitrary")),
        name="dsa_attention",
    )(jnp.asarray(qi), jnp.asarray(kj), qlt, ckv, ckvt, mask_t, bias_tiles, w_uvt_b)


def _dsa_layer(x2, w_in, q_norm, kv_norm, w_uq, w_iq, w_uk, w_uv, w_out, rel_bias, ln_g, ln_b, batch, seq):
    ql_dim = q_norm.shape[0]
    kvl = kv_norm.shape[0]
    topk = min(IDX_TOPK, seq // 4)
    cq, ckv, ckvt, kidx, widx_t = _proj1(x2, w_in, q_norm, kv_norm, ql_dim, kvl, batch, seq)
    qlt, qidx = _qside(cq, w_uq.astype(BF16), w_uk.astype(BF16), w_iq.astype(BF16))
    mask_t = _indexer(qidx, widx_t, kidx, batch, seq, topk)
    bias_tiles = _bias_tiles(rel_bias, DSA_TQ, DSA_TK)
    w_uvt = jnp.swapaxes(w_uv, 1, 2).astype(BF16)
    o = _dsa_attention(qlt, ckv, ckvt, mask_t, bias_tiles, w_uvt, batch, seq)
    return _outproj_ln(o, w_out.astype(BF16), x2, ln_g, ln_b)


def kernel(x, a_w_in, a_b_f, a_w_out, b_w_in, b_q_norm, b_kv_norm, b_w_uq, b_w_iq, b_w_uk, b_w_uv, b_w_out,
           rel_bias, ln1_g, ln1_b, ln2_g, ln2_b, router_w, router_b, w_gate, w_up, w_down, sh_gate, sh_up,
           sh_down):
    batch, seq, d = x.shape
    x2 = x.reshape(batch * seq, d)
    x2 = _fox_layer(x2, a_w_in[0], a_b_f[0], a_w_out[0], ln1_g[0], ln1_b[0], batch, seq)
    x2 = _moe_layer(x2, router_w[0], router_b[0], w_gate, w_up, w_down, 0, sh_gate[0], sh_up[0],
                    sh_down[0], ln2_g[0], ln2_b[0])
    x2 = _dsa_layer(x2, b_w_in[0], b_q_norm[0], b_kv_norm[0], b_w_uq[0], b_w_iq[0], b_w_uk[0], b_w_uv[0],
                    b_w_out[0], rel_bias, ln1_g[1], ln1_b[1], batch, seq)
    x2 = _moe_layer(x2, router_w[1], router_b[1], w_gate, w_up, w_down, 1, sh_gate[1], sh_up[1],
                    sh_down[1], ln2_g[1], ln2_b[1])
    return x2.reshape(batch, seq, d)
```

```python
import functools
import math

import numpy as np
import jax
import jax.numpy as jnp
from jax import lax
from jax.experimental import pallas as pl
from jax.experimental.pallas import tpu as pltpu

BF16 = jnp.bfloat16
F32 = jnp.float32
I32 = jnp.int32
U32 = jnp.uint32

A_HEADS = 16
A_HEAD_DIM = 128
B_HEADS = 16
B_HEAD_DIM = 128
B_V_DIM = 128
IDX_HEADS = 16
IDX_DIM = 64
IDX_TOPK = 256
CHUNK = 64
REL_BUCKETS = 32
REL_MAX_DIST = 128
N_EXPERTS = 64
TOP_K = 8
ROUTED_SCALE = 2.5
DEPTH = 2
DEEPNORM_ALPHA = (2 * DEPTH) ** 0.25
LN_EPS = 1e-5
RMS_EPS = 1e-6

LANES = 128
SLAB = 8
HALF_BITS = 16
HIGH_HALF_MASK = 0xFFFF0000
VMEM_LIMIT = 56 * 1024 * 1024
LOG2E = math.log2(math.e)
NEG_BIG = -1e30

EXPERT_ROWS = 512
ROW_TILE = 512
PROJ0_TM, PROJ0_TN = 1024, 1024
CUMSUM_T = 256
DISPATCH_TT = 512
COMBINE_TT = 256
FOX_T = 512
FOX_HEADS_PER_STEP = 16
DSA_TQ = 256
DSA_TK = 512
IDX_TQ = 256
IDX_TKC = 512


def _cparams(sem, vmem=VMEM_LIMIT):
    return pltpu.CompilerParams(dimension_semantics=sem, vmem_limit_bytes=vmem)


def _dot(a, b):
    return jnp.dot(a, b, preferred_element_type=F32)


def _dot_nt(a, b):
    return lax.dot_general(a, b, (((1,), (1,)), ((), ())), preferred_element_type=F32)


def _lane_tile(a, width):
    return a if width == LANES else jnp.concatenate([a] * (width // LANES), axis=1)


def _split2(a):
    hi = a.astype(BF16)
    lo = (a - hi.astype(F32)).astype(BF16)
    return hi, lo


def _split3(a):
    hi = a.astype(BF16)
    r = a - hi.astype(F32)
    mid = r.astype(BF16)
    lo = (r - mid.astype(F32)).astype(BF16)
    return hi, mid, lo


def _dot_x3(a, b):
    ah, al = _split2(a)
    bh, bl = _split2(b)
    return _dot(ah, bh) + (_dot(ah, bl) + _dot(al, bh))


def _layer_norm_rows(z, g, b):
    mu = jnp.mean(z, axis=-1, keepdims=True)
    d = z - mu
    var = jnp.mean(d * d, axis=-1, keepdims=True)
    return d * lax.rsqrt(var + LN_EPS) * g + b


def _rms_norm_rows(z, g):
    ms = jnp.mean(z * z, axis=-1, keepdims=True)
    return z * lax.rsqrt(ms + RMS_EPS) * g


def _proj0_kernel(x_ref, w_ref, wf_ref, qkv_ref, fl_ref, xb_sc, *, n_q_blocks, q_scale):
    j = pl.program_id(1)

    @pl.when(j == 0)
    def _():
        x = x_ref[...]
        xb_sc[...] = x.astype(BF16)
        fl_ref[...] = _dot_x3(x, wf_ref[...])

    acc = _dot(xb_sc[...], w_ref[...])
    scale = jnp.where(j < n_q_blocks, q_scale, 1.0).astype(F32)
    qkv_ref[...] = (acc * scale).astype(BF16)


def _proj0(x2, w_qkv_b, w_f):
    n, d = x2.shape
    nout = w_qkv_b.shape[1]
    tm, tn = PROJ0_TM, PROJ0_TN
    dq = A_HEADS * A_HEAD_DIM
    kern = functools.partial(_proj0_kernel, n_q_blocks=dq // tn, q_scale=A_HEAD_DIM ** -0.5 * LOG2E)
    return pl.pallas_call(
        kern,
        out_shape=(jax.ShapeDtypeStruct((n, nout), BF16),
                   jax.ShapeDtypeStruct((n, A_HEADS), F32)),
        grid=(n // tm, nout // tn),
        in_specs=[pl.BlockSpec((tm, d), lambda i, j: (i, 0)),
                  pl.BlockSpec((d, tn), lambda i, j: (0, j)),
                  pl.BlockSpec((d, A_HEADS), lambda i, j: (0, 0))],
        out_specs=(pl.BlockSpec((tm, tn), lambda i, j: (i, j)),
                   pl.BlockSpec((tm, A_HEADS), lambda i, j: (i, 0))),
        scratch_shapes=[pltpu.VMEM((tm, d), BF16)],
        compiler_params=_cparams(("arbitrary", "arbitrary")),
        name="fox_proj",
    )(x2, w_qkv_b, w_f)


def _forget_cumsum_kernel(fl_ref, bf_ref, f_ref, carry_sc, *, t):
    @pl.when(pl.program_id(1) == 0)
    def _():
        carry_sc[...] = jnp.zeros_like(carry_sc)

    z = fl_ref[...] + bf_ref[...]
    logf = jnp.minimum(z, 0.0) - jnp.log1p(jnp.exp(-jnp.abs(z)))
    row = lax.broadcasted_iota(I32, (t, t), 0)
    col = lax.broadcasted_iota(I32, (t, t), 1)
    tri = (col <= row).astype(BF16)
    hi, mid, lo = _split3(logf)
    cs = _dot(tri, hi) + (_dot(tri, mid) + _dot(tri, lo)) + carry_sc[...]
    f_ref[...] = cs * LOG2E
    carry_sc[...] = cs[t - 1:t, :]


def _forget_cumsum(fl, b_f, batch, seq):
    t = CUMSUM_T
    nb = seq // t
    return pl.pallas_call(
        functools.partial(_forget_cumsum_kernel, t=t),
        out_shape=jax.ShapeDtypeStruct(fl.shape, F32),
        grid=(batch, nb),
        in_specs=[pl.BlockSpec((t, A_HEADS), lambda b, i: (b * nb + i, 0)),
                  pl.BlockSpec((1, A_HEADS), lambda b, i: (0, 0))],
        out_specs=pl.BlockSpec((t, A_HEADS), lambda b, i: (b * nb + i, 0)),
        scratch_shapes=[pltpu.VMEM((1, A_HEADS), F32)],
        compiler_params=_cparams(("arbitrary", "arbitrary")),
        name="fox_forget_cumsum",
    )(fl, b_f.reshape(1, A_HEADS))


def _fox_attn_kernel(qi_ref, kj_ref, q_ref, k_ref, v_ref, fk_ref, o_ref, m_sc, l_sc, acc_sc, *, tq, tk, hp):
    p = pl.program_id(2)
    i = qi_ref[p]
    j = kj_ref[p]
    dh = A_HEAD_DIM

    @pl.when(j == 0)
    def _():
        m_sc[...] = jnp.full_like(m_sc, -jnp.inf)
        l_sc[...] = jnp.zeros_like(l_sc)
        acc_sc[...] = jnp.zeros_like(acc_sc)

    def step(diag):
        for hh in range(hp):
            cols = slice(hh * dh, (hh + 1) * dh)
            s = _dot_nt(q_ref[:, cols], k_ref[:, cols]) - fk_ref[hh]
            if diag:
                row = lax.broadcasted_iota(I32, (tq, tk), 0)
                col = lax.broadcasted_iota(I32, (tq, tk), 1)
                s = jnp.where(col <= row, s, -jnp.inf)
            m_prev = m_sc[hh]
            m_new = jnp.maximum(m_prev, jnp.max(s, axis=1, keepdims=True))
            alpha = jnp.exp2(m_prev - m_new)
            pexp = jnp.exp2(s - _lane_tile(m_new, tk))
            l_sc[hh] = alpha * l_sc[hh] + jnp.sum(pexp, axis=1, keepdims=True)
            acc_sc[hh] = alpha * acc_sc[hh] + _dot(pexp.astype(BF16), v_ref[:, cols])
            m_sc[hh] = m_new

    @pl.when(j < i)
    def _():
        step(False)

    @pl.when(j == i)
    def _():
        step(True)
        for hh in range(hp):
            o_ref[:, hh * dh:(hh + 1) * dh] = (acc_sc[hh] / l_sc[hh]).astype(o_ref.dtype)


def _fox_attention(qkv, f_rows, batch, seq):
    t = min(FOX_T, seq)
    nq = seq // t
    qi = np.concatenate([np.full(i + 1, i) for i in range(nq)]).astype(np.int32)
    kj = np.concatenate([np.arange(i + 1) for i in range(nq)]).astype(np.int32)
    hp = FOX_HEADS_PER_STEP
    hg = A_HEADS // hp
    dh = A_HEAD_DIM
    w = hp * dh
    kern = functools.partial(_fox_attn_kernel, tq=t, tk=t, hp=hp)
    grid_spec = pltpu.PrefetchScalarGridSpec(
        num_scalar_prefetch=2,
        grid=(batch, hg, len(qi)),
        in_specs=[
            pl.BlockSpec((t, w), lambda b, h, p, qi, kj: (b * nq + qi[p], h)),
            pl.BlockSpec((t, w), lambda b, h, p, qi, kj: (b * nq + kj[p], hg + h)),
            pl.BlockSpec((t, w), lambda b, h, p, qi, kj: (b * nq + kj[p], 2 * hg + h)),
            pl.BlockSpec((hp, 1, t), lambda b, h, p, qi, kj: (b * hg + h, 0, kj[p])),
        ],
        out_specs=pl.BlockSpec((t, w), lambda b, h, p, qi, kj: (b * nq + qi[p], h)),
        scratch_shapes=[pltpu.VMEM((hp, t, dh), F32), pltpu.VMEM((hp, t, dh), F32),
                        pltpu.VMEM((hp, t, dh), F32)],
    )
    return pl.pallas_call(
        kern,
        out_shape=jax.ShapeDtypeStruct((batch * seq, A_HEADS * dh), BF16),
        grid_spec=grid_spec,
        compiler_params=_cparams(("arbitrary", "arbitrary", "arbitrary")),
        name="fox_attention",
    )(jnp.asarray(qi), jnp.asarray(kj), qkv, qkv, qkv, f_rows)


def _outproj_ln_kernel(o_ref, w_ref, x_ref, g_ref, b_ref, y_ref):
    z = DEEPNORM_ALPHA * x_ref[...] + _dot(o_ref[...], w_ref[...])
    y_ref[...] = _layer_norm_rows(z, g_ref[...], b_ref[...])


def _outproj_ln(o, w_b, x2, g, b):
    n, d = x2.shape
    k = o.shape[1]
    tm = ROW_TILE
    return pl.pallas_call(
        _outproj_ln_kernel,
        out_shape=jax.ShapeDtypeStruct((n, d), F32),
        grid=(n // tm,),
        in_specs=[pl.BlockSpec((tm, k), lambda i: (i, 0)),
                  pl.BlockSpec((k, d), lambda i: (0, 0)),
                  pl.BlockSpec((tm, d), lambda i: (i, 0)),
                  pl.BlockSpec((1, d), lambda i: (0, 0)),
                  pl.BlockSpec((1, d), lambda i: (0, 0))],
        out_specs=pl.BlockSpec((tm, d), lambda i: (i, 0)),
        compiler_params=_cparams(("arbitrary",)),
        name="outproj_deepnorm",
    )(o, w_b, x2, g.reshape(1, d), b.reshape(1, d))


def _fox_layer(x2, w_in, b_f, w_out, ln_g, ln_b, batch, seq):
    dq = A_HEADS * A_HEAD_DIM
    qkv, fl = _proj0(x2, w_in[:, :3 * dq].astype(BF16), w_in[:, 3 * dq:])
    f = _forget_cumsum(fl, b_f, batch, seq)
    f_rows = f.reshape(batch, seq, A_HEADS).transpose(0, 2, 1).reshape(batch * A_HEADS, 1, seq)
    o = _fox_attention(qkv, f_rows, batch, seq)
    return _outproj_ln(o, w_out.astype(BF16), x2, ln_g, ln_b)


def _router_kernel(x_ref, rw_ref, rb_ref, sel_ref, gate_ref, rank_ref, cnt_ref, carry_sc, *, tm):
    @pl.when(pl.program_id(0) == 0)
    def _():
        carry_sc[...] = jnp.zeros_like(carry_sc)

    e = N_EXPERTS
    scores = jax.nn.sigmoid(_dot_x3(x_ref[...], rw_ref[...]))
    lane = lax.broadcasted_iota(I32, (tm, e), 1)
    slot = lax.broadcasted_iota(I32, (tm, TOP_K), 1)
    work = scores + rb_ref[...]
    chosen = jnp.zeros((tm, e), F32)
    sel = jnp.zeros((tm, TOP_K), I32)
    gate = jnp.zeros((tm, TOP_K), F32)
    idxs = []
    for k in range(TOP_K):
        mx = jnp.max(work, axis=1, keepdims=True)
        idx = jnp.min(jnp.where(work == mx, lane, e), axis=1, keepdims=True)
        hit = lane == idx
        gk = jnp.sum(jnp.where(hit, scores, 0.0), axis=1, keepdims=True)
        work = jnp.where(hit, -jnp.inf, work)
        chosen = jnp.where(hit, 1.0, chosen)
        sel = jnp.where(slot == k, idx, sel)
        gate = jnp.where(slot == k, gk, gate)
        idxs.append(idx)
    gate = gate / jnp.sum(gate, axis=1, keepdims=True) * ROUTED_SCALE

    row = lax.broadcasted_iota(I32, (tm, tm), 0)
    col = lax.broadcasted_iota(I32, (tm, tm), 1)
    before = _dot((col < row).astype(BF16), chosen.astype(BF16)) + carry_sc[...]
    rank = jnp.zeros((tm, TOP_K), F32)
    for k in range(TOP_K):
        rk = jnp.sum(jnp.where(lane == idxs[k], before, 0.0), axis=1, keepdims=True)
        rank = jnp.where(slot == k, rk, rank)
    total = carry_sc[...] + jnp.sum(chosen, axis=0, keepdims=True)
    carry_sc[...] = total
    sel_ref[...] = sel
    gate_ref[...] = gate
    rank_ref[...] = rank.astype(I32)
    cnt_ref[...] = total.astype(I32)


def _router(x2, rw, rb):
    n, d = x2.shape
    tm = ROW_TILE
    e = N_EXPERTS
    return pl.pallas_call(
        functools.partial(_router_kernel, tm=tm),
        out_shape=(jax.ShapeDtypeStruct((n, TOP_K), I32),
                   jax.ShapeDtypeStruct((n, TOP_K), F32),
                   jax.ShapeDtypeStruct((n, TOP_K), I32),
                   jax.ShapeDtypeStruct((1, e), I32)),
        grid=(n // tm,),
        in_specs=[pl.BlockSpec((tm, d), lambda i: (i, 0)),
                  pl.BlockSpec((d, e), lambda i: (0, 0)),
                  pl.BlockSpec((1, e), lambda i: (0, 0))],
        out_specs=(pl.BlockSpec((tm, TOP_K), lambda i: (i, 0)),
                   pl.BlockSpec((tm, TOP_K), lambda i: (i, 0)),
                   pl.BlockSpec((tm, TOP_K), lambda i: (i, 0)),
                   pl.BlockSpec((1, e), lambda i: (0, 0))),
        scratch_shapes=[pltpu.VMEM((1, e), F32)],
        compiler_params=_cparams(("arbitrary",)),
        name="moe_router",
    )(x2, rw, rb.reshape(1, e))


def _pack_bf16_pairs(y):
    w = y.shape[1] // 2
    lo = lax.bitcast_convert_type(y[:, :w].astype(BF16).astype(F32), U32)
    hi = lax.bitcast_convert_type(y[:, w:].astype(BF16).astype(F32), U32)
    return lax.shift_right_logical(lo, jnp.uint32(HALF_BITS)) | hi


def _low_half_f32(words):
    return lax.bitcast_convert_type(lax.shift_left(words, jnp.uint32(HALF_BITS)), F32)


def _high_half_f32(words):
    return lax.bitcast_convert_type(words & jnp.uint32(HIGH_HALF_MASK), F32)


def _unpack_bf16_pairs(words):
    return _low_half_f32(words).astype(BF16), _high_half_f32(words).astype(BF16)


def _slab_load(ref, rows):
    return jnp.concatenate([ref[pl.ds(c, rows, stride=SLAB), :] for c in range(SLAB)], axis=1)


def _slab_store(ref, rows, val):
    for c in range(SLAB):
        ref[pl.ds(c, rows, stride=SLAB), :] = val[:, c * LANES:(c + 1) * LANES]


def _dispatch_kernel(pad_ref, dest_hbm, x_ref, xs_hbm, dest_sm, slab_sc, zero_sc, sem_idx, sem_zero, sem_row,
                     *, tt, n_tiles):
    i = pl.program_id(0)

    def zero_copy(e):
        start = pl.multiple_of(pad_ref[e] * SLAB, SLAB)
        return pltpu.make_async_copy(zero_sc, xs_hbm.at[pl.ds(start, EXPERT_ROWS * SLAB)], sem_zero)

    @pl.when(i == 0)
    def _():
        zero_sc[...] = jnp.zeros_like(zero_sc)

        def start(e, c):
            @pl.when(pad_ref[N_EXPERTS + e] > 0)
            def _():
                zero_copy(e).start()
            return c

        def wait(e, c):
            @pl.when(pad_ref[N_EXPERTS + e] > 0)
            def _():
                zero_copy(e).wait()
            return c

        lax.fori_loop(0, N_EXPERTS, start, 0)
        lax.fori_loop(0, N_EXPERTS, wait, 0)

    slot = lax.rem(i, 2)
    idx_copy = pltpu.make_async_copy(dest_hbm.at[i], dest_sm, sem_idx)
    idx_copy.start()

    def row_copy(s, t, dst_row):
        src = pl.multiple_of(t * SLAB, SLAB)
        dst = pl.multiple_of(dst_row * SLAB, SLAB)
        return pltpu.make_async_copy(slab_sc.at[s, pl.ds(src, SLAB)], xs_hbm.at[pl.ds(dst, SLAB)], sem_row.at[s])

    def wait_rows(s):
        def wait(t, c):
            for k in range(TOP_K):
                row_copy(s, 0, 0).wait()
            return c
        lax.fori_loop(0, tt, wait, 0, unroll=2)

    @pl.when(i >= 2)
    def _():
        wait_rows(slot)

    _slab_store(slab_sc.at[slot], tt, _pack_bf16_pairs(x_ref[...]))
    idx_copy.wait()

    def start(t, c):
        for k in range(TOP_K):
            row_copy(slot, t, dest_sm[t * TOP_K + k]).start(priority=k % 2)
        return c

    lax.fori_loop(0, tt, start, 0, unroll=2)

    @pl.when(i == n_tiles - 1)
    def _():
        if n_tiles > 1:
            wait_rows(1 - slot)
        wait_rows(slot)


def _dispatch(pad_info, dest, x2, p_rows):
    n, d = x2.shape
    tt = DISPATCH_TT
    assert d == 2 * SLAB * LANES
    dest2 = dest.reshape(n // tt, tt * TOP_K)
    grid_spec = pltpu.PrefetchScalarGridSpec(
        num_scalar_prefetch=1,
        grid=(n // tt,),
        in_specs=[pl.BlockSpec(memory_space=pl.ANY),
                  pl.BlockSpec((tt, d), lambda i, pad: (i, 0))],
        out_specs=pl.BlockSpec(memory_space=pl.ANY),
        scratch_shapes=[pltpu.SMEM((tt * TOP_K,), I32),
                        pltpu.VMEM((2, tt * SLAB, LANES), U32),
                        pltpu.VMEM((EXPERT_ROWS * SLAB, LANES), U32),
                        pltpu.SemaphoreType.DMA, pltpu.SemaphoreType.DMA, pltpu.SemaphoreType.DMA((2,))],
    )
    return pl.pallas_call(
        functools.partial(_dispatch_kernel, tt=tt, n_tiles=n // tt),
        out_shape=jax.ShapeDtypeStruct((p_rows * SLAB, LANES), U32),
        grid_spec=grid_spec,
        compiler_params=_cparams(("arbitrary",)),
        name="moe_dispatch",
    )(pad_info, dest2, x2)


def _expert_mlp_kernel(be_ref, nu_ref, slot_ref, next_ref, xs_ref, wg_hbm, wu_hbm, wd_hbm, ys_ref,
                       wg_f, wu_f, wd_f, wg_sc, wu_sc, wd_sc, sem, *, layer):
    b = pl.program_id(0)
    e = be_ref[b]
    first = jnp.logical_or(b == 0, e != be_ref[jnp.maximum(b - 1, 0)])
    active = b < nu_ref[0]

    def fetch(expert, s):
        return (pltpu.make_async_copy(wg_hbm.at[layer, expert], wg_f.at[s], sem.at[s]),
                pltpu.make_async_copy(wu_hbm.at[layer, expert], wu_f.at[s], sem.at[s]),
                pltpu.make_async_copy(wd_hbm.at[layer, expert], wd_f.at[s], sem.at[s]))

    @pl.when(jnp.logical_and(active, first))
    def _():
        s = slot_ref[e]

        @pl.when(b == 0)
        def _():
            for c in fetch(e, s):
                c.start()

        for c in fetch(e, s):
            c.wait()
        nxt = next_ref[e]

        @pl.when(nxt >= 0)
        def _():
            for c in fetch(nxt, 1 - s):
                c.start()

        wg_sc[...] = wg_f[s].astype(BF16)
        wu_sc[...] = wu_f[s].astype(BF16)
        wd_sc[...] = wd_f[s].astype(BF16)

    @pl.when(active)
    def _():
        r = EXPERT_ROWS
        x = jnp.concatenate(_unpack_bf16_pairs(_slab_load(xs_ref, r)), axis=1)
        g = _dot(x, wg_sc[...])
        u = _dot(x, wu_sc[...])
        h = (g * jax.nn.sigmoid(g) * u).astype(BF16)
        _slab_store(ys_ref, r, _pack_bf16_pairs(_dot(h, wd_sc[...])))


def _expert_mlp(blk_e, n_used, w_slot, w_next, xs, w_gate, w_up, w_down, layer):
    r = EXPERT_ROWS
    nblk = xs.shape[0] // (r * SLAB)
    d, de = w_gate.shape[2], w_gate.shape[3]

    def row_map(b, be, nu, ws, wn):
        return (jnp.minimum(b, nu[0] - 1), 0)

    grid_spec = pltpu.PrefetchScalarGridSpec(
        num_scalar_prefetch=4,
        grid=(nblk,),
        in_specs=[pl.BlockSpec((r * SLAB, LANES), row_map),
                  pl.BlockSpec(memory_space=pl.ANY),
                  pl.BlockSpec(memory_space=pl.ANY),
                  pl.BlockSpec(memory_space=pl.ANY)],
        out_specs=pl.BlockSpec((r * SLAB, LANES), row_map),
        scratch_shapes=[pltpu.VMEM((2, d, de), F32), pltpu.VMEM((2, d, de), F32), pltpu.VMEM((2, de, d), F32),
                        pltpu.VMEM((d, de), BF16), pltpu.VMEM((d, de), BF16), pltpu.VMEM((de, d), BF16),
                        pltpu.SemaphoreType.DMA((2,))],
    )
    return pl.pallas_call(
        functools.partial(_expert_mlp_kernel, layer=layer),
        out_shape=jax.ShapeDtypeStruct(xs.shape, U32),
        grid_spec=grid_spec,
        compiler_params=_cparams(("arbitrary",)),
        name="moe_experts",
    )(blk_e, n_used, w_slot, w_next, xs, w_gate, w_up, w_down)


def _combine_kernel(dest_hbm, ys_hbm, x_ref, gate_ref, sg_ref, su_ref, sd_ref, g_ref, b_ref,
                    y_ref, dest_sm, rows_sc, sem_idx, sem_row, *, tt, n_tiles):
    i = pl.program_id(0)
    slot = lax.rem(i, 2)
    n_rows = tt * TOP_K

    def idx_copy(tile, s):
        return pltpu.make_async_copy(dest_hbm.at[tile], dest_sm.at[pl.ds(pl.multiple_of(s * n_rows, n_rows), n_rows)],
                                     sem_idx.at[s])

    def row_copy(s, t, k, src_row):
        return pltpu.make_async_copy(ys_hbm.at[pl.ds(pl.multiple_of(src_row * SLAB, SLAB), SLAB)],
                                     rows_sc.at[s, k, pl.ds(pl.multiple_of(t * SLAB, SLAB), SLAB)], sem_row.at[s])

    def start_rows(s):
        def start(t, c):
            for k in range(TOP_K):
                row_copy(s, t, k, dest_sm[s * n_rows + t * TOP_K + k]).start(priority=k % 2)
            return c
        lax.fori_loop(0, tt, start, 0, unroll=2)

    @pl.when(i == 0)
    def _():
        idx_copy(0, 0).start()
        idx_copy(0, 0).wait()
        start_rows(0)
        if n_tiles > 1:
            idx_copy(1, 1).start()

    @pl.when(i + 1 < n_tiles)
    def _():
        idx_copy(i + 1, 1 - slot).wait()
        start_rows(1 - slot)

    @pl.when(i + 2 < n_tiles)
    def _():
        idx_copy(i + 2, slot).start()

    x = x_ref[...]
    xb = x.astype(BF16)
    g = _dot(xb, sg_ref[...])
    u = _dot(xb, su_ref[...])
    h = (g * jax.nn.sigmoid(g) * u).astype(BF16)
    z = DEEPNORM_ALPHA * x + _dot(h, sd_ref[...])

    def wait(t, c):
        for k in range(TOP_K):
            row_copy(slot, 0, 0, 0).wait()
        return c
    lax.fori_loop(0, tt, wait, 0, unroll=2)

    gate = gate_ref[...]
    lo_pieces, hi_pieces = [], []
    for c in range(SLAB):
        acc_lo = jnp.zeros((tt, LANES), F32)
        acc_hi = jnp.zeros((tt, LANES), F32)
        for k in range(TOP_K):
            words = rows_sc[slot, k, pl.ds(c, tt, stride=SLAB), :]
            gk = gate[:, k:k + 1]
            acc_lo = acc_lo + gk * _low_half_f32(words)
            acc_hi = acc_hi + gk * _high_half_f32(words)
        lo_pieces.append(acc_lo)
        hi_pieces.append(acc_hi)
    routed = jnp.concatenate(lo_pieces + hi_pieces, axis=1)
    y_ref[...] = _layer_norm_rows(z + routed, g_ref[...], b_ref[...])


def _combine(dest, ys, x2, gate, sg_b, su_b, sd_b, ln_g, ln_b):
    n, d = x2.shape
    ds_ = sg_b.shape[1]
    tt = COMBINE_TT
    n_tiles = n // tt
    dest2 = dest.reshape(n_tiles, tt * TOP_K)
    return pl.pallas_call(
        functools.partial(_combine_kernel, tt=tt, n_tiles=n_tiles),
        out_shape=jax.ShapeDtypeStruct((n, d), F32),
        grid=(n_tiles,),
        in_specs=[pl.BlockSpec(memory_space=pl.ANY),
                  pl.BlockSpec(memory_space=pl.ANY),
                  pl.BlockSpec((tt, d), lambda i: (i, 0)),
                  pl.BlockSpec((tt, TOP_K), lambda i: (i, 0)),
                  pl.BlockSpec((d, ds_), lambda i: (0, 0)),
                  pl.BlockSpec((d, ds_), lambda i: (0, 0)),
                  pl.BlockSpec((ds_, d), lambda i: (0, 0)),
                  pl.BlockSpec((1, d), lambda i: (0, 0)),
                  pl.BlockSpec((1, d), lambda i: (0, 0))],
        out_specs=pl.BlockSpec((tt, d), lambda i: (i, 0)),
        scratch_shapes=[pltpu.SMEM((2 * tt * TOP_K,), I32),
                        pltpu.VMEM((2, TOP_K, tt * SLAB, LANES), U32),
                        pltpu.SemaphoreType.DMA((2,)), pltpu.SemaphoreType.DMA((2,))],
        compiler_params=_cparams(("arbitrary",)),
        name="moe_combine",
    )(dest2, ys, x2, gate, sg_b, su_b, sd_b, ln_g.reshape(1, d), ln_b.reshape(1, d))


def _moe_layer(x2, rw, rb, w_gate, w_up, w_down, layer, sh_gate, sh_up, sh_down, ln_g, ln_b):
    n, d = x2.shape
    e = N_EXPERTS
    r = EXPERT_ROWS
    sel, gate, rank, counts = _router(x2, rw, rb)
    counts = counts.reshape(e)
    padded = (counts + r - 1) // r * r
    pend = jnp.cumsum(padded)
    pstart = pend - padded
    dest = rank + jnp.sum(jnp.where(sel[..., None] == jnp.arange(e, dtype=I32), pstart.astype(I32), 0), axis=-1)
    p_rows = n * TOP_K + e * r
    nblk = p_rows // r
    blk_start = jnp.arange(nblk, dtype=I32) * r
    blk_e = jnp.minimum(jnp.sum((pend[None, :] <= blk_start[:, None]).astype(I32), axis=1), e - 1)
    n_used = (pend[-1] // r).astype(I32).reshape(1)
    pad_info = jnp.concatenate([jnp.maximum(pend - r, 0), padded]).astype(I32)
    used = padded > 0
    eid = jnp.arange(e, dtype=I32)
    w_slot = ((jnp.cumsum(used.astype(I32)) - 1) & 1).astype(I32)
    later = lax.cummin(jnp.where(used, eid, e), reverse=True)
    w_next = jnp.concatenate([later[1:], jnp.full((1,), e, I32)])
    w_next = jnp.where(w_next < e, w_next, -1).astype(I32)
    xs = _dispatch(pad_info, dest.astype(I32), x2, p_rows)
    ys = _expert_mlp(blk_e, n_used, w_slot, w_next, xs, w_gate, w_up, w_down, layer)
    return _combine(dest.astype(I32), ys, x2, gate, sh_gate.astype(BF16), sh_up.astype(BF16),
                    sh_down.astype(BF16), ln_g, ln_b)


FAR_BUCKET = REL_BUCKETS // 2 - 1


def _n_near_offsets(tq, tk):
    return (tk + REL_MAX_DIST - 1 + tq - 1) // tq


def _t5_bucket(rel):
    half = REL_BUCKETS // 2
    max_exact = half // 2
    n = jnp.abs(rel)
    large = max_exact + (jnp.log(jnp.maximum(n, 1).astype(F32) / max_exact)
                         / math.log(REL_MAX_DIST / max_exact) * (half - max_exact)).astype(I32)
    large = jnp.minimum(large, half - 1)
    return jnp.where(rel > 0, half, 0) + jnp.where(n < max_exact, n, large)


def _bias_tiles_kernel(rb_ref, out_ref, *, tq, tk):
    offset = -tq * pl.program_id(0)
    key = lax.broadcasted_iota(I32, (tk, tq), 0)
    qry = lax.broadcasted_iota(I32, (tk, tq), 1)
    bucket = _t5_bucket(offset + key - qry)
    for h in range(B_HEADS):
        far = rb_ref[FAR_BUCKET * B_HEADS + h]
        acc = jnp.zeros((tk, tq), F32)
        for b in range(REL_BUCKETS):
            acc = jnp.where(bucket == b, (rb_ref[b * B_HEADS + h] - far) * LOG2E, acc)
        out_ref[0, h] = acc


def _bias_tiles(rel_bias, tq, tk):
    n_off = _n_near_offsets(tq, tk)
    grid_spec = pltpu.PrefetchScalarGridSpec(
        num_scalar_prefetch=1,
        grid=(n_off,),
        in_specs=[],
        out_specs=pl.BlockSpec((1, B_HEADS, tk, tq), lambda i, rb: (i, 0, 0, 0)),
    )
    return pl.pallas_call(
        functools.partial(_bias_tiles_kernel, tq=tq, tk=tk),
        out_shape=jax.ShapeDtypeStruct((n_off, B_HEADS, tk, tq), F32),
        grid_spec=grid_spec,
        compiler_params=_cparams(("arbitrary",)),
        name="dsa_bias_tiles",
    )(rel_bias.reshape(-1))


def _proj1_kernel(x_ref, w_ref, wit_ref, qn_ref, kvn_ref, cq_ref, ckv_ref, ckvt_ref, ki_ref, wi_ref,
                  *, o1, o2, o3, wscale):
    x = x_ref[...]
    proj = _dot(x.astype(BF16), w_ref[...])
    cq_ref[...] = _rms_norm_rows(proj[:, :o1], qn_ref[...]).astype(BF16)
    ckv = _rms_norm_rows(proj[:, o1:o2], kvn_ref[...])
    ckv_ref[...] = ckv.astype(BF16)
    ckvt_ref[...] = ckv.T.astype(BF16)
    ki_ref[...] = proj[:, o2:o3].astype(BF16)
    wi_ref[...] = _dot_nt(wit_ref[...], x) * wscale


def _proj1(x2, w_in, q_norm, kv_norm, ql, kvl, batch, seq):
    n, d = x2.shape
    o1, o2, o3 = ql, ql + kvl, ql + kvl + IDX_DIM
    tm = ROW_TILE
    nt = seq // tm
    w_main = w_in[:, :o3].astype(BF16)
    w_idx_t = w_in[:, o3:].T
    kern = functools.partial(_proj1_kernel, o1=o1, o2=o2, o3=o3,
                             wscale=(IDX_HEADS ** -0.5) * (IDX_DIM ** -0.5))
    return pl.pallas_call(
        kern,
        out_shape=(jax.ShapeDtypeStruct((n, ql), BF16), jax.ShapeDtypeStruct((n, kvl), BF16),
                   jax.ShapeDtypeStruct((batch * kvl, seq), BF16),
                   jax.ShapeDtypeStruct((n, IDX_DIM), BF16), jax.ShapeDtypeStruct((IDX_HEADS, n), F32)),
        grid=(n // tm,),
        in_specs=[pl.BlockSpec((tm, d), lambda i: (i, 0)),
                  pl.BlockSpec((d, o3), lambda i: (0, 0)),
                  pl.BlockSpec((IDX_HEADS, d), lambda i: (0, 0)),
                  pl.BlockSpec((1, ql), lambda i: (0, 0)),
                  pl.BlockSpec((1, kvl), lambda i: (0, 0))],
        out_specs=(pl.BlockSpec((tm, ql), lambda i: (i, 0)), pl.BlockSpec((tm, kvl), lambda i: (i, 0)),
                   pl.BlockSpec((kvl, tm), lambda i: (i // nt, i % nt)),
                   pl.BlockSpec((tm, IDX_DIM), lambda i: (i, 0)), pl.BlockSpec((IDX_HEADS, tm), lambda i: (0, i))),
        compiler_params=_cparams(("arbitrary",)),
        name="dsa_proj",
    )(x2, w_main, w_idx_t, q_norm.reshape(1, ql), kv_norm.reshape(1, kvl))


def _qside_kernel(cq_ref, wuq_ref, wuk_ref, wiq_ref, ql_ref, qi_ref, *, scale):
    cq = cq_ref[...]
    q = _dot(cq, wuq_ref[...]).astype(BF16)
    for h in range(B_HEADS):
        qh = q[:, h * B_HEAD_DIM:(h + 1) * B_HEAD_DIM]
        ql_ref[h] = (_dot_nt(wuk_ref[h], qh) * scale).astype(BF16)
    qi_ref[...] = _dot(cq, wiq_ref[...]).astype(BF16)


def _qside(cq, w_uq_b, w_uk_b, w_iq_b):
    n, ql = cq.shape
    kvl = w_uk_b.shape[1]
    tm = ROW_TILE
    kern = functools.partial(_qside_kernel, scale=B_HEAD_DIM ** -0.5 * LOG2E)
    return pl.pallas_call(
        kern,
        out_shape=(jax.ShapeDtypeStruct((B_HEADS, kvl, n), BF16),
                   jax.ShapeDtypeStruct((n, IDX_HEADS * IDX_DIM), BF16)),
        grid=(n // tm,),
        in_specs=[pl.BlockSpec((tm, ql), lambda i: (i, 0)),
                  pl.BlockSpec(w_uq_b.shape, lambda i: (0, 0)),
                  pl.BlockSpec(w_uk_b.shape, lambda i: (0, 0, 0)),
                  pl.BlockSpec(w_iq_b.shape, lambda i: (0, 0))],
        out_specs=(pl.BlockSpec((B_HEADS, kvl, tm), lambda i: (0, 0, i)),
                   pl.BlockSpec((tm, IDX_HEADS * IDX_DIM), lambda i: (i, 0))),
        compiler_params=_cparams(("arbitrary",)),
        name="dsa_qside",
    )(cq, w_uq_b, w_uk_b, w_iq_b)


INT_MIN = -2 ** 31
KEY_NEG_INF = (0xFF800000 ^ 0x7FFFFFFF) - 2 ** 32


def _ordered_key(v):
    bits = lax.bitcast_convert_type(v, I32)
    return bits ^ (lax.shift_right_arithmetic(bits, 31) & 0x7FFFFFFF)


def _indexer_kernel(qi_ref, wi_ref, ki_ref, mask_ref, key_sc, *, tq, tkc, topk):
    i = pl.program_id(1)
    n_chunks = (i * tq + tq + tkc - 1) // tkc
    qpos = i * tq + lax.broadcasted_iota(I32, (1, tq), 1)
    limit = (lax.shift_right_logical(qpos, CHUNK.bit_length() - 1) + 1) * CHUNK
    w = wi_ref[...]
    key_sc[...] = jnp.full(key_sc.shape, KEY_NEG_INF, I32)

    def score_chunk(c, carry):
        start = pl.multiple_of(c * tkc, tkc)
        k = ki_ref[pl.ds(start, tkc), :]
        acc = jnp.zeros((tkc, tq), F32)
        for h in range(IDX_HEADS):
            sc = _dot_nt(k, qi_ref[:, h * IDX_DIM:(h + 1) * IDX_DIM])
            acc = acc + jnp.maximum(sc, 0.0) * w[h:h + 1, :]
        kpos = start + lax.broadcasted_iota(I32, (tkc, tq), 0)
        key_sc[pl.ds(start, tkc), :] = _ordered_key(jnp.where(kpos < limit, acc, -jnp.inf))
        return carry

    lax.fori_loop(0, n_chunks, score_chunk, 0)

    def count(pred_fn):
        def body(c, acc):
            start = pl.multiple_of(c * tkc, tkc)
            hit = pred_fn(key_sc[pl.ds(start, tkc), :]).astype(I32)
            return acc + jnp.sum(hit.reshape(tkc // SLAB, SLAB, tq), axis=0)
        acc = lax.fori_loop(0, n_chunks, body, jnp.zeros((SLAB, tq), I32))
        return jnp.sum(acc, axis=0, keepdims=True)

    def bisect(it, prefix):
        cand = prefix + lax.shift_left(jnp.int32(1), 31 - it)
        cnt = count(lambda kk: kk >= cand)
        return jnp.where(cnt >= topk, cand, prefix)

    thr = lax.fori_loop(0, 32, bisect, jnp.full((1, tq), INT_MIN, I32))
    n_gt = count(lambda kk: kk > thr)
    n_eq = count(lambda kk: kk == thr)
    need = topk - n_gt
    tie_break = jnp.max(jnp.where(jnp.logical_and(thr > KEY_NEG_INF, n_eq > need), 1, 0)) > 0

    mask_ref[...] = jnp.full(mask_ref.shape, NEG_BIG, F32)

    @pl.when(jnp.logical_not(tie_break))
    def _():
        def write(c, carry):
            start = pl.multiple_of(c * tkc, tkc)
            kk = key_sc[pl.ds(start, tkc), :]
            sel = jnp.logical_and(kk >= thr, kk > KEY_NEG_INF)
            mask_ref[pl.ds(start, tkc), :] = jnp.where(sel, 0.0, NEG_BIG)
            return carry
        lax.fori_loop(0, n_chunks, write, 0)

    @pl.when(tie_break)
    def _():
        r_ = lax.broadcasted_iota(I32, (LANES, LANES), 0)
        c_ = lax.broadcasted_iota(I32, (LANES, LANES), 1)
        lower = (c_ < r_).astype(BF16)

        def write(c, seen):
            start = pl.multiple_of(c * LANES, LANES)
            kk = key_sc[pl.ds(start, LANES), :]
            eq = kk == thr
            before = seen + _dot(lower, eq.astype(BF16))
            sel = jnp.logical_or(kk > thr, jnp.logical_and(eq, before < need.astype(F32)))
            sel = jnp.logical_and(sel, kk > KEY_NEG_INF)
            mask_ref[pl.ds(start, LANES), :] = jnp.where(sel, 0.0, NEG_BIG)
            return seen + jnp.sum(eq.astype(F32), axis=0, keepdims=True)
        lax.fori_loop(0, n_chunks * (tkc // LANES), write, jnp.zeros((1, tq), F32))


def _indexer(qidx, widx_t, kidx, batch, seq, topk):
    tq = IDX_TQ
    tkc = min(IDX_TKC, seq)
    nq = seq // tq
    kern = functools.partial(_indexer_kernel, tq=tq, tkc=tkc, topk=topk)
    return pl.pallas_call(
        kern,
        out_shape=jax.ShapeDtypeStruct((batch * seq, seq), F32),
        grid=(batch, nq),
        in_specs=[pl.BlockSpec((tq, IDX_HEADS * IDX_DIM), lambda b, i: (b * nq + i, 0)),
                  pl.BlockSpec((IDX_HEADS, tq), lambda b, i: (0, b * nq + i)),
                  pl.BlockSpec((seq, IDX_DIM), lambda b, i: (b, 0))],
        out_specs=pl.BlockSpec((seq, tq), lambda b, i: (b, i)),
        scratch_shapes=[pltpu.VMEM((seq, tq), I32)],
        compiler_params=_cparams(("arbitrary", "arbitrary")),
        name="dsa_indexer",
    )(qidx, widx_t, kidx)


def _dsa_attn_kernel(qi_ref, kj_ref, qlt_ref, kv_ref, kvt_ref, mask_ref, bias_ref, wuvt_ref, o_ref,
                     m_sc, l_sc, acc_sc, *, tq, tk, n_near):
    p = pl.program_id(1)
    i = qi_ref[p]
    j = kj_ref[p]
    nh = B_HEADS
    behind = (i * tq - j * tk) // tq

    @pl.when(j == 0)
    def _():
        m_sc[...] = jnp.full_like(m_sc, -jnp.inf)
        l_sc[...] = jnp.zeros_like(l_sc)
        acc_sc[...] = jnp.zeros_like(acc_sc)

    def step(near):
        kv = kv_ref[...]
        kvt = kvt_ref[...]
        msk = mask_ref[...]
        for h in range(nh):
            s = _dot(kv, qlt_ref[h]) + msk
            if near:
                s = s + bias_ref[behind, h]
            m_prev = m_sc[h]
            m_new = jnp.maximum(m_prev, jnp.max(s, axis=0, keepdims=True))
            alpha = jnp.exp2(m_prev - m_new)
            pexp = jnp.exp2(s - m_new)
            l_sc[h] = alpha * l_sc[h] + jnp.sum(pexp, axis=0, keepdims=True)
            acc_sc[h] = alpha * acc_sc[h] + _dot(kvt, pexp.astype(BF16))
            m_sc[h] = m_new

    @pl.when(behind >= n_near)
    def _():
        step(False)

    @pl.when(behind < n_near)
    def _():
        step(True)

    @pl.when(j == (i * tq + tq - 1) // tk)
    def _():
        for h in range(nh):
            o_lat_t = (acc_sc[h] / l_sc[h]).astype(BF16)
            o_t = _dot(wuvt_ref[h], o_lat_t)
            o_ref[:, h * B_V_DIM:(h + 1) * B_V_DIM] = o_t.T.astype(o_ref.dtype)


def _dsa_attention(qlt, ckv, ckvt, mask_t, bias_tiles, w_uvt_b, batch, seq):
    tq, tk = DSA_TQ, DSA_TK
    kvl = ckv.shape[1]
    nq = seq // tq
    nk = seq // tk
    last = [(i * tq + tq - 1) // tk for i in range(nq)]
    qi = np.concatenate([np.full(last[i] + 1, i) for i in range(nq)]).astype(np.int32)
    kj = np.concatenate([np.arange(last[i] + 1) for i in range(nq)]).astype(np.int32)
    kern = functools.partial(_dsa_attn_kernel, tq=tq, tk=tk, n_near=bias_tiles.shape[0])
    grid_spec = pltpu.PrefetchScalarGridSpec(
        num_scalar_prefetch=2,
        grid=(batch, len(qi)),
        in_specs=[
            pl.BlockSpec((B_HEADS, kvl, tq), lambda b, p, qi, kj: (0, 0, b * nq + qi[p])),
            pl.BlockSpec((tk, kvl), lambda b, p, qi, kj: (b * nk + kj[p], 0)),
            pl.BlockSpec((kvl, tk), lambda b, p, qi, kj: (b, kj[p])),
            pl.BlockSpec((tk, tq), lambda b, p, qi, kj: (b * nk + kj[p], qi[p])),
            pl.BlockSpec(bias_tiles.shape, lambda b, p, qi, kj: (0, 0, 0, 0), pipeline_mode=pl.Buffered(1)),
            pl.BlockSpec(w_uvt_b.shape, lambda b, p, qi, kj: (0, 0, 0), pipeline_mode=pl.Buffered(1)),
        ],
        out_specs=pl.BlockSpec((tq, B_HEADS * B_V_DIM), lambda b, p, qi, kj: (b * nq + qi[p], 0)),
        scratch_shapes=[pltpu.VMEM((B_HEADS, 1, tq), F32), pltpu.VMEM((B_HEADS, 1, tq), F32),
                        pltpu.VMEM((B_HEADS, kvl, tq), F32)],
    )
    return pl.pallas_call(
        kern,
        out_shape=jax.ShapeDtypeStruct((batch * seq, B_HEADS * B_V_DIM), BF16),
        grid_spec=grid_spec,
        compiler_params=_cparams(("arbitrary", "arbitrary")),
        name="dsa_attention",
    )(jnp.asarray(qi), jnp.asarray(kj), qlt, ckv, ckvt, mask_t, bias_tiles, w_uvt_b)


def _dsa_layer(x2, w_in, q_norm, kv_norm, w_uq, w_iq, w_uk, w_uv, w_out, rel_bias, ln_g, ln_b, batch, seq):
    ql_dim = q_norm.shape[0]
    kvl = kv_norm.shape[0]
    topk = min(IDX_TOPK, seq // 4)
    cq, ckv, ckvt, kidx, widx_t = _proj1(x2, w_in, q_norm, kv_norm, ql_dim, kvl, batch, seq)
    qlt, qidx = _qside(cq, w_uq.astype(BF16), w_uk.astype(BF16), w_iq.astype(BF16))
    mask_t = _indexer(qidx, widx_t, kidx, batch, seq, topk)
    bias_tiles = _bias_tiles(rel_bias, DSA_TQ, DSA_TK)
    w_uvt = jnp.swapaxes(w_uv, 1, 2).astype(BF16)
    o = _dsa_attention(qlt, ckv, ckvt, mask_t, bias_tiles, w_uvt, batch, seq)
    return _outproj_ln(o, w_out.astype(BF16), x2, ln_g, ln_b)


def kernel(x, a_w_in, a_b_f, a_w_out, b_w_in, b_q_norm, b_kv_norm, b_w_uq, b_w_iq, b_w_uk, b_w_uv, b_w_out,
           rel_bias, ln1_g, ln1_b, ln2_g, ln2_b, router_w, router_b, w_gate, w_up, w_down, sh_gate, sh_up,
           sh_down):
    batch, seq, d = x.shape
    x2 = x.reshape(batch * seq, d)
    x2 = _fox_layer(x2, a_w_in[0], a_b_f[0], a_w_out[0], ln1_g[0], ln1_b[0], batch, seq)
    x2 = _moe_layer(x2, router_w[0], router_b[0], w_gate, w_up, w_down, 0, sh_gate[0], sh_up[0],
                    sh_down[0], ln2_g[0], ln2_b[0])
    x2 = _dsa_layer(x2, b_w_in[0], b_q_norm[0], b_kv_norm[0], b_w_uq[0], b_w_iq[0], b_w_uk[0], b_w_uv[0],
                    b_w_out[0], rel_bias, ln1_g[1], ln1_b[1], batch, seq)
    x2 = _moe_layer(x2, router_w[1], router_b[1], w_gate, w_up, w_down, 1, sh_gate[1], sh_up[1],
                    sh_down[1], ln2_g[1], ln2_b[1])
    return x2.reshape(batch, seq, d)
```

```python
import functools
import math

import numpy as np
import jax
import jax.numpy as jnp
from jax import lax
from jax.experimental import pallas as pl
from jax.experimental.pallas import tpu as pltpu

BF16 = jnp.bfloat16
F32 = jnp.float32
I32 = jnp.int32
U32 = jnp.uint32

A_HEADS = 16
A_HEAD_DIM = 128
B_HEADS = 16
B_HEAD_DIM = 128
B_V_DIM = 128
IDX_HEADS = 16
IDX_DIM = 64
IDX_TOPK = 256
CHUNK = 64
REL_BUCKETS = 32
REL_MAX_DIST = 128
N_EXPERTS = 64
TOP_K = 8
ROUTED_SCALE = 2.5
DEPTH = 2
DEEPNORM_ALPHA = (2 * DEPTH) ** 0.25
LN_EPS = 1e-5
RMS_EPS = 1e-6

LANES = 128
SLAB = 8
HALF_BITS = 16
HIGH_HALF_MASK = 0xFFFF0000
VMEM_LIMIT = 56 * 1024 * 1024
LOG2E = math.log2(math.e)
NEG_BIG = -1e30

EXPERT_ROWS = 512
ROW_TILE = 512
PROJ0_TM, PROJ0_TN = 1024, 1024
CUMSUM_T = 256
DISPATCH_TT = 512
COMBINE_TT = 256
FOX_T = 512
FOX_HEADS_PER_STEP = 16
DSA_TQ = 256
DSA_TK = 512
IDX_TQ = 256
IDX_TKC = 512


def _cparams(sem, vmem=VMEM_LIMIT):
    return pltpu.CompilerParams(dimension_semantics=sem, vmem_limit_bytes=vmem)


def _dot(a, b):
    return jnp.dot(a, b, preferred_element_type=F32)


def _dot_nt(a, b):
    return lax.dot_general(a, b, (((1,), (1,)), ((), ())), preferred_element_type=F32)


def _lane_tile(a, width):
    return a if width == LANES else jnp.concatenate([a] * (width // LANES), axis=1)


def _split2(a):
    hi = a.astype(BF16)
    lo = (a - hi.astype(F32)).astype(BF16)
    return hi, lo


def _split3(a):
    hi = a.astype(BF16)
    r = a - hi.astype(F32)
    mid = r.astype(BF16)
    lo = (r - mid.astype(F32)).astype(BF16)
    return hi, mid, lo


def _dot_x3(a, b):
    ah, al = _split2(a)
    bh, bl = _split2(b)
    return _dot(ah, bh) + (_dot(ah, bl) + _dot(al, bh))


def _dot_x3_nt(a, b):
    ah, al = _split2(a)
    bh, bl = _split2(b)
    return _dot_nt(ah, bh) + (_dot_nt(ah, bl) + _dot_nt(al, bh))


def _layer_norm_rows(z, g, b):
    mu = jnp.mean(z, axis=-1, keepdims=True)
    d = z - mu
    var = jnp.mean(d * d, axis=-1, keepdims=True)
    return d * lax.rsqrt(var + LN_EPS) * g + b


def _rms_norm_rows(z, g):
    ms = jnp.mean(z * z, axis=-1, keepdims=True)
    return z * lax.rsqrt(ms + RMS_EPS) * g


def _proj0_kernel(x_ref, w_ref, wf_ref, qkv_ref, fl_ref, xb_sc, *, n_q_blocks, q_scale):
    j = pl.program_id(1)

    @pl.when(j == 0)
    def _():
        x = x_ref[...]
        xb_sc[...] = x.astype(BF16)
        fl_ref[...] = _dot_x3(x, wf_ref[...])

    acc = _dot(xb_sc[...], w_ref[...])
    scale = jnp.where(j < n_q_blocks, q_scale, 1.0).astype(F32)
    qkv_ref[...] = (acc * scale).astype(BF16)


def _proj0(x2, w_qkv_b, w_f):
    n, d = x2.shape
    nout = w_qkv_b.shape[1]
    tm, tn = PROJ0_TM, PROJ0_TN
    dq = A_HEADS * A_HEAD_DIM
    kern = functools.partial(_proj0_kernel, n_q_blocks=dq // tn, q_scale=A_HEAD_DIM ** -0.5 * LOG2E)
    return pl.pallas_call(
        kern,
        out_shape=(jax.ShapeDtypeStruct((n, nout), BF16),
                   jax.ShapeDtypeStruct((n, A_HEADS), F32)),
        grid=(n // tm, nout // tn),
        in_specs=[pl.BlockSpec((tm, d), lambda i, j: (i, 0)),
                  pl.BlockSpec((d, tn), lambda i, j: (0, j)),
                  pl.BlockSpec((d, A_HEADS), lambda i, j: (0, 0))],
        out_specs=(pl.BlockSpec((tm, tn), lambda i, j: (i, j)),
                   pl.BlockSpec((tm, A_HEADS), lambda i, j: (i, 0))),
        scratch_shapes=[pltpu.VMEM((tm, d), BF16)],
        compiler_params=_cparams(("arbitrary", "arbitrary")),
        name="fox_proj",
    )(x2, w_qkv_b, w_f)


def _forget_cumsum_kernel(fl_ref, bf_ref, f_ref, carry_sc, *, t):
    @pl.when(pl.program_id(1) == 0)
    def _():
        carry_sc[...] = jnp.zeros_like(carry_sc)

    z = fl_ref[...] + bf_ref[...]
    logf = jnp.minimum(z, 0.0) - jnp.log1p(jnp.exp(-jnp.abs(z)))
    row = lax.broadcasted_iota(I32, (t, t), 0)
    col = lax.broadcasted_iota(I32, (t, t), 1)
    tri = (col <= row).astype(BF16)
    hi, mid, lo = _split3(logf)
    cs = _dot(tri, hi) + (_dot(tri, mid) + _dot(tri, lo)) + carry_sc[...]
    f_ref[...] = cs * LOG2E
    carry_sc[...] = cs[t - 1:t, :]


def _forget_cumsum(fl, b_f, batch, seq):
    t = CUMSUM_T
    nb = seq // t
    return pl.pallas_call(
        functools.partial(_forget_cumsum_kernel, t=t),
        out_shape=jax.ShapeDtypeStruct(fl.shape, F32),
        grid=(batch, nb),
        in_specs=[pl.BlockSpec((t, A_HEADS), lambda b, i: (b * nb + i, 0)),
                  pl.BlockSpec((1, A_HEADS), lambda b, i: (0, 0))],
        out_specs=pl.BlockSpec((t, A_HEADS), lambda b, i: (b * nb + i, 0)),
        scratch_shapes=[pltpu.VMEM((1, A_HEADS), F32)],
        compiler_params=_cparams(("arbitrary", "arbitrary")),
        name="fox_forget_cumsum",
    )(fl, b_f.reshape(1, A_HEADS))


def _fox_attn_kernel(qi_ref, kj_ref, q_ref, k_ref, v_ref, fk_ref, o_ref, m_sc, l_sc, acc_sc, *, tq, tk, hp):
    p = pl.program_id(2)
    i = qi_ref[p]
    j = kj_ref[p]
    dh = A_HEAD_DIM

    @pl.when(j == 0)
    def _():
        m_sc[...] = jnp.full_like(m_sc, -jnp.inf)
        l_sc[...] = jnp.zeros_like(l_sc)
        acc_sc[...] = jnp.zeros_like(acc_sc)

    def step(diag):
        for hh in range(hp):
            cols = slice(hh * dh, (hh + 1) * dh)
            s = _dot_nt(q_ref[:, cols], k_ref[:, cols]) - fk_ref[hh]
            if diag:
                row = lax.broadcasted_iota(I32, (tq, tk), 0)
                col = lax.broadcasted_iota(I32, (tq, tk), 1)
                s = jnp.where(col <= row, s, -jnp.inf)
            m_prev = m_sc[hh]
            m_new = jnp.maximum(m_prev, jnp.max(s, axis=1, keepdims=True))
            alpha = jnp.exp2(m_prev - m_new)
            pexp = jnp.exp2(s - _lane_tile(m_new, tk))
            l_sc[hh] = alpha * l_sc[hh] + jnp.sum(pexp, axis=1, keepdims=True)
            acc_sc[hh] = alpha * acc_sc[hh] + _dot(pexp.astype(BF16), v_ref[:, cols])
            m_sc[hh] = m_new

    @pl.when(j < i)
    def _():
        step(False)

    @pl.when(j == i)
    def _():
        step(True)
        for hh in range(hp):
            o_ref[:, hh * dh:(hh + 1) * dh] = (acc_sc[hh] / l_sc[hh]).astype(o_ref.dtype)


def _fox_attention(qkv, f_rows, batch, seq):
    t = min(FOX_T, seq)
    nq = seq // t
    qi = np.concatenate([np.full(i + 1, i) for i in range(nq)]).astype(np.int32)
    kj = np.concatenate([np.arange(i + 1) for i in range(nq)]).astype(np.int32)
    hp = FOX_HEADS_PER_STEP
    hg = A_HEADS // hp
    dh = A_HEAD_DIM
    w = hp * dh
    kern = functools.partial(_fox_attn_kernel, tq=t, tk=t, hp=hp)
    grid_spec = pltpu.PrefetchScalarGridSpec(
        num_scalar_prefetch=2,
        grid=(batch, hg, len(qi)),
        in_specs=[
            pl.BlockSpec((t, w), lambda b, h, p, qi, kj: (b * nq + qi[p], h)),
            pl.BlockSpec((t, w), lambda b, h, p, qi, kj: (b * nq + kj[p], hg + h)),
            pl.BlockSpec((t, w), lambda b, h, p, qi, kj: (b * nq + kj[p], 2 * hg + h)),
            pl.BlockSpec((hp, 1, t), lambda b, h, p, qi, kj: (b * hg + h, 0, kj[p])),
        ],
        out_specs=pl.BlockSpec((t, w), lambda b, h, p, qi, kj: (b * nq + qi[p], h)),
        scratch_shapes=[pltpu.VMEM((hp, t, dh), F32), pltpu.VMEM((hp, t, dh), F32),
                        pltpu.VMEM((hp, t, dh), F32)],
    )
    return pl.pallas_call(
        kern,
        out_shape=jax.ShapeDtypeStruct((batch * seq, A_HEADS * dh), BF16),
        grid_spec=grid_spec,
        compiler_params=_cparams(("arbitrary", "arbitrary", "arbitrary")),
        name="fox_attention",
    )(jnp.asarray(qi), jnp.asarray(kj), qkv, qkv, qkv, f_rows)


def _outproj_ln_kernel(o_ref, w_ref, x_ref, g_ref, b_ref, y_ref):
    z = DEEPNORM_ALPHA * x_ref[...] + _dot(o_ref[...], w_ref[...])
    y_ref[...] = _layer_norm_rows(z, g_ref[...], b_ref[...])


def _outproj_ln(o, w_b, x2, g, b):
    n, d = x2.shape
    k = o.shape[1]
    tm = ROW_TILE
    return pl.pallas_call(
        _outproj_ln_kernel,
        out_shape=jax.ShapeDtypeStruct((n, d), F32),
        grid=(n // tm,),
        in_specs=[pl.BlockSpec((tm, k), lambda i: (i, 0)),
                  pl.BlockSpec((k, d), lambda i: (0, 0)),
                  pl.BlockSpec((tm, d), lambda i: (i, 0)),
                  pl.BlockSpec((1, d), lambda i: (0, 0)),
                  pl.BlockSpec((1, d), lambda i: (0, 0))],
        out_specs=pl.BlockSpec((tm, d), lambda i: (i, 0)),
        compiler_params=_cparams(("arbitrary",)),
        name="outproj_deepnorm",
    )(o, w_b, x2, g.reshape(1, d), b.reshape(1, d))


def _fox_layer(x2, w_in, b_f, w_out, ln_g, ln_b, batch, seq):
    dq = A_HEADS * A_HEAD_DIM
    qkv, fl = _proj0(x2, w_in[:, :3 * dq].astype(BF16), w_in[:, 3 * dq:])
    f = _forget_cumsum(fl, b_f, batch, seq)
    f_rows = f.reshape(batch, seq, A_HEADS).transpose(0, 2, 1).reshape(batch * A_HEADS, 1, seq)
    o = _fox_attention(qkv, f_rows, batch, seq)
    return _outproj_ln(o, w_out.astype(BF16), x2, ln_g, ln_b)


def _router_kernel(x_ref, rwt_ref, rb_ref, sel_ref, gate_ref, rank_ref, cnt_ref, carry_sc, *, tm):
    @pl.when(pl.program_id(0) == 0)
    def _():
        carry_sc[...] = jnp.zeros_like(carry_sc)

    e = N_EXPERTS
    scores = jax.nn.sigmoid(_dot_x3_nt(rwt_ref[...], x_ref[...]))
    expert = lax.broadcasted_iota(I32, (e, tm), 0)
    slot = lax.broadcasted_iota(I32, (TOP_K, tm), 0)
    work = scores + rb_ref[...]
    chosen = jnp.zeros((e, tm), F32)
    sel = jnp.zeros((TOP_K, tm), I32)
    gate = jnp.zeros((TOP_K, tm), F32)
    idxs = []
    for k in range(TOP_K):
        mx = jnp.max(work, axis=0, keepdims=True)
        idx = jnp.min(jnp.where(work == mx, expert, e), axis=0, keepdims=True)
        hit = expert == idx
        gk = jnp.sum(jnp.where(hit, scores, 0.0), axis=0, keepdims=True)
        work = jnp.where(hit, -jnp.inf, work)
        chosen = jnp.where(hit, 1.0, chosen)
        sel = jnp.where(slot == k, idx, sel)
        gate = jnp.where(slot == k, gk, gate)
        idxs.append(idx)
    gate = gate / jnp.sum(gate, axis=0, keepdims=True) * ROUTED_SCALE

    src = lax.broadcasted_iota(I32, (tm, tm), 0)
    dst = lax.broadcasted_iota(I32, (tm, tm), 1)
    before = _dot(chosen.astype(BF16), (src < dst).astype(BF16)) + carry_sc[...]
    rank = jnp.zeros((TOP_K, tm), F32)
    for k in range(TOP_K):
        rk = jnp.sum(jnp.where(expert == idxs[k], before, 0.0), axis=0, keepdims=True)
        rank = jnp.where(slot == k, rk, rank)
    total = carry_sc[...] + jnp.sum(chosen, axis=1, keepdims=True)
    carry_sc[...] = total
    sel_ref[...] = sel
    gate_ref[...] = gate
    rank_ref[...] = rank.astype(I32)
    cnt_ref[...] = total.astype(I32)


def _router(x2, rw, rb):
    n, d = x2.shape
    tm = ROW_TILE
    e = N_EXPERTS
    return pl.pallas_call(
        functools.partial(_router_kernel, tm=tm),
        out_shape=(jax.ShapeDtypeStruct((TOP_K, n), I32),
                   jax.ShapeDtypeStruct((TOP_K, n), F32),
                   jax.ShapeDtypeStruct((TOP_K, n), I32),
                   jax.ShapeDtypeStruct((e, 1), I32)),
        grid=(n // tm,),
        in_specs=[pl.BlockSpec((tm, d), lambda i: (i, 0)),
                  pl.BlockSpec((e, d), lambda i: (0, 0)),
                  pl.BlockSpec((e, 1), lambda i: (0, 0))],
        out_specs=(pl.BlockSpec((TOP_K, tm), lambda i: (0, i)),
                   pl.BlockSpec((TOP_K, tm), lambda i: (0, i)),
                   pl.BlockSpec((TOP_K, tm), lambda i: (0, i)),
                   pl.BlockSpec((e, 1), lambda i: (0, 0))),
        scratch_shapes=[pltpu.VMEM((e, 1), F32)],
        compiler_params=_cparams(("arbitrary",)),
        name="moe_router",
    )(x2, rw.T, rb.reshape(e, 1))


def _pack_bf16_pairs(y):
    w = y.shape[1] // 2
    lo = lax.bitcast_convert_type(y[:, :w].astype(BF16).astype(F32), U32)
    hi = lax.bitcast_convert_type(y[:, w:].astype(BF16).astype(F32), U32)
    return lax.shift_right_logical(lo, jnp.uint32(HALF_BITS)) | hi


def _low_half_f32(words):
    return lax.bitcast_convert_type(lax.shift_left(words, jnp.uint32(HALF_BITS)), F32)


def _high_half_f32(words):
    return lax.bitcast_convert_type(words & jnp.uint32(HIGH_HALF_MASK), F32)


def _unpack_bf16_pairs(words):
    return _low_half_f32(words).astype(BF16), _high_half_f32(words).astype(BF16)


def _slab_load(ref, rows):
    return jnp.concatenate([ref[pl.ds(c, rows, stride=SLAB), :] for c in range(SLAB)], axis=1)


def _slab_store(ref, rows, val):
    for c in range(SLAB):
        ref[pl.ds(c, rows, stride=SLAB), :] = val[:, c * LANES:(c + 1) * LANES]


def _dispatch_kernel(pad_ref, dest_hbm, x_ref, xs_hbm, dest_sm, slab_sc, zero_sc, sem_idx, sem_zero, sem_row,
                     *, tt, n_tiles):
    i = pl.program_id(0)

    def zero_copy(e):
        start = pl.multiple_of(pad_ref[e] * SLAB, SLAB)
        return pltpu.make_async_copy(zero_sc, xs_hbm.at[pl.ds(start, EXPERT_ROWS * SLAB)], sem_zero)

    @pl.when(i == 0)
    def _():
        zero_sc[...] = jnp.zeros_like(zero_sc)

        def start(e, c):
            @pl.when(pad_ref[N_EXPERTS + e] > 0)
            def _():
                zero_copy(e).start()
            return c

        def wait(e, c):
            @pl.when(pad_ref[N_EXPERTS + e] > 0)
            def _():
                zero_copy(e).wait()
            return c

        lax.fori_loop(0, N_EXPERTS, start, 0)
        lax.fori_loop(0, N_EXPERTS, wait, 0)

    slot = lax.rem(i, 2)
    idx_copy = pltpu.make_async_copy(dest_hbm.at[i], dest_sm, sem_idx)
    idx_copy.start()

    def row_copy(s, t, dst_row):
        src = pl.multiple_of(t * SLAB, SLAB)
        dst = pl.multiple_of(dst_row * SLAB, SLAB)
        return pltpu.make_async_copy(slab_sc.at[s, pl.ds(src, SLAB)], xs_hbm.at[pl.ds(dst, SLAB)], sem_row.at[s])

    def wait_rows(s):
        def wait(t, c):
            for k in range(TOP_K):
                row_copy(s, 0, 0).wait()
            return c
        lax.fori_loop(0, tt, wait, 0, unroll=2)

    @pl.when(i >= 2)
    def _():
        wait_rows(slot)

    _slab_store(slab_sc.at[slot], tt, _pack_bf16_pairs(x_ref[...]))
    idx_copy.wait()

    def start(t, c):
        for k in range(TOP_K):
            row_copy(slot, t, dest_sm[t * TOP_K + k]).start(priority=k % 2)
        return c

    lax.fori_loop(0, tt, start, 0, unroll=2)

    @pl.when(i == n_tiles - 1)
    def _():
        if n_tiles > 1:
            wait_rows(1 - slot)
        wait_rows(slot)


def _dispatch(pad_info, dest, x2, p_rows):
    n, d = x2.shape
    tt = DISPATCH_TT
    assert d == 2 * SLAB * LANES
    dest2 = dest.reshape(n // tt, tt * TOP_K)
    grid_spec = pltpu.PrefetchScalarGridSpec(
        num_scalar_prefetch=1,
        grid=(n // tt,),
        in_specs=[pl.BlockSpec(memory_space=pl.ANY),
                  pl.BlockSpec((tt, d), lambda i, pad: (i, 0))],
        out_specs=pl.BlockSpec(memory_space=pl.ANY),
        scratch_shapes=[pltpu.SMEM((tt * TOP_K,), I32),
                        pltpu.VMEM((2, tt * SLAB, LANES), U32),
                        pltpu.VMEM((EXPERT_ROWS * SLAB, LANES), U32),
                        pltpu.SemaphoreType.DMA, pltpu.SemaphoreType.DMA, pltpu.SemaphoreType.DMA((2,))],
    )
    return pl.pallas_call(
        functools.partial(_dispatch_kernel, tt=tt, n_tiles=n // tt),
        out_shape=jax.ShapeDtypeStruct((p_rows * SLAB, LANES), U32),
        grid_spec=grid_spec,
        compiler_params=_cparams(("arbitrary",)),
        name="moe_dispatch",
    )(pad_info, dest2, x2)


def _expert_mlp_kernel(be_ref, nu_ref, slot_ref, next_ref, xs_ref, wg_hbm, wu_hbm, wd_hbm, ys_ref,
                       wg_f, wu_f, wd_f, wg_sc, wu_sc, wd_sc, sem, *, layer):
    b = pl.program_id(0)
    e = be_ref[b]
    first = jnp.logical_or(b == 0, e != be_ref[jnp.maximum(b - 1, 0)])
    active = b < nu_ref[0]

    def fetch(expert, s):
        return (pltpu.make_async_copy(wg_hbm.at[layer, expert], wg_f.at[s], sem.at[s]),
                pltpu.make_async_copy(wu_hbm.at[layer, expert], wu_f.at[s], sem.at[s]),
                pltpu.make_async_copy(wd_hbm.at[layer, expert], wd_f.at[s], sem.at[s]))

    @pl.when(jnp.logical_and(active, first))
    def _():
        s = slot_ref[e]

        @pl.when(b == 0)
        def _():
            for c in fetch(e, s):
                c.start()

        for c in fetch(e, s):
            c.wait()
        nxt = next_ref[e]

        @pl.when(nxt >= 0)
        def _():
            for c in fetch(nxt, 1 - s):
                c.start()

        wg_sc[...] = wg_f[s].astype(BF16)
        wu_sc[...] = wu_f[s].astype(BF16)
        wd_sc[...] = wd_f[s].astype(BF16)

    @pl.when(active)
    def _():
        r = EXPERT_ROWS
        x = jnp.concatenate(_unpack_bf16_pairs(_slab_load(xs_ref, r)), axis=1)
        g = _dot(x, wg_sc[...])
        u = _dot(x, wu_sc[...])
        h = (g * jax.nn.sigmoid(g) * u).astype(BF16)
        _slab_store(ys_ref, r, _pack_bf16_pairs(_dot(h, wd_sc[...])))


def _expert_mlp(blk_e, n_used, w_slot, w_next, xs, w_gate, w_up, w_down, layer):
    r = EXPERT_ROWS
    nblk = xs.shape[0] // (r * SLAB)
    d, de = w_gate.shape[2], w_gate.shape[3]

    def row_map(b, be, nu, ws, wn):
        return (jnp.minimum(b, nu[0] - 1), 0)

    grid_spec = pltpu.PrefetchScalarGridSpec(
        num_scalar_prefetch=4,
        grid=(nblk,),
        in_specs=[pl.BlockSpec((r * SLAB, LANES), row_map),
                  pl.BlockSpec(memory_space=pl.ANY),
                  pl.BlockSpec(memory_space=pl.ANY),
                  pl.BlockSpec(memory_space=pl.ANY)],
        out_specs=pl.BlockSpec((r * SLAB, LANES), row_map),
        scratch_shapes=[pltpu.VMEM((2, d, de), F32), pltpu.VMEM((2, d, de), F32), pltpu.VMEM((2, de, d), F32),
                        pltpu.VMEM((d, de), BF16), pltpu.VMEM((d, de), BF16), pltpu.VMEM((de, d), BF16),
                        pltpu.SemaphoreType.DMA((2,))],
    )
    return pl.pallas_call(
        functools.partial(_expert_mlp_kernel, layer=layer),
        out_shape=jax.ShapeDtypeStruct(xs.shape, U32),
        grid_spec=grid_spec,
        compiler_params=_cparams(("arbitrary",)),
        name="moe_experts",
    )(blk_e, n_used, w_slot, w_next, xs, w_gate, w_up, w_down)


def _combine_kernel(dest_hbm, ys_hbm, x_ref, gate_ref, sg_ref, su_ref, sd_ref, g_ref, b_ref,
                    y_ref, dest_sm, rows_sc, sem_idx, sem_row, *, tt, n_tiles):
    i = pl.program_id(0)
    slot = lax.rem(i, 2)
    n_rows = tt * TOP_K

    def idx_copy(tile, s):
        return pltpu.make_async_copy(dest_hbm.at[tile], dest_sm.at[pl.ds(pl.multiple_of(s * n_rows, n_rows), n_rows)],
                                     sem_idx.at[s])

    def row_copy(s, t, k, src_row):
        return pltpu.make_async_copy(ys_hbm.at[pl.ds(pl.multiple_of(src_row * SLAB, SLAB), SLAB)],
                                     rows_sc.at[s, k, pl.ds(pl.multiple_of(t * SLAB, SLAB), SLAB)], sem_row.at[s])

    def start_rows(s):
        def start(t, c):
            for k in range(TOP_K):
                row_copy(s, t, k, dest_sm[s * n_rows + t * TOP_K + k]).start(priority=k % 2)
            return c
        lax.fori_loop(0, tt, start, 0, unroll=2)

    @pl.when(i == 0)
    def _():
        idx_copy(0, 0).start()
        idx_copy(0, 0).wait()
        start_rows(0)
        if n_tiles > 1:
            idx_copy(1, 1).start()

    @pl.when(i + 1 < n_tiles)
    def _():
        idx_copy(i + 1, 1 - slot).wait()
        start_rows(1 - slot)

    @pl.when(i + 2 < n_tiles)
    def _():
        idx_copy(i + 2, slot).start()

    x = x_ref[...]
    xb = x.astype(BF16)
    g = _dot(xb, sg_ref[...])
    u = _dot(xb, su_ref[...])
    h = (g * jax.nn.sigmoid(g) * u).astype(BF16)
    z = DEEPNORM_ALPHA * x + _dot(h, sd_ref[...])

    def wait(t, c):
        for k in range(TOP_K):
            row_copy(slot, 0, 0, 0).wait()
        return c
    lax.fori_loop(0, tt, wait, 0, unroll=2)

    gate = gate_ref[...]
    lo_pieces, hi_pieces = [], []
    for c in range(SLAB):
        acc_lo = jnp.zeros((tt, LANES), F32)
        acc_hi = jnp.zeros((tt, LANES), F32)
        for k in range(TOP_K):
            words = rows_sc[slot, k, pl.ds(c, tt, stride=SLAB), :]
            gk = gate[:, k:k + 1]
            acc_lo = acc_lo + gk * _low_half_f32(words)
            acc_hi = acc_hi + gk * _high_half_f32(words)
        lo_pieces.append(acc_lo)
        hi_pieces.append(acc_hi)
    routed = jnp.concatenate(lo_pieces + hi_pieces, axis=1)
    y_ref[...] = _layer_norm_rows(z + routed, g_ref[...], b_ref[...])


def _combine(dest, ys, x2, gate, sg_b, su_b, sd_b, ln_g, ln_b):
    n, d = x2.shape
    ds_ = sg_b.shape[1]
    tt = COMBINE_TT
    n_tiles = n // tt
    dest2 = dest.reshape(n_tiles, tt * TOP_K)
    return pl.pallas_call(
        functools.partial(_combine_kernel, tt=tt, n_tiles=n_tiles),
        out_shape=jax.ShapeDtypeStruct((n, d), F32),
        grid=(n_tiles,),
        in_specs=[pl.BlockSpec(memory_space=pl.ANY),
                  pl.BlockSpec(memory_space=pl.ANY),
                  pl.BlockSpec((tt, d), lambda i: (i, 0)),
                  pl.BlockSpec((tt, TOP_K), lambda i: (i, 0)),
                  pl.BlockSpec((d, ds_), lambda i: (0, 0)),
                  pl.BlockSpec((d, ds_), lambda i: (0, 0)),
                  pl.BlockSpec((ds_, d), lambda i: (0, 0)),
                  pl.BlockSpec((1, d), lambda i: (0, 0)),
                  pl.BlockSpec((1, d), lambda i: (0, 0))],
        out_specs=pl.BlockSpec((tt, d), lambda i: (i, 0)),
        scratch_shapes=[pltpu.SMEM((2 * tt * TOP_K,), I32),
                        pltpu.VMEM((2, TOP_K, tt * SLAB, LANES), U32),
                        pltpu.SemaphoreType.DMA((2,)), pltpu.SemaphoreType.DMA((2,))],
        compiler_params=_cparams(("arbitrary",)),
        name="moe_combine",
    )(dest2, ys, x2, gate, sg_b, su_b, sd_b, ln_g.reshape(1, d), ln_b.reshape(1, d))


def _moe_layer(x2, rw, rb, w_gate, w_up, w_down, layer, sh_gate, sh_up, sh_down, ln_g, ln_b):
    n, d = x2.shape
    e = N_EXPERTS
    r = EXPERT_ROWS
    sel_t, gate_t, rank_t, counts = _router(x2, rw, rb)
    counts = counts.reshape(e)
    padded = (counts + r - 1) // r * r
    pend = jnp.cumsum(padded)
    pstart = pend - padded
    offset_t = jnp.sum(jnp.where(sel_t[None] == jnp.arange(e, dtype=I32)[:, None, None],
                                 pstart.astype(I32)[:, None, None], 0), axis=0)
    dest = (rank_t + offset_t).T
    gate = gate_t.T
    p_rows = n * TOP_K + e * r
    nblk = p_rows // r
    blk_start = jnp.arange(nblk, dtype=I32) * r
    blk_e = jnp.minimum(jnp.sum((pend[None, :] <= blk_start[:, None]).astype(I32), axis=1), e - 1)
    n_used = (pend[-1] // r).astype(I32).reshape(1)
    pad_info = jnp.concatenate([jnp.maximum(pend - r, 0), padded]).astype(I32)
    used = padded > 0
    eid = jnp.arange(e, dtype=I32)
    w_slot = ((jnp.cumsum(used.astype(I32)) - 1) & 1).astype(I32)
    later = lax.cummin(jnp.where(used, eid, e), reverse=True)
    w_next = jnp.concatenate([later[1:], jnp.full((1,), e, I32)])
    w_next = jnp.where(w_next < e, w_next, -1).astype(I32)
    xs = _dispatch(pad_info, dest.astype(I32), x2, p_rows)
    ys = _expert_mlp(blk_e, n_used, w_slot, w_next, xs, w_gate, w_up, w_down, layer)
    return _combine(dest.astype(I32), ys, x2, gate, sh_gate.astype(BF16), sh_up.astype(BF16),
                    sh_down.astype(BF16), ln_g, ln_b)


FAR_BUCKET = REL_BUCKETS // 2 - 1


def _n_near_offsets(tq, tk):
    return (tk + REL_MAX_DIST - 1 + tq - 1) // tq


def _t5_bucket(rel):
    half = REL_BUCKETS // 2
    max_exact = half // 2
    n = jnp.abs(rel)
    large = max_exact + (jnp.log(jnp.maximum(n, 1).astype(F32) / max_exact)
                         / math.log(REL_MAX_DIST / max_exact) * (half - max_exact)).astype(I32)
    large = jnp.minimum(large, half - 1)
    return jnp.where(rel > 0, half, 0) + jnp.where(n < max_exact, n, large)


def _bias_tiles_kernel(rb_ref, out_ref, *, tq, tk):
    offset = -tq * pl.program_id(0)
    key = lax.broadcasted_iota(I32, (tk, tq), 0)
    qry = lax.broadcasted_iota(I32, (tk, tq), 1)
    bucket = _t5_bucket(offset + key - qry)
    for h in range(B_HEADS):
        far = rb_ref[FAR_BUCKET * B_HEADS + h]
        acc = jnp.zeros((tk, tq), F32)
        for b in range(REL_BUCKETS):
            acc = jnp.where(bucket == b, (rb_ref[b * B_HEADS + h] - far) * LOG2E, acc)
        out_ref[0, h] = acc


def _bias_tiles(rel_bias, tq, tk):
    n_off = _n_near_offsets(tq, tk)
    grid_spec = pltpu.PrefetchScalarGridSpec(
        num_scalar_prefetch=1,
        grid=(n_off,),
        in_specs=[],
        out_specs=pl.BlockSpec((1, B_HEADS, tk, tq), lambda i, rb: (i, 0, 0, 0)),
    )
    return pl.pallas_call(
        functools.partial(_bias_tiles_kernel, tq=tq, tk=tk),
        out_shape=jax.ShapeDtypeStruct((n_off, B_HEADS, tk, tq), F32),
        grid_spec=grid_spec,
        compiler_params=_cparams(("arbitrary",)),
        name="dsa_bias_tiles",
    )(rel_bias.reshape(-1))


def _proj1_kernel(x_ref, w_ref, wit_ref, qn_ref, kvn_ref, cq_ref, ckv_ref, ckvt_ref, ki_ref, wi_ref,
                  *, o1, o2, o3, wscale):
    x = x_ref[...]
    proj = _dot(x.astype(BF16), w_ref[...])
    cq_ref[...] = _rms_norm_rows(proj[:, :o1], qn_ref[...]).astype(BF16)
    ckv = _rms_norm_rows(proj[:, o1:o2], kvn_ref[...])
    ckv_ref[...] = ckv.astype(BF16)
    ckvt_ref[...] = ckv.T.astype(BF16)
    ki_ref[...] = proj[:, o2:o3].astype(BF16)
    wi_ref[...] = _dot_nt(wit_ref[...], x) * wscale


def _proj1(x2, w_in, q_norm, kv_norm, ql, kvl, batch, seq):
    n, d = x2.shape
    o1, o2, o3 = ql, ql + kvl, ql + kvl + IDX_DIM
    tm = ROW_TILE
    nt = seq // tm
    w_main = w_in[:, :o3].astype(BF16)
    w_idx_t = w_in[:, o3:].T
    kern = functools.partial(_proj1_kernel, o1=o1, o2=o2, o3=o3,
                             wscale=(IDX_HEADS ** -0.5) * (IDX_DIM ** -0.5))
    return pl.pallas_call(
        kern,
        out_shape=(jax.ShapeDtypeStruct((n, ql), BF16), jax.ShapeDtypeStruct((n, kvl), BF16),
                   jax.ShapeDtypeStruct((batch * kvl, seq), BF16),
                   jax.ShapeDtypeStruct((n, IDX_DIM), BF16), jax.ShapeDtypeStruct((IDX_HEADS, n), F32)),
        grid=(n // tm,),
        in_specs=[pl.BlockSpec((tm, d), lambda i: (i, 0)),
                  pl.BlockSpec((d, o3), lambda i: (0, 0)),
                  pl.BlockSpec((IDX_HEADS, d), lambda i: (0, 0)),
                  pl.BlockSpec((1, ql), lambda i: (0, 0)),
                  pl.BlockSpec((1, kvl), lambda i: (0, 0))],
        out_specs=(pl.BlockSpec((tm, ql), lambda i: (i, 0)), pl.BlockSpec((tm, kvl), lambda i: (i, 0)),
                   pl.BlockSpec((kvl, tm), lambda i: (i // nt, i % nt)),
                   pl.BlockSpec((tm, IDX_DIM), lambda i: (i, 0)), pl.BlockSpec((IDX_HEADS, tm), lambda i: (0, i))),
        compiler_params=_cparams(("arbitrary",)),
        name="dsa_proj",
    )(x2, w_main, w_idx_t, q_norm.reshape(1, ql), kv_norm.reshape(1, kvl))


def _qside_kernel(cq_ref, wuq_ref, wuk_ref, wiq_ref, ql_ref, qi_ref, *, scale):
    cq = cq_ref[...]
    q = _dot(cq, wuq_ref[...]).astype(BF16)
    for h in range(B_HEADS):
        qh = q[:, h * B_HEAD_DIM:(h + 1) * B_HEAD_DIM]
        ql_ref[h] = (_dot_nt(wuk_ref[h], qh) * scale).astype(BF16)
    qi_ref[...] = _dot(cq, wiq_ref[...]).astype(BF16)


def _qside(cq, w_uq_b, w_uk_b, w_iq_b):
    n, ql = cq.shape
    kvl = w_uk_b.shape[1]
    tm = ROW_TILE
    kern = functools.partial(_qside_kernel, scale=B_HEAD_DIM ** -0.5 * LOG2E)
    return pl.pallas_call(
        kern,
        out_shape=(jax.ShapeDtypeStruct((B_HEADS, kvl, n), BF16),
                   jax.ShapeDtypeStruct((n, IDX_HEADS * IDX_DIM), BF16)),
        grid=(n // tm,),
        in_specs=[pl.BlockSpec((tm, ql), lambda i: (i, 0)),
                  pl.BlockSpec(w_uq_b.shape, lambda i: (0, 0)),
                  pl.BlockSpec(w_uk_b.shape, lambda i: (0, 0, 0)),
                  pl.BlockSpec(w_iq_b.shape, lambda i: (0, 0))],
        out_specs=(pl.BlockSpec((B_HEADS, kvl, tm), lambda i: (0, 0, i)),
                   pl.BlockSpec((tm, IDX_HEADS * IDX_DIM), lambda i: (i, 0))),
        compiler_params=_cparams(("arbitrary",)),
        name="dsa_qside",
    )(cq, w_uq_b, w_uk_b, w_iq_b)


INT_MIN = -2 ** 31
KEY_NEG_INF = (0xFF800000 ^ 0x7FFFFFFF) - 2 ** 32


def _ordered_key(v):
    bits = lax.bitcast_convert_type(v, I32)
    return bits ^ (lax.shift_right_arithmetic(bits, 31) & 0x7FFFFFFF)


def _indexer_kernel(qi_ref, wi_ref, ki_ref, mask_ref, key_sc, *, tq, tkc, topk):
    i = pl.program_id(1)
    n_chunks = (i * tq + tq + tkc - 1) // tkc
    qpos = i * tq + lax.broadcasted_iota(I32, (1, tq), 1)
    limit = (lax.shift_right_logical(qpos, CHUNK.bit_length() - 1) + 1) * CHUNK
    w = wi_ref[...]
    key_sc[...] = jnp.full(key_sc.shape, KEY_NEG_INF, I32)

    def score_chunk(c, carry):
        start = pl.multiple_of(c * tkc, tkc)
        k = ki_ref[pl.ds(start, tkc), :]
        acc = jnp.zeros((tkc, tq), F32)
        for h in range(IDX_HEADS):
            sc = _dot_nt(k, qi_ref[:, h * IDX_DIM:(h + 1) * IDX_DIM])
            acc = acc + jnp.maximum(sc, 0.0) * w[h:h + 1, :]
        kpos = start + lax.broadcasted_iota(I32, (tkc, tq), 0)
        key_sc[pl.ds(start, tkc), :] = _ordered_key(jnp.where(kpos < limit, acc, -jnp.inf))
        return carry

    lax.fori_loop(0, n_chunks, score_chunk, 0)

    def count(pred_fn):
        def body(c, acc):
            start = pl.multiple_of(c * tkc, tkc)
            hit = pred_fn(key_sc[pl.ds(start, tkc), :]).astype(I32)
            return acc + jnp.sum(hit.reshape(tkc // SLAB, SLAB, tq), axis=0)
        acc = lax.fori_loop(0, n_chunks, body, jnp.zeros((SLAB, tq), I32))
        return jnp.sum(acc, axis=0, keepdims=True)

    def bisect(it, prefix):
        cand = prefix + lax.shift_left(jnp.int32(1), 31 - it)
        cnt = count(lambda kk: kk >= cand)
        return jnp.where(cnt >= topk, cand, prefix)

    thr = lax.fori_loop(0, 32, bisect, jnp.full((1, tq), INT_MIN, I32))
    n_gt = count(lambda kk: kk > thr)
    n_eq = count(lambda kk: kk == thr)
    need = topk - n_gt
    tie_break = jnp.max(jnp.where(jnp.logical_and(thr > KEY_NEG_INF, n_eq > need), 1, 0)) > 0

    mask_ref[...] = jnp.full(mask_ref.shape, NEG_BIG, F32)

    @pl.when(jnp.logical_not(tie_break))
    def _():
        def write(c, carry):
            start = pl.multiple_of(c * tkc, tkc)
            kk = key_sc[pl.ds(start, tkc), :]
            sel = jnp.logical_and(kk >= thr, kk > KEY_NEG_INF)
            mask_ref[pl.ds(start, tkc), :] = jnp.where(sel, 0.0, NEG_BIG)
            return carry
        lax.fori_loop(0, n_chunks, write, 0)

    @pl.when(tie_break)
    def _():
        r_ = lax.broadcasted_iota(I32, (LANES, LANES), 0)
        c_ = lax.broadcasted_iota(I32, (LANES, LANES), 1)
        lower = (c_ < r_).astype(BF16)

        def write(c, seen):
            start = pl.multiple_of(c * LANES, LANES)
            kk = key_sc[pl.ds(start, LANES), :]
            eq = kk == thr
            before = seen + _dot(lower, eq.astype(BF16))
            sel = jnp.logical_or(kk > thr, jnp.logical_and(eq, before < need.astype(F32)))
            sel = jnp.logical_and(sel, kk > KEY_NEG_INF)
            mask_ref[pl.ds(start, LANES), :] = jnp.where(sel, 0.0, NEG_BIG)
            return seen + jnp.sum(eq.astype(F32), axis=0, keepdims=True)
        lax.fori_loop(0, n_chunks * (tkc // LANES), write, jnp.zeros((1, tq), F32))


def _indexer(qidx, widx_t, kidx, batch, seq, topk):
    tq = IDX_TQ
    tkc = min(IDX_TKC, seq)
    nq = seq // tq
    kern = functools.partial(_indexer_kernel, tq=tq, tkc=tkc, topk=topk)
    return pl.pallas_call(
        kern,
        out_shape=jax.ShapeDtypeStruct((batch * seq, seq), F32),
        grid=(batch, nq),
        in_specs=[pl.BlockSpec((tq, IDX_HEADS * IDX_DIM), lambda b, i: (b * nq + i, 0)),
                  pl.BlockSpec((IDX_HEADS, tq), lambda b, i: (0, b * nq + i)),
                  pl.BlockSpec((seq, IDX_DIM), lambda b, i: (b, 0))],
        out_specs=pl.BlockSpec((seq, tq), lambda b, i: (b, i)),
        scratch_shapes=[pltpu.VMEM((seq, tq), I32)],
        compiler_params=_cparams(("arbitrary", "arbitrary")),
        name="dsa_indexer",
    )(qidx, widx_t, kidx)


def _dsa_attn_kernel(qi_ref, kj_ref, qlt_ref, kv_ref, kvt_ref, mask_ref, bias_ref, wuvt_ref, o_ref,
                     m_sc, l_sc, acc_sc, *, tq, tk, n_near):
    p = pl.program_id(1)
    i = qi_ref[p]
    j = kj_ref[p]
    nh = B_HEADS
    behind = (i * tq - j * tk) // tq

    @pl.when(j == 0)
    def _():
        m_sc[...] = jnp.full_like(m_sc, -jnp.inf)
        l_sc[...] = jnp.zeros_like(l_sc)
        acc_sc[...] = jnp.zeros_like(acc_sc)

    def step(near):
        kv = kv_ref[...]
        kvt = kvt_ref[...]
        msk = mask_ref[...]
        for h in range(nh):
            s = _dot(kv, qlt_ref[h]) + msk
            if near:
                s = s + bias_ref[behind, h]
            m_prev = m_sc[h]
            m_new = jnp.maximum(m_prev, jnp.max(s, axis=0, keepdims=True))
            alpha = jnp.exp2(m_prev - m_new)
            pexp = jnp.exp2(s - m_new)
            l_sc[h] = alpha * l_sc[h] + jnp.sum(pexp, axis=0, keepdims=True)
            acc_sc[h] = alpha * acc_sc[h] + _dot(kvt, pexp.astype(BF16))
            m_sc[h] = m_new

    @pl.when(behind >= n_near)
    def _():
        step(False)

    @pl.when(behind < n_near)
    def _():
        step(True)

    @pl.when(j == (i * tq + tq - 1) // tk)
    def _():
        for h in range(nh):
            o_lat_t = (acc_sc[h] / l_sc[h]).astype(BF16)
            o_t = _dot(wuvt_ref[h], o_lat_t)
            o_ref[:, h * B_V_DIM:(h + 1) * B_V_DIM] = o_t.T.astype(o_ref.dtype)


def _dsa_attention(qlt, ckv, ckvt, mask_t, bias_tiles, w_uvt_b, batch, seq):
    tq, tk = DSA_TQ, DSA_TK
    kvl = ckv.shape[1]
    nq = seq // tq
    nk = seq // tk
    last = [(i * tq + tq - 1) // tk for i in range(nq)]
    qi = np.concatenate([np.full(last[i] + 1, i) for i in range(nq)]).astype(np.int32)
    kj = np.concatenate([np.arange(last[i] + 1) for i in range(nq)]).astype(np.int32)
    kern = functools.partial(_dsa_attn_kernel, tq=tq, tk=tk, n_near=bias_tiles.shape[0])
    grid_spec = pltpu.PrefetchScalarGridSpec(
        num_scalar_prefetch=2,
        grid=(batch, len(qi)),
        in_specs=[
            pl.BlockSpec((B_HEADS, kvl, tq), lambda b, p, qi, kj: (0, 0, b * nq + qi[p])),
            pl.BlockSpec((tk, kvl), lambda b, p, qi, kj: (b * nk + kj[p], 0)),
            pl.BlockSpec((kvl, tk), lambda b, p, qi, kj: (b, kj[p])),
            pl.BlockSpec((tk, tq), lambda b, p, qi, kj: (b * nk + kj[p], qi[p])),
            pl.BlockSpec(bias_tiles.shape, lambda b, p, qi, kj: (0, 0, 0, 0), pipeline_mode=pl.Buffered(1)),
            pl.BlockSpec(w_uvt_b.shape, lambda b, p, qi, kj: (0, 0, 0), pipeline_mode=pl.Buffered(1)),
        ],
        out_specs=pl.BlockSpec((tq, B_HEADS * B_V_DIM), lambda b, p, qi, kj: (b * nq + qi[p], 0)),
        scratch_shapes=[pltpu.VMEM((B_HEADS, 1, tq), F32), pltpu.VMEM((B_HEADS, 1, tq), F32),
                        pltpu.VMEM((B_HEADS, kvl, tq), F32)],
    )
    return pl.pallas_call(
        kern,
        out_shape=jax.ShapeDtypeStruct((batch * seq, B_HEADS * B_V_DIM), BF16),
        grid_spec=grid_spec,
        compiler_params=_cparams(("arbitrary", "arbitrary")),
        name="dsa_attention",
    )(jnp.asarray(qi), jnp.asarray(kj), qlt, ckv, ckvt, mask_t, bias_tiles, w_uvt_b)


def _dsa_layer(x2, w_in, q_norm, kv_norm, w_uq, w_iq, w_uk, w_uv, w_out, rel_bias, ln_g, ln_b, batch, seq):
    ql_dim = q_norm.shape[0]
    kvl = kv_norm.shape[0]
    topk = min(IDX_TOPK, seq // 4)
    cq, ckv, ckvt, kidx, widx_t = _proj1(x2, w_in, q_norm, kv_norm, ql_dim, kvl, batch, seq)
    qlt, qidx = _qside(cq, w_uq.astype(BF16), w_uk.astype(BF16), w_iq.astype(BF16))
    mask_t = _indexer(qidx, widx_t, kidx, batch, seq, topk)
    bias_tiles = _bias_tiles(rel_bias, DSA_TQ, DSA_TK)
    w_uvt = jnp.swapaxes(w_uv, 1, 2).astype(BF16)
    o = _dsa_attention(qlt, ckv, ckvt, mask_t, bias_tiles, w_uvt, batch, seq)
    return _outproj_ln(o, w_out.astype(BF16), x2, ln_g, ln_b)


def kernel(x, a_w_in, a_b_f, a_w_out, b_w_in, b_q_norm, b_kv_norm, b_w_uq, b_w_iq, b_w_uk, b_w_uv, b_w_out,
           rel_bias, ln1_g, ln1_b, ln2_g, ln2_b, router_w, router_b, w_gate, w_up, w_down, sh_gate, sh_up,
           sh_down):
    batch, seq, d = x.shape
    x2 = x.reshape(batch * seq, d)
    x2 = _fox_layer(x2, a_w_in[0], a_b_f[0], a_w_out[0], ln1_g[0], ln1_b[0], batch, seq)
    x2 = _moe_layer(x2, router_w[0], router_b[0], w_gate, w_up, w_down, 0, sh_gate[0], sh_up[0],
                    sh_down[0], ln2_g[0], ln2_b[0])
    x2 = _dsa_layer(x2, b_w_in[0], b_q_norm[0], b_kv_norm[0], b_w_uq[0], b_w_iq[0], b_w_uk[0], b_w_uv[0],
                    b_w_out[0], rel_bias, ln1_g[1], ln1_b[1], batch, seq)
    x2 = _moe_layer(x2, router_w[1], router_b[1], w_gate, w_up, w_down, 1, sh_gate[1], sh_up[1],
                    sh_down[1], ln2_g[1], ln2_b[1])
    return x2.reshape(batch, seq, d)
```

```python
import functools
import math

import numpy as np
import jax
import jax.numpy as jnp
from jax import lax
from jax.experimental import pallas as pl
from jax.experimental.pallas import tpu as pltpu

BF16 = jnp.bfloat16
F32 = jnp.float32
I32 = jnp.int32
U32 = jnp.uint32

A_HEADS = 16
A_HEAD_DIM = 128
B_HEADS = 16
B_HEAD_DIM = 128
B_V_DIM = 128
IDX_HEADS = 16
IDX_DIM = 64
IDX_TOPK = 256
CHUNK = 64
REL_BUCKETS = 32
REL_MAX_DIST = 128
N_EXPERTS = 64
TOP_K = 8
ROUTED_SCALE = 2.5
DEPTH = 2
DEEPNORM_ALPHA = (2 * DEPTH) ** 0.25
LN_EPS = 1e-5
RMS_EPS = 1e-6

LANES = 128
SLAB = 8
HALF_BITS = 16
HIGH_HALF_MASK = 0xFFFF0000
VMEM_LIMIT = 56 * 1024 * 1024
LOG2E = math.log2(math.e)
NEG_BIG = -1e30

EXPERT_ROWS = 512
ROW_TILE = 512
PROJ0_TM, PROJ0_TN = 1024, 2048
CUMSUM_T = 256
DISPATCH_TT = 512
COMBINE_TT = 256
FOX_T = 512
FOX_HEADS_PER_STEP = 16
DSA_TQ = 256
DSA_TK = 512
IDX_TQ = 256
IDX_TKC = 512


def _cparams(sem, vmem=VMEM_LIMIT):
    return pltpu.CompilerParams(dimension_semantics=sem, vmem_limit_bytes=vmem)


def _dot(a, b):
    return jnp.dot(a, b, preferred_element_type=F32)


def _dot_nt(a, b):
    return lax.dot_general(a, b, (((1,), (1,)), ((), ())), preferred_element_type=F32)


def _lane_tile(a, width):
    return a if width == LANES else jnp.concatenate([a] * (width // LANES), axis=1)


def _split2(a):
    hi = a.astype(BF16)
    lo = (a - hi.astype(F32)).astype(BF16)
    return hi, lo


def _split3(a):
    hi = a.astype(BF16)
    r = a - hi.astype(F32)
    mid = r.astype(BF16)
    lo = (r - mid.astype(F32)).astype(BF16)
    return hi, mid, lo


def _dot_x3(a, b):
    ah, al = _split2(a)
    bh, bl = _split2(b)
    return _dot(ah, bh) + (_dot(ah, bl) + _dot(al, bh))


def _dot_x3_nt(a, b):
    ah, al = _split2(a)
    bh, bl = _split2(b)
    return _dot_nt(ah, bh) + (_dot_nt(ah, bl) + _dot_nt(al, bh))


def _layer_norm_rows(z, g, b):
    mu = jnp.mean(z, axis=-1, keepdims=True)
    d = z - mu
    var = jnp.mean(d * d, axis=-1, keepdims=True)
    return d * lax.rsqrt(var + LN_EPS) * g + b


def _rms_norm_rows(z, g):
    ms = jnp.mean(z * z, axis=-1, keepdims=True)
    return z * lax.rsqrt(ms + RMS_EPS) * g


def _proj0_kernel(x_ref, w_ref, wf_ref, qkv_ref, fl_ref, xb_sc, *, n_q_blocks, q_scale):
    j = pl.program_id(1)

    @pl.when(j == 0)
    def _():
        x = x_ref[...]
        xb_sc[...] = x.astype(BF16)
        fl_ref[...] = _dot_x3(x, wf_ref[...])

    acc = _dot(xb_sc[...], w_ref[...])
    scale = jnp.where(j < n_q_blocks, q_scale, 1.0).astype(F32)
    qkv_ref[...] = (acc * scale).astype(BF16)


def _proj0(x2, w_qkv_b, w_f):
    n, d = x2.shape
    nout = w_qkv_b.shape[1]
    tm, tn = PROJ0_TM, PROJ0_TN
    dq = A_HEADS * A_HEAD_DIM
    kern = functools.partial(_proj0_kernel, n_q_blocks=dq // tn, q_scale=A_HEAD_DIM ** -0.5 * LOG2E)
    return pl.pallas_call(
        kern,
        out_shape=(jax.ShapeDtypeStruct((n, nout), BF16),
                   jax.ShapeDtypeStruct((n, A_HEADS), F32)),
        grid=(n // tm, nout // tn),
        in_specs=[pl.BlockSpec((tm, d), lambda i, j: (i, 0)),
                  pl.BlockSpec((d, tn), lambda i, j: (0, j)),
                  pl.BlockSpec((d, A_HEADS), lambda i, j: (0, 0))],
        out_specs=(pl.BlockSpec((tm, tn), lambda i, j: (i, j)),
                   pl.BlockSpec((tm, A_HEADS), lambda i, j: (i, 0))),
        scratch_shapes=[pltpu.VMEM((tm, d), BF16)],
        compiler_params=_cparams(("arbitrary", "arbitrary")),
        name="fox_proj",
    )(x2, w_qkv_b, w_f)


def _forget_cumsum_kernel(fl_ref, bf_ref, f_ref, carry_sc, *, t):
    @pl.when(pl.program_id(1) == 0)
    def _():
        carry_sc[...] = jnp.zeros_like(carry_sc)

    z = fl_ref[...] + bf_ref[...]
    logf = jnp.minimum(z, 0.0) - jnp.log1p(jnp.exp(-jnp.abs(z)))
    row = lax.broadcasted_iota(I32, (t, t), 0)
    col = lax.broadcasted_iota(I32, (t, t), 1)
    tri = (col <= row).astype(BF16)
    hi, mid, lo = _split3(logf)
    cs = _dot(tri, hi) + (_dot(tri, mid) + _dot(tri, lo)) + carry_sc[...]
    f_ref[...] = cs * LOG2E
    carry_sc[...] = cs[t - 1:t, :]


def _forget_cumsum(fl, b_f, batch, seq):
    t = CUMSUM_T
    nb = seq // t
    return pl.pallas_call(
        functools.partial(_forget_cumsum_kernel, t=t),
        out_shape=jax.ShapeDtypeStruct(fl.shape, F32),
        grid=(batch, nb),
        in_specs=[pl.BlockSpec((t, A_HEADS), lambda b, i: (b * nb + i, 0)),
                  pl.BlockSpec((1, A_HEADS), lambda b, i: (0, 0))],
        out_specs=pl.BlockSpec((t, A_HEADS), lambda b, i: (b * nb + i, 0)),
        scratch_shapes=[pltpu.VMEM((1, A_HEADS), F32)],
        compiler_params=_cparams(("arbitrary", "arbitrary")),
        name="fox_forget_cumsum",
    )(fl, b_f.reshape(1, A_HEADS))


def _fox_attn_kernel(qi_ref, kj_ref, q_ref, k_ref, v_ref, fk_ref, o_ref, m_sc, l_sc, acc_sc, *, tq, tk, hp):
    p = pl.program_id(2)
    i = qi_ref[p]
    j = kj_ref[p]
    dh = A_HEAD_DIM

    @pl.when(j == 0)
    def _():
        m_sc[...] = jnp.full_like(m_sc, -jnp.inf)
        l_sc[...] = jnp.zeros_like(l_sc)
        acc_sc[...] = jnp.zeros_like(acc_sc)

    def step(diag):
        for hh in range(hp):
            cols = slice(hh * dh, (hh + 1) * dh)
            s = _dot_nt(q_ref[:, cols], k_ref[:, cols]) - fk_ref[hh]
            if diag:
                row = lax.broadcasted_iota(I32, (tq, tk), 0)
                col = lax.broadcasted_iota(I32, (tq, tk), 1)
                s = jnp.where(col <= row, s, -jnp.inf)
            m_prev = m_sc[hh]
            m_new = jnp.maximum(m_prev, jnp.max(s, axis=1, keepdims=True))
            alpha = jnp.exp2(m_prev - m_new)
            pexp = jnp.exp2(s - _lane_tile(m_new, tk))
            l_sc[hh] = alpha * l_sc[hh] + jnp.sum(pexp, axis=1, keepdims=True)
            acc_sc[hh] = alpha * acc_sc[hh] + _dot(pexp.astype(BF16), v_ref[:, cols])
            m_sc[hh] = m_new

    @pl.when(j < i)
    def _():
        step(False)

    @pl.when(j == i)
    def _():
        step(True)
        for hh in range(hp):
            o_ref[:, hh * dh:(hh + 1) * dh] = (acc_sc[hh] / l_sc[hh]).astype(o_ref.dtype)


def _fox_attention(qkv, f_rows, batch, seq):
    t = min(FOX_T, seq)
    nq = seq // t
    qi = np.concatenate([np.full(i + 1, i) for i in range(nq)]).astype(np.int32)
    kj = np.concatenate([np.arange(i + 1) for i in range(nq)]).astype(np.int32)
    hp = FOX_HEADS_PER_STEP
    hg = A_HEADS // hp
    dh = A_HEAD_DIM
    w = hp * dh
    kern = functools.partial(_fox_attn_kernel, tq=t, tk=t, hp=hp)
    grid_spec = pltpu.PrefetchScalarGridSpec(
        num_scalar_prefetch=2,
        grid=(batch, hg, len(qi)),
        in_specs=[
            pl.BlockSpec((t, w), lambda b, h, p, qi, kj: (b * nq + qi[p], h)),
            pl.BlockSpec((t, w), lambda b, h, p, qi, kj: (b * nq + kj[p], hg + h)),
            pl.BlockSpec((t, w), lambda b, h, p, qi, kj: (b * nq + kj[p], 2 * hg + h)),
            pl.BlockSpec((hp, 1, t), lambda b, h, p, qi, kj: (b * hg + h, 0, kj[p])),
        ],
        out_specs=pl.BlockSpec((t, w), lambda b, h, p, qi, kj: (b * nq + qi[p], h)),
        scratch_shapes=[pltpu.VMEM((hp, t, dh), F32), pltpu.VMEM((hp, t, dh), F32),
                        pltpu.VMEM((hp, t, dh), F32)],
    )
    return pl.pallas_call(
        kern,
        out_shape=jax.ShapeDtypeStruct((batch * seq, A_HEADS * dh), BF16),
        grid_spec=grid_spec,
        compiler_params=_cparams(("arbitrary", "arbitrary", "arbitrary")),
        name="fox_attention",
    )(jnp.asarray(qi), jnp.asarray(kj), qkv, qkv, qkv, f_rows)


def _outproj_ln_kernel(o_ref, w_ref, x_ref, g_ref, b_ref, y_ref):
    z = DEEPNORM_ALPHA * x_ref[...] + _dot(o_ref[...], w_ref[...])
    y_ref[...] = _layer_norm_rows(z, g_ref[...], b_ref[...])


def _outproj_ln(o, w_b, x2, g, b):
    n, d = x2.shape
    k = o.shape[1]
    tm = ROW_TILE
    return pl.pallas_call(
        _outproj_ln_kernel,
        out_shape=jax.ShapeDtypeStruct((n, d), F32),
        grid=(n // tm,),
        in_specs=[pl.BlockSpec((tm, k), lambda i: (i, 0)),
                  pl.BlockSpec((k, d), lambda i: (0, 0)),
                  pl.BlockSpec((tm, d), lambda i: (i, 0)),
                  pl.BlockSpec((1, d), lambda i: (0, 0)),
                  pl.BlockSpec((1, d), lambda i: (0, 0))],
        out_specs=pl.BlockSpec((tm, d), lambda i: (i, 0)),
        compiler_params=_cparams(("arbitrary",)),
        name="outproj_deepnorm",
    )(o, w_b, x2, g.reshape(1, d), b.reshape(1, d))


def _fox_layer(x2, w_in, b_f, w_out, ln_g, ln_b, batch, seq):
    dq = A_HEADS * A_HEAD_DIM
    qkv, fl = _proj0(x2, w_in[:, :3 * dq].astype(BF16), w_in[:, 3 * dq:])
    f = _forget_cumsum(fl, b_f, batch, seq)
    f_rows = f.reshape(batch, seq, A_HEADS).transpose(0, 2, 1).reshape(batch * A_HEADS, 1, seq)
    o = _fox_attention(qkv, f_rows, batch, seq)
    return _outproj_ln(o, w_out.astype(BF16), x2, ln_g, ln_b)


def _router_kernel(x_ref, rwt_ref, rb_ref, sel_ref, gate_ref, rank_ref, cnt_ref, carry_sc, *, tm):
    @pl.when(pl.program_id(0) == 0)
    def _():
        carry_sc[...] = jnp.zeros_like(carry_sc)

    e = N_EXPERTS
    scores = jax.nn.sigmoid(_dot_x3_nt(rwt_ref[...], x_ref[...]))
    expert = lax.broadcasted_iota(I32, (e, tm), 0)
    slot = lax.broadcasted_iota(I32, (TOP_K, tm), 0)
    work = scores + rb_ref[...]
    chosen = jnp.zeros((e, tm), F32)
    sel = jnp.zeros((TOP_K, tm), I32)
    gate = jnp.zeros((TOP_K, tm), F32)
    idxs = []
    for k in range(TOP_K):
        mx = jnp.max(work, axis=0, keepdims=True)
        idx = jnp.min(jnp.where(work == mx, expert, e), axis=0, keepdims=True)
        hit = expert == idx
        gk = jnp.sum(jnp.where(hit, scores, 0.0), axis=0, keepdims=True)
        work = jnp.where(hit, -jnp.inf, work)
        chosen = jnp.where(hit, 1.0, chosen)
        sel = jnp.where(slot == k, idx, sel)
        gate = jnp.where(slot == k, gk, gate)
        idxs.append(idx)
    gate = gate / jnp.sum(gate, axis=0, keepdims=True) * ROUTED_SCALE

    src = lax.broadcasted_iota(I32, (tm, tm), 0)
    dst = lax.broadcasted_iota(I32, (tm, tm), 1)
    before = _dot(chosen.astype(BF16), (src < dst).astype(BF16)) + carry_sc[...]
    rank = jnp.zeros((TOP_K, tm), F32)
    for k in range(TOP_K):
        rk = jnp.sum(jnp.where(expert == idxs[k], before, 0.0), axis=0, keepdims=True)
        rank = jnp.where(slot == k, rk, rank)
    total = carry_sc[...] + jnp.sum(chosen, axis=1, keepdims=True)
    carry_sc[...] = total
    sel_ref[...] = sel
    gate_ref[...] = gate
    rank_ref[...] = rank.astype(I32)
    cnt_ref[...] = total.astype(I32)


def _router(x2, rw, rb):
    n, d = x2.shape
    tm = ROW_TILE
    e = N_EXPERTS
    return pl.pallas_call(
        functools.partial(_router_kernel, tm=tm),
        out_shape=(jax.ShapeDtypeStruct((TOP_K, n), I32),
                   jax.ShapeDtypeStruct((TOP_K, n), F32),
                   jax.ShapeDtypeStruct((TOP_K, n), I32),
                   jax.ShapeDtypeStruct((e, 1), I32)),
        grid=(n // tm,),
        in_specs=[pl.BlockSpec((tm, d), lambda i: (i, 0)),
                  pl.BlockSpec((e, d), lambda i: (0, 0)),
                  pl.BlockSpec((e, 1), lambda i: (0, 0))],
        out_specs=(pl.BlockSpec((TOP_K, tm), lambda i: (0, i)),
                   pl.BlockSpec((TOP_K, tm), lambda i: (0, i)),
                   pl.BlockSpec((TOP_K, tm), lambda i: (0, i)),
                   pl.BlockSpec((e, 1), lambda i: (0, 0))),
        scratch_shapes=[pltpu.VMEM((e, 1), F32)],
        compiler_params=_cparams(("arbitrary",)),
        name="moe_router",
    )(x2, rw.T, rb.reshape(e, 1))


def _pack_bf16_pairs(y):
    w = y.shape[1] // 2
    lo = lax.bitcast_convert_type(y[:, :w].astype(BF16).astype(F32), U32)
    hi = lax.bitcast_convert_type(y[:, w:].astype(BF16).astype(F32), U32)
    return lax.shift_right_logical(lo, jnp.uint32(HALF_BITS)) | hi


def _low_half_f32(words):
    return lax.bitcast_convert_type(lax.shift_left(words, jnp.uint32(HALF_BITS)), F32)


def _high_half_f32(words):
    return lax.bitcast_convert_type(words & jnp.uint32(HIGH_HALF_MASK), F32)


def _unpack_bf16_pairs(words):
    return _low_half_f32(words).astype(BF16), _high_half_f32(words).astype(BF16)


def _slab_load(ref, rows):
    return jnp.concatenate([ref[pl.ds(c, rows, stride=SLAB), :] for c in range(SLAB)], axis=1)


def _slab_store(ref, rows, val):
    for c in range(SLAB):
        ref[pl.ds(c, rows, stride=SLAB), :] = val[:, c * LANES:(c + 1) * LANES]


def _dispatch_kernel(pad_ref, dest_hbm, x_ref, xs_hbm, dest_sm, slab_sc, zero_sc, sem_idx, sem_zero, sem_row,
                     *, tt, n_tiles):
    i = pl.program_id(0)

    def zero_copy(e):
        start = pl.multiple_of(pad_ref[e] * SLAB, SLAB)
        return pltpu.make_async_copy(zero_sc, xs_hbm.at[pl.ds(start, EXPERT_ROWS * SLAB)], sem_zero)

    @pl.when(i == 0)
    def _():
        zero_sc[...] = jnp.zeros_like(zero_sc)

        def start(e, c):
            @pl.when(pad_ref[N_EXPERTS + e] > 0)
            def _():
                zero_copy(e).start()
            return c

        def wait(e, c):
            @pl.when(pad_ref[N_EXPERTS + e] > 0)
            def _():
                zero_copy(e).wait()
            return c

        lax.fori_loop(0, N_EXPERTS, start, 0)
        lax.fori_loop(0, N_EXPERTS, wait, 0)

    slot = lax.rem(i, 2)
    idx_copy = pltpu.make_async_copy(dest_hbm.at[i], dest_sm, sem_idx)
    idx_copy.start()

    def row_copy(s, t, dst_row):
        src = pl.multiple_of(t * SLAB, SLAB)
        dst = pl.multiple_of(dst_row * SLAB, SLAB)
        return pltpu.make_async_copy(slab_sc.at[s, pl.ds(src, SLAB)], xs_hbm.at[pl.ds(dst, SLAB)], sem_row.at[s])

    def wait_rows(s):
        def wait(t, c):
            for k in range(TOP_K):
                row_copy(s, 0, 0).wait()
            return c
        lax.fori_loop(0, tt, wait, 0, unroll=2)

    @pl.when(i >= 2)
    def _():
        wait_rows(slot)

    _slab_store(slab_sc.at[slot], tt, _pack_bf16_pairs(x_ref[...]))
    idx_copy.wait()

    def start(t, c):
        for k in range(TOP_K):
            row_copy(slot, t, dest_sm[t * TOP_K + k]).start(priority=k % 2)
        return c

    lax.fori_loop(0, tt, start, 0, unroll=2)

    @pl.when(i == n_tiles - 1)
    def _():
        if n_tiles > 1:
            wait_rows(1 - slot)
        wait_rows(slot)


def _dispatch(pad_info, dest, x2, p_rows):
    n, d = x2.shape
    tt = DISPATCH_TT
    assert d == 2 * SLAB * LANES
    dest2 = dest.reshape(n // tt, tt * TOP_K)
    grid_spec = pltpu.PrefetchScalarGridSpec(
        num_scalar_prefetch=1,
        grid=(n // tt,),
        in_specs=[pl.BlockSpec(memory_space=pl.ANY),
                  pl.BlockSpec((tt, d), lambda i, pad: (i, 0))],
        out_specs=pl.BlockSpec(memory_space=pl.ANY),
        scratch_shapes=[pltpu.SMEM((tt * TOP_K,), I32),
                        pltpu.VMEM((2, tt * SLAB, LANES), U32),
                        pltpu.VMEM((EXPERT_ROWS * SLAB, LANES), U32),
                        pltpu.SemaphoreType.DMA, pltpu.SemaphoreType.DMA, pltpu.SemaphoreType.DMA((2,))],
    )
    return pl.pallas_call(
        functools.partial(_dispatch_kernel, tt=tt, n_tiles=n // tt),
        out_shape=jax.ShapeDtypeStruct((p_rows * SLAB, LANES), U32),
        grid_spec=grid_spec,
        compiler_params=_cparams(("arbitrary",)),
        name="moe_dispatch",
    )(pad_info, dest2, x2)


def _expert_mlp_kernel(be_ref, nu_ref, slot_ref, next_ref, xs_ref, wg_hbm, wu_hbm, wd_hbm, ys_ref,
                       wg_f, wu_f, wd_f, wg_sc, wu_sc, wd_sc, sem, *, layer):
    b = pl.program_id(0)
    e = be_ref[b]
    first = jnp.logical_or(b == 0, e != be_ref[jnp.maximum(b - 1, 0)])
    active = b < nu_ref[0]

    def fetch(expert, s):
        return (pltpu.make_async_copy(wg_hbm.at[layer, expert], wg_f.at[s], sem.at[s]),
                pltpu.make_async_copy(wu_hbm.at[layer, expert], wu_f.at[s], sem.at[s]),
                pltpu.make_async_copy(wd_hbm.at[layer, expert], wd_f.at[s], sem.at[s]))

    @pl.when(jnp.logical_and(active, first))
    def _():
        s = slot_ref[e]

        @pl.when(b == 0)
        def _():
            for c in fetch(e, s):
                c.start()

        for c in fetch(e, s):
            c.wait()
        nxt = next_ref[e]

        @pl.when(nxt >= 0)
        def _():
            for c in fetch(nxt, 1 - s):
                c.start()

        wg_sc[...] = wg_f[s].astype(BF16)
        wu_sc[...] = wu_f[s].astype(BF16)
        wd_sc[...] = wd_f[s].astype(BF16)

    @pl.when(active)
    def _():
        r = EXPERT_ROWS
        x = jnp.concatenate(_unpack_bf16_pairs(_slab_load(xs_ref, r)), axis=1)
        g = _dot(x, wg_sc[...])
        u = _dot(x, wu_sc[...])
        h = (g * jax.nn.sigmoid(g) * u).astype(BF16)
        _slab_store(ys_ref, r, _pack_bf16_pairs(_dot(h, wd_sc[...])))


def _expert_mlp(blk_e, n_used, w_slot, w_next, xs, w_gate, w_up, w_down, layer):
    r = EXPERT_ROWS
    nblk = xs.shape[0] // (r * SLAB)
    d, de = w_gate.shape[2], w_gate.shape[3]

    def row_map(b, be, nu, ws, wn):
        return (jnp.minimum(b, nu[0] - 1), 0)

    grid_spec = pltpu.PrefetchScalarGridSpec(
        num_scalar_prefetch=4,
        grid=(nblk,),
        in_specs=[pl.BlockSpec((r * SLAB, LANES), row_map),
                  pl.BlockSpec(memory_space=pl.ANY),
                  pl.BlockSpec(memory_space=pl.ANY),
                  pl.BlockSpec(memory_space=pl.ANY)],
        out_specs=pl.BlockSpec((r * SLAB, LANES), row_map),
        scratch_shapes=[pltpu.VMEM((2, d, de), F32), pltpu.VMEM((2, d, de), F32), pltpu.VMEM((2, de, d), F32),
                        pltpu.VMEM((d, de), BF16), pltpu.VMEM((d, de), BF16), pltpu.VMEM((de, d), BF16),
                        pltpu.SemaphoreType.DMA((2,))],
    )
    return pl.pallas_call(
        functools.partial(_expert_mlp_kernel, layer=layer),
        out_shape=jax.ShapeDtypeStruct(xs.shape, U32),
        grid_spec=grid_spec,
        compiler_params=_cparams(("arbitrary",)),
        name="moe_experts",
    )(blk_e, n_used, w_slot, w_next, xs, w_gate, w_up, w_down)


def _combine_kernel(dest_hbm, ys_hbm, x_ref, gate_ref, sg_ref, su_ref, sd_ref, g_ref, b_ref,
                    y_ref, dest_sm, rows_sc, sem_idx, sem_row, *, tt, n_tiles):
    i = pl.program_id(0)
    slot = lax.rem(i, 2)
    n_rows = tt * TOP_K

    def idx_copy(tile, s):
        return pltpu.make_async_copy(dest_hbm.at[tile], dest_sm.at[pl.ds(pl.multiple_of(s * n_rows, n_rows), n_rows)],
                                     sem_idx.at[s])

    def row_copy(s, t, k, src_row):
        return pltpu.make_async_copy(ys_hbm.at[pl.ds(pl.multiple_of(src_row * SLAB, SLAB), SLAB)],
                                     rows_sc.at[s, k, pl.ds(pl.multiple_of(t * SLAB, SLAB), SLAB)], sem_row.at[s])

    def start_rows(s):
        def start(t, c):
            for k in range(TOP_K):
                row_copy(s, t, k, dest_sm[s * n_rows + t * TOP_K + k]).start(priority=k % 2)
            return c
        lax.fori_loop(0, tt, start, 0, unroll=2)

    @pl.when(i == 0)
    def _():
        idx_copy(0, 0).start()
        idx_copy(0, 0).wait()
        start_rows(0)
        if n_tiles > 1:
            idx_copy(1, 1).start()

    @pl.when(i + 1 < n_tiles)
    def _():
        idx_copy(i + 1, 1 - slot).wait()
        start_rows(1 - slot)

    @pl.when(i + 2 < n_tiles)
    def _():
        idx_copy(i + 2, slot).start()

    x = x_ref[...]
    xb = x.astype(BF16)
    g = _dot(xb, sg_ref[...])
    u = _dot(xb, su_ref[...])
    h = (g * jax.nn.sigmoid(g) * u).astype(BF16)
    z = DEEPNORM_ALPHA * x + _dot(h, sd_ref[...])

    def wait(t, c):
        for k in range(TOP_K):
            row_copy(slot, 0, 0, 0).wait()
        return c
    lax.fori_loop(0, tt, wait, 0, unroll=2)

    gate = gate_ref[...]
    lo_pieces, hi_pieces = [], []
    for c in range(SLAB):
        acc_lo = jnp.zeros((tt, LANES), F32)
        acc_hi = jnp.zeros((tt, LANES), F32)
        for k in range(TOP_K):
            words = rows_sc[slot, k, pl.ds(c, tt, stride=SLAB), :]
            gk = gate[:, k:k + 1]
            acc_lo = acc_lo + gk * _low_half_f32(words)
            acc_hi = acc_hi + gk * _high_half_f32(words)
        lo_pieces.append(acc_lo)
        hi_pieces.append(acc_hi)
    routed = jnp.concatenate(lo_pieces + hi_pieces, axis=1)
    y_ref[...] = _layer_norm_rows(z + routed, g_ref[...], b_ref[...])


def _combine(dest, ys, x2, gate, sg_b, su_b, sd_b, ln_g, ln_b):
    n, d = x2.shape
    ds_ = sg_b.shape[1]
    tt = COMBINE_TT
    n_tiles = n // tt
    dest2 = dest.reshape(n_tiles, tt * TOP_K)
    return pl.pallas_call(
        functools.partial(_combine_kernel, tt=tt, n_tiles=n_tiles),
        out_shape=jax.ShapeDtypeStruct((n, d), F32),
        grid=(n_tiles,),
        in_specs=[pl.BlockSpec(memory_space=pl.ANY),
                  pl.BlockSpec(memory_space=pl.ANY),
                  pl.BlockSpec((tt, d), lambda i: (i, 0)),
                  pl.BlockSpec((tt, TOP_K), lambda i: (i, 0)),
                  pl.BlockSpec((d, ds_), lambda i: (0, 0)),
                  pl.BlockSpec((d, ds_), lambda i: (0, 0)),
                  pl.BlockSpec((ds_, d), lambda i: (0, 0)),
                  pl.BlockSpec((1, d), lambda i: (0, 0)),
                  pl.BlockSpec((1, d), lambda i: (0, 0))],
        out_specs=pl.BlockSpec((tt, d), lambda i: (i, 0)),
        scratch_shapes=[pltpu.SMEM((2 * tt * TOP_K,), I32),
                        pltpu.VMEM((2, TOP_K, tt * SLAB, LANES), U32),
                        pltpu.SemaphoreType.DMA((2,)), pltpu.SemaphoreType.DMA((2,))],
        compiler_params=_cparams(("arbitrary",)),
        name="moe_combine",
    )(dest2, ys, x2, gate, sg_b, su_b, sd_b, ln_g.reshape(1, d), ln_b.reshape(1, d))


def _moe_layer(x2, rw, rb, w_gate, w_up, w_down, layer, sh_gate, sh_up, sh_down, ln_g, ln_b):
    n, d = x2.shape
    e = N_EXPERTS
    r = EXPERT_ROWS
    sel_t, gate_t, rank_t, counts = _router(x2, rw, rb)
    counts = counts.reshape(e)
    padded = (counts + r - 1) // r * r
    pend = jnp.cumsum(padded)
    pstart = pend - padded
    offset_t = jnp.sum(jnp.where(sel_t[None] == jnp.arange(e, dtype=I32)[:, None, None],
                                 pstart.astype(I32)[:, None, None], 0), axis=0)
    dest = (rank_t + offset_t).T
    gate = gate_t.T
    p_rows = n * TOP_K + e * r
    nblk = p_rows // r
    blk_start = jnp.arange(nblk, dtype=I32) * r
    blk_e = jnp.minimum(jnp.sum((pend[None, :] <= blk_start[:, None]).astype(I32), axis=1), e - 1)
    n_used = (pend[-1] // r).astype(I32).reshape(1)
    pad_info = jnp.concatenate([jnp.maximum(pend - r, 0), padded]).astype(I32)
    used = padded > 0
    eid = jnp.arange(e, dtype=I32)
    w_slot = ((jnp.cumsum(used.astype(I32)) - 1) & 1).astype(I32)
    later = lax.cummin(jnp.where(used, eid, e), reverse=True)
    w_next = jnp.concatenate([later[1:], jnp.full((1,), e, I32)])
    w_next = jnp.where(w_next < e, w_next, -1).astype(I32)
    xs = _dispatch(pad_info, dest.astype(I32), x2, p_rows)
    ys = _expert_mlp(blk_e, n_used, w_slot, w_next, xs, w_gate, w_up, w_down, layer)
    return _combine(dest.astype(I32), ys, x2, gate, sh_gate.astype(BF16), sh_up.astype(BF16),
                    sh_down.astype(BF16), ln_g, ln_b)


FAR_BUCKET = REL_BUCKETS // 2 - 1


def _n_near_offsets(tq, tk):
    return (tk + REL_MAX_DIST - 1 + tq - 1) // tq


def _t5_bucket(rel):
    half = REL_BUCKETS // 2
    max_exact = half // 2
    n = jnp.abs(rel)
    large = max_exact + (jnp.log(jnp.maximum(n, 1).astype(F32) / max_exact)
                         / math.log(REL_MAX_DIST / max_exact) * (half - max_exact)).astype(I32)
    large = jnp.minimum(large, half - 1)
    return jnp.where(rel > 0, half, 0) + jnp.where(n < max_exact, n, large)


def _bias_tiles_kernel(rb_ref, out_ref, *, tq, tk):
    offset = -tq * pl.program_id(0)
    key = lax.broadcasted_iota(I32, (tk, tq), 0)
    qry = lax.broadcasted_iota(I32, (tk, tq), 1)
    bucket = _t5_bucket(offset + key - qry)
    for h in range(B_HEADS):
        far = rb_ref[FAR_BUCKET * B_HEADS + h]
        acc = jnp.zeros((tk, tq), F32)
        for b in range(REL_BUCKETS):
            acc = jnp.where(bucket == b, (rb_ref[b * B_HEADS + h] - far) * LOG2E, acc)
        out_ref[0, h] = acc


def _bias_tiles(rel_bias, tq, tk):
    n_off = _n_near_offsets(tq, tk)
    grid_spec = pltpu.PrefetchScalarGridSpec(
        num_scalar_prefetch=1,
        grid=(n_off,),
        in_specs=[],
        out_specs=pl.BlockSpec((1, B_HEADS, tk, tq), lambda i, rb: (i, 0, 0, 0)),
    )
    return pl.pallas_call(
        functools.partial(_bias_tiles_kernel, tq=tq, tk=tk),
        out_shape=jax.ShapeDtypeStruct((n_off, B_HEADS, tk, tq), F32),
        grid_spec=grid_spec,
        compiler_params=_cparams(("arbitrary",)),
        name="dsa_bias_tiles",
    )(rel_bias.reshape(-1))


def _proj1_kernel(x_ref, w_ref, wit_ref, qn_ref, kvn_ref, cq_ref, ckv_ref, ckvt_ref, ki_ref, wi_ref,
                  *, o1, o2, o3, wscale):
    x = x_ref[...]
    proj = _dot(x.astype(BF16), w_ref[...])
    cq_ref[...] = _rms_norm_rows(proj[:, :o1], qn_ref[...]).astype(BF16)
    ckv = _rms_norm_rows(proj[:, o1:o2], kvn_ref[...])
    ckv_ref[...] = ckv.astype(BF16)
    ckvt_ref[...] = ckv.T.astype(BF16)
    ki_ref[...] = proj[:, o2:o3].astype(BF16)
    wi_ref[...] = _dot_nt(wit_ref[...], x) * wscale


def _proj1(x2, w_in, q_norm, kv_norm, ql, kvl, batch, seq):
    n, d = x2.shape
    o1, o2, o3 = ql, ql + kvl, ql + kvl + IDX_DIM
    tm = ROW_TILE
    nt = seq // tm
    w_main = w_in[:, :o3].astype(BF16)
    w_idx_t = w_in[:, o3:].T
    kern = functools.partial(_proj1_kernel, o1=o1, o2=o2, o3=o3,
                             wscale=(IDX_HEADS ** -0.5) * (IDX_DIM ** -0.5))
    return pl.pallas_call(
        kern,
        out_shape=(jax.ShapeDtypeStruct((n, ql), BF16), jax.ShapeDtypeStruct((n, kvl), BF16),
                   jax.ShapeDtypeStruct((batch * kvl, seq), BF16),
                   jax.ShapeDtypeStruct((n, IDX_DIM), BF16), jax.ShapeDtypeStruct((IDX_HEADS, n), F32)),
        grid=(n // tm,),
        in_specs=[pl.BlockSpec((tm, d), lambda i: (i, 0)),
                  pl.BlockSpec((d, o3), lambda i: (0, 0)),
                  pl.BlockSpec((IDX_HEADS, d), lambda i: (0, 0)),
                  pl.BlockSpec((1, ql), lambda i: (0, 0)),
                  pl.BlockSpec((1, kvl), lambda i: (0, 0))],
        out_specs=(pl.BlockSpec((tm, ql), lambda i: (i, 0)), pl.BlockSpec((tm, kvl), lambda i: (i, 0)),
                   pl.BlockSpec((kvl, tm), lambda i: (i // nt, i % nt)),
                   pl.BlockSpec((tm, IDX_DIM), lambda i: (i, 0)), pl.BlockSpec((IDX_HEADS, tm), lambda i: (0, i))),
        compiler_params=_cparams(("arbitrary",)),
        name="dsa_proj",
    )(x2, w_main, w_idx_t, q_norm.reshape(1, ql), kv_norm.reshape(1, kvl))


def _qside_kernel(cq_ref, wuq_ref, wuk_ref, wiq_ref, ql_ref, qi_ref, *, scale):
    cq = cq_ref[...]
    q = _dot(cq, wuq_ref[...]).astype(BF16)
    for h in range(B_HEADS):
        qh = q[:, h * B_HEAD_DIM:(h + 1) * B_HEAD_DIM]
        ql_ref[h] = (_dot_nt(wuk_ref[h], qh) * scale).astype(BF16)
    qi_ref[...] = _dot(cq, wiq_ref[...]).astype(BF16)


def _qside(cq, w_uq_b, w_uk_b, w_iq_b):
    n, ql = cq.shape
    kvl = w_uk_b.shape[1]
    tm = ROW_TILE
    kern = functools.partial(_qside_kernel, scale=B_HEAD_DIM ** -0.5 * LOG2E)
    return pl.pallas_call(
        kern,
        out_shape=(jax.ShapeDtypeStruct((B_HEADS, kvl, n), BF16),
                   jax.ShapeDtypeStruct((n, IDX_HEADS * IDX_DIM), BF16)),
        grid=(n // tm,),
        in_specs=[pl.BlockSpec((tm, ql), lambda i: (i, 0)),
                  pl.BlockSpec(w_uq_b.shape, lambda i: (0, 0)),
                  pl.BlockSpec(w_uk_b.shape, lambda i: (0, 0, 0)),
                  pl.BlockSpec(w_iq_b.shape, lambda i: (0, 0))],
        out_specs=(pl.BlockSpec((B_HEADS, kvl, tm), lambda i: (0, 0, i)),
                   pl.BlockSpec((tm, IDX_HEADS * IDX_DIM), lambda i: (i, 0))),
        compiler_params=_cparams(("arbitrary",)),
        name="dsa_qside",
    )(cq, w_uq_b, w_uk_b, w_iq_b)


INT_MIN = -2 ** 31
KEY_NEG_INF = (0xFF800000 ^ 0x7FFFFFFF) - 2 ** 32


def _ordered_key(v):
    bits = lax.bitcast_convert_type(v, I32)
    return bits ^ (lax.shift_right_arithmetic(bits, 31) & 0x7FFFFFFF)


def _indexer_kernel(qi_ref, wi_ref, ki_ref, mask_ref, key_sc, *, tq, tkc, topk):
    i = pl.program_id(1)
    n_chunks = (i * tq + tq + tkc - 1) // tkc
    qpos = i * tq + lax.broadcasted_iota(I32, (1, tq), 1)
    limit = (lax.shift_right_logical(qpos, CHUNK.bit_length() - 1) + 1) * CHUNK
    w = wi_ref[...]
    key_sc[...] = jnp.full(key_sc.shape, KEY_NEG_INF, I32)

    def score_chunk(c, carry):
        start = pl.multiple_of(c * tkc, tkc)
        k = ki_ref[pl.ds(start, tkc), :]
        acc = jnp.zeros((tkc, tq), F32)
        for h in range(IDX_HEADS):
            sc = _dot_nt(k, qi_ref[:, h * IDX_DIM:(h + 1) * IDX_DIM])
            acc = acc + jnp.maximum(sc, 0.0) * w[h:h + 1, :]
        kpos = start + lax.broadcasted_iota(I32, (tkc, tq), 0)
        key_sc[pl.ds(start, tkc), :] = _ordered_key(jnp.where(kpos < limit, acc, -jnp.inf))
        return carry

    lax.fori_loop(0, n_chunks, score_chunk, 0)

    def count(pred_fn):
        def body(c, acc):
            start = pl.multiple_of(c * tkc, tkc)
            hit = pred_fn(key_sc[pl.ds(start, tkc), :]).astype(I32)
            return acc + jnp.sum(hit.reshape(tkc // SLAB, SLAB, tq), axis=0)
        acc = lax.fori_loop(0, n_chunks, body, jnp.zeros((SLAB, tq), I32))
        return jnp.sum(acc, axis=0, keepdims=True)

    def bisect(it, prefix):
        cand = prefix + lax.shift_left(jnp.int32(1), 31 - it)
        cnt = count(lambda kk: kk >= cand)
        return jnp.where(cnt >= topk, cand, prefix)

    thr = lax.fori_loop(0, 32, bisect, jnp.full((1, tq), INT_MIN, I32))
    n_gt = count(lambda kk: kk > thr)
    n_eq = count(lambda kk: kk == thr)
    need = topk - n_gt
    tie_break = jnp.max(jnp.where(jnp.logical_and(thr > KEY_NEG_INF, n_eq > need), 1, 0)) > 0

    mask_ref[...] = jnp.full(mask_ref.shape, NEG_BIG, F32)

    @pl.when(jnp.logical_not(tie_break))
    def _():
        def write(c, carry):
            start = pl.multiple_of(c * tkc, tkc)
            kk = key_sc[pl.ds(start, tkc), :]
            sel = jnp.logical_and(kk >= thr, kk > KEY_NEG_INF)
            mask_ref[pl.ds(start, tkc), :] = jnp.where(sel, 0.0, NEG_BIG)
            return carry
        lax.fori_loop(0, n_chunks, write, 0)

    @pl.when(tie_break)
    def _():
        r_ = lax.broadcasted_iota(I32, (LANES, LANES), 0)
        c_ = lax.broadcasted_iota(I32, (LANES, LANES), 1)
        lower = (c_ < r_).astype(BF16)

        def write(c, seen):
            start = pl.multiple_of(c * LANES, LANES)
            kk = key_sc[pl.ds(start, LANES), :]
            eq = kk == thr
            before = seen + _dot(lower, eq.astype(BF16))
            sel = jnp.logical_or(kk > thr, jnp.logical_and(eq, before < need.astype(F32)))
            sel = jnp.logical_and(sel, kk > KEY_NEG_INF)
            mask_ref[pl.ds(start, LANES), :] = jnp.where(sel, 0.0, NEG_BIG)
            return seen + jnp.sum(eq.astype(F32), axis=0, keepdims=True)
        lax.fori_loop(0, n_chunks * (tkc // LANES), write, jnp.zeros((1, tq), F32))


def _indexer(qidx, widx_t, kidx, batch, seq, topk):
    tq = IDX_TQ
    tkc = min(IDX_TKC, seq)
    nq = seq // tq
    kern = functools.partial(_indexer_kernel, tq=tq, tkc=tkc, topk=topk)
    return pl.pallas_call(
        kern,
        out_shape=jax.ShapeDtypeStruct((batch * seq, seq), F32),
        grid=(batch, nq),
        in_specs=[pl.BlockSpec((tq, IDX_HEADS * IDX_DIM), lambda b, i: (b * nq + i, 0)),
                  pl.BlockSpec((IDX_HEADS, tq), lambda b, i: (0, b * nq + i)),
                  pl.BlockSpec((seq, IDX_DIM), lambda b, i: (b, 0))],
        out_specs=pl.BlockSpec((seq, tq), lambda b, i: (b, i)),
        scratch_shapes=[pltpu.VMEM((seq, tq), I32)],
        compiler_params=_cparams(("arbitrary", "arbitrary")),
        name="dsa_indexer",
    )(qidx, widx_t, kidx)


def _dsa_attn_kernel(qi_ref, kj_ref, qlt_ref, kv_ref, kvt_ref, mask_ref, bias_ref, wuvt_ref, o_ref,
                     m_sc, l_sc, acc_sc, *, tq, tk, n_near):
    p = pl.program_id(1)
    i = qi_ref[p]
    j = kj_ref[p]
    nh = B_HEADS
    behind = (i * tq - j * tk) // tq

    @pl.when(j == 0)
    def _():
        m_sc[...] = jnp.full_like(m_sc, -jnp.inf)
        l_sc[...] = jnp.zeros_like(l_sc)
        acc_sc[...] = jnp.zeros_like(acc_sc)

    def step(near):
        kv = kv_ref[...]
        kvt = kvt_ref[...]
        msk = mask_ref[...]
        for h in range(nh):
            s = _dot(kv, qlt_ref[h]) + msk
            if near:
                s = s + bias_ref[behind, h]
            m_prev = m_sc[h]
            m_new = jnp.maximum(m_prev, jnp.max(s, axis=0, keepdims=True))
            alpha = jnp.exp2(m_prev - m_new)
            pexp = jnp.exp2(s - m_new)
            l_sc[h] = alpha * l_sc[h] + jnp.sum(pexp, axis=0, keepdims=True)
            acc_sc[h] = alpha * acc_sc[h] + _dot(kvt, pexp.astype(BF16))
            m_sc[h] = m_new

    @pl.when(behind >= n_near)
    def _():
        step(False)

    @pl.when(behind < n_near)
    def _():
        step(True)

    @pl.when(j == (i * tq + tq - 1) // tk)
    def _():
        for h in range(nh):
            o_lat_t = (acc_sc[h] / l_sc[h]).astype(BF16)
            o_t = _dot(wuvt_ref[h], o_lat_t)
            o_ref[:, h * B_V_DIM:(h + 1) * B_V_DIM] = o_t.T.astype(o_ref.dtype)


def _dsa_attention(qlt, ckv, ckvt, mask_t, bias_tiles, w_uvt_b, batch, seq):
    tq, tk = DSA_TQ, DSA_TK
    kvl = ckv.shape[1]
    nq = seq // tq
    nk = seq // tk
    last = [(i * tq + tq - 1) // tk for i in range(nq)]
    qi = np.concatenate([np.full(last[i] + 1, i) for i in range(nq)]).astype(np.int32)
    kj = np.concatenate([np.arange(last[i] + 1) for i in range(nq)]).astype(np.int32)
    kern = functools.partial(_dsa_attn_kernel, tq=tq, tk=tk, n_near=bias_tiles.shape[0])
    grid_spec = pltpu.PrefetchScalarGridSpec(
        num_scalar_prefetch=2,
        grid=(batch, len(qi)),
        in_specs=[
            pl.BlockSpec((B_HEADS, kvl, tq), lambda b, p, qi, kj: (0, 0, b * nq + qi[p])),
            pl.BlockSpec((tk, kvl), lambda b, p, qi, kj: (b * nk + kj[p], 0)),
            pl.BlockSpec((kvl, tk), lambda b, p, qi, kj: (b, kj[p])),
            pl.BlockSpec((tk, tq), lambda b, p, qi, kj: (b * nk + kj[p], qi[p])),
            pl.BlockSpec(bias_tiles.shape, lambda b, p, qi, kj: (0, 0, 0, 0), pipeline_mode=pl.Buffered(1)),
            pl.BlockSpec(w_uvt_b.shape, lambda b, p, qi, kj: (0, 0, 0), pipeline_mode=pl.Buffered(1)),
        ],
        out_specs=pl.BlockSpec((tq, B_HEADS * B_V_DIM), lambda b, p, qi, kj: (b * nq + qi[p], 0)),
        scratch_shapes=[pltpu.VMEM((B_HEADS, 1, tq), F32), pltpu.VMEM((B_HEADS, 1, tq), F32),
                        pltpu.VMEM((B_HEADS, kvl, tq), F32)],
    )
    return pl.pallas_call(
        kern,
        out_shape=jax.ShapeDtypeStruct((batch * seq, B_HEADS * B_V_DIM), BF16),
        grid_spec=grid_spec,
        compiler_params=_cparams(("arbitrary", "arbitrary")),
        name="dsa_attention",
    )(jnp.asarray(qi), jnp.asarray(kj), qlt, ckv, ckvt, mask_t, bias_tiles, w_uvt_b)


def _dsa_layer(x2, w_in, q_norm, kv_norm, w_uq, w_iq, w_uk, w_uv, w_out, rel_bias, ln_g, ln_b, batch, seq):
    ql_dim = q_norm.shape[0]
    kvl = kv_norm.shape[0]
    topk = min(IDX_TOPK, seq // 4)
    cq, ckv, ckvt, kidx, widx_t = _proj1(x2, w_in, q_norm, kv_norm, ql_dim, kvl, batch, seq)
    qlt, qidx = _qside(cq, w_uq.astype(BF16), w_uk.astype(BF16), w_iq.astype(BF16))
    mask_t = _indexer(qidx, widx_t, kidx, batch, seq, topk)
    bias_tiles = _bias_tiles(rel_bias, DSA_TQ, DSA_TK)
    w_uvt = jnp.swapaxes(w_uv, 1, 2).astype(BF16)
    o = _dsa_attention(qlt, ckv, ckvt, mask_t, bias_tiles, w_uvt, batch, seq)
    return _outproj_ln(o, w_out.astype(BF16), x2, ln_g, ln_b)


def kernel(x, a_w_in, a_b_f, a_w_out, b_w_in, b_q_norm, b_kv_norm, b_w_uq, b_w_iq, b_w_uk, b_w_uv, b_w_out,
           rel_bias, ln1_g, ln1_b, ln2_g, ln2_b, router_w, router_b, w_gate, w_up, w_down, sh_gate, sh_up,
           sh_down):
    batch, seq, d = x.shape
    x2 = x.reshape(batch * seq, d)
    x2 = _fox_layer(x2, a_w_in[0], a_b_f[0], a_w_out[0], ln1_g[0], ln1_b[0], batch, seq)
    x2 = _moe_layer(x2, router_w[0], router_b[0], w_gate, w_up, w_down, 0, sh_gate[0], sh_up[0],
                    sh_down[0], ln2_g[0], ln2_b[0])
    x2 = _dsa_layer(x2, b_w_in[0], b_q_norm[0], b_kv_norm[0], b_w_uq[0], b_w_iq[0], b_w_uk[0], b_w_uv[0],
                    b_w_out[0], rel_bias, ln1_g[1], ln1_b[1], batch, seq)
    x2 = _moe_layer(x2, router_w[1], router_b[1], w_gate, w_up, w_down, 1, sh_gate[1], sh_up[1],
                    sh_down[1], ln2_g[1], ln2_b[1])
    return x2.reshape(batch, seq, d)
```
